```python
import math
import jax, jax.numpy as jnp
from jax import lax
import numpy as np

D_MODEL = 1024
BATCH = 4
SEQ = 4096
DEPTH = 4

D_BRANCH = 512
N_BRANCHES = 3
EPS = 1e-6
GMLP_CHUNK = 128
GMLP_GROUPS = 4
GMLP_GDIM = D_BRANCH // GMLP_GROUPS
GLA_HEADS = 4
GLA_DK = D_BRANCH // 2 // GLA_HEADS
GLA_DV = D_BRANCH // GLA_HEADS
GLA_GATE_RANK = 16
GLA_GATE_TAU = 16.0
GLA_CHUNK = 64
DSA_HEADS = 4
DSA_HDIM = D_BRANCH // DSA_HEADS
DSA_IDX_HEADS = 4
DSA_IDX_DIM = 64
DSA_QBLOCK = 128
DSA_TOPK_MAX = 256
N_BUCKETS = 32
MAX_DISTANCE = 128
N_GROUPS = 4
EXPERTS_PER_GROUP = 4
N_EXPERTS = N_GROUPS * EXPERTS_PER_GROUP
D_EXPERT = 256
TOPK_IN_GROUP = 2

IN_SIZES = (
    D_BRANCH, D_BRANCH,
    GLA_HEADS * GLA_DK, GLA_HEADS * GLA_DK,
    GLA_HEADS * GLA_DV, GLA_HEADS * GLA_DV,
    GLA_GATE_RANK,
    DSA_HEADS * DSA_HDIM, DSA_HEADS * DSA_HDIM, DSA_HEADS * DSA_HDIM,
    DSA_IDX_HEADS * DSA_IDX_DIM, DSA_IDX_DIM, DSA_IDX_HEADS,
    N_BRANCHES * D_MODEL,
)
N_IN = sum(IN_SIZES)

kernel_name = "hybrid_gmlp_gla_dsa_hier_moe"


def rms_norm(x, g):
    xf = x.astype(jnp.float32)
    y = xf * lax.rsqrt(jnp.mean(xf * xf, axis=-1, keepdims=True) + EPS)
    return (y * g.astype(jnp.float32)).astype(x.dtype)


def layer_norm(x, g, b):
    xf = x.astype(jnp.float32)
    mu = jnp.mean(xf, axis=-1, keepdims=True)
    var = jnp.mean(jnp.square(xf - mu), axis=-1, keepdims=True)
    y = (xf - mu) * lax.rsqrt(var + EPS)
    return (y * g.astype(jnp.float32) + b.astype(jnp.float32)).astype(x.dtype)


def t5_bucket(rel):
    n = jnp.maximum(rel, 0)
    max_exact = N_BUCKETS // 2
    large = max_exact + (
        jnp.log(jnp.maximum(n, max_exact).astype(jnp.float32) / max_exact)
        / math.log(MAX_DISTANCE / max_exact) * (N_BUCKETS - max_exact)
    ).astype(jnp.int32)
    large = jnp.minimum(large, N_BUCKETS - 1)
    return jnp.where(n < max_exact, n, large)


def gmlp_branch(u, v, ln_g, ln_b, w_s, b_s):
    bsz, s, _ = v.shape
    v = layer_norm(v, ln_g, ln_b)
    vc = v.reshape(bsz, s // GMLP_CHUNK, GMLP_CHUNK, GMLP_GROUPS, GMLP_GDIM)
    causal = jnp.tril(jnp.ones((GMLP_CHUNK, GMLP_CHUNK), dtype=bool))
    w = jnp.where(causal[None], w_s, jnp.zeros_like(w_s))
    mixed = jnp.einsum('gts,bcsgd->bctgd', w, vc) + b_s.T[:, :, None]
    return u * mixed.reshape(bsz, s, D_BRANCH)


def gla_scan(q, k, v, g):
    bsz, s, h, _ = q.shape
    out_dtype = v.dtype
    q = q.astype(jnp.float32) * (GLA_DK ** -0.5)

    def to_chunks(t):
        t = t.astype(jnp.float32)
        return t.reshape(bsz, s // GLA_CHUNK, GLA_CHUNK, h, t.shape[-1]).transpose(1, 0, 3, 2, 4)

    causal = jnp.tril(jnp.ones((GLA_CHUNK, GLA_CHUNK), dtype=bool))[None, None, :, :, None]

    def step(state, inp):
        qc, kc, vc, gc = inp
        b = jnp.cumsum(gc, axis=2)
        b_last = b[:, :, -1:, :]
        o_inter = jnp.einsum('bhtd,bhde->bhte', qc * jnp.exp(b), state)
        diff = b[:, :, :, None, :] - b[:, :, None, :, :]
        decay = jnp.where(causal, jnp.exp(jnp.minimum(diff, 0.0)), 0.0)
        attn = jnp.einsum('bhtd,bhsd,bhtsd->bhts', qc, kc, decay)
        o = o_inter + jnp.einsum('bhts,bhse->bhte', attn, vc)
        new_state = (jnp.exp(b_last[:, :, 0, :])[..., None] * state
                     + jnp.einsum('bhsd,bhse->bhde', kc * jnp.exp(b_last - b), vc))
        return new_state, o

    state0 = jnp.zeros((bsz, h, GLA_DK, GLA_DV), jnp.float32)
    _, o = lax.scan(step, state0, (to_chunks(q), to_chunks(k), to_chunks(v), to_chunks(g)))
    return o.transpose(1, 0, 3, 2, 4).reshape(bsz, s, h, GLA_DV).astype(out_dtype)


def dsa_attention(q, k, v, iq, ik, iw, rel_bias):
    bsz, s, h, hd = q.shape
    n_blocks = s // DSA_QBLOCK
    topk = min(DSA_TOPK_MAX, s // 4)
    key_pos = jnp.arange(s, dtype=jnp.int32)
    gather = jax.vmap(lambda arr, idx: arr[idx])

    def block(i):
        start = i * DSA_QBLOCK
        qb = lax.dynamic_slice_in_dim(q, start, DSA_QBLOCK, axis=1)
        iqb = lax.dynamic_slice_in_dim(iq, start, DSA_QBLOCK, axis=1)
        iwb = lax.dynamic_slice_in_dim(iw, start, DSA_QBLOCK, axis=1)
        qpos = start + jnp.arange(DSA_QBLOCK, dtype=jnp.int32)
        raw = jnp.einsum('bthd,bsd->bths', iqb, ik)
        score = jnp.einsum('bths,bth->bts', jax.nn.relu(raw), iwb).astype(jnp.float32)
        admissible = key_pos[None, :] <= qpos[:, None]
        score = jnp.where(admissible[None], score, -jnp.inf)
        _, top_idx = lax.top_k(score, topk)
        valid = top_idx <= qpos[None, :, None]
        kg = gather(k, top_idx)
        vg = gather(v, top_idx)
        logits = jnp.einsum('bthd,btkhd->btkh', qb, kg).astype(jnp.float32) * (hd ** -0.5)
        bias = rel_bias[t5_bucket(qpos[None, :, None] - top_idx)].astype(jnp.float32)
        logits = jnp.where(valid[..., None], logits + bias, -1e30)
        p = jax.nn.softmax(logits, axis=2).astype(v.dtype)
        return jnp.einsum('btkh,btkhd->bthd', p, vg)

    out = lax.map(block, jnp.arange(n_blocks))
    return out.transpose(1, 0, 2, 3, 4).reshape(bsz, s, h * hd)


def hybrid_mixer(h, w_in, gmlp_ln_g, gmlp_ln_b, gmlp_w_s, gmlp_b_s, gla_w_gate2, gla_b_gate,
                 gla_norm_g, dsa_qnorm_g, dsa_knorm_g, rel_bias, w_branch, b_branch_gate, w_out):
    bsz, s, _ = h.shape
    proj = h @ w_in
    splits = np.cumsum(IN_SIZES)[:-1].tolist()
    (a_u, a_v, g_q, g_k, g_v, g_r, g_a, d_q, d_k, d_v, d_iq, d_ik, d_iw, gates) = jnp.split(
        proj, splits, axis=-1)

    y_a = gmlp_branch(jax.nn.gelu(a_u), jax.nn.gelu(a_v), gmlp_ln_g, gmlp_ln_b, gmlp_w_s, gmlp_b_s)

    log_alpha = jax.nn.log_sigmoid((g_a @ gla_w_gate2 + gla_b_gate).astype(jnp.float32)) / GLA_GATE_TAU
    o_b = gla_scan(g_q.reshape(bsz, s, GLA_HEADS, GLA_DK), g_k.reshape(bsz, s, GLA_HEADS, GLA_DK),
                   g_v.reshape(bsz, s, GLA_HEADS, GLA_DV),
                   log_alpha.reshape(bsz, s, GLA_HEADS, GLA_DK))
    o_b = rms_norm(o_b, gla_norm_g) * jax.nn.silu(g_r.reshape(bsz, s, GLA_HEADS, GLA_DV))
    y_b = o_b.reshape(bsz, s, D_BRANCH)

    q = rms_norm(d_q.reshape(bsz, s, DSA_HEADS, DSA_HDIM), dsa_qnorm_g)
    k = rms_norm(d_k.reshape(bsz, s, DSA_HEADS, DSA_HDIM), dsa_knorm_g)
    y_c = dsa_attention(q, k, d_v.reshape(bsz, s, DSA_HEADS, DSA_HDIM),
                        d_iq.reshape(bsz, s, DSA_IDX_HEADS, DSA_IDX_DIM), d_ik, d_iw, rel_bias)

    ys = jnp.stack([y_a, y_b.astype(y_a.dtype), y_c.astype(y_a.dtype)], axis=0)
    up = jnp.einsum('nbse,ned->bsnd', ys, w_branch)
    gate = jax.nn.sigmoid(gates.reshape(bsz, s, N_BRANCHES, D_MODEL) + b_branch_gate)
    merged = jnp.sum(gate * up, axis=2)
    return (merged @ w_out).astype(h.dtype)


def hier_moe(h, w_group, b_group, w_router, b_router, w_exp_gate, w_exp_up, w_exp_down):
    bsz, s, d = h.shape
    hf = h.reshape(-1, d)
    grp_prob = jax.nn.softmax((hf @ w_group + b_group).astype(jnp.float32), axis=-1)
    p_g, g_idx = lax.top_k(grp_prob, 1)
    g_onehot = jax.nn.one_hot(g_idx[:, 0], N_GROUPS, dtype=jnp.float32)
    exp_logits = (hf @ w_router + b_router).astype(jnp.float32).reshape(-1, N_GROUPS, EXPERTS_PER_GROUP)
    sel_logits = jnp.einsum('tge,tg->te', exp_logits, g_onehot)
    top_vals, e_idx = lax.top_k(sel_logits, TOPK_IN_GROUP)
    weights = p_g * jax.nn.softmax(top_vals, axis=-1)
    eid = g_idx * EXPERTS_PER_GROUP + e_idx
    combine = jnp.einsum('tk,tke->te', weights,
                         jax.nn.one_hot(eid, N_EXPERTS, dtype=jnp.float32)).astype(hf.dtype)
    y = jnp.zeros_like(hf)
    for gi in range(N_GROUPS):
        sl = slice(gi * EXPERTS_PER_GROUP, (gi + 1) * EXPERTS_PER_GROUP)
        hg = jnp.einsum('td,edf->etf', hf, w_exp_gate[sl])
        hu = jnp.einsum('td,edf->etf', hf, w_exp_up[sl])
        act = jax.nn.silu(hg) * hu * combine[:, sl].T[:, :, None]
        y = y + jnp.einsum('etf,efd->td', act, w_exp_down[sl]).astype(hf.dtype)
    return y.reshape(bsz, s, d)


def setup_inputs(seed: int = 0) -> dict:
    key = jax.random.key(seed)
    ks = jax.random.split(key, 28)
    L = DEPTH

    def nrm(k, shape, scale):
        return jax.random.normal(k, shape, jnp.float32) * scale

    return {
        "x": nrm(ks[0], (BATCH, SEQ, D_MODEL), 1.0),
        "c": nrm(ks[1], (BATCH, D_MODEL), 1.0),
        "w_mod": nrm(ks[2], (L, D_MODEL, 6 * D_MODEL), 0.5 * D_MODEL ** -0.5),
        "b_mod": nrm(ks[3], (L, 6 * D_MODEL), 0.02),
        "g_norm1": 1.0 + nrm(ks[4], (L, D_MODEL), 0.1),
        "g_norm2": 1.0 + nrm(ks[5], (L, D_MODEL), 0.1),
        "w_in": nrm(ks[6], (L, D_MODEL, N_IN), D_MODEL ** -0.5),
        "gmlp_ln_g": 1.0 + nrm(ks[7], (L, D_BRANCH), 0.1),
        "gmlp_ln_b": nrm(ks[8], (L, D_BRANCH), 0.02),
        "gmlp_w_s": nrm(ks[9], (L, GMLP_GROUPS, GMLP_CHUNK, GMLP_CHUNK), GMLP_CHUNK ** -0.5),
        "gmlp_b_s": 1.0 + nrm(ks[10], (L, GMLP_GROUPS, GMLP_CHUNK), 0.1),
        "gla_w_gate2": nrm(ks[11], (L, GLA_GATE_RANK, GLA_HEADS * GLA_DK), GLA_GATE_RANK ** -0.5),
        "gla_b_gate": nrm(ks[12], (L, GLA_HEADS * GLA_DK), 0.1),
        "gla_norm_g": 1.0 + nrm(ks[13], (L, GLA_DV), 0.1),
        "dsa_qnorm_g": 1.0 + nrm(ks[14], (L, DSA_HDIM), 0.1),
        "dsa_knorm_g": 1.0 + nrm(ks[15], (L, DSA_HDIM), 0.1),
        "rel_bias": nrm(ks[16], (N_BUCKETS, DSA_HEADS), 0.5),
        "w_branch": nrm(ks[17], (L, N_BRANCHES, D_BRANCH, D_MODEL), D_BRANCH ** -0.5),
        "b_branch_gate": nrm(ks[18], (L, N_BRANCHES, D_MODEL), 0.1),
        "w_out": nrm(ks[19], (L, D_MODEL, D_MODEL), D_MODEL ** -0.5),
        "w_group": nrm(ks[20], (L, D_MODEL, N_GROUPS), D_MODEL ** -0.5),
        "b_group": nrm(ks[21], (L, N_GROUPS), 0.01),
        "w_router": nrm(ks[22], (L, D_MODEL, N_EXPERTS), D_MODEL ** -0.5),
        "b_router": nrm(ks[23], (L, N_EXPERTS), 0.01),
        "w_exp_gate": nrm(ks[24], (L, N_EXPERTS, D_MODEL, D_EXPERT), D_MODEL ** -0.5),
        "w_exp_up": nrm(ks[25], (L, N_EXPERTS, D_MODEL, D_EXPERT), D_MODEL ** -0.5),
        "w_exp_down": nrm(ks[26], (L, N_EXPERTS, D_EXPERT, D_MODEL), D_EXPERT ** -0.5),
    }


def reference(x, c, w_mod, b_mod, g_norm1, g_norm2, w_in, gmlp_ln_g, gmlp_ln_b, gmlp_w_s, gmlp_b_s,
              gla_w_gate2, gla_b_gate, gla_norm_g, dsa_qnorm_g, dsa_knorm_g, rel_bias, w_branch,
              b_branch_gate, w_out, w_group, b_group, w_router, b_router, w_exp_gate, w_exp_up,
              w_exp_down):
    c_act = jax.nn.silu(c)
    for l in range(DEPTH):
        mod = c_act @ w_mod[l] + b_mod[l]
        sh1, sc1, gt1, sh2, sc2, gt2 = [m[:, None, :] for m in jnp.split(mod, 6, axis=-1)]
        h = rms_norm(x, g_norm1[l]) * (1.0 + sc1) + sh1
        mix = hybrid_mixer(h, w_in[l], gmlp_ln_g[l], gmlp_ln_b[l], gmlp_w_s[l], gmlp_b_s[l],
                           gla_w_gate2[l], gla_b_gate[l], gla_norm_g[l], dsa_qnorm_g[l],
                           dsa_knorm_g[l], rel_bias, w_branch[l], b_branch_gate[l], w_out[l])
        x = (x + gt1 * mix).astype(x.dtype)
        h2 = rms_norm(x, g_norm2[l]) * (1.0 + sc2) + sh2
        ffn = hier_moe(h2, w_group[l], b_group[l], w_router[l], b_router[l],
                       w_exp_gate[l], w_exp_up[l], w_exp_down[l])
        x = (x + gt2 * ffn).astype(x.dtype)
    return x
```

```python
import functools
import math

import numpy as np
import jax
import jax.numpy as jnp
from jax import lax
from jax.experimental import pallas as pl
from jax.experimental.pallas import tpu as pltpu

F32 = jnp.float32
BF16 = jnp.bfloat16

D_MODEL = 1024
D_BRANCH = 512
EPS = 1e-6
GMLP_CHUNK = 128
GMLP_GROUPS = 4
GLA_HEADS = 4
GLA_DK = 64
GLA_DV = 128
GLA_GATE_RANK = 16
GLA_GATE_TAU = 16.0
GLA_CHUNK = 64
GLA_SUB = 16
DSA_HEADS = 4
DSA_HDIM = 128
DSA_IDX_HEADS = 4
DSA_IDX_DIM = 64
DSA_QBLOCK = 128
DSA_TOPK_MAX = 256
N_BUCKETS = 32
MAX_DISTANCE = 128
N_GROUPS = 4
EXPERTS_PER_GROUP = 4
N_EXPERTS = 16
D_EXPERT = 256

LANES = 128
COL_TILE = 512
VMEM_LIMIT = 56 * 1024 * 1024
INT_MIN = -(2 ** 31)
NEG_BIG = -1e30

C32_U, C32_V, C32_GQ, C32_GK, C32_R, C32_GATES, C32_SMALL = 0, 512, 1024, 1280, 1536, 2048, 5120
C32_GA, C32_IW = C32_SMALL, C32_SMALL + GLA_GATE_RANK
N32 = 5632
C16_Q, C16_K, C16_V, C16_GV, C16_IQ, C16_IK = 0, 512, 1024, 1536, 2048, 2304
N16 = 2560


def _dot(a, b):
    return jnp.dot(a, b, preferred_element_type=F32)


def _dot_nt(a, b):
    return lax.dot_general(a, b, (((1,), (1,)), ((), ())), preferred_element_type=F32)


def _dot_tn(a, b):
    return lax.dot_general(a, b, (((0,), (0,)), ((), ())), preferred_element_type=F32)


def _split2(a):
    hi = a.astype(BF16)
    lo = (a - hi.astype(F32)).astype(BF16)
    return hi, lo


def _dot3(a, w_hi, w_lo):
    a_hi, a_lo = _split2(a)
    return _dot(a_hi, w_hi) + (_dot(a_lo, w_hi) + _dot(a_hi, w_lo))


def _head_rms(y, g, scale):
    outs = []
    for h in range(y.shape[1] // LANES):
        yh = y[:, h * LANES:(h + 1) * LANES]
        ms = jnp.mean(yh * yh, axis=-1, keepdims=True)
        o = yh * lax.rsqrt(ms + EPS) * g
        if scale != 1.0:
            o = o * scale
        outs.append(o)
    return jnp.concatenate(outs, axis=1)


def _mod_kernel(c_ref, w_ref, b_ref, o_ref):
    a = jax.nn.silu(c_ref[...])
    w_hi, w_lo = _split2(w_ref[...])
    o_ref[...] = _dot3(a, w_hi, w_lo) + b_ref[...]


def _modulation(c, w_mod, b_mod):
    L, d, n = w_mod.shape
    bsz = c.shape[0]
    rows = 8 * pl.cdiv(bsz, 8)
    c_pad = jnp.zeros((rows, d), F32).at[:bsz].set(c)
    tn = 1536
    out = pl.pallas_call(
        _mod_kernel,
        grid=(L, n // tn),
        in_specs=[
            pl.BlockSpec((rows, d), lambda l, j: (0, 0)),
            pl.BlockSpec((None, d, tn), lambda l, j: (l, 0, j)),
            pl.BlockSpec((None, 1, tn), lambda l, j: (l, 0, j)),
        ],
        out_specs=pl.BlockSpec((None, rows, tn), lambda l, j: (l, 0, j)),
        out_shape=jax.ShapeDtypeStruct((L, rows, n), F32),
        compiler_params=pltpu.CompilerParams(
            dimension_semantics=("arbitrary", "arbitrary"), vmem_limit_bytes=VMEM_LIMIT),
        name="adaln_modulation",
    )(c_pad, w_mod, b_mod.reshape(L, 1, n))
    return out[:, :bsz].reshape(L, bsz, 6, d)


def _t5_bucket_table():
    n = np.arange(2 * DSA_QBLOCK)
    max_exact = N_BUCKETS // 2
    large = max_exact + (
        np.log(np.maximum(n, max_exact).astype(np.float32) / max_exact)
        / math.log(MAX_DISTANCE / max_exact) * (N_BUCKETS - max_exact)).astype(np.int32)
    large = np.minimum(large, N_BUCKETS - 1)
    return np.where(n < max_exact, n, large).astype(np.int32)


def _bias_kernel(rb_ref, bucket_ref, o_ref):
    for t in range(2):
        bucket = bucket_ref[t]
        for h in range(DSA_HEADS):
            acc = jnp.zeros(bucket.shape, F32)
            for b in range(N_BUCKETS):
                acc = jnp.where(bucket == b, rb_ref[b, h], acc)
            o_ref[t, h] = acc


def _bias_tiles(rel_bias):
    table = _t5_bucket_table()
    t = np.arange(DSA_QBLOCK)[:, None]
    s = np.arange(DSA_QBLOCK)[None, :]
    diag = table[np.maximum(t - s, 0)]
    near = table[DSA_QBLOCK + t - s]
    buckets = jnp.asarray(np.stack([diag, near]).astype(np.int32))
    return pl.pallas_call(
        _bias_kernel,
        in_specs=[pl.BlockSpec(memory_space=pltpu.SMEM), pl.BlockSpec(memory_space=pltpu.VMEM)],
        out_specs=pl.BlockSpec(memory_space=pltpu.VMEM),
        out_shape=jax.ShapeDtypeStruct((2, DSA_HEADS, DSA_QBLOCK, DSA_QBLOCK), F32),
        name="t5_bias_tiles",
    )(rel_bias, buckets)


def _proj_kernel(x_ref, mod_ref, gn_ref, w_ref, *rest, epilogues, n_aux):
    aux = rest[:n_aux]
    o_ref = rest[n_aux]
    h_scr = rest[n_aux + 1]
    j = pl.program_id(1)

    @pl.when(j == 0)
    def _():
        x = x_ref[...]
        y = x * lax.rsqrt(jnp.mean(x * x, axis=-1, keepdims=True) + EPS) * gn_ref[...]
        h = y * (1.0 + mod_ref[0, 1:2, :]) + mod_ref[0, 0:1, :]
        h_scr[...] = h.astype(BF16)

    for jj, epi in enumerate(epilogues):
        @pl.when(j == jj)
        def _(epi=epi):
            y = _dot(h_scr[...], w_ref[...])
            o_ref[...] = epi(y, aux).astype(o_ref.dtype)


def _norm_proj(x2d, mod_l, gn, w, aux, epilogues, out_dtype, seq, tm):
    tokens, d = x2d.shape
    n = w.shape[1]
    n_tiles = n // COL_TILE
    assert n_tiles == len(epilogues)
    tiles_per_seq = seq // tm
    aux_specs = [pl.BlockSpec(a.shape, lambda i, j: (0, 0)) for a in aux]
    return pl.pallas_call(
        functools.partial(_proj_kernel, epilogues=epilogues, n_aux=len(aux)),
        grid=(tokens // tm, n_tiles),
        in_specs=[
            pl.BlockSpec((tm, d), lambda i, j: (i, 0)),
            pl.BlockSpec((1, 6, d), lambda i, j: (i // tiles_per_seq, 0, 0)),
            pl.BlockSpec((1, d), lambda i, j: (0, 0)),
            pl.BlockSpec((d, COL_TILE), lambda i, j: (0, j)),
        ] + aux_specs,
        out_specs=pl.BlockSpec((tm, COL_TILE), lambda i, j: (i, j)),
        out_shape=jax.ShapeDtypeStruct((tokens, n), out_dtype),
        scratch_shapes=[pltpu.VMEM((tm, d), BF16)],
        compiler_params=pltpu.CompilerParams(
            dimension_semantics=("parallel", "arbitrary"), vmem_limit_bytes=VMEM_LIMIT),
        name="norm_proj_" + jnp.dtype(out_dtype).name,
    )(x2d, mod_l, gn, w, *aux)


def _epi_raw(y, aux):
    return y


def _epi_gelu(y, aux):
    return jax.nn.gelu(y)


def _epi_gelu_ln(y, aux):
    v = jax.nn.gelu(y)
    mu = jnp.mean(v, axis=-1, keepdims=True)
    var = jnp.mean(jnp.square(v - mu), axis=-1, keepdims=True)
    return (v - mu) * lax.rsqrt(var + EPS) * aux[0][...] + aux[1][...]


def _epi_silu(y, aux):
    return jax.nn.silu(y)


def _epi_gate(k):
    def epi(y, aux):
        return jax.nn.sigmoid(y + aux[2][:, k * COL_TILE:(k + 1) * COL_TILE])
    return epi


def _epi_qnorm(y, aux):
    return _head_rms(y, aux[0][...], DSA_HDIM ** -0.5)


def _epi_knorm(y, aux):
    return _head_rms(y, aux[1][...], 1.0)


EPILOGUES_32 = ([_epi_gelu, _epi_gelu_ln, _epi_raw, _epi_silu]
                + [_epi_gate(k) for k in range(6)] + [_epi_raw])
EPILOGUES_16 = [_epi_qnorm, _epi_knorm, _epi_raw, _epi_raw, _epi_raw]


def _prep_w_in(w_in):
    sizes = (512, 512, 256, 256, 512, 512, 16, 512, 512, 512, 256, 64, 4, 3072)
    offs = np.concatenate([[0], np.cumsum(sizes)])
    seg = lambda k: w_in[:, :, offs[k]:offs[k + 1]]
    (a_u, a_v, g_q, g_k, g_v, g_r, g_a, d_q, d_k, d_v, d_iq, d_ik, d_iw, gates) = [seg(k) for k in range(14)]
    L, d, _ = w_in.shape
    zeros = lambda n: jnp.zeros((L, d, n), w_in.dtype)
    w32 = jnp.concatenate([a_u, a_v, g_q, g_k, g_r, gates, g_a, d_iw,
                           zeros(N32 - C32_SMALL - 20)], axis=-1).astype(BF16)
    w16 = jnp.concatenate([d_q, d_k, d_v, g_v, d_iq, d_ik, d_ik, zeros(N16 - C16_IK - 128)],
                          axis=-1).astype(BF16)
    return w32, w16


def _gmlp_kernel(u_ref, v_ref, w_ref, b_ref, o_ref):
    tm = u_ref.shape[0]
    for c in range(tm // GMLP_CHUNK):
        rows = slice(c * GMLP_CHUNK, (c + 1) * GMLP_CHUNK)
        for g in range(GMLP_GROUPS):
            cols = slice(g * LANES, (g + 1) * LANES)
            mixed = _dot(w_ref[g], v_ref[rows, cols].astype(BF16)) + b_ref[:, cols]
            o_ref[rows, cols] = (u_ref[rows, cols] * mixed).astype(o_ref.dtype)


def _gmlp(p32, w_tril, bias_full, tm):
    tokens = p32.shape[0]
    return pl.pallas_call(
        _gmlp_kernel,
        grid=(tokens // tm,),
        in_specs=[
            pl.BlockSpec((tm, D_BRANCH), lambda i: (i, C32_U // D_BRANCH)),
            pl.BlockSpec((tm, D_BRANCH), lambda i: (i, C32_V // D_BRANCH)),
            pl.BlockSpec(w_tril.shape, lambda i: (0, 0, 0)),
            pl.BlockSpec(bias_full.shape, lambda i: (0, 0)),
        ],
        out_specs=pl.BlockSpec((tm, D_BRANCH), lambda i: (i, 0)),
        out_shape=jax.ShapeDtypeStruct((tokens, D_BRANCH), BF16),
        compiler_params=pltpu.CompilerParams(
            dimension_semantics=("parallel",), vmem_limit_bytes=VMEM_LIMIT),
        name="gmlp_branch",
    )(p32, p32, w_tril, bias_full)


def _gla_kernel(qk_ref, v_ref, r_ref, ga_ref, w2_ref, b2_ref, ng_ref, tril_ref, exp_ref,
                o_ref, st_ref, d_scr):
    C, SUB, H, DK, DV = GLA_CHUNK, GLA_SUB, GLA_HEADS, GLA_DK, GLA_DV
    HK = H * DK

    @pl.when(pl.program_id(1) == 0)
    def _():
        st_ref[...] = jnp.zeros_like(st_ref)

    lane = lax.broadcasted_iota(jnp.int32, (1, HK), 1)
    head_mask = [(lane >= h * DK) & (lane < (h + 1) * DK) for h in range(H)]
    row = lax.broadcasted_iota(jnp.int32, (C, C), 0)
    col = lax.broadcasted_iota(jnp.int32, (C, C), 1)
    sub_shift = SUB.bit_length() - 1
    blk_lower = (row >> sub_shift) > (col >> sub_shift)
    sub_t = lax.broadcasted_iota(jnp.int32, (SUB, 1), 0)

    def chunk(ci, carry):
        r0 = pl.multiple_of(ci * C, C)
        rows = pl.ds(r0, C)
        q = qk_ref[rows, 0:HK] * (DK ** -0.5)
        k = qk_ref[rows, HK:2 * HK]
        v = v_ref[rows, :]
        graw = _dot(ga_ref[rows, 0:LANES].astype(BF16), w2_ref[...]) + b2_ref[...]
        g = jax.nn.log_sigmoid(graw) / GLA_GATE_TAU
        g_hi = g.astype(BF16)
        g_r1 = g - g_hi.astype(F32)
        g_mid = g_r1.astype(BF16)
        g_lo = (g_r1 - g_mid.astype(F32)).astype(BF16)
        tril = tril_ref[...]
        b = _dot(tril, g_hi) + (_dot(tril, g_mid) + _dot(tril, g_lo))
        b_last = b[C - 1:C, :]
        st = st_ref[...]

        q_in = q * jnp.exp(b)
        k_dec = (k * jnp.exp(b_last - b)).astype(BF16)

        a_off = [jnp.zeros((C, C), F32) for _ in range(H)]
        for j in range(C // SUB - 1):
            bj = b[(j + 1) * SUB - 1:(j + 1) * SUB, :]
            qj = q * jnp.exp(jnp.minimum(b - bj, 0.0))
            in_blk = (lax.broadcasted_iota(jnp.int32, (C, 1), 0) >> sub_shift) == j
            kj = jnp.where(in_blk, k * jnp.exp(jnp.minimum(bj - b, 0.0)), 0.0).astype(BF16)
            for h in range(H):
                a_off[h] = a_off[h] + _dot_nt(jnp.where(head_mask[h], qj, 0.0).astype(BF16), kj)

        for i in range(C // SUB):
            rs = slice(i * SUB, (i + 1) * SUB)
            qi, bi = q[rs, :], b[rs, :]
            for s in range(SUB):
                ks = k[i * SUB + s:i * SUB + s + 1, :]
                bs = b[i * SUB + s:i * SUB + s + 1, :]
                dterm = qi * ks * jnp.exp(jnp.minimum(bi - bs, 0.0))
                dterm = jnp.where(sub_t >= s, dterm, 0.0)
                d_scr[s * SUB:(s + 1) * SUB, :] = dterm.astype(BF16)
            gsum = _dot(d_scr[...], exp_ref[...])
            od = jnp.zeros((SUB, H * DV), F32)
            for s in range(SUB):
                vs = v[i * SUB + s:i * SUB + s + 1, :].astype(F32)
                od = od + gsum[s * SUB:(s + 1) * SUB, :] * vs
            o_ref[pl.ds(r0 + i * SUB, SUB), :] = od

        outs = []
        for h in range(H):
            vh = v[:, h * DV:(h + 1) * DV]
            o_inter = _dot_nt(jnp.where(head_mask[h], q_in, 0.0).astype(BF16), st.astype(BF16))
            a_h = jnp.where(blk_lower, a_off[h], 0.0).astype(BF16)
            outs.append(o_inter + _dot(a_h, vh))
        o = o_ref[rows, :] + jnp.concatenate(outs, axis=1)

        upd = _dot_tn(v, k_dec)
        new_st = st * jnp.exp(b_last)
        for h in range(H):
            new_st = new_st + jnp.where(head_mask[h], upd[h * DV:(h + 1) * DV, :], 0.0)
        st_ref[...] = new_st

        y = _head_rms(o, ng_ref[...], 1.0) * r_ref[rows, :]
        o_ref[rows, :] = y
        return carry

    lax.fori_loop(0, qk_ref.shape[0] // C, chunk, 0)


def _gla(p32, p16, w2p, b2, norm_g, bsz, seq, ts):
    C, SUB, H, DK, DV = GLA_CHUNK, GLA_SUB, GLA_HEADS, GLA_DK, GLA_DV
    tokens = p32.shape[0]
    blocks_per_seq = seq // ts
    tril = jnp.asarray(np.tril(np.ones((C, C), np.float32))).astype(BF16)
    expand = np.zeros((H * DK, H * DV), np.float32)
    for h in range(H):
        expand[h * DK:(h + 1) * DK, h * DV:(h + 1) * DV] = 1.0
    expand = jnp.asarray(expand).astype(BF16)
    row_map = lambda cb: (lambda b, i: (b * blocks_per_seq + i, cb))
    const2 = lambda b, i: (0, 0)
    out = pl.pallas_call(
        _gla_kernel,
        grid=(bsz, blocks_per_seq),
        in_specs=[
            pl.BlockSpec((ts, 2 * H * DK), row_map(C32_GQ // (2 * H * DK))),
            pl.BlockSpec((ts, H * DV), row_map(C16_GV // (H * DV))),
            pl.BlockSpec((ts, H * DV), row_map(C32_R // (H * DV))),
            pl.BlockSpec((ts, COL_TILE), row_map(C32_SMALL // COL_TILE)),
            pl.BlockSpec(w2p.shape, const2),
            pl.BlockSpec(b2.shape, const2),
            pl.BlockSpec(norm_g.shape, const2),
            pl.BlockSpec(tril.shape, const2),
            pl.BlockSpec(expand.shape, const2),
        ],
        out_specs=pl.BlockSpec((ts, H * DV), lambda b, i: (b * blocks_per_seq + i, 0)),
        out_shape=jax.ShapeDtypeStruct((tokens, H * DV), F32),
        scratch_shapes=[pltpu.VMEM((DV, H * DK), F32), pltpu.VMEM((SUB * SUB, H * DK), BF16)],
        compiler_params=pltpu.CompilerParams(
            dimension_semantics=("parallel", "arbitrary"), vmem_limit_bytes=VMEM_LIMIT),
        name="gla_branch",
    )(p32, p16, p32, p32, w2p, b2, norm_g, tril, expand)
    return out


def _dsa_kernel(q_ref, iq_ref, iw_ref, k_ref, v_ref, ik_ref, bias_ref, farb_ref, upper_ref,
                o_ref, key_scr, m_scr, l_scr, acc_scr, cnt_scr, *, topk):
    T = DSA_QBLOCK
    H, HD = DSA_HEADS, DSA_HDIM
    qb = pl.program_id(1)
    n_kb = qb + 1
    lane = lax.broadcasted_iota(jnp.int32, (T, T), 1)
    rowi = lax.broadcasted_iota(jnp.int32, (T, T), 0)

    lo_mask = lane < DSA_IDX_DIM
    iq = [iq_ref[:, 0:LANES], iq_ref[:, LANES:2 * LANES]]
    iq_h = [jnp.where(lo_mask, iq[0], 0), jnp.where(lo_mask, 0, iq[0]),
            jnp.where(lo_mask, iq[1], 0), jnp.where(lo_mask, 0, iq[1])]
    iw_h = [jnp.broadcast_to(iw_ref[:, GLA_GATE_RANK + h:GLA_GATE_RANK + h + 1], (T, T))
            for h in range(DSA_IDX_HEADS)]

    def score_tile(kb):
        ik2 = ik_ref[pl.ds(pl.multiple_of(kb * T, T), T), C16_IK - C16_IQ:C16_IK - C16_IQ + LANES]
        score = jnp.zeros((T, T), F32)
        for h in range(DSA_IDX_HEADS):
            score = score + iw_h[h] * jnp.maximum(_dot_nt(iq_h[h], ik2), 0.0)
        bits = lax.bitcast_convert_type(score, jnp.int32)
        return jnp.where(bits >= 0, bits, bits ^ jnp.int32(0x7FFFFFFF))

    def score_body(kb, c):
        key_scr[kb] = score_tile(kb)
        return c

    lax.fori_loop(0, qb, score_body, 0)
    key_scr[qb] = jnp.where(lane <= rowi, score_tile(qb), jnp.int32(INT_MIN))

    kf = float(topk)

    def count(pred):
        def body(kb, acc):
            return acc + jnp.where(pred(key_scr[kb]), 1.0, 0.0)
        acc = lax.fori_loop(0, n_kb, body, jnp.zeros((T, T), F32))
        return jnp.sum(acc, axis=1, keepdims=True)

    def search():
        c0 = count(lambda k: k >= 0)
        base0 = jnp.where(c0 >= kf, jnp.int32(0), jnp.int32(INT_MIN))
        base0 = jnp.broadcast_to(base0, (T, T))

        def bit_body(i, base):
            cand = base | jnp.left_shift(jnp.int32(1), 30 - i)
            cnt = count(lambda k: k >= cand)
            return jnp.where(cnt >= kf, cand, base)

        return lax.fori_loop(0, 31, bit_body, base0)

    tau = lax.cond(n_kb * T > topk, search,
                   lambda: jnp.full((T, T), INT_MIN, jnp.int32))
    tau = jnp.maximum(tau, jnp.int32(INT_MIN + 1))
    need = kf - count(lambda k: k > tau)

    for h in range(H):
        m_scr[h] = jnp.full((T, 1), NEG_BIG, F32)
        l_scr[h] = jnp.zeros((T, 1), F32)
        acc_scr[h] = jnp.zeros((T, HD), F32)
    cnt_scr[...] = jnp.zeros_like(cnt_scr)

    def attend(kb, bias_of_head):
        key = key_scr[kb]
        eq = key == tau
        pref = _dot(jnp.where(eq, 1.0, 0.0).astype(BF16), upper_ref[...]) + cnt_scr[...]
        sel = (key > tau) | (eq & (pref <= need))
        cnt_scr[...] = cnt_scr[...] + jnp.sum(jnp.where(eq, 1.0, 0.0), axis=1, keepdims=True)
        r0 = pl.multiple_of(kb * T, T)
        for h in range(H):
            cols = slice(h * HD, (h + 1) * HD)
            s = _dot_nt(q_ref[:, cols], k_ref[pl.ds(r0, T), cols]) + bias_of_head(h)
            s = jnp.where(sel, s, NEG_BIG)
            m_old = m_scr[h]
            m_new = jnp.maximum(m_old, jnp.max(s, axis=1, keepdims=True))
            alpha = jnp.exp(m_old - m_new)
            p = jnp.exp(s - m_new)
            l_scr[h] = alpha * l_scr[h] + jnp.sum(p, axis=1, keepdims=True)
            acc_scr[h] = alpha * acc_scr[h] + _dot(p.astype(BF16), v_ref[pl.ds(r0, T), cols])
            m_scr[h] = m_new

    def far_body(kb, c):
        attend(kb, lambda h: farb_ref[h])
        return c

    lax.fori_loop(0, jnp.maximum(qb - 1, 0), far_body, 0)

    @pl.when(qb >= 1)
    def _():
        attend(qb - 1, lambda h: bias_ref[1, h])

    attend(qb, lambda h: bias_ref[0, h])

    for h in range(H):
        o_ref[:, h * HD:(h + 1) * HD] = (acc_scr[h] / l_scr[h]).astype(o_ref.dtype)


def _dsa(p16, p32, bias_tiles, far_bias, bsz, seq):
    T = DSA_QBLOCK
    tokens = p16.shape[0]
    nqb = seq // T
    topk = min(DSA_TOPK_MAX, seq // 4)
    upper = jnp.asarray(np.triu(np.ones((T, T), np.float32))).astype(BF16)
    qmap = lambda cb: (lambda b, i: (b * nqb + i, cb))
    smap = lambda cb: (lambda b, i: (b, cb))
    w = D_BRANCH
    return pl.pallas_call(
        functools.partial(_dsa_kernel, topk=topk),
        grid=(bsz, nqb),
        in_specs=[
            pl.BlockSpec((T, w), qmap(C16_Q // w)),
            pl.BlockSpec((T, w), qmap(C16_IQ // w)),
            pl.BlockSpec((T, w), qmap(C32_SMALL // w)),
            pl.BlockSpec((seq, w), smap(C16_K // w)),
            pl.BlockSpec((seq, w), smap(C16_V // w)),
            pl.BlockSpec((seq, w), smap(C16_IQ // w)),
            pl.BlockSpec(bias_tiles.shape, lambda b, i: (0, 0, 0, 0)),
            pl.BlockSpec(memory_space=pltpu.SMEM),
            pl.BlockSpec(upper.shape, lambda b, i: (0, 0)),
        ],
        out_specs=pl.BlockSpec((T, w), lambda b, i: (b * nqb + i, 0)),
        out_shape=jax.ShapeDtypeStruct((tokens, w), BF16),
        scratch_shapes=[
            pltpu.VMEM((nqb, T, T), jnp.int32),
            pltpu.VMEM((DSA_HEADS, T, 1), F32),
            pltpu.VMEM((DSA_HEADS, T, 1), F32),
            pltpu.VMEM((DSA_HEADS, T, DSA_HDIM), F32),
            pltpu.VMEM((T, 1), F32),
        ],
        compiler_params=pltpu.CompilerParams(
            dimension_semantics=("parallel", "arbitrary"), vmem_limit_bytes=VMEM_LIMIT),
        name="dsa_branch",
    )(p16, p16, p32, p16, p16, p16, bias_tiles, far_bias, upper)


def _merge_kernel(ya_ref, yb_ref, yc_ref, ga_ref, gb_ref, gc_ref, x_ref, mod_ref, wbr_ref, wout_ref,
                  gn2_ref, wr_hi_ref, wr_lo_ref, br_ref, x1_ref, h2_ref, lg_ref):
    merged = ga_ref[...] * _dot(ya_ref[...], wbr_ref[0])
    merged = merged + gb_ref[...] * _dot(yb_ref[...].astype(BF16), wbr_ref[1])
    merged = merged + gc_ref[...] * _dot(yc_ref[...], wbr_ref[2])
    mix = _dot(merged.astype(BF16), wout_ref[...])
    x1 = x_ref[...] + mod_ref[0, 2:3, :] * mix
    x1_ref[...] = x1
    y = x1 * lax.rsqrt(jnp.mean(x1 * x1, axis=-1, keepdims=True) + EPS) * gn2_ref[...]
    h2 = y * (1.0 + mod_ref[0, 4:5, :]) + mod_ref[0, 3:4, :]
    h2_ref[...] = h2.astype(BF16)
    lg_ref[...] = _dot3(h2, wr_hi_ref[...], wr_lo_ref[...]) + br_ref[...]


def _merge(ya, yb, yc, p32, x2d, mod_l, wbr, wout, gn2, wr_hi, wr_lo, br, seq, tm):
    tokens, d = x2d.shape
    tiles_per_seq = seq // tm
    row = lambda i: (i, 0)
    gate = lambda k: (lambda i: (i, C32_GATES // d + k))
    c2 = lambda i: (0, 0)
    return pl.pallas_call(
        _merge_kernel,
        grid=(tokens // tm,),
        in_specs=[
            pl.BlockSpec((tm, D_BRANCH), row), pl.BlockSpec((tm, D_BRANCH), row),
            pl.BlockSpec((tm, D_BRANCH), row),
            pl.BlockSpec((tm, d), gate(0)), pl.BlockSpec((tm, d), gate(1)), pl.BlockSpec((tm, d), gate(2)),
            pl.BlockSpec((tm, d), row),
            pl.BlockSpec((1, 6, d), lambda i: (i // tiles_per_seq, 0, 0)),
            pl.BlockSpec(wbr.shape, lambda i: (0, 0, 0)),
            pl.BlockSpec(wout.shape, c2),
            pl.BlockSpec(gn2.shape, c2),
            pl.BlockSpec(wr_hi.shape, c2), pl.BlockSpec(wr_lo.shape, c2), pl.BlockSpec(br.shape, c2),
        ],
        out_specs=[pl.BlockSpec((tm, d), row), pl.BlockSpec((tm, d), row), pl.BlockSpec((tm, LANES), row)],
        out_shape=[jax.ShapeDtypeStruct((tokens, d), F32), jax.ShapeDtypeStruct((tokens, d), BF16),
                   jax.ShapeDtypeStruct((tokens, LANES), F32)],
        compiler_params=pltpu.CompilerParams(
            dimension_semantics=("parallel",), vmem_limit_bytes=VMEM_LIMIT),
        name="merge_norm_router",
    )(ya, yb, yc, p32, p32, p32, x2d, mod_l, wbr, wout, gn2, wr_hi, wr_lo, br)


def _route(lg):
    lane = lax.broadcasted_iota(jnp.int32, lg.shape, 1)
    big = jnp.int32(10 ** 6)
    is_grp = lane < N_GROUPS
    gl = jnp.where(is_grp, lg, -jnp.inf)
    gmax = jnp.max(gl, axis=1, keepdims=True)
    gsum = jnp.sum(jnp.where(is_grp, jnp.exp(lg - gmax), 0.0), axis=1, keepdims=True)
    p_g = 1.0 / gsum
    g_idx = jnp.min(jnp.where(gl == gmax, lane, big), axis=1, keepdims=True)
    first = N_GROUPS + g_idx * EXPERTS_PER_GROUP
    in_grp = (lane >= first) & (lane < first + EXPERTS_PER_GROUP)
    e1 = jnp.where(in_grp, lg, -jnp.inf)
    v1 = jnp.max(e1, axis=1, keepdims=True)
    i1 = jnp.min(jnp.where(e1 == v1, lane, big), axis=1, keepdims=True)
    e2 = jnp.where(in_grp & (lane != i1), lg, -jnp.inf)
    v2 = jnp.max(e2, axis=1, keepdims=True)
    i2 = jnp.min(jnp.where(e2 == v2, lane, big), axis=1, keepdims=True)
    t = jnp.exp(v2 - v1)
    w1 = p_g * (1.0 / (1.0 + t))
    w2 = p_g * (t / (1.0 + t))
    return jnp.where(lane == i1, w1, jnp.where(lane == i2, w2, 0.0))


def _moe_kernel(h_ref, lg_ref, x1_ref, mod_ref, wg_ref, wu_ref, wd_ref, o_ref, comb_scr, acc_scr):
    e = pl.program_id(1)

    @pl.when(e == 0)
    def _():
        comb_scr[...] = _route(lg_ref[...])
        acc_scr[...] = jnp.zeros_like(acc_scr)

    lane = lax.broadcasted_iota(jnp.int32, comb_scr.shape, 1)
    cw = jnp.sum(jnp.where(lane == e + N_GROUPS, comb_scr[...], 0.0), axis=1, keepdims=True)
    h = h_ref[...]
    act = jax.nn.silu(_dot(h, wg_ref[...])) * _dot(h, wu_ref[...]) * cw
    acc_scr[...] += _dot(act.astype(BF16), wd_ref[...])

    @pl.when(e == pl.num_programs(1) - 1)
    def _():
        o_ref[...] = x1_ref[...] + mod_ref[0, 5:6, :] * acc_scr[...]


def _moe(h2, lg, x1, mod_l, wg, wu, wd, seq, tm):
    tokens, d = x1.shape
    tiles_per_seq = seq // tm
    row = lambda i, e: (i, 0)
    return pl.pallas_call(
        _moe_kernel,
        grid=(tokens // tm, N_EXPERTS),
        in_specs=[
            pl.BlockSpec((tm, d), row), pl.BlockSpec((tm, LANES), row), pl.BlockSpec((tm, d), row),
            pl.BlockSpec((1, 6, d), lambda i, e: (i // tiles_per_seq, 0, 0)),
            pl.BlockSpec((None, d, D_EXPERT), lambda i, e: (e, 0, 0)),
            pl.BlockSpec((None, d, D_EXPERT), lambda i, e: (e, 0, 0)),
            pl.BlockSpec((None, D_EXPERT, d), lambda i, e: (e, 0, 0)),
        ],
        out_specs=pl.BlockSpec((tm, d), row),
        out_shape=jax.ShapeDtypeStruct((tokens, d), F32),
        scratch_shapes=[pltpu.VMEM((tm, LANES), F32), pltpu.VMEM((tm, d), F32)],
        compiler_params=pltpu.CompilerParams(
            dimension_semantics=("parallel", "arbitrary"), vmem_limit_bytes=VMEM_LIMIT),
        name="hier_moe",
    )(h2, lg, x1, mod_l, wg, wu, wd)


def _row_tile(seq, want):
    t = min(want, seq)
    assert seq % t == 0
    return t


def kernel(x, c, w_mod, b_mod, g_norm1, g_norm2, w_in, gmlp_ln_g, gmlp_ln_b, gmlp_w_s, gmlp_b_s,
           gla_w_gate2, gla_b_gate, gla_norm_g, dsa_qnorm_g, dsa_knorm_g, rel_bias, w_branch,
           b_branch_gate, w_out, w_group, b_group, w_router, b_router, w_exp_gate, w_exp_up,
           w_exp_down):
    bsz, seq, d = x.shape
    depth = w_mod.shape[0]
    assert d == D_MODEL and seq % DSA_QBLOCK == 0 and seq % GLA_CHUNK == 0
    tokens = bsz * seq
    tm = _row_tile(seq, 512)

    mods = _modulation(c, w_mod, b_mod)
    bias_tiles = _bias_tiles(rel_bias)
    far_bias = rel_bias[N_BUCKETS - 1]
    w32, w16 = _prep_w_in(w_in)

    causal = np.tril(np.ones((GMLP_CHUNK, GMLP_CHUNK), bool))
    w_tril = jnp.where(causal[None, None], gmlp_w_s, 0.0).astype(BF16)
    gmlp_bias = jnp.repeat(jnp.swapaxes(gmlp_b_s, 1, 2), LANES, axis=-1)
    w2p = jnp.zeros((depth, LANES, GLA_HEADS * GLA_DK), F32).at[:, :GLA_GATE_RANK].set(gla_w_gate2)
    w2p = w2p.astype(BF16)
    wr = jnp.zeros((depth, d, LANES), F32)
    wr = wr.at[:, :, :N_GROUPS].set(w_group).at[:, :, N_GROUPS:N_GROUPS + N_EXPERTS].set(w_router)
    wr_hi = wr.astype(BF16)
    wr_lo = (wr - wr_hi.astype(F32)).astype(BF16)
    br = jnp.zeros((depth, 1, LANES), F32)
    br = br.at[:, 0, :N_GROUPS].set(b_group).at[:, 0, N_GROUPS:N_GROUPS + N_EXPERTS].set(b_router)
    wbr = w_branch.astype(BF16)
    wout = w_out.astype(BF16)
    wg, wu, wd = w_exp_gate.astype(BF16), w_exp_up.astype(BF16), w_exp_down.astype(BF16)

    x2d = x.reshape(tokens, d)
    for l in range(depth):
        mod_l = mods[l]
        aux32 = [gmlp_ln_g[l][None], gmlp_ln_b[l][None], b_branch_gate[l].reshape(1, -1)]
        aux16 = [dsa_qnorm_g[l][None], dsa_knorm_g[l][None]]
        p32 = _norm_proj(x2d, mod_l, g_norm1[l][None], w32[l], aux32, EPILOGUES_32, F32, seq, tm)
        p16 = _norm_proj(x2d, mod_l, g_norm1[l][None], w16[l], aux16, EPILOGUES_16, BF16, seq, tm)
        ya = _gmlp(p32, w_tril[l], gmlp_bias[l], tm)
        yb = _gla(p32, p16, w2p[l], gla_b_gate[l][None], gla_norm_g[l][None], bsz, seq, tm)
        yc = _dsa(p16, p32, bias_tiles, far_bias, bsz, seq)
        x1, h2, lg = _merge(ya, yb, yc, p32, x2d, mod_l, wbr[l], wout[l], g_norm2[l][None],
                            wr_hi[l], wr_lo[l], br[l], seq, tm)
        x2d = _moe(h2, lg, x1, mod_l, wg[l], wu[l], wd[l], seq, tm)
    return x2d.reshape(bsz, seq, d)
```

```python
import functools
import math

import numpy as np
import jax
import jax.numpy as jnp
from jax import lax
from jax.experimental import pallas as pl
from jax.experimental.pallas import tpu as pltpu

F32 = jnp.float32
BF16 = jnp.bfloat16

D_MODEL = 1024
D_BRANCH = 512
EPS = 1e-6
GMLP_CHUNK = 128
GMLP_GROUPS = 4
GLA_HEADS = 4
GLA_DK = 64
GLA_DV = 128
GLA_GATE_RANK = 16
GLA_GATE_TAU = 16.0
GLA_CHUNK = 64
GLA_SUB = 16
DSA_HEADS = 4
DSA_HDIM = 128
DSA_IDX_HEADS = 4
DSA_IDX_DIM = 64
DSA_QBLOCK = 128
DSA_KTILE = 512
DSA_TOPK_MAX = 256
N_BUCKETS = 32
MAX_DISTANCE = 128
N_GROUPS = 4
EXPERTS_PER_GROUP = 4
N_EXPERTS = 16
D_EXPERT = 256

LANES = 128
COL_TILE = 512
VMEM_LIMIT = 56 * 1024 * 1024
INT_MIN = -(2 ** 31)
NEG_BIG = -1e30

C32_U, C32_V, C32_GQ, C32_GK, C32_R, C32_GATES, C32_SMALL = 0, 512, 1024, 1280, 1536, 2048, 5120
C32_GA, C32_IW = C32_SMALL, C32_SMALL + GLA_GATE_RANK
N32 = 5632
C16_Q, C16_K, C16_V, C16_GV, C16_IQ, C16_IK = 0, 512, 1024, 1536, 2048, 2304
N16 = 2560


def _dot(a, b):
    return jnp.dot(a, b, preferred_element_type=F32)


def _dot_nt(a, b):
    return lax.dot_general(a, b, (((1,), (1,)), ((), ())), preferred_element_type=F32)


def _dot_tn(a, b):
    return lax.dot_general(a, b, (((0,), (0,)), ((), ())), preferred_element_type=F32)


def _split2(a):
    hi = a.astype(BF16)
    lo = (a - hi.astype(F32)).astype(BF16)
    return hi, lo


def _dot3(a, w_hi, w_lo):
    a_hi, a_lo = _split2(a)
    return _dot(a_hi, w_hi) + (_dot(a_lo, w_hi) + _dot(a_hi, w_lo))


def _head_rms(y, g, scale):
    outs = []
    for h in range(y.shape[1] // LANES):
        yh = y[:, h * LANES:(h + 1) * LANES]
        ms = jnp.mean(yh * yh, axis=-1, keepdims=True)
        o = yh * lax.rsqrt(ms + EPS) * g
        if scale != 1.0:
            o = o * scale
        outs.append(o)
    return jnp.concatenate(outs, axis=1)


def _mod_kernel(c_ref, w_ref, b_ref, o_ref):
    a = jax.nn.silu(c_ref[...])
    w_hi, w_lo = _split2(w_ref[...])
    o_ref[...] = _dot3(a, w_hi, w_lo) + b_ref[...]


def _modulation(c, w_mod, b_mod):
    L, d, n = w_mod.shape
    bsz = c.shape[0]
    rows = 8 * pl.cdiv(bsz, 8)
    c_pad = jnp.zeros((rows, d), F32).at[:bsz].set(c)
    tn = 1536
    out = pl.pallas_call(
        _mod_kernel,
        grid=(L, n // tn),
        in_specs=[
            pl.BlockSpec((rows, d), lambda l, j: (0, 0)),
            pl.BlockSpec((None, d, tn), lambda l, j: (l, 0, j)),
            pl.BlockSpec((None, 1, tn), lambda l, j: (l, 0, j)),
        ],
        out_specs=pl.BlockSpec((None, rows, tn), lambda l, j: (l, 0, j)),
        out_shape=jax.ShapeDtypeStruct((L, rows, n), F32),
        compiler_params=pltpu.CompilerParams(
            dimension_semantics=("arbitrary", "arbitrary"), vmem_limit_bytes=VMEM_LIMIT),
        name="adaln_modulation",
    )(c_pad, w_mod, b_mod.reshape(L, 1, n))
    return out[:, :bsz].reshape(L, bsz, 6, d)


def _t5_bucket_table():
    n = np.arange(2 * DSA_QBLOCK)
    max_exact = N_BUCKETS // 2
    large = max_exact + (
        np.log(np.maximum(n, max_exact).astype(np.float32) / max_exact)
        / math.log(MAX_DISTANCE / max_exact) * (N_BUCKETS - max_exact)).astype(np.int32)
    large = np.minimum(large, N_BUCKETS - 1)
    return np.where(n < max_exact, n, large).astype(np.int32)


def _bias_kernel(rb_ref, bucket_ref, o_ref):
    for t in range(bucket_ref.shape[0]):
        bucket = bucket_ref[t]
        for h in range(DSA_HEADS):
            acc = jnp.zeros(bucket.shape, F32)
            for b in range(N_BUCKETS):
                acc = jnp.where(bucket == b, rb_ref[b, h], acc)
            o_ref[t, h] = acc


def _bias_tiles(rel_bias):
    table = _t5_bucket_table()
    assert (table[MAX_DISTANCE:] == N_BUCKETS - 1).all()
    t = np.arange(DSA_QBLOCK)[None, :]
    s = np.arange(DSA_QBLOCK)[:, None]
    diag = table[np.maximum(t - s, 0)]
    near = table[DSA_QBLOCK + t - s]
    far = np.full_like(diag, N_BUCKETS - 1)
    buckets = jnp.asarray(np.stack([diag, near, far]).astype(np.int32))
    return pl.pallas_call(
        _bias_kernel,
        in_specs=[pl.BlockSpec(memory_space=pltpu.SMEM), pl.BlockSpec(memory_space=pltpu.VMEM)],
        out_specs=pl.BlockSpec(memory_space=pltpu.VMEM),
        out_shape=jax.ShapeDtypeStruct((3, DSA_HEADS, DSA_QBLOCK, DSA_QBLOCK), F32),
        name="t5_bias_tiles",
    )(rel_bias, buckets)


def _proj_kernel(x_ref, mod_ref, gn_ref, w_ref, *rest, epilogues, n_aux):
    aux = rest[:n_aux]
    o_ref = rest[n_aux]
    h_scr = rest[n_aux + 1]
    j = pl.program_id(1)

    @pl.when(j == 0)
    def _():
        x = x_ref[...]
        y = x * lax.rsqrt(jnp.mean(x * x, axis=-1, keepdims=True) + EPS) * gn_ref[...]
        h = y * (1.0 + mod_ref[0, 1:2, :]) + mod_ref[0, 0:1, :]
        h_scr[...] = h.astype(BF16)

    for jj, epi in enumerate(epilogues):
        @pl.when(j == jj)
        def _(epi=epi):
            y = _dot(h_scr[...], w_ref[...])
            o_ref[...] = epi(y, aux).astype(o_ref.dtype)


def _norm_proj(x2d, mod_l, gn, w, aux, epilogues, out_dtype, seq, tm):
    tokens, d = x2d.shape
    n = w.shape[1]
    n_tiles = n // COL_TILE
    assert n_tiles == len(epilogues)
    tiles_per_seq = seq // tm
    aux_specs = [pl.BlockSpec(a.shape, lambda i, j: (0, 0)) for a in aux]
    return pl.pallas_call(
        functools.partial(_proj_kernel, epilogues=epilogues, n_aux=len(aux)),
        grid=(tokens // tm, n_tiles),
        in_specs=[
            pl.BlockSpec((tm, d), lambda i, j: (i, 0)),
            pl.BlockSpec((1, 6, d), lambda i, j: (i // tiles_per_seq, 0, 0)),
            pl.BlockSpec((1, d), lambda i, j: (0, 0)),
            pl.BlockSpec((d, COL_TILE), lambda i, j: (0, j)),
        ] + aux_specs,
        out_specs=pl.BlockSpec((tm, COL_TILE), lambda i, j: (i, j)),
        out_shape=jax.ShapeDtypeStruct((tokens, n), out_dtype),
        scratch_shapes=[pltpu.VMEM((tm, d), BF16)],
        compiler_params=pltpu.CompilerParams(
            dimension_semantics=("parallel", "arbitrary"), vmem_limit_bytes=VMEM_LIMIT),
        name="norm_proj_" + jnp.dtype(out_dtype).name,
    )(x2d, mod_l, gn, w, *aux)


def _epi_raw(y, aux):
    return y


def _epi_gelu(y, aux):
    return jax.nn.gelu(y)


def _epi_gelu_ln(y, aux):
    v = jax.nn.gelu(y)
    mu = jnp.mean(v, axis=-1, keepdims=True)
    var = jnp.mean(jnp.square(v - mu), axis=-1, keepdims=True)
    return (v - mu) * lax.rsqrt(var + EPS) * aux[0][...] + aux[1][...]


def _epi_silu(y, aux):
    return jax.nn.silu(y)


def _epi_gate(k):
    def epi(y, aux):
        return jax.nn.sigmoid(y + aux[2][:, k * COL_TILE:(k + 1) * COL_TILE])
    return epi


def _epi_qnorm(y, aux):
    return _head_rms(y, aux[0][...], DSA_HDIM ** -0.5)


def _epi_knorm(y, aux):
    return _head_rms(y, aux[1][...], 1.0)


EPILOGUES_32 = ([_epi_gelu, _epi_gelu_ln, _epi_raw, _epi_silu]
                + [_epi_gate(k) for k in range(6)] + [_epi_raw])
EPILOGUES_16 = [_epi_qnorm, _epi_knorm, _epi_raw, _epi_raw, _epi_raw]


def _prep_w_in(w_in):
    sizes = (512, 512, 256, 256, 512, 512, 16, 512, 512, 512, 256, 64, 4, 3072)
    offs = np.concatenate([[0], np.cumsum(sizes)])
    seg = lambda k: w_in[:, :, offs[k]:offs[k + 1]]
    (a_u, a_v, g_q, g_k, g_v, g_r, g_a, d_q, d_k, d_v, d_iq, d_ik, d_iw, gates) = [seg(k) for k in range(14)]
    L, d, _ = w_in.shape
    zeros = lambda n: jnp.zeros((L, d, n), w_in.dtype)
    w32 = jnp.concatenate([a_u, a_v, g_q, g_k, g_r, gates, g_a, d_iw,
                           zeros(N32 - C32_SMALL - 20)], axis=-1).astype(BF16)
    w16 = jnp.concatenate([d_q, d_k, d_v, g_v, d_iq, d_ik, d_ik, zeros(N16 - C16_IK - 128)],
                          axis=-1).astype(BF16)
    return w32, w16


def _gmlp_kernel(u_ref, v_ref, w_ref, b_ref, o_ref):
    tm = u_ref.shape[0]
    for c in range(tm // GMLP_CHUNK):
        rows = slice(c * GMLP_CHUNK, (c + 1) * GMLP_CHUNK)
        for g in range(GMLP_GROUPS):
            cols = slice(g * LANES, (g + 1) * LANES)
            mixed = _dot(w_ref[g], v_ref[rows, cols].astype(BF16)) + b_ref[:, cols]
            o_ref[rows, cols] = (u_ref[rows, cols] * mixed).astype(o_ref.dtype)


def _gmlp(p32, w_tril, bias_full, tm):
    tokens = p32.shape[0]
    return pl.pallas_call(
        _gmlp_kernel,
        grid=(tokens // tm,),
        in_specs=[
            pl.BlockSpec((tm, D_BRANCH), lambda i: (i, C32_U // D_BRANCH)),
            pl.BlockSpec((tm, D_BRANCH), lambda i: (i, C32_V // D_BRANCH)),
            pl.BlockSpec(w_tril.shape, lambda i: (0, 0, 0)),
            pl.BlockSpec(bias_full.shape, lambda i: (0, 0)),
        ],
        out_specs=pl.BlockSpec((tm, D_BRANCH), lambda i: (i, 0)),
        out_shape=jax.ShapeDtypeStruct((tokens, D_BRANCH), BF16),
        compiler_params=pltpu.CompilerParams(
            dimension_semantics=("parallel",), vmem_limit_bytes=VMEM_LIMIT),
        name="gmlp_branch",
    )(p32, p32, w_tril, bias_full)


def _gla_kernel(qk_ref, v_ref, r_ref, ga_ref, w2_ref, b2_ref, ng_ref, tril_ref, exp_ref,
                o_ref, st_ref, d_scr):
    C, SUB, H, DK, DV = GLA_CHUNK, GLA_SUB, GLA_HEADS, GLA_DK, GLA_DV
    HK = H * DK

    @pl.when(pl.program_id(1) == 0)
    def _():
        st_ref[...] = jnp.zeros_like(st_ref)

    lane = lax.broadcasted_iota(jnp.int32, (1, HK), 1)
    head_mask = [(lane >= h * DK) & (lane < (h + 1) * DK) for h in range(H)]
    row = lax.broadcasted_iota(jnp.int32, (C, C), 0)
    col = lax.broadcasted_iota(jnp.int32, (C, C), 1)
    sub_shift = SUB.bit_length() - 1
    blk_lower = (row >> sub_shift) > (col >> sub_shift)
    sub_t = lax.broadcasted_iota(jnp.int32, (SUB, 1), 0)

    def chunk(ci, carry):
        r0 = pl.multiple_of(ci * C, C)
        rows = pl.ds(r0, C)
        q = qk_ref[rows, 0:HK] * (DK ** -0.5)
        k = qk_ref[rows, HK:2 * HK]
        v = v_ref[rows, :]
        graw = _dot(ga_ref[rows, 0:LANES].astype(BF16), w2_ref[...]) + b2_ref[...]
        g = jax.nn.log_sigmoid(graw) / GLA_GATE_TAU
        g_hi = g.astype(BF16)
        g_r1 = g - g_hi.astype(F32)
        g_mid = g_r1.astype(BF16)
        g_lo = (g_r1 - g_mid.astype(F32)).astype(BF16)
        tril = tril_ref[...]
        b = _dot(tril, g_hi) + (_dot(tril, g_mid) + _dot(tril, g_lo))
        b_last = b[C - 1:C, :]
        st = st_ref[...]

        q_in = q * jnp.exp(b)
        k_dec = (k * jnp.exp(b_last - b)).astype(BF16)

        a_off = [jnp.zeros((C, C), F32) for _ in range(H)]
        for j in range(C // SUB - 1):
            bj = b[(j + 1) * SUB - 1:(j + 1) * SUB, :]
            qj = q * jnp.exp(jnp.minimum(b - bj, 0.0))
            in_blk = (lax.broadcasted_iota(jnp.int32, (C, 1), 0) >> sub_shift) == j
            kj = jnp.where(in_blk, k * jnp.exp(jnp.minimum(bj - b, 0.0)), 0.0).astype(BF16)
            for h in range(H):
                a_off[h] = a_off[h] + _dot_nt(jnp.where(head_mask[h], qj, 0.0).astype(BF16), kj)

        for i in range(C // SUB):
            rs = slice(i * SUB, (i + 1) * SUB)
            qi, bi = q[rs, :], b[rs, :]
            for s in range(SUB):
                ks = k[i * SUB + s:i * SUB + s + 1, :]
                bs = b[i * SUB + s:i * SUB + s + 1, :]
                dterm = qi * ks * jnp.exp(jnp.minimum(bi - bs, 0.0))
                dterm = jnp.where(sub_t >= s, dterm, 0.0)
                d_scr[s * SUB:(s + 1) * SUB, :] = dterm.astype(BF16)
            gsum = _dot(d_scr[...], exp_ref[...])
            od = jnp.zeros((SUB, H * DV), F32)
            for s in range(SUB):
                vs = v[i * SUB + s:i * SUB + s + 1, :].astype(F32)
                od = od + gsum[s * SUB:(s + 1) * SUB, :] * vs
            o_ref[pl.ds(r0 + i * SUB, SUB), :] = od

        outs = []
        for h in range(H):
            vh = v[:, h * DV:(h + 1) * DV]
            o_inter = _dot_nt(jnp.where(head_mask[h], q_in, 0.0).astype(BF16), st.astype(BF16))
            a_h = jnp.where(blk_lower, a_off[h], 0.0).astype(BF16)
            outs.append(o_inter + _dot(a_h, vh))
        o = o_ref[rows, :] + jnp.concatenate(outs, axis=1)

        upd = _dot_tn(v, k_dec)
        new_st = st * jnp.exp(b_last)
        for h in range(H):
            new_st = new_st + jnp.where(head_mask[h], upd[h * DV:(h + 1) * DV, :], 0.0)
        st_ref[...] = new_st

        y = _head_rms(o, ng_ref[...], 1.0) * r_ref[rows, :]
        o_ref[rows, :] = y
        return carry

    lax.fori_loop(0, qk_ref.shape[0] // C, chunk, 0)


def _gla(p32, p16, w2p, b2, norm_g, bsz, seq, ts):
    C, SUB, H, DK, DV = GLA_CHUNK, GLA_SUB, GLA_HEADS, GLA_DK, GLA_DV
    tokens = p32.shape[0]
    blocks_per_seq = seq // ts
    tril = jnp.asarray(np.tril(np.ones((C, C), np.float32))).astype(BF16)
    expand = np.zeros((H * DK, H * DV), np.float32)
    for h in range(H):
        expand[h * DK:(h + 1) * DK, h * DV:(h + 1) * DV] = 1.0
    expand = jnp.asarray(expand).astype(BF16)
    row_map = lambda cb: (lambda b, i: (b * blocks_per_seq + i, cb))
    const2 = lambda b, i: (0, 0)
    out = pl.pallas_call(
        _gla_kernel,
        grid=(bsz, blocks_per_seq),
        in_specs=[
            pl.BlockSpec((ts, 2 * H * DK), row_map(C32_GQ // (2 * H * DK))),
            pl.BlockSpec((ts, H * DV), row_map(C16_GV // (H * DV))),
            pl.BlockSpec((ts, H * DV), row_map(C32_R // (H * DV))),
            pl.BlockSpec((ts, COL_TILE), row_map(C32_SMALL // COL_TILE)),
            pl.BlockSpec(w2p.shape, const2),
            pl.BlockSpec(b2.shape, const2),
            pl.BlockSpec(norm_g.shape, const2),
            pl.BlockSpec(tril.shape, const2),
            pl.BlockSpec(expand.shape, const2),
        ],
        out_specs=pl.BlockSpec((ts, H * DV), lambda b, i: (b * blocks_per_seq + i, 0)),
        out_shape=jax.ShapeDtypeStruct((tokens, H * DV), F32),
        scratch_shapes=[pltpu.VMEM((DV, H * DK), F32), pltpu.VMEM((SUB * SUB, H * DK), BF16)],
        compiler_params=pltpu.CompilerParams(
            dimension_semantics=("parallel", "arbitrary"), vmem_limit_bytes=VMEM_LIMIT),
        name="gla_branch",
    )(p32, p16, p32, p32, w2p, b2, norm_g, tril, expand)
    return out


def _dsa_kernel(q_ref, iq_ref, iwt_ref, k_ref, vt_ref, ik_ref, bias_ref, ltri_ref,
                o_ref, key_scr, s_scr, acc_scr, cnt_scr, *, topk):
    T, TK = DSA_QBLOCK, DSA_KTILE
    SUBS = TK // T
    H, HD = DSA_HEADS, DSA_HDIM
    qb = pl.program_id(1)
    kt_last = qb // SUBS
    n_kt = kt_last + 1
    key_minus_query = (lax.broadcasted_iota(jnp.int32, (TK, T), 0)
                       - lax.broadcasted_iota(jnp.int32, (TK, T), 1))

    def fold(x, op):
        return op(x.reshape(x.shape[0] // 8, 8, T), axis=0)

    def rows_to_one(x, op):
        return op(x, axis=0, keepdims=True)

    def key_rows(kt):
        return pl.ds(pl.multiple_of(kt * TK, TK), TK)

    lo_mask = lax.broadcasted_iota(jnp.int32, (T, LANES), 1) < DSA_IDX_DIM
    iq = [iq_ref[:, 0:LANES], iq_ref[:, LANES:2 * LANES]]
    iq_h = [jnp.where(lo_mask, iq[0], 0), jnp.where(lo_mask, 0, iq[0]),
            jnp.where(lo_mask, iq[1], 0), jnp.where(lo_mask, 0, iq[1])]
    iw_h = [iwt_ref[h:h + 1, :] for h in range(DSA_IDX_HEADS)]

    def score_tile(kt):
        ik2 = ik_ref[key_rows(kt), :]
        score = jnp.zeros((TK, T), F32)
        for h in range(DSA_IDX_HEADS):
            score = score + iw_h[h] * jnp.maximum(_dot_nt(ik2, iq_h[h]), 0.0)
        bits = lax.bitcast_convert_type(score, jnp.int32)
        return jnp.where(bits >= 0, bits, bits ^ jnp.int32(0x7FFFFFFF))

    def score_body(kt, c):
        key_scr[kt] = score_tile(kt)
        return c

    lax.fori_loop(0, kt_last, score_body, 0)
    admissible = key_minus_query <= qb * T - kt_last * TK
    key_scr[kt_last] = jnp.where(admissible, score_tile(kt_last), jnp.int32(INT_MIN))

    def count(pred):
        def body(kt, acc):
            return acc + fold(jnp.where(pred(key_scr[kt]), 1, 0), jnp.sum)
        acc = lax.fori_loop(0, n_kt, body, jnp.zeros((8, T), jnp.int32))
        return rows_to_one(acc, jnp.sum)

    def search():
        c0 = count(lambda k: k >= 0)
        pos = c0 >= topk
        base0 = jnp.where(pos, jnp.int32(0), jnp.int32(INT_MIN))
        n_ge0 = jnp.where(pos, c0, jnp.int32(2 ** 30))

        def bit_body(i, carry):
            base, n_ge = carry
            cand = base | jnp.left_shift(jnp.int32(1), 30 - i)
            cnt = count(lambda k: k >= cand)
            ok = cnt >= topk
            return jnp.where(ok, cand, base), jnp.where(ok, cnt, n_ge)

        return lax.fori_loop(0, 31, bit_body, (base0, n_ge0))

    tau, n_ge = lax.cond((qb + 1) * T > topk, search,
                         lambda: (jnp.full((1, T), INT_MIN, jnp.int32), jnp.zeros((1, T), jnp.int32)))
    tau = jnp.maximum(tau, jnp.int32(INT_MIN + 1))

    @pl.when(jnp.max(n_ge) > topk)
    def _():
        need = (topk - count(lambda k: k > tau)).astype(F32)
        cnt_scr[...] = jnp.zeros_like(cnt_scr)

        def tie_body(kt, c):
            key = key_scr[kt]
            eq = key == tau
            eq_f = jnp.where(eq, 1.0, 0.0)
            pref = _dot(ltri_ref[...], eq_f.astype(BF16)) + cnt_scr[...]
            key_scr[kt] = jnp.where(eq & (pref > need), tau - 1, key)
            cnt_scr[...] = cnt_scr[...] + rows_to_one(eq_f, jnp.sum)
            return c

        lax.fori_loop(0, n_kt, tie_body, 0)

    q_h = [q_ref[:, h * HD:(h + 1) * HD] for h in range(H)]

    def logits_body(kt, m_run):
        sel = key_scr[kt] >= tau
        kinds = [jnp.clip(qb - (kt * SUBS + j), 0, 2) for j in range(SUBS)]
        new_m = []
        for h in range(H):
            bias = jnp.concatenate([bias_ref[kinds[j], h] for j in range(SUBS)], axis=0)
            s = _dot_nt(k_ref[key_rows(kt), h * HD:(h + 1) * HD], q_h[h]) + bias
            s = jnp.where(sel, s, NEG_BIG)
            s_scr[h, kt] = s
            new_m.append(jnp.maximum(m_run[h], fold(s, jnp.max)))
        return tuple(new_m)

    m_run = lax.fori_loop(0, n_kt, logits_body,
                          tuple(jnp.full((8, T), NEG_BIG, F32) for _ in range(H)))
    m_h = [rows_to_one(m, jnp.max) for m in m_run]

    for h in range(H):
        acc_scr[h] = jnp.zeros((HD, T), F32)

    def pv_body(kt, l_run):
        new_l = []
        for h in range(H):
            p = jnp.exp(s_scr[h, kt] - m_h[h])
            new_l.append(l_run[h] + fold(p, jnp.sum))
            acc_scr[h] = acc_scr[h] + _dot(vt_ref[kt, h * HD:(h + 1) * HD, :], p.astype(BF16))
        return tuple(new_l)

    l_run = lax.fori_loop(0, n_kt, pv_body, tuple(jnp.zeros((8, T), F32) for _ in range(H)))

    for h in range(H):
        out_t = acc_scr[h] / rows_to_one(l_run[h], jnp.sum)
        o_ref[:, h * HD:(h + 1) * HD] = out_t.T.astype(o_ref.dtype)


def _dsa(p16, p32, bias_tiles, bsz, seq):
    T = DSA_QBLOCK
    tokens = p16.shape[0]
    nqb = seq // T
    topk = min(DSA_TOPK_MAX, seq // 4)
    TK = min(DSA_KTILE, seq)
    assert TK == DSA_KTILE and seq % TK == 0
    nkt = seq // TK
    ltri = jnp.asarray(np.tril(np.ones((TK, TK), np.float32))).astype(BF16)
    w = D_BRANCH
    v_t = jnp.swapaxes(p16[:, C16_V:C16_V + w].reshape(bsz, nkt, TK, w), 2, 3)
    iw_t = jnp.zeros((8, tokens), F32).at[:DSA_IDX_HEADS].set(
        p32[:, C32_IW:C32_IW + DSA_IDX_HEADS].T)
    qmap = lambda cb: (lambda b, i: (b * nqb + i, cb))
    return pl.pallas_call(
        functools.partial(_dsa_kernel, topk=topk),
        grid=(bsz, nqb),
        in_specs=[
            pl.BlockSpec((T, w), qmap(C16_Q // w)),
            pl.BlockSpec((T, 2 * LANES), qmap(C16_IQ // (2 * LANES))),
            pl.BlockSpec((8, T), lambda b, i: (0, b * nqb + i)),
            pl.BlockSpec((seq, w), lambda b, i: (b, C16_K // w)),
            pl.BlockSpec((None, nkt, w, TK), lambda b, i: (b, 0, 0, 0)),
            pl.BlockSpec((seq, LANES), lambda b, i: (b, C16_IK // LANES)),
            pl.BlockSpec(bias_tiles.shape, lambda b, i: (0, 0, 0, 0)),
            pl.BlockSpec(ltri.shape, lambda b, i: (0, 0)),
        ],
        out_specs=pl.BlockSpec((T, w), lambda b, i: (b * nqb + i, 0)),
        out_shape=jax.ShapeDtypeStruct((tokens, w), BF16),
        scratch_shapes=[
            pltpu.VMEM((nkt, TK, T), jnp.int32),
            pltpu.VMEM((DSA_HEADS, nkt, TK, T), F32),
            pltpu.VMEM((DSA_HEADS, DSA_HDIM, T), F32),
            pltpu.VMEM((1, T), F32),
        ],
        compiler_params=pltpu.CompilerParams(
            dimension_semantics=("parallel", "arbitrary"), vmem_limit_bytes=VMEM_LIMIT),
        name="dsa_branch",
    )(p16, p16, iw_t, p16, v_t, p16, bias_tiles, ltri)


def _merge_kernel(ya_ref, yb_ref, yc_ref, ga_ref, gb_ref, gc_ref, x_ref, mod_ref, wbr_ref, wout_ref,
                  gn2_ref, wr_hi_ref, wr_lo_ref, br_ref, x1_ref, h2_ref, lg_ref):
    merged = ga_ref[...] * _dot(ya_ref[...], wbr_ref[0])
    merged = merged + gb_ref[...] * _dot(yb_ref[...].astype(BF16), wbr_ref[1])
    merged = merged + gc_ref[...] * _dot(yc_ref[...], wbr_ref[2])
    mix = _dot(merged.astype(BF16), wout_ref[...])
    x1 = x_ref[...] + mod_ref[0, 2:3, :] * mix
    x1_ref[...] = x1
    y = x1 * lax.rsqrt(jnp.mean(x1 * x1, axis=-1, keepdims=True) + EPS) * gn2_ref[...]
    h2 = y * (1.0 + mod_ref[0, 4:5, :]) + mod_ref[0, 3:4, :]
    h2_ref[...] = h2.astype(BF16)
    lg_ref[...] = _dot3(h2, wr_hi_ref[...], wr_lo_ref[...]) + br_ref[...]


def _merge(ya, yb, yc, p32, x2d, mod_l, wbr, wout, gn2, wr_hi, wr_lo, br, seq, tm):
    tokens, d = x2d.shape
    tiles_per_seq = seq // tm
    row = lambda i: (i, 0)
    gate = lambda k: (lambda i: (i, C32_GATES // d + k))
    c2 = lambda i: (0, 0)
    return pl.pallas_call(
        _merge_kernel,
        grid=(tokens // tm,),
        in_specs=[
            pl.BlockSpec((tm, D_BRANCH), row), pl.BlockSpec((tm, D_BRANCH), row),
            pl.BlockSpec((tm, D_BRANCH), row),
            pl.BlockSpec((tm, d), gate(0)), pl.BlockSpec((tm, d), gate(1)), pl.BlockSpec((tm, d), gate(2)),
            pl.BlockSpec((tm, d), row),
            pl.BlockSpec((1, 6, d), lambda i: (i // tiles_per_seq, 0, 0)),
            pl.BlockSpec(wbr.shape, lambda i: (0, 0, 0)),
            pl.BlockSpec(wout.shape, c2),
            pl.BlockSpec(gn2.shape, c2),
            pl.BlockSpec(wr_hi.shape, c2), pl.BlockSpec(wr_lo.shape, c2), pl.BlockSpec(br.shape, c2),
        ],
        out_specs=[pl.BlockSpec((tm, d), row), pl.BlockSpec((tm, d), row), pl.BlockSpec((tm, LANES), row)],
        out_shape=[jax.ShapeDtypeStruct((tokens, d), F32), jax.ShapeDtypeStruct((tokens, d), BF16),
                   jax.ShapeDtypeStruct((tokens, LANES), F32)],
        compiler_params=pltpu.CompilerParams(
            dimension_semantics=("parallel",), vmem_limit_bytes=VMEM_LIMIT),
        name="merge_norm_router",
    )(ya, yb, yc, p32, p32, p32, x2d, mod_l, wbr, wout, gn2, wr_hi, wr_lo, br)


def _route(lg):
    lane = lax.broadcasted_iota(jnp.int32, lg.shape, 1)
    big = jnp.int32(10 ** 6)
    is_grp = lane < N_GROUPS
    gl = jnp.where(is_grp, lg, -jnp.inf)
    gmax = jnp.max(gl, axis=1, keepdims=True)
    gsum = jnp.sum(jnp.where(is_grp, jnp.exp(lg - gmax), 0.0), axis=1, keepdims=True)
    p_g = 1.0 / gsum
    g_idx = jnp.min(jnp.where(gl == gmax, lane, big), axis=1, keepdims=True)
    first = N_GROUPS + g_idx * EXPERTS_PER_GROUP
    in_grp = (lane >= first) & (lane < first + EXPERTS_PER_GROUP)
    e1 = jnp.where(in_grp, lg, -jnp.inf)
    v1 = jnp.max(e1, axis=1, keepdims=True)
    i1 = jnp.min(jnp.where(e1 == v1, lane, big), axis=1, keepdims=True)
    e2 = jnp.where(in_grp & (lane != i1), lg, -jnp.inf)
    v2 = jnp.max(e2, axis=1, keepdims=True)
    i2 = jnp.min(jnp.where(e2 == v2, lane, big), axis=1, keepdims=True)
    t = jnp.exp(v2 - v1)
    w1 = p_g * (1.0 / (1.0 + t))
    w2 = p_g * (t / (1.0 + t))
    return jnp.where(lane == i1, w1, jnp.where(lane == i2, w2, 0.0))


def _moe_kernel(h_ref, lg_ref, x1_ref, mod_ref, wg_ref, wu_ref, wd_ref, o_ref, comb_scr, acc_scr):
    e = pl.program_id(1)

    @pl.when(e == 0)
    def _():
        comb_scr[...] = _route(lg_ref[...])
        acc_scr[...] = jnp.zeros_like(acc_scr)

    lane = lax.broadcasted_iota(jnp.int32, comb_scr.shape, 1)
    cw = jnp.sum(jnp.where(lane == e + N_GROUPS, comb_scr[...], 0.0), axis=1, keepdims=True)
    h = h_ref[...]
    act = jax.nn.silu(_dot(h, wg_ref[...])) * _dot(h, wu_ref[...]) * cw
    acc_scr[...] += _dot(act.astype(BF16), wd_ref[...])

    @pl.when(e == pl.num_programs(1) - 1)
    def _():
        o_ref[...] = x1_ref[...] + mod_ref[0, 5:6, :] * acc_scr[...]


def _moe(h2, lg, x1, mod_l, wg, wu, wd, seq, tm):
    tokens, d = x1.shape
    tiles_per_seq = seq // tm
    row = lambda i, e: (i, 0)
    return pl.pallas_call(
        _moe_kernel,
        grid=(tokens // tm, N_EXPERTS),
        in_specs=[
            pl.BlockSpec((tm, d), row), pl.BlockSpec((tm, LANES), row), pl.BlockSpec((tm, d), row),
            pl.BlockSpec((1, 6, d), lambda i, e: (i // tiles_per_seq, 0, 0)),
            pl.BlockSpec((None, d, D_EXPERT), lambda i, e: (e, 0, 0)),
            pl.BlockSpec((None, d, D_EXPERT), lambda i, e: (e, 0, 0)),
            pl.BlockSpec((None, D_EXPERT, d), lambda i, e: (e, 0, 0)),
        ],
        out_specs=pl.BlockSpec((tm, d), row),
        out_shape=jax.ShapeDtypeStruct((tokens, d), F32),
        scratch_shapes=[pltpu.VMEM((tm, LANES), F32), pltpu.VMEM((tm, d), F32)],
        compiler_params=pltpu.CompilerParams(
            dimension_semantics=("parallel", "arbitrary"), vmem_limit_bytes=VMEM_LIMIT),
        name="hier_moe",
    )(h2, lg, x1, mod_l, wg, wu, wd)


def _row_tile(seq, want):
    t = min(want, seq)
    assert seq % t == 0
    return t


def kernel(x, c, w_mod, b_mod, g_norm1, g_norm2, w_in, gmlp_ln_g, gmlp_ln_b, gmlp_w_s, gmlp_b_s,
           gla_w_gate2, gla_b_gate, gla_norm_g, dsa_qnorm_g, dsa_knorm_g, rel_bias, w_branch,
           b_branch_gate, w_out, w_group, b_group, w_router, b_router, w_exp_gate, w_exp_up,
           w_exp_down):
    bsz, seq, d = x.shape
    depth = w_mod.shape[0]
    assert d == D_MODEL and seq % DSA_QBLOCK == 0 and seq % GLA_CHUNK == 0
    tokens = bsz * seq
    tm = _row_tile(seq, 512)

    mods = _modulation(c, w_mod, b_mod)
    bias_tiles = _bias_tiles(rel_bias)
    w32, w16 = _prep_w_in(w_in)

    causal = np.tril(np.ones((GMLP_CHUNK, GMLP_CHUNK), bool))
    w_tril = jnp.where(causal[None, None], gmlp_w_s, 0.0).astype(BF16)
    gmlp_bias = jnp.repeat(jnp.swapaxes(gmlp_b_s, 1, 2), LANES, axis=-1)
    w2p = jnp.zeros((depth, LANES, GLA_HEADS * GLA_DK), F32).at[:, :GLA_GATE_RANK].set(gla_w_gate2)
    w2p = w2p.astype(BF16)
    wr = jnp.zeros((depth, d, LANES), F32)
    wr = wr.at[:, :, :N_GROUPS].set(w_group).at[:, :, N_GROUPS:N_GROUPS + N_EXPERTS].set(w_router)
    wr_hi = wr.astype(BF16)
    wr_lo = (wr - wr_hi.astype(F32)).astype(BF16)
    br = jnp.zeros((depth, 1, LANES), F32)
    br = br.at[:, 0, :N_GROUPS].set(b_group).at[:, 0, N_GROUPS:N_GROUPS + N_EXPERTS].set(b_router)
    wbr = w_branch.astype(BF16)
    wout = w_out.astype(BF16)
    wg, wu, wd = w_exp_gate.astype(BF16), w_exp_up.astype(BF16), w_exp_down.astype(BF16)

    x2d = x.reshape(tokens, d)
    for l in range(depth):
        mod_l = mods[l]
        aux32 = [gmlp_ln_g[l][None], gmlp_ln_b[l][None], b_branch_gate[l].reshape(1, -1)]
        aux16 = [dsa_qnorm_g[l][None], dsa_knorm_g[l][None]]
        p32 = _norm_proj(x2d, mod_l, g_norm1[l][None], w32[l], aux32, EPILOGUES_32, F32, seq, tm)
        p16 = _norm_proj(x2d, mod_l, g_norm1[l][None], w16[l], aux16, EPILOGUES_16, BF16, seq, tm)
        ya = _gmlp(p32, w_tril[l], gmlp_bias[l], tm)
        yb = _gla(p32, p16, w2p[l], gla_b_gate[l][None], gla_norm_g[l][None], bsz, seq, tm)
        yc = _dsa(p16, p32, bias_tiles, bsz, seq)
        x1, h2, lg = _merge(ya, yb, yc, p32, x2d, mod_l, wbr[l], wout[l], g_norm2[l][None],
                            wr_hi[l], wr_lo[l], br[l], seq, tm)
        x2d = _moe(h2, lg, x1, mod_l, wg[l], wu[l], wd[l], seq, tm)
    return x2d.reshape(bsz, seq, d)
```

```python
import functools
import math

import numpy as np
import jax
import jax.numpy as jnp
from jax import lax
from jax.experimental import pallas as pl
from jax.experimental.pallas import tpu as pltpu

F32 = jnp.float32
BF16 = jnp.bfloat16

D_MODEL = 1024
D_BRANCH = 512
EPS = 1e-6
GMLP_CHUNK = 128
GMLP_GROUPS = 4
GLA_HEADS = 4
GLA_DK = 64
GLA_DV = 128
GLA_GATE_RANK = 16
GLA_GATE_TAU = 16.0
GLA_CHUNK = 64
GLA_SUB = 16
DSA_HEADS = 4
DSA_HDIM = 128
DSA_IDX_HEADS = 4
DSA_IDX_DIM = 64
DSA_QBLOCK = 128
DSA_KTILE = 512
DSA_TOPK_MAX = 256
N_BUCKETS = 32
MAX_DISTANCE = 128
N_GROUPS = 4
EXPERTS_PER_GROUP = 4
N_EXPERTS = 16
D_EXPERT = 256

LANES = 128
COL_TILE = 512
VMEM_LIMIT = 56 * 1024 * 1024
INT_MIN = -(2 ** 31)
NEG_BIG = -1e30

C32_U, C32_V, C32_GQ, C32_GK, C32_R, C32_GATES, C32_SMALL = 0, 512, 1024, 1280, 1536, 2048, 5120
C32_GA, C32_IW = C32_SMALL, C32_SMALL + GLA_GATE_RANK
N32 = C32_SMALL + LANES
C16_Q, C16_K, C16_V, C16_GV, C16_IQ, C16_IK = 0, 512, 1024, 1536, 2048, 2304
N16 = 2560


def _dot(a, b):
    return jnp.dot(a, b, preferred_element_type=F32)


def _dot_nt(a, b):
    return lax.dot_general(a, b, (((1,), (1,)), ((), ())), preferred_element_type=F32)


def _dot_tn(a, b):
    return lax.dot_general(a, b, (((0,), (0,)), ((), ())), preferred_element_type=F32)


def _split2(a):
    hi = a.astype(BF16)
    lo = (a - hi.astype(F32)).astype(BF16)
    return hi, lo


def _dot3(a, w_hi, w_lo):
    a_hi, a_lo = _split2(a)
    return _dot(a_hi, w_hi) + (_dot(a_lo, w_hi) + _dot(a_hi, w_lo))


def _head_rms(y, g, scale):
    outs = []
    for h in range(y.shape[1] // LANES):
        yh = y[:, h * LANES:(h + 1) * LANES]
        ms = jnp.mean(yh * yh, axis=-1, keepdims=True)
        o = yh * lax.rsqrt(ms + EPS) * g
        if scale != 1.0:
            o = o * scale
        outs.append(o)
    return jnp.concatenate(outs, axis=1)


def _mod_kernel(c_ref, w_ref, b_ref, o_ref):
    a = jax.nn.silu(c_ref[...])
    w_hi, w_lo = _split2(w_ref[...])
    o_ref[...] = _dot3(a, w_hi, w_lo) + b_ref[...]


def _modulation(c, w_mod, b_mod):
    L, d, n = w_mod.shape
    bsz = c.shape[0]
    rows = 8 * pl.cdiv(bsz, 8)
    c_pad = jnp.zeros((rows, d), F32).at[:bsz].set(c)
    tn = 1536
    out = pl.pallas_call(
        _mod_kernel,
        grid=(L, n // tn),
        in_specs=[
            pl.BlockSpec((rows, d), lambda l, j: (0, 0)),
            pl.BlockSpec((None, d, tn), lambda l, j: (l, 0, j)),
            pl.BlockSpec((None, 1, tn), lambda l, j: (l, 0, j)),
        ],
        out_specs=pl.BlockSpec((None, rows, tn), lambda l, j: (l, 0, j)),
        out_shape=jax.ShapeDtypeStruct((L, rows, n), F32),
        compiler_params=pltpu.CompilerParams(
            dimension_semantics=("arbitrary", "arbitrary"), vmem_limit_bytes=VMEM_LIMIT),
        name="adaln_modulation",
    )(c_pad, w_mod, b_mod.reshape(L, 1, n))
    return out[:, :bsz].reshape(L, bsz, 6, d)


def _t5_bucket_table():
    n = np.arange(2 * DSA_QBLOCK)
    max_exact = N_BUCKETS // 2
    large = max_exact + (
        np.log(np.maximum(n, max_exact).astype(np.float32) / max_exact)
        / math.log(MAX_DISTANCE / max_exact) * (N_BUCKETS - max_exact)).astype(np.int32)
    large = np.minimum(large, N_BUCKETS - 1)
    return np.where(n < max_exact, n, large).astype(np.int32)


def _bias_kernel(rb_ref, bucket_ref, o_ref):
    for t in range(bucket_ref.shape[0]):
        bucket = bucket_ref[t]
        for h in range(DSA_HEADS):
            acc = jnp.zeros(bucket.shape, F32)
            for b in range(N_BUCKETS):
                acc = jnp.where(bucket == b, rb_ref[b, h], acc)
            o_ref[t, h] = acc


def _bias_tiles(rel_bias):
    table = _t5_bucket_table()
    assert (table[MAX_DISTANCE:] == N_BUCKETS - 1).all()
    t = np.arange(DSA_QBLOCK)[None, :]
    s = np.arange(DSA_QBLOCK)[:, None]
    diag = table[np.maximum(t - s, 0)]
    near = table[DSA_QBLOCK + t - s]
    far = np.full_like(diag, N_BUCKETS - 1)
    buckets = jnp.asarray(np.stack([diag, near, far]).astype(np.int32))
    return pl.pallas_call(
        _bias_kernel,
        in_specs=[pl.BlockSpec(memory_space=pltpu.SMEM), pl.BlockSpec(memory_space=pltpu.VMEM)],
        out_specs=pl.BlockSpec(memory_space=pltpu.VMEM),
        out_shape=jax.ShapeDtypeStruct((3, DSA_HEADS, DSA_QBLOCK, DSA_QBLOCK), F32),
        name="t5_bias_tiles",
    )(rel_bias, buckets)


def _proj_kernel(x_ref, mod_ref, gn_ref, w32_ref, w16_ref, *rest):
    aux32, aux16 = rest[:3], rest[3:5]
    o32_ref, o16_ref = rest[5], rest[6]
    x = x_ref[...]
    y = x * lax.rsqrt(jnp.mean(x * x, axis=-1, keepdims=True) + EPS) * gn_ref[...]
    h = (y * (1.0 + mod_ref[0, 1:2, :]) + mod_ref[0, 0:1, :]).astype(BF16)
    for w_ref, o_ref, epilogues, aux in ((w32_ref, o32_ref, EPILOGUES_32, aux32),
                                         (w16_ref, o16_ref, EPILOGUES_16, aux16)):
        col = 0
        for width, epi in epilogues:
            cols = slice(col, col + width)
            o_ref[:, cols] = epi(_dot(h, w_ref[:, cols]), aux).astype(o_ref.dtype)
            col += width
        assert col == o_ref.shape[1]


def _norm_proj(x2d, mod_l, gn, w32, w16, aux32, aux16, seq, tm):
    tokens, d = x2d.shape
    tiles_per_seq = seq // tm
    const = lambda a: pl.BlockSpec(a.shape, lambda i: (0, 0))
    resident = lambda a: pl.BlockSpec(a.shape, lambda i: (0, 0), pipeline_mode=pl.Buffered(1))
    return pl.pallas_call(
        _proj_kernel,
        grid=(tokens // tm,),
        in_specs=[
            pl.BlockSpec((tm, d), lambda i: (i, 0)),
            pl.BlockSpec((1, 6, d), lambda i: (i // tiles_per_seq, 0, 0)),
            const(gn), resident(w32), resident(w16),
        ] + [const(a) for a in aux32 + aux16],
        out_specs=[pl.BlockSpec((tm, N32), lambda i: (i, 0)), pl.BlockSpec((tm, N16), lambda i: (i, 0))],
        out_shape=[jax.ShapeDtypeStruct((tokens, N32), F32), jax.ShapeDtypeStruct((tokens, N16), BF16)],
        compiler_params=pltpu.CompilerParams(
            dimension_semantics=("parallel",), vmem_limit_bytes=VMEM_LIMIT),
        name="norm_proj",
    )(x2d, mod_l, gn, w32, w16, *aux32, *aux16)


def _epi_raw(y, aux):
    return y


def _epi_gelu(y, aux):
    return jax.nn.gelu(y)


def _epi_gelu_ln(y, aux):
    v = jax.nn.gelu(y)
    mu = jnp.mean(v, axis=-1, keepdims=True)
    var = jnp.mean(jnp.square(v - mu), axis=-1, keepdims=True)
    return (v - mu) * lax.rsqrt(var + EPS) * aux[0][...] + aux[1][...]


def _epi_silu(y, aux):
    return jax.nn.silu(y)


def _epi_gate(k):
    def epi(y, aux):
        return jax.nn.sigmoid(y + aux[2][:, k * COL_TILE:(k + 1) * COL_TILE])
    return epi


def _epi_qnorm(y, aux):
    return _head_rms(y, aux[0][...], DSA_HDIM ** -0.5)


def _epi_knorm(y, aux):
    return _head_rms(y, aux[1][...], 1.0)


EPILOGUES_32 = ([(COL_TILE, e) for e in (_epi_gelu, _epi_gelu_ln, _epi_raw, _epi_silu)]
                + [(COL_TILE, _epi_gate(k)) for k in range(6)] + [(LANES, _epi_raw)])
EPILOGUES_16 = [(COL_TILE, e) for e in (_epi_qnorm, _epi_knorm, _epi_raw, _epi_raw, _epi_raw)]


def _prep_w_in(w_in):
    sizes = (512, 512, 256, 256, 512, 512, 16, 512, 512, 512, 256, 64, 4, 3072)
    offs = np.concatenate([[0], np.cumsum(sizes)])
    seg = lambda k: w_in[:, :, offs[k]:offs[k + 1]]
    (a_u, a_v, g_q, g_k, g_v, g_r, g_a, d_q, d_k, d_v, d_iq, d_ik, d_iw, gates) = [seg(k) for k in range(14)]
    L, d, _ = w_in.shape
    zeros = lambda n: jnp.zeros((L, d, n), w_in.dtype)
    w32 = jnp.concatenate([a_u, a_v, g_q, g_k, g_r, gates, g_a, d_iw,
                           zeros(N32 - C32_SMALL - 20)], axis=-1).astype(BF16)
    w16 = jnp.concatenate([d_q, d_k, d_v, g_v, d_iq, d_ik, d_ik, zeros(N16 - C16_IK - 128)],
                          axis=-1).astype(BF16)
    return w32, w16


def _gmlp_kernel(u_ref, v_ref, w_ref, b_ref, o_ref):
    tm = u_ref.shape[0]
    for c in range(tm // GMLP_CHUNK):
        rows = slice(c * GMLP_CHUNK, (c + 1) * GMLP_CHUNK)
        for g in range(GMLP_GROUPS):
            cols = slice(g * LANES, (g + 1) * LANES)
            mixed = _dot(w_ref[g], v_ref[rows, cols].astype(BF16)) + b_ref[:, cols]
            o_ref[rows, cols] = (u_ref[rows, cols] * mixed).astype(o_ref.dtype)


def _gmlp(p32, w_tril, bias_full, tm):
    tokens = p32.shape[0]
    return pl.pallas_call(
        _gmlp_kernel,
        grid=(tokens // tm,),
        in_specs=[
            pl.BlockSpec((tm, D_BRANCH), lambda i: (i, C32_U // D_BRANCH)),
            pl.BlockSpec((tm, D_BRANCH), lambda i: (i, C32_V // D_BRANCH)),
            pl.BlockSpec(w_tril.shape, lambda i: (0, 0, 0)),
            pl.BlockSpec(bias_full.shape, lambda i: (0, 0)),
        ],
        out_specs=pl.BlockSpec((tm, D_BRANCH), lambda i: (i, 0)),
        out_shape=jax.ShapeDtypeStruct((tokens, D_BRANCH), BF16),
        compiler_params=pltpu.CompilerParams(
            dimension_semantics=("parallel",), vmem_limit_bytes=VMEM_LIMIT),
        name="gmlp_branch",
    )(p32, p32, w_tril, bias_full)


def _gla_kernel(qk_ref, v_ref, r_ref, ga_ref, w2_ref, b2_ref, ng_ref, tril_ref, exp_ref,
                o_ref, st_ref, d_scr):
    C, SUB, H, DK, DV = GLA_CHUNK, GLA_SUB, GLA_HEADS, GLA_DK, GLA_DV
    HK = H * DK

    @pl.when(pl.program_id(1) == 0)
    def _():
        st_ref[...] = jnp.zeros_like(st_ref)

    lane = lax.broadcasted_iota(jnp.int32, (1, HK), 1)
    head_mask = [(lane >= h * DK) & (lane < (h + 1) * DK) for h in range(H)]
    row = lax.broadcasted_iota(jnp.int32, (C, C), 0)
    col = lax.broadcasted_iota(jnp.int32, (C, C), 1)
    sub_shift = SUB.bit_length() - 1
    blk_lower = (row >> sub_shift) > (col >> sub_shift)
    sub_t = lax.broadcasted_iota(jnp.int32, (SUB, 1), 0)

    def chunk(ci, carry):
        r0 = pl.multiple_of(ci * C, C)
        rows = pl.ds(r0, C)
        q = qk_ref[rows, 0:HK] * (DK ** -0.5)
        k = qk_ref[rows, HK:2 * HK]
        v = v_ref[rows, :]
        graw = _dot(ga_ref[rows, 0:LANES].astype(BF16), w2_ref[...]) + b2_ref[...]
        g = jax.nn.log_sigmoid(graw) / GLA_GATE_TAU
        g_hi = g.astype(BF16)
        g_r1 = g - g_hi.astype(F32)
        g_mid = g_r1.astype(BF16)
        g_lo = (g_r1 - g_mid.astype(F32)).astype(BF16)
        tril = tril_ref[...]
        b = _dot(tril, g_hi) + (_dot(tril, g_mid) + _dot(tril, g_lo))
        b_last = b[C - 1:C, :]
        st = st_ref[...]

        q_in = q * jnp.exp(b)
        k_dec = (k * jnp.exp(b_last - b)).astype(BF16)

        a_off = [jnp.zeros((C, C), F32) for _ in range(H)]
        for j in range(C // SUB - 1):
            bj = b[(j + 1) * SUB - 1:(j + 1) * SUB, :]
            qj = q * jnp.exp(jnp.minimum(b - bj, 0.0))
            in_blk = (lax.broadcasted_iota(jnp.int32, (C, 1), 0) >> sub_shift) == j
            kj = jnp.where(in_blk, k * jnp.exp(jnp.minimum(bj - b, 0.0)), 0.0).astype(BF16)
            for h in range(H):
                a_off[h] = a_off[h] + _dot_nt(jnp.where(head_mask[h], qj, 0.0).astype(BF16), kj)

        for i in range(C // SUB):
            rs = slice(i * SUB, (i + 1) * SUB)
            qi, bi = q[rs, :], b[rs, :]
            for s in range(SUB):
                ks = k[i * SUB + s:i * SUB + s + 1, :]
                bs = b[i * SUB + s:i * SUB + s + 1, :]
                dterm = qi * ks * jnp.exp(jnp.minimum(bi - bs, 0.0))
                dterm = jnp.where(sub_t >= s, dterm, 0.0)
                d_scr[s * SUB:(s + 1) * SUB, :] = dterm.astype(BF16)
            gsum = _dot(d_scr[...], exp_ref[...])
            od = jnp.zeros((SUB, H * DV), F32)
            for s in range(SUB):
                vs = v[i * SUB + s:i * SUB + s + 1, :].astype(F32)
                od = od + gsum[s * SUB:(s + 1) * SUB, :] * vs
            o_ref[pl.ds(r0 + i * SUB, SUB), :] = od

        outs = []
        for h in range(H):
            vh = v[:, h * DV:(h + 1) * DV]
            o_inter = _dot_nt(jnp.where(head_mask[h], q_in, 0.0).astype(BF16), st.astype(BF16))
            a_h = jnp.where(blk_lower, a_off[h], 0.0).astype(BF16)
            outs.append(o_inter + _dot(a_h, vh))
        o = o_ref[rows, :] + jnp.concatenate(outs, axis=1)

        upd = _dot_tn(v, k_dec)
        new_st = st * jnp.exp(b_last)
        for h in range(H):
            new_st = new_st + jnp.where(head_mask[h], upd[h * DV:(h + 1) * DV, :], 0.0)
        st_ref[...] = new_st

        y = _head_rms(o, ng_ref[...], 1.0) * r_ref[rows, :]
        o_ref[rows, :] = y
        return carry

    lax.fori_loop(0, qk_ref.shape[0] // C, chunk, 0)


def _gla(p32, p16, w2p, b2, norm_g, bsz, seq, ts):
    C, SUB, H, DK, DV = GLA_CHUNK, GLA_SUB, GLA_HEADS, GLA_DK, GLA_DV
    tokens = p32.shape[0]
    blocks_per_seq = seq // ts
    tril = jnp.asarray(np.tril(np.ones((C, C), np.float32))).astype(BF16)
    expand = np.zeros((H * DK, H * DV), np.float32)
    for h in range(H):
        expand[h * DK:(h + 1) * DK, h * DV:(h + 1) * DV] = 1.0
    expand = jnp.asarray(expand).astype(BF16)
    row_map = lambda cb: (lambda b, i: (b * blocks_per_seq + i, cb))
    const2 = lambda b, i: (0, 0)
    out = pl.pallas_call(
        _gla_kernel,
        grid=(bsz, blocks_per_seq),
        in_specs=[
            pl.BlockSpec((ts, 2 * H * DK), row_map(C32_GQ // (2 * H * DK))),
            pl.BlockSpec((ts, H * DV), row_map(C16_GV // (H * DV))),
            pl.BlockSpec((ts, H * DV), row_map(C32_R // (H * DV))),
            pl.BlockSpec((ts, LANES), row_map(C32_SMALL // LANES)),
            pl.BlockSpec(w2p.shape, const2),
            pl.BlockSpec(b2.shape, const2),
            pl.BlockSpec(norm_g.shape, const2),
            pl.BlockSpec(tril.shape, const2),
            pl.BlockSpec(expand.shape, const2),
        ],
        out_specs=pl.BlockSpec((ts, H * DV), lambda b, i: (b * blocks_per_seq + i, 0)),
        out_shape=jax.ShapeDtypeStruct((tokens, H * DV), F32),
        scratch_shapes=[pltpu.VMEM((DV, H * DK), F32), pltpu.VMEM((SUB * SUB, H * DK), BF16)],
        compiler_params=pltpu.CompilerParams(
            dimension_semantics=("parallel", "arbitrary"), vmem_limit_bytes=VMEM_LIMIT),
        name="gla_branch",
    )(p32, p16, p32, p32, w2p, b2, norm_g, tril, expand)
    return out


def _dsa_kernel(q_ref, iq_ref, iwt_ref, k_ref, vt_ref, ik_ref, bias_ref, ltri_ref,
                o_ref, key_scr, s_scr, acc_scr, cnt_scr, *, topk):
    T, TK = DSA_QBLOCK, DSA_KTILE
    SUBS = TK // T
    H, HD = DSA_HEADS, DSA_HDIM
    qb = pl.program_id(1)
    kt_last = qb // SUBS
    n_kt = kt_last + 1
    key_minus_query = (lax.broadcasted_iota(jnp.int32, (TK, T), 0)
                       - lax.broadcasted_iota(jnp.int32, (TK, T), 1))

    def fold(x, op):
        return op(x.reshape(x.shape[0] // 8, 8, T), axis=0)

    def rows_to_one(x, op):
        return op(x, axis=0, keepdims=True)

    def key_rows(kt):
        return pl.ds(pl.multiple_of(kt * TK, TK), TK)

    lo_mask = lax.broadcasted_iota(jnp.int32, (T, LANES), 1) < DSA_IDX_DIM
    iq = [iq_ref[:, 0:LANES], iq_ref[:, LANES:2 * LANES]]
    iq_h = [jnp.where(lo_mask, iq[0], 0), jnp.where(lo_mask, 0, iq[0]),
            jnp.where(lo_mask, iq[1], 0), jnp.where(lo_mask, 0, iq[1])]
    iw_h = [iwt_ref[h:h + 1, :] for h in range(DSA_IDX_HEADS)]

    def score_tile(kt):
        ik2 = ik_ref[key_rows(kt), :]
        score = jnp.zeros((TK, T), F32)
        for h in range(DSA_IDX_HEADS):
            score = score + iw_h[h] * jnp.maximum(_dot_nt(ik2, iq_h[h]), 0.0)
        bits = lax.bitcast_convert_type(score, jnp.int32)
        return jnp.where(bits >= 0, bits, bits ^ jnp.int32(0x7FFFFFFF))

    def score_body(kt, c):
        key_scr[kt] = score_tile(kt)
        return c

    lax.fori_loop(0, kt_last, score_body, 0)
    admissible = key_minus_query <= qb * T - kt_last * TK
    key_scr[kt_last] = jnp.where(admissible, score_tile(kt_last), jnp.int32(INT_MIN))

    def count(pred):
        def body(kt, acc):
            return acc + fold(jnp.where(pred(key_scr[kt]), 1, 0), jnp.sum)
        acc = lax.fori_loop(0, n_kt, body, jnp.zeros((8, T), jnp.int32))
        return rows_to_one(acc, jnp.sum)

    def search():
        c0 = count(lambda k: k >= 0)
        pos = c0 >= topk
        base0 = jnp.where(pos, jnp.int32(0), jnp.int32(INT_MIN))
        n_ge0 = jnp.where(pos, c0, jnp.int32(2 ** 30))

        def bit_body(i, carry):
            base, n_ge = carry
            cand = base | jnp.left_shift(jnp.int32(1), 30 - i)
            cnt = count(lambda k: k >= cand)
            ok = cnt >= topk
            return jnp.where(ok, cand, base), jnp.where(ok, cnt, n_ge)

        return lax.fori_loop(0, 31, bit_body, (base0, n_ge0))

    tau, n_ge = lax.cond((qb + 1) * T > topk, search,
                         lambda: (jnp.full((1, T), INT_MIN, jnp.int32), jnp.zeros((1, T), jnp.int32)))
    tau = jnp.maximum(tau, jnp.int32(INT_MIN + 1))

    @pl.when(jnp.max(n_ge) > topk)
    def _():
        need = (topk - count(lambda k: k > tau)).astype(F32)
        cnt_scr[...] = jnp.zeros_like(cnt_scr)

        def tie_body(kt, c):
            key = key_scr[kt]
            eq = key == tau
            eq_f = jnp.where(eq, 1.0, 0.0)
            pref = _dot(ltri_ref[...], eq_f.astype(BF16)) + cnt_scr[...]
            key_scr[kt] = jnp.where(eq & (pref > need), tau - 1, key)
            cnt_scr[...] = cnt_scr[...] + rows_to_one(eq_f, jnp.sum)
            return c

        lax.fori_loop(0, n_kt, tie_body, 0)

    q_h = [q_ref[:, h * HD:(h + 1) * HD] for h in range(H)]

    def logits_body(kt, m_run):
        sel = key_scr[kt] >= tau
        kinds = [jnp.clip(qb - (kt * SUBS + j), 0, 2) for j in range(SUBS)]
        new_m = []
        for h in range(H):
            bias = jnp.concatenate([bias_ref[kinds[j], h] for j in range(SUBS)], axis=0)
            s = _dot_nt(k_ref[key_rows(kt), h * HD:(h + 1) * HD], q_h[h]) + bias
            s = jnp.where(sel, s, NEG_BIG)
            s_scr[h, kt] = s
            new_m.append(jnp.maximum(m_run[h], fold(s, jnp.max)))
        return tuple(new_m)

    m_run = lax.fori_loop(0, n_kt, logits_body,
                          tuple(jnp.full((8, T), NEG_BIG, F32) for _ in range(H)))
    m_h = [rows_to_one(m, jnp.max) for m in m_run]

    for h in range(H):
        acc_scr[h] = jnp.zeros((HD, T), F32)

    def pv_body(kt, l_run):
        new_l = []
        for h in range(H):
            p = jnp.exp(s_scr[h, kt] - m_h[h])
            new_l.append(l_run[h] + fold(p, jnp.sum))
            acc_scr[h] = acc_scr[h] + _dot(vt_ref[kt, h * HD:(h + 1) * HD, :], p.astype(BF16))
        return tuple(new_l)

    l_run = lax.fori_loop(0, n_kt, pv_body, tuple(jnp.zeros((8, T), F32) for _ in range(H)))

    for h in range(H):
        out_t = acc_scr[h] / rows_to_one(l_run[h], jnp.sum)
        o_ref[:, h * HD:(h + 1) * HD] = out_t.T.astype(o_ref.dtype)


def _dsa(p16, p32, bias_tiles, bsz, seq):
    T = DSA_QBLOCK
    tokens = p16.shape[0]
    nqb = seq // T
    topk = min(DSA_TOPK_MAX, seq // 4)
    TK = min(DSA_KTILE, seq)
    assert TK == DSA_KTILE and seq % TK == 0
    nkt = seq // TK
    ltri = jnp.asarray(np.tril(np.ones((TK, TK), np.float32))).astype(BF16)
    w = D_BRANCH
    v_t = jnp.swapaxes(p16[:, C16_V:C16_V + w].reshape(bsz, nkt, TK, w), 2, 3)
    iw_t = jnp.zeros((8, tokens), F32).at[:DSA_IDX_HEADS].set(
        p32[:, C32_IW:C32_IW + DSA_IDX_HEADS].T)
    qmap = lambda cb: (lambda b, i: (b * nqb + i, cb))
    return pl.pallas_call(
        functools.partial(_dsa_kernel, topk=topk),
        grid=(bsz, nqb),
        in_specs=[
            pl.BlockSpec((T, w), qmap(C16_Q // w)),
            pl.BlockSpec((T, 2 * LANES), qmap(C16_IQ // (2 * LANES))),
            pl.BlockSpec((8, T), lambda b, i: (0, b * nqb + i)),
            pl.BlockSpec((seq, w), lambda b, i: (b, C16_K // w)),
            pl.BlockSpec((None, nkt, w, TK), lambda b, i: (b, 0, 0, 0)),
            pl.BlockSpec((seq, LANES), lambda b, i: (b, C16_IK // LANES)),
            pl.BlockSpec(bias_tiles.shape, lambda b, i: (0, 0, 0, 0)),
            pl.BlockSpec(ltri.shape, lambda b, i: (0, 0)),
        ],
        out_specs=pl.BlockSpec((T, w), lambda b, i: (b * nqb + i, 0)),
        out_shape=jax.ShapeDtypeStruct((tokens, w), BF16),
        scratch_shapes=[
            pltpu.VMEM((nkt, TK, T), jnp.int32),
            pltpu.VMEM((DSA_HEADS, nkt, TK, T), F32),
            pltpu.VMEM((DSA_HEADS, DSA_HDIM, T), F32),
            pltpu.VMEM((1, T), F32),
        ],
        compiler_params=pltpu.CompilerParams(
            dimension_semantics=("parallel", "arbitrary"), vmem_limit_bytes=VMEM_LIMIT),
        name="dsa_branch",
    )(p16, p16, iw_t, p16, v_t, p16, bias_tiles, ltri)


def _merge_kernel(ya_ref, yb_ref, yc_ref, ga_ref, gb_ref, gc_ref, x_ref, mod_ref, wbr_ref, wout_ref,
                  gn2_ref, wr_hi_ref, wr_lo_ref, br_ref, x1_ref, h2_ref, lg_ref):
    merged = ga_ref[...] * _dot(ya_ref[...], wbr_ref[0])
    merged = merged + gb_ref[...] * _dot(yb_ref[...].astype(BF16), wbr_ref[1])
    merged = merged + gc_ref[...] * _dot(yc_ref[...], wbr_ref[2])
    mix = _dot(merged.astype(BF16), wout_ref[...])
    x1 = x_ref[...] + mod_ref[0, 2:3, :] * mix
    x1_ref[...] = x1
    y = x1 * lax.rsqrt(jnp.mean(x1 * x1, axis=-1, keepdims=True) + EPS) * gn2_ref[...]
    h2 = y * (1.0 + mod_ref[0, 4:5, :]) + mod_ref[0, 3:4, :]
    h2_ref[...] = h2.astype(BF16)
    lg_ref[...] = _dot3(h2, wr_hi_ref[...], wr_lo_ref[...]) + br_ref[...]


def _merge(ya, yb, yc, p32, x2d, mod_l, wbr, wout, gn2, wr_hi, wr_lo, br, seq, tm):
    tokens, d = x2d.shape
    tiles_per_seq = seq // tm
    row = lambda i: (i, 0)
    gate = lambda k: (lambda i: (i, C32_GATES // d + k))
    c2 = lambda i: (0, 0)
    return pl.pallas_call(
        _merge_kernel,
        grid=(tokens // tm,),
        in_specs=[
            pl.BlockSpec((tm, D_BRANCH), row), pl.BlockSpec((tm, D_BRANCH), row),
            pl.BlockSpec((tm, D_BRANCH), row),
            pl.BlockSpec((tm, d), gate(0)), pl.BlockSpec((tm, d), gate(1)), pl.BlockSpec((tm, d), gate(2)),
            pl.BlockSpec((tm, d), row),
            pl.BlockSpec((1, 6, d), lambda i: (i // tiles_per_seq, 0, 0)),
            pl.BlockSpec(wbr.shape, lambda i: (0, 0, 0)),
            pl.BlockSpec(wout.shape, c2),
            pl.BlockSpec(gn2.shape, c2),
            pl.BlockSpec(wr_hi.shape, c2), pl.BlockSpec(wr_lo.shape, c2), pl.BlockSpec(br.shape, c2),
        ],
        out_specs=[pl.BlockSpec((tm, d), row), pl.BlockSpec((tm, d), row), pl.BlockSpec((tm, LANES), row)],
        out_shape=[jax.ShapeDtypeStruct((tokens, d), F32), jax.ShapeDtypeStruct((tokens, d), BF16),
                   jax.ShapeDtypeStruct((tokens, LANES), F32)],
        compiler_params=pltpu.CompilerParams(
            dimension_semantics=("parallel",), vmem_limit_bytes=VMEM_LIMIT),
        name="merge_norm_router",
    )(ya, yb, yc, p32, p32, p32, x2d, mod_l, wbr, wout, gn2, wr_hi, wr_lo, br)


def _route(lg):
    lane = lax.broadcasted_iota(jnp.int32, lg.shape, 1)
    big = jnp.int32(10 ** 6)
    is_grp = lane < N_GROUPS
    gl = jnp.where(is_grp, lg, -jnp.inf)
    gmax = jnp.max(gl, axis=1, keepdims=True)
    gsum = jnp.sum(jnp.where(is_grp, jnp.exp(lg - gmax), 0.0), axis=1, keepdims=True)
    p_g = 1.0 / gsum
    g_idx = jnp.min(jnp.where(gl == gmax, lane, big), axis=1, keepdims=True)
    first = N_GROUPS + g_idx * EXPERTS_PER_GROUP
    in_grp = (lane >= first) & (lane < first + EXPERTS_PER_GROUP)
    e1 = jnp.where(in_grp, lg, -jnp.inf)
    v1 = jnp.max(e1, axis=1, keepdims=True)
    i1 = jnp.min(jnp.where(e1 == v1, lane, big), axis=1, keepdims=True)
    e2 = jnp.where(in_grp & (lane != i1), lg, -jnp.inf)
    v2 = jnp.max(e2, axis=1, keepdims=True)
    i2 = jnp.min(jnp.where(e2 == v2, lane, big), axis=1, keepdims=True)
    t = jnp.exp(v2 - v1)
    w1 = p_g * (1.0 / (1.0 + t))
    w2 = p_g * (t / (1.0 + t))
    return jnp.where(lane == i1, w1, jnp.where(lane == i2, w2, 0.0))


def _moe_kernel(h_ref, lg_ref, x1_ref, mod_ref, wg_ref, wu_ref, wd_ref, o_ref, comb_scr, acc_scr):
    e = pl.program_id(1)

    @pl.when(e == 0)
    def _():
        comb_scr[...] = _route(lg_ref[...])
        acc_scr[...] = jnp.zeros_like(acc_scr)

    lane = lax.broadcasted_iota(jnp.int32, comb_scr.shape, 1)
    cw = jnp.sum(jnp.where(lane == e + N_GROUPS, comb_scr[...], 0.0), axis=1, keepdims=True)
    h = h_ref[...]
    act = jax.nn.silu(_dot(h, wg_ref[...])) * _dot(h, wu_ref[...]) * cw
    acc_scr[...] += _dot(act.astype(BF16), wd_ref[...])

    @pl.when(e == pl.num_programs(1) - 1)
    def _():
        o_ref[...] = x1_ref[...] + mod_ref[0, 5:6, :] * acc_scr[...]


def _moe(h2, lg, x1, mod_l, wg, wu, wd, seq, tm):
    tokens, d = x1.shape
    tiles_per_seq = seq // tm
    row = lambda i, e: (i, 0)
    return pl.pallas_call(
        _moe_kernel,
        grid=(tokens // tm, N_EXPERTS),
        in_specs=[
            pl.BlockSpec((tm, d), row), pl.BlockSpec((tm, LANES), row), pl.BlockSpec((tm, d), row),
            pl.BlockSpec((1, 6, d), lambda i, e: (i // tiles_per_seq, 0, 0)),
            pl.BlockSpec((None, d, D_EXPERT), lambda i, e: (e, 0, 0)),
            pl.BlockSpec((None, d, D_EXPERT), lambda i, e: (e, 0, 0)),
            pl.BlockSpec((None, D_EXPERT, d), lambda i, e: (e, 0, 0)),
        ],
        out_specs=pl.BlockSpec((tm, d), row),
        out_shape=jax.ShapeDtypeStruct((tokens, d), F32),
        scratch_shapes=[pltpu.VMEM((tm, LANES), F32), pltpu.VMEM((tm, d), F32)],
        compiler_params=pltpu.CompilerParams(
            dimension_semantics=("parallel", "arbitrary"), vmem_limit_bytes=VMEM_LIMIT),
        name="hier_moe",
    )(h2, lg, x1, mod_l, wg, wu, wd)


def _row_tile(seq, want):
    t = min(want, seq)
    assert seq % t == 0
    return t


def kernel(x, c, w_mod, b_mod, g_norm1, g_norm2, w_in, gmlp_ln_g, gmlp_ln_b, gmlp_w_s, gmlp_b_s,
           gla_w_gate2, gla_b_gate, gla_norm_g, dsa_qnorm_g, dsa_knorm_g, rel_bias, w_branch,
           b_branch_gate, w_out, w_group, b_group, w_router, b_router, w_exp_gate, w_exp_up,
           w_exp_down):
    bsz, seq, d = x.shape
    depth = w_mod.shape[0]
    assert d == D_MODEL and seq % DSA_QBLOCK == 0 and seq % GLA_CHUNK == 0
    tokens = bsz * seq
    tm = _row_tile(seq, 512)

    mods = _modulation(c, w_mod, b_mod)
    bias_tiles = _bias_tiles(rel_bias)
    w32, w16 = _prep_w_in(w_in)

    causal = np.tril(np.ones((GMLP_CHUNK, GMLP_CHUNK), bool))
    w_tril = jnp.where(causal[None, None], gmlp_w_s, 0.0).astype(BF16)
    gmlp_bias = jnp.repeat(jnp.swapaxes(gmlp_b_s, 1, 2), LANES, axis=-1)
    w2p = jnp.zeros((depth, LANES, GLA_HEADS * GLA_DK), F32).at[:, :GLA_GATE_RANK].set(gla_w_gate2)
    w2p = w2p.astype(BF16)
    wr = jnp.zeros((depth, d, LANES), F32)
    wr = wr.at[:, :, :N_GROUPS].set(w_group).at[:, :, N_GROUPS:N_GROUPS + N_EXPERTS].set(w_router)
    wr_hi = wr.astype(BF16)
    wr_lo = (wr - wr_hi.astype(F32)).astype(BF16)
    br = jnp.zeros((depth, 1, LANES), F32)
    br = br.at[:, 0, :N_GROUPS].set(b_group).at[:, 0, N_GROUPS:N_GROUPS + N_EXPERTS].set(b_router)
    wbr = w_branch.astype(BF16)
    wout = w_out.astype(BF16)
    wg, wu, wd = w_exp_gate.astype(BF16), w_exp_up.astype(BF16), w_exp_down.astype(BF16)

    x2d = x.reshape(tokens, d)
    for l in range(depth):
        mod_l = mods[l]
        aux32 = [gmlp_ln_g[l][None], gmlp_ln_b[l][None], b_branch_gate[l].reshape(1, -1)]
        aux16 = [dsa_qnorm_g[l][None], dsa_knorm_g[l][None]]
        p32, p16 = _norm_proj(x2d, mod_l, g_norm1[l][None], w32[l], w16[l], aux32, aux16, seq, tm)
        ya = _gmlp(p32, w_tril[l], gmlp_bias[l], tm)
        yb = _gla(p32, p16, w2p[l], gla_b_gate[l][None], gla_norm_g[l][None], bsz, seq, tm)
        yc = _dsa(p16, p32, bias_tiles, bsz, seq)
        x1, h2, lg = _merge(ya, yb, yc, p32, x2d, mod_l, wbr[l], wout[l], g_norm2[l][None],
                            wr_hi[l], wr_lo[l], br[l], seq, tm)
        x2d = _moe(h2, lg, x1, mod_l, wg[l], wu[l], wd[l], seq, tm)
    return x2d.reshape(bsz, seq, d)
```

```python
import functools
import math

import numpy as np
import jax
import jax.numpy as jnp
from jax import lax
from jax.experimental import pallas as pl
from jax.experimental.pallas import tpu as pltpu

F32 = jnp.float32
BF16 = jnp.bfloat16

D_MODEL = 1024
D_BRANCH = 512
EPS = 1e-6
GMLP_CHUNK = 128
GMLP_GROUPS = 4
GLA_HEADS = 4
GLA_DK = 64
GLA_DV = 128
GLA_GATE_RANK = 16
GLA_GATE_TAU = 16.0
GLA_CHUNK = 64
GLA_SUB = 16
DSA_HEADS = 4
DSA_HDIM = 128
DSA_IDX_HEADS = 4
DSA_IDX_DIM = 64
DSA_QBLOCK = 128
DSA_KTILE = 512
DSA_TOPK_MAX = 256
DSA_VALUE_PASSES = 24
N_BUCKETS = 32
MAX_DISTANCE = 128
N_GROUPS = 4
EXPERTS_PER_GROUP = 4
N_EXPERTS = 16
D_EXPERT = 256

LANES = 128
COL_TILE = 512
VMEM_LIMIT = 56 * 1024 * 1024
INT_MIN = -(2 ** 31)
NEG_BIG = -1e30

C32_U, C32_V, C32_GQ, C32_GK, C32_R, C32_GATES, C32_SMALL = 0, 512, 1024, 1280, 1536, 2048, 5120
C32_GA, C32_IW = C32_SMALL, C32_SMALL + GLA_GATE_RANK
N32 = C32_SMALL + LANES
C16_Q, C16_K, C16_V, C16_GV, C16_IQ, C16_IK = 0, 512, 1024, 1536, 2048, 2304
N16 = 2560


def _dot(a, b):
    return jnp.dot(a, b, preferred_element_type=F32)


def _dot_nt(a, b):
    return lax.dot_general(a, b, (((1,), (1,)), ((), ())), preferred_element_type=F32)


def _dot_tn(a, b):
    return lax.dot_general(a, b, (((0,), (0,)), ((), ())), preferred_element_type=F32)


def _split2(a):
    hi = a.astype(BF16)
    lo = (a - hi.astype(F32)).astype(BF16)
    return hi, lo


def _dot3(a, w_hi, w_lo):
    a_hi, a_lo = _split2(a)
    return _dot(a_hi, w_hi) + (_dot(a_lo, w_hi) + _dot(a_hi, w_lo))


def _head_rms(y, g, scale):
    outs = []
    for h in range(y.shape[1] // LANES):
        yh = y[:, h * LANES:(h + 1) * LANES]
        ms = jnp.mean(yh * yh, axis=-1, keepdims=True)
        o = yh * lax.rsqrt(ms + EPS) * g
        if scale != 1.0:
            o = o * scale
        outs.append(o)
    return jnp.concatenate(outs, axis=1)


def _mod_kernel(c_ref, w_ref, b_ref, o_ref):
    a = jax.nn.silu(c_ref[...])
    w_hi, w_lo = _split2(w_ref[...])
    o_ref[...] = _dot3(a, w_hi, w_lo) + b_ref[...]


def _modulation(c, w_mod, b_mod):
    L, d, n = w_mod.shape
    bsz = c.shape[0]
    rows = 8 * pl.cdiv(bsz, 8)
    c_pad = jnp.zeros((rows, d), F32).at[:bsz].set(c)
    tn = 1536
    out = pl.pallas_call(
        _mod_kernel,
        grid=(L, n // tn),
        in_specs=[
            pl.BlockSpec((rows, d), lambda l, j: (0, 0)),
            pl.BlockSpec((None, d, tn), lambda l, j: (l, 0, j)),
            pl.BlockSpec((None, 1, tn), lambda l, j: (l, 0, j)),
        ],
        out_specs=pl.BlockSpec((None, rows, tn), lambda l, j: (l, 0, j)),
        out_shape=jax.ShapeDtypeStruct((L, rows, n), F32),
        compiler_params=pltpu.CompilerParams(
            dimension_semantics=("arbitrary", "arbitrary"), vmem_limit_bytes=VMEM_LIMIT),
        name="adaln_modulation",
    )(c_pad, w_mod, b_mod.reshape(L, 1, n))
    return out[:, :bsz].reshape(L, bsz, 6, d)


def _t5_bucket_table():
    n = np.arange(2 * DSA_QBLOCK)
    max_exact = N_BUCKETS // 2
    large = max_exact + (
        np.log(np.maximum(n, max_exact).astype(np.float32) / max_exact)
        / math.log(MAX_DISTANCE / max_exact) * (N_BUCKETS - max_exact)).astype(np.int32)
    large = np.minimum(large, N_BUCKETS - 1)
    return np.where(n < max_exact, n, large).astype(np.int32)


def _bias_kernel(rb_ref, bucket_ref, o_ref):
    for t in range(bucket_ref.shape[0]):
        bucket = bucket_ref[t]
        for h in range(DSA_HEADS):
            acc = jnp.zeros(bucket.shape, F32)
            for b in range(N_BUCKETS):
                acc = jnp.where(bucket == b, rb_ref[b, h], acc)
            o_ref[t, h] = acc


def _bias_tiles(rel_bias):
    table = _t5_bucket_table()
    assert (table[MAX_DISTANCE:] == N_BUCKETS - 1).all()
    t = np.arange(DSA_QBLOCK)[None, :]
    s = np.arange(DSA_QBLOCK)[:, None]
    diag = table[np.maximum(t - s, 0)]
    near = table[DSA_QBLOCK + t - s]
    far = np.full_like(diag, N_BUCKETS - 1)
    buckets = jnp.asarray(np.stack([diag, near, far]).astype(np.int32))
    return pl.pallas_call(
        _bias_kernel,
        in_specs=[pl.BlockSpec(memory_space=pltpu.SMEM), pl.BlockSpec(memory_space=pltpu.VMEM)],
        out_specs=pl.BlockSpec(memory_space=pltpu.VMEM),
        out_shape=jax.ShapeDtypeStruct((3, DSA_HEADS, DSA_QBLOCK, DSA_QBLOCK), F32),
        name="t5_bias_tiles",
    )(rel_bias, buckets)


def _proj_kernel(x_ref, mod_ref, gn_ref, w32_ref, w16_ref, *rest):
    aux32, aux16 = rest[:3], rest[3:5]
    o32_ref, o16_ref = rest[5], rest[6]
    x = x_ref[...]
    y = x * lax.rsqrt(jnp.mean(x * x, axis=-1, keepdims=True) + EPS) * gn_ref[...]
    h = (y * (1.0 + mod_ref[0, 1:2, :]) + mod_ref[0, 0:1, :]).astype(BF16)
    for w_ref, o_ref, epilogues, aux in ((w32_ref, o32_ref, EPILOGUES_32, aux32),
                                         (w16_ref, o16_ref, EPILOGUES_16, aux16)):
        col = 0
        for width, epi in epilogues:
            cols = slice(col, col + width)
            o_ref[:, cols] = epi(_dot(h, w_ref[:, cols]), aux).astype(o_ref.dtype)
            col += width
        assert col == o_ref.shape[1]


def _norm_proj(x2d, mod_l, gn, w32, w16, aux32, aux16, seq, tm):
    tokens, d = x2d.shape
    tiles_per_seq = seq // tm
    const = lambda a: pl.BlockSpec(a.shape, lambda i: (0, 0))
    resident = lambda a: pl.BlockSpec(a.shape, lambda i: (0, 0), pipeline_mode=pl.Buffered(1))
    return pl.pallas_call(
        _proj_kernel,
        grid=(tokens // tm,),
        in_specs=[
            pl.BlockSpec((tm, d), lambda i: (i, 0)),
            pl.BlockSpec((1, 6, d), lambda i: (i // tiles_per_seq, 0, 0)),
            const(gn), resident(w32), resident(w16),
        ] + [const(a) for a in aux32 + aux16],
        out_specs=[pl.BlockSpec((tm, N32), lambda i: (i, 0)), pl.BlockSpec((tm, N16), lambda i: (i, 0))],
        out_shape=[jax.ShapeDtypeStruct((tokens, N32), F32), jax.ShapeDtypeStruct((tokens, N16), BF16)],
        compiler_params=pltpu.CompilerParams(
            dimension_semantics=("parallel",), vmem_limit_bytes=VMEM_LIMIT),
        name="norm_proj",
    )(x2d, mod_l, gn, w32, w16, *aux32, *aux16)


def _epi_raw(y, aux):
    return y


def _epi_gelu(y, aux):
    return jax.nn.gelu(y)


def _epi_gelu_ln(y, aux):
    v = jax.nn.gelu(y)
    mu = jnp.mean(v, axis=-1, keepdims=True)
    var = jnp.mean(jnp.square(v - mu), axis=-1, keepdims=True)
    return (v - mu) * lax.rsqrt(var + EPS) * aux[0][...] + aux[1][...]


def _epi_silu(y, aux):
    return jax.nn.silu(y)


def _epi_gate(k):
    def epi(y, aux):
        return jax.nn.sigmoid(y + aux[2][:, k * COL_TILE:(k + 1) * COL_TILE])
    return epi


def _epi_qnorm(y, aux):
    return _head_rms(y, aux[0][...], DSA_HDIM ** -0.5)


def _epi_knorm(y, aux):
    return _head_rms(y, aux[1][...], 1.0)


EPILOGUES_32 = ([(COL_TILE, e) for e in (_epi_gelu, _epi_gelu_ln, _epi_raw, _epi_silu)]
                + [(COL_TILE, _epi_gate(k)) for k in range(6)] + [(LANES, _epi_raw)])
EPILOGUES_16 = [(COL_TILE, e) for e in (_epi_qnorm, _epi_knorm, _epi_raw, _epi_raw, _epi_raw)]


def _prep_w_in(w_in):
    sizes = (512, 512, 256, 256, 512, 512, 16, 512, 512, 512, 256, 64, 4, 3072)
    offs = np.concatenate([[0], np.cumsum(sizes)])
    seg = lambda k: w_in[:, :, offs[k]:offs[k + 1]]
    (a_u, a_v, g_q, g_k, g_v, g_r, g_a, d_q, d_k, d_v, d_iq, d_ik, d_iw, gates) = [seg(k) for k in range(14)]
    L, d, _ = w_in.shape
    zeros = lambda n: jnp.zeros((L, d, n), w_in.dtype)
    w32 = jnp.concatenate([a_u, a_v, g_q, g_k, g_r, gates, g_a, d_iw,
                           zeros(N32 - C32_SMALL - 20)], axis=-1).astype(BF16)
    w16 = jnp.concatenate([d_q, d_k, d_v, g_v, d_iq, d_ik, d_ik, zeros(N16 - C16_IK - 128)],
                          axis=-1).astype(BF16)
    return w32, w16


def _gmlp_kernel(u_ref, v_ref, w_ref, b_ref, o_ref):
    tm = u_ref.shape[0]
    for c in range(tm // GMLP_CHUNK):
        rows = slice(c * GMLP_CHUNK, (c + 1) * GMLP_CHUNK)
        for g in range(GMLP_GROUPS):
            cols = slice(g * LANES, (g + 1) * LANES)
            mixed = _dot(w_ref[g], v_ref[rows, cols].astype(BF16)) + b_ref[:, cols]
            o_ref[rows, cols] = (u_ref[rows, cols] * mixed).astype(o_ref.dtype)


def _gmlp(p32, w_tril, bias_full, tm):
    tokens = p32.shape[0]
    return pl.pallas_call(
        _gmlp_kernel,
        grid=(tokens // tm,),
        in_specs=[
            pl.BlockSpec((tm, D_BRANCH), lambda i: (i, C32_U // D_BRANCH)),
            pl.BlockSpec((tm, D_BRANCH), lambda i: (i, C32_V // D_BRANCH)),
            pl.BlockSpec(w_tril.shape, lambda i: (0, 0, 0)),
            pl.BlockSpec(bias_full.shape, lambda i: (0, 0)),
        ],
        out_specs=pl.BlockSpec((tm, D_BRANCH), lambda i: (i, 0)),
        out_shape=jax.ShapeDtypeStruct((tokens, D_BRANCH), BF16),
        compiler_params=pltpu.CompilerParams(
            dimension_semantics=("parallel",), vmem_limit_bytes=VMEM_LIMIT),
        name="gmlp_branch",
    )(p32, p32, w_tril, bias_full)


def _gla_kernel(qk_ref, v_ref, r_ref, ga_ref, w2_ref, b2_ref, ng_ref, tril_ref, exp_ref,
                o_ref, st_ref, d_scr):
    C, SUB, H, DK, DV = GLA_CHUNK, GLA_SUB, GLA_HEADS, GLA_DK, GLA_DV
    HK = H * DK

    @pl.when(pl.program_id(1) == 0)
    def _():
        st_ref[...] = jnp.zeros_like(st_ref)

    lane = lax.broadcasted_iota(jnp.int32, (1, HK), 1)
    head_mask = [(lane >= h * DK) & (lane < (h + 1) * DK) for h in range(H)]
    row = lax.broadcasted_iota(jnp.int32, (C, C), 0)
    col = lax.broadcasted_iota(jnp.int32, (C, C), 1)
    sub_shift = SUB.bit_length() - 1
    blk_lower = (row >> sub_shift) > (col >> sub_shift)
    sub_t = lax.broadcasted_iota(jnp.int32, (SUB, 1), 0)

    def chunk(ci, carry):
        r0 = pl.multiple_of(ci * C, C)
        rows = pl.ds(r0, C)
        q = qk_ref[rows, 0:HK] * (DK ** -0.5)
        k = qk_ref[rows, HK:2 * HK]
        v = v_ref[rows, :]
        graw = _dot(ga_ref[rows, 0:LANES].astype(BF16), w2_ref[...]) + b2_ref[...]
        g = jax.nn.log_sigmoid(graw) / GLA_GATE_TAU
        g_hi = g.astype(BF16)
        g_r1 = g - g_hi.astype(F32)
        g_mid = g_r1.astype(BF16)
        g_lo = (g_r1 - g_mid.astype(F32)).astype(BF16)
        tril = tril_ref[...]
        b = _dot(tril, g_hi) + (_dot(tril, g_mid) + _dot(tril, g_lo))
        b_last = b[C - 1:C, :]
        st = st_ref[...]

        q_in = q * jnp.exp(b)
        k_dec = (k * jnp.exp(b_last - b)).astype(BF16)

        a_off = [jnp.zeros((C, C), F32) for _ in range(H)]
        for j in range(C // SUB - 1):
            bj = b[(j + 1) * SUB - 1:(j + 1) * SUB, :]
            qj = q * jnp.exp(jnp.minimum(b - bj, 0.0))
            in_blk = (lax.broadcasted_iota(jnp.int32, (C, 1), 0) >> sub_shift) == j
            kj = jnp.where(in_blk, k * jnp.exp(jnp.minimum(bj - b, 0.0)), 0.0).astype(BF16)
            for h in range(H):
                a_off[h] = a_off[h] + _dot_nt(jnp.where(head_mask[h], qj, 0.0).astype(BF16), kj)

        for i in range(C // SUB):
            rs = slice(i * SUB, (i + 1) * SUB)
            qi, bi = q[rs, :], b[rs, :]
            for s in range(SUB):
                ks = k[i * SUB + s:i * SUB + s + 1, :]
                bs = b[i * SUB + s:i * SUB + s + 1, :]
                dterm = qi * ks * jnp.exp(jnp.minimum(bi - bs, 0.0))
                dterm = jnp.where(sub_t >= s, dterm, 0.0)
                d_scr[s * SUB:(s + 1) * SUB, :] = dterm.astype(BF16)
            gsum = _dot(d_scr[...], exp_ref[...])
            od = jnp.zeros((SUB, H * DV), F32)
            for s in range(SUB):
                vs = v[i * SUB + s:i * SUB + s + 1, :].astype(F32)
                od = od + gsum[s * SUB:(s + 1) * SUB, :] * vs
            o_ref[pl.ds(r0 + i * SUB, SUB), :] = od

        outs = []
        for h in range(H):
            vh = v[:, h * DV:(h + 1) * DV]
            o_inter = _dot_nt(jnp.where(head_mask[h], q_in, 0.0).astype(BF16), st.astype(BF16))
            a_h = jnp.where(blk_lower, a_off[h], 0.0).astype(BF16)
            outs.append(o_inter + _dot(a_h, vh))
        o = o_ref[rows, :] + jnp.concatenate(outs, axis=1)

        upd = _dot_tn(v, k_dec)
        new_st = st * jnp.exp(b_last)
        for h in range(H):
            new_st = new_st + jnp.where(head_mask[h], upd[h * DV:(h + 1) * DV, :], 0.0)
        st_ref[...] = new_st

        y = _head_rms(o, ng_ref[...], 1.0) * r_ref[rows, :]
        o_ref[rows, :] = y
        return carry

    lax.fori_loop(0, qk_ref.shape[0] // C, chunk, 0)


def _gla(p32, p16, w2p, b2, norm_g, bsz, seq, ts):
    C, SUB, H, DK, DV = GLA_CHUNK, GLA_SUB, GLA_HEADS, GLA_DK, GLA_DV
    tokens = p32.shape[0]
    blocks_per_seq = seq // ts
    tril = jnp.asarray(np.tril(np.ones((C, C), np.float32))).astype(BF16)
    expand = np.zeros((H * DK, H * DV), np.float32)
    for h in range(H):
        expand[h * DK:(h + 1) * DK, h * DV:(h + 1) * DV] = 1.0
    expand = jnp.asarray(expand).astype(BF16)
    row_map = lambda cb: (lambda b, i: (b * blocks_per_seq + i, cb))
    const2 = lambda b, i: (0, 0)
    out = pl.pallas_call(
        _gla_kernel,
        grid=(bsz, blocks_per_seq),
        in_specs=[
            pl.BlockSpec((ts, 2 * H * DK), row_map(C32_GQ // (2 * H * DK))),
            pl.BlockSpec((ts, H * DV), row_map(C16_GV // (H * DV))),
            pl.BlockSpec((ts, H * DV), row_map(C32_R // (H * DV))),
            pl.BlockSpec((ts, LANES), row_map(C32_SMALL // LANES)),
            pl.BlockSpec(w2p.shape, const2),
            pl.BlockSpec(b2.shape, const2),
            pl.BlockSpec(norm_g.shape, const2),
            pl.BlockSpec(tril.shape, const2),
            pl.BlockSpec(expand.shape, const2),
        ],
        out_specs=pl.BlockSpec((ts, H * DV), lambda b, i: (b * blocks_per_seq + i, 0)),
        out_shape=jax.ShapeDtypeStruct((tokens, H * DV), F32),
        scratch_shapes=[pltpu.VMEM((DV, H * DK), F32), pltpu.VMEM((SUB * SUB, H * DK), BF16)],
        compiler_params=pltpu.CompilerParams(
            dimension_semantics=("parallel", "arbitrary"), vmem_limit_bytes=VMEM_LIMIT),
        name="gla_branch",
    )(p32, p16, p32, p32, w2p, b2, norm_g, tril, expand)
    return out


def _dsa_kernel(q_ref, iq_ref, iwt_ref, k_ref, vt_ref, ik_ref, bias_ref, ltri_ref,
                o_ref, key_scr, s_scr, acc_scr, cnt_scr, *, topk):
    T, TK = DSA_QBLOCK, DSA_KTILE
    SUBS = TK // T
    H, HD = DSA_HEADS, DSA_HDIM
    qb = pl.program_id(1)
    kt_last = qb // SUBS
    n_kt = kt_last + 1
    key_minus_query = (lax.broadcasted_iota(jnp.int32, (TK, T), 0)
                       - lax.broadcasted_iota(jnp.int32, (TK, T), 1))

    def fold(x, op):
        return op(x.reshape(x.shape[0] // 8, 8, T), axis=0)

    def rows_to_one(x, op):
        return op(x, axis=0, keepdims=True)

    def key_rows(kt):
        return pl.ds(pl.multiple_of(kt * TK, TK), TK)

    lo_mask = lax.broadcasted_iota(jnp.int32, (T, LANES), 1) < DSA_IDX_DIM
    iq = [iq_ref[:, 0:LANES], iq_ref[:, LANES:2 * LANES]]
    iq_h = [jnp.where(lo_mask, iq[0], 0), jnp.where(lo_mask, 0, iq[0]),
            jnp.where(lo_mask, iq[1], 0), jnp.where(lo_mask, 0, iq[1])]
    iw_h = [iwt_ref[h:h + 1, :] for h in range(DSA_IDX_HEADS)]

    def to_key(x):
        bits = lax.bitcast_convert_type(x, jnp.int32)
        return jnp.where(x == 0.0, 0, jnp.where(bits >= 0, bits, bits ^ jnp.int32(0x7FFFFFFF)))

    def to_value(key):
        bits = jnp.where(key >= 0, key, key ^ jnp.int32(0x7FFFFFFF))
        return lax.bitcast_convert_type(bits, F32)

    def score_tile(kt):
        ik2 = ik_ref[key_rows(kt), :]
        score = jnp.zeros((TK, T), F32)
        for h in range(DSA_IDX_HEADS):
            score = score + iw_h[h] * jnp.maximum(_dot_nt(ik2, iq_h[h]), 0.0)
        return to_key(score)

    def score_body(kt, carry):
        kmax, kmin = carry
        key = score_tile(kt)
        key_scr[kt] = key
        return jnp.maximum(kmax, fold(key, jnp.max)), jnp.minimum(kmin, fold(key, jnp.min))

    int_max = jnp.int32(2 ** 31 - 1)
    kmax, kmin = lax.fori_loop(0, kt_last, score_body,
                               (jnp.full((8, T), INT_MIN, jnp.int32), jnp.full((8, T), int_max, jnp.int32)))
    admissible = key_minus_query <= qb * T - kt_last * TK
    key = score_tile(kt_last)
    key_scr[kt_last] = jnp.where(admissible, key, jnp.int32(INT_MIN))
    kmax = rows_to_one(jnp.maximum(kmax, fold(jnp.where(admissible, key, INT_MIN), jnp.max)), jnp.max)
    kmin = rows_to_one(jnp.minimum(kmin, fold(jnp.where(admissible, key, int_max), jnp.min)), jnp.min)

    def count(pred):
        def body(kt, acc):
            return acc + fold(jnp.where(pred(key_scr[kt]), 1, 0), jnp.sum)
        acc = lax.fori_loop(0, n_kt, body, jnp.zeros((8, T), jnp.int32))
        return rows_to_one(acc, jnp.sum)

    def search():
        n_adm = qb * T + lax.broadcasted_iota(jnp.int32, (1, T), 1) + 1
        c_zero = count(lambda k: k >= 0)
        c_pos = count(lambda k: k >= 1)
        pos = c_pos >= topk
        at_zero = (c_zero >= topk) & ~pos
        lo = jnp.where(pos, 1, jnp.where(at_zero, 0, kmin))
        n_lo = jnp.where(pos, c_pos, jnp.where(at_zero, c_zero, n_adm))
        hi = jnp.where(pos, kmax + 1, jnp.where(at_zero, 1, 0))
        done = ((n_adm <= topk) | (n_lo == topk) | (hi == lo + 1)).astype(jnp.int32)

        def step(it, lo, n_lo, hi, done):
            mid_v = to_key(0.5 * to_value(lo) + 0.5 * to_value(hi))
            mid_k = (lo >> 1) + (hi >> 1) + (lo & hi & 1)
            cand = jnp.clip(jnp.where(it < DSA_VALUE_PASSES, mid_v, mid_k), lo + 1, hi - 1)
            cnt = count(lambda k: k >= cand)
            live = done == 0
            up = live & (cnt >= topk)
            down = live & (cnt < topk)
            lo, n_lo, hi = jnp.where(up, cand, lo), jnp.where(up, cnt, n_lo), jnp.where(down, cand, hi)
            return lo, n_lo, hi, done | ((n_lo == topk) | (hi == lo + 1)).astype(jnp.int32)

        def cond(state):
            return jnp.min(state[4]) == 0

        def body(state):
            it, lo, n_lo, hi, done = state
            lo, n_lo, hi, done = step(it, lo, n_lo, hi, done)
            lo, n_lo, hi, done = step(it + 1, lo, n_lo, hi, done)
            return it + 2, lo, n_lo, hi, done

        _, lo, n_lo, _, _ = lax.while_loop(cond, body, (jnp.int32(0), lo, n_lo, hi, done))
        small = n_adm <= topk
        return jnp.where(small, jnp.int32(INT_MIN + 1), lo), jnp.where(small, 0, n_lo)

    tau, n_ge = lax.cond((qb + 1) * T > topk, search,
                         lambda: (jnp.full((1, T), INT_MIN + 1, jnp.int32), jnp.zeros((1, T), jnp.int32)))

    @pl.when(jnp.max(n_ge) > topk)
    def _():
        need = (topk - count(lambda k: k > tau)).astype(F32)
        cnt_scr[...] = jnp.zeros_like(cnt_scr)

        def tie_body(kt, c):
            key = key_scr[kt]
            eq = key == tau
            eq_f = jnp.where(eq, 1.0, 0.0)
            pref = _dot(ltri_ref[...], eq_f.astype(BF16)) + cnt_scr[...]
            key_scr[kt] = jnp.where(eq & (pref > need), tau - 1, key)
            cnt_scr[...] = cnt_scr[...] + rows_to_one(eq_f, jnp.sum)
            return c

        lax.fori_loop(0, n_kt, tie_body, 0)

    q_h = [q_ref[:, h * HD:(h + 1) * HD] for h in range(H)]

    def logits_body(kt, m_run):
        sel = key_scr[kt] >= tau
        kinds = [jnp.clip(qb - (kt * SUBS + j), 0, 2) for j in range(SUBS)]
        new_m = []
        for h in range(H):
            bias = jnp.concatenate([bias_ref[kinds[j], h] for j in range(SUBS)], axis=0)
            s = _dot_nt(k_ref[key_rows(kt), h * HD:(h + 1) * HD], q_h[h]) + bias
            s = jnp.where(sel, s, NEG_BIG)
            s_scr[h, kt] = s
            new_m.append(jnp.maximum(m_run[h], fold(s, jnp.max)))
        return tuple(new_m)

    m_run = lax.fori_loop(0, n_kt, logits_body,
                          tuple(jnp.full((8, T), NEG_BIG, F32) for _ in range(H)))
    m_h = [rows_to_one(m, jnp.max) for m in m_run]

    for h in range(H):
        acc_scr[h] = jnp.zeros((HD, T), F32)

    def pv_body(kt, l_run):
        new_l = []
        for h in range(H):
            p = jnp.exp(s_scr[h, kt] - m_h[h])
            new_l.append(l_run[h] + fold(p, jnp.sum))
            acc_scr[h] = acc_scr[h] + _dot(vt_ref[kt, h * HD:(h + 1) * HD, :], p.astype(BF16))
        return tuple(new_l)

    l_run = lax.fori_loop(0, n_kt, pv_body, tuple(jnp.zeros((8, T), F32) for _ in range(H)))

    for h in range(H):
        out_t = acc_scr[h] / rows_to_one(l_run[h], jnp.sum)
        o_ref[:, h * HD:(h + 1) * HD] = out_t.T.astype(o_ref.dtype)


def _dsa(p16, p32, bias_tiles, bsz, seq):
    T = DSA_QBLOCK
    tokens = p16.shape[0]
    nqb = seq // T
    topk = min(DSA_TOPK_MAX, seq // 4)
    TK = min(DSA_KTILE, seq)
    assert TK == DSA_KTILE and seq % TK == 0
    nkt = seq // TK
    ltri = jnp.asarray(np.tril(np.ones((TK, TK), np.float32))).astype(BF16)
    w = D_BRANCH
    v_t = jnp.swapaxes(p16[:, C16_V:C16_V + w].reshape(bsz, nkt, TK, w), 2, 3)
    iw_t = jnp.zeros((8, tokens), F32).at[:DSA_IDX_HEADS].set(
        p32[:, C32_IW:C32_IW + DSA_IDX_HEADS].T)
    qmap = lambda cb: (lambda b, i: (b * nqb + i, cb))
    return pl.pallas_call(
        functools.partial(_dsa_kernel, topk=topk),
        grid=(bsz, nqb),
        in_specs=[
            pl.BlockSpec((T, w), qmap(C16_Q // w)),
            pl.BlockSpec((T, 2 * LANES), qmap(C16_IQ // (2 * LANES))),
            pl.BlockSpec((8, T), lambda b, i: (0, b * nqb + i)),
            pl.BlockSpec((seq, w), lambda b, i: (b, C16_K // w)),
            pl.BlockSpec((None, nkt, w, TK), lambda b, i: (b, 0, 0, 0)),
            pl.BlockSpec((seq, LANES), lambda b, i: (b, C16_IK // LANES)),
            pl.BlockSpec(bias_tiles.shape, lambda b, i: (0, 0, 0, 0)),
            pl.BlockSpec(ltri.shape, lambda b, i: (0, 0)),
        ],
        out_specs=pl.BlockSpec((T, w), lambda b, i: (b * nqb + i, 0)),
        out_shape=jax.ShapeDtypeStruct((tokens, w), BF16),
        scratch_shapes=[
            pltpu.VMEM((nkt, TK, T), jnp.int32),
            pltpu.VMEM((DSA_HEADS, nkt, TK, T), F32),
            pltpu.VMEM((DSA_HEADS, DSA_HDIM, T), F32),
            pltpu.VMEM((1, T), F32),
        ],
        compiler_params=pltpu.CompilerParams(
            dimension_semantics=("parallel", "arbitrary"), vmem_limit_bytes=VMEM_LIMIT),
        name="dsa_branch",
    )(p16, p16, iw_t, p16, v_t, p16, bias_tiles, ltri)


def _merge_kernel(ya_ref, yb_ref, yc_ref, ga_ref, gb_ref, gc_ref, x_ref, mod_ref, wbr_ref, wout_ref,
                  gn2_ref, wr_hi_ref, wr_lo_ref, br_ref, x1_ref, h2_ref, lg_ref):
    merged = ga_ref[...] * _dot(ya_ref[...], wbr_ref[0])
    merged = merged + gb_ref[...] * _dot(yb_ref[...].astype(BF16), wbr_ref[1])
    merged = merged + gc_ref[...] * _dot(yc_ref[...], wbr_ref[2])
    mix = _dot(merged.astype(BF16), wout_ref[...])
    x1 = x_ref[...] + mod_ref[0, 2:3, :] * mix
    x1_ref[...] = x1
    y = x1 * lax.rsqrt(jnp.mean(x1 * x1, axis=-1, keepdims=True) + EPS) * gn2_ref[...]
    h2 = y * (1.0 + mod_ref[0, 4:5, :]) + mod_ref[0, 3:4, :]
    h2_ref[...] = h2.astype(BF16)
    lg_ref[...] = _dot3(h2, wr_hi_ref[...], wr_lo_ref[...]) + br_ref[...]


def _merge(ya, yb, yc, p32, x2d, mod_l, wbr, wout, gn2, wr_hi, wr_lo, br, seq, tm):
    tokens, d = x2d.shape
    tiles_per_seq = seq // tm
    row = lambda i: (i, 0)
    gate = lambda k: (lambda i: (i, C32_GATES // d + k))
    c2 = lambda i: (0, 0)
    return pl.pallas_call(
        _merge_kernel,
        grid=(tokens // tm,),
        in_specs=[
            pl.BlockSpec((tm, D_BRANCH), row), pl.BlockSpec((tm, D_BRANCH), row),
            pl.BlockSpec((tm, D_BRANCH), row),
            pl.BlockSpec((tm, d), gate(0)), pl.BlockSpec((tm, d), gate(1)), pl.BlockSpec((tm, d), gate(2)),
            pl.BlockSpec((tm, d), row),
            pl.BlockSpec((1, 6, d), lambda i: (i // tiles_per_seq, 0, 0)),
            pl.BlockSpec(wbr.shape, lambda i: (0, 0, 0)),
            pl.BlockSpec(wout.shape, c2),
            pl.BlockSpec(gn2.shape, c2),
            pl.BlockSpec(wr_hi.shape, c2), pl.BlockSpec(wr_lo.shape, c2), pl.BlockSpec(br.shape, c2),
        ],
        out_specs=[pl.BlockSpec((tm, d), row), pl.BlockSpec((tm, d), row), pl.BlockSpec((tm, LANES), row)],
        out_shape=[jax.ShapeDtypeStruct((tokens, d), F32), jax.ShapeDtypeStruct((tokens, d), BF16),
                   jax.ShapeDtypeStruct((tokens, LANES), F32)],
        compiler_params=pltpu.CompilerParams(
            dimension_semantics=("parallel",), vmem_limit_bytes=VMEM_LIMIT),
        name="merge_norm_router",
    )(ya, yb, yc, p32, p32, p32, x2d, mod_l, wbr, wout, gn2, wr_hi, wr_lo, br)


def _route(lg):
    lane = lax.broadcasted_iota(jnp.int32, lg.shape, 1)
    big = jnp.int32(10 ** 6)
    is_grp = lane < N_GROUPS
    gl = jnp.where(is_grp, lg, -jnp.inf)
    gmax = jnp.max(gl, axis=1, keepdims=True)
    gsum = jnp.sum(jnp.where(is_grp, jnp.exp(lg - gmax), 0.0), axis=1, keepdims=True)
    p_g = 1.0 / gsum
    g_idx = jnp.min(jnp.where(gl == gmax, lane, big), axis=1, keepdims=True)
    first = N_GROUPS + g_idx * EXPERTS_PER_GROUP
    in_grp = (lane >= first) & (lane < first + EXPERTS_PER_GROUP)
    e1 = jnp.where(in_grp, lg, -jnp.inf)
    v1 = jnp.max(e1, axis=1, keepdims=True)
    i1 = jnp.min(jnp.where(e1 == v1, lane, big), axis=1, keepdims=True)
    e2 = jnp.where(in_grp & (lane != i1), lg, -jnp.inf)
    v2 = jnp.max(e2, axis=1, keepdims=True)
    i2 = jnp.min(jnp.where(e2 == v2, lane, big), axis=1, keepdims=True)
    t = jnp.exp(v2 - v1)
    w1 = p_g * (1.0 / (1.0 + t))
    w2 = p_g * (t / (1.0 + t))
    return jnp.where(lane == i1, w1, jnp.where(lane == i2, w2, 0.0))


def _moe_kernel(h_ref, lg_ref, x1_ref, mod_ref, wg_ref, wu_ref, wd_ref, o_ref, comb_scr, acc_scr):
    e = pl.program_id(1)

    @pl.when(e == 0)
    def _():
        comb_scr[...] = _route(lg_ref[...])
        acc_scr[...] = jnp.zeros_like(acc_scr)

    lane = lax.broadcasted_iota(jnp.int32, comb_scr.shape, 1)
    cw = jnp.sum(jnp.where(lane == e + N_GROUPS, comb_scr[...], 0.0), axis=1, keepdims=True)
    h = h_ref[...]
    act = jax.nn.silu(_dot(h, wg_ref[...])) * _dot(h, wu_ref[...]) * cw
    acc_scr[...] += _dot(act.astype(BF16), wd_ref[...])

    @pl.when(e == pl.num_programs(1) - 1)
    def _():
        o_ref[...] = x1_ref[...] + mod_ref[0, 5:6, :] * acc_scr[...]


def _moe(h2, lg, x1, mod_l, wg, wu, wd, seq, tm):
    tokens, d = x1.shape
    tiles_per_seq = seq // tm
    row = lambda i, e: (i, 0)
    return pl.pallas_call(
        _moe_kernel,
        grid=(tokens // tm, N_EXPERTS),
        in_specs=[
            pl.BlockSpec((tm, d), row), pl.BlockSpec((tm, LANES), row), pl.BlockSpec((tm, d), row),
            pl.BlockSpec((1, 6, d), lambda i, e: (i // tiles_per_seq, 0, 0)),
            pl.BlockSpec((None, d, D_EXPERT), lambda i, e: (e, 0, 0)),
            pl.BlockSpec((None, d, D_EXPERT), lambda i, e: (e, 0, 0)),
            pl.BlockSpec((None, D_EXPERT, d), lambda i, e: (e, 0, 0)),
        ],
        out_specs=pl.BlockSpec((tm, d), row),
        out_shape=jax.ShapeDtypeStruct((tokens, d), F32),
        scratch_shapes=[pltpu.VMEM((tm, LANES), F32), pltpu.VMEM((tm, d), F32)],
        compiler_params=pltpu.CompilerParams(
            dimension_semantics=("parallel", "arbitrary"), vmem_limit_bytes=VMEM_LIMIT),
        name="hier_moe",
    )(h2, lg, x1, mod_l, wg, wu, wd)


def _row_tile(seq, want):
    t = min(want, seq)
    assert seq % t == 0
    return t


def kernel(x, c, w_mod, b_mod, g_norm1, g_norm2, w_in, gmlp_ln_g, gmlp_ln_b, gmlp_w_s, gmlp_b_s,
           gla_w_gate2, gla_b_gate, gla_norm_g, dsa_qnorm_g, dsa_knorm_g, rel_bias, w_branch,
           b_branch_gate, w_out, w_group, b_group, w_router, b_router, w_exp_gate, w_exp_up,
           w_exp_down):
    bsz, seq, d = x.shape
    depth = w_mod.shape[0]
    assert d == D_MODEL and seq % DSA_QBLOCK == 0 and seq % GLA_CHUNK == 0
    tokens = bsz * seq
    tm = _row_tile(seq, 512)

    mods = _modulation(c, w_mod, b_mod)
    bias_tiles = _bias_tiles(rel_bias)
    w32, w16 = _prep_w_in(w_in)

    causal = np.tril(np.ones((GMLP_CHUNK, GMLP_CHUNK), bool))
    w_tril = jnp.where(causal[None, None], gmlp_w_s, 0.0).astype(BF16)
    gmlp_bias = jnp.repeat(jnp.swapaxes(gmlp_b_s, 1, 2), LANES, axis=-1)
    w2p = jnp.zeros((depth, LANES, GLA_HEADS * GLA_DK), F32).at[:, :GLA_GATE_RANK].set(gla_w_gate2)
    w2p = w2p.astype(BF16)
    wr = jnp.zeros((depth, d, LANES), F32)
    wr = wr.at[:, :, :N_GROUPS].set(w_group).at[:, :, N_GROUPS:N_GROUPS + N_EXPERTS].set(w_router)
    wr_hi = wr.astype(BF16)
    wr_lo = (wr - wr_hi.astype(F32)).astype(BF16)
    br = jnp.zeros((depth, 1, LANES), F32)
    br = br.at[:, 0, :N_GROUPS].set(b_group).at[:, 0, N_GROUPS:N_GROUPS + N_EXPERTS].set(b_router)
    wbr = w_branch.astype(BF16)
    wout = w_out.astype(BF16)
    wg, wu, wd = w_exp_gate.astype(BF16), w_exp_up.astype(BF16), w_exp_down.astype(BF16)

    x2d = x.reshape(tokens, d)
    for l in range(depth):
        mod_l = mods[l]
        aux32 = [gmlp_ln_g[l][None], gmlp_ln_b[l][None], b_branch_gate[l].reshape(1, -1)]
        aux16 = [dsa_qnorm_g[l][None], dsa_knorm_g[l][None]]
        p32, p16 = _norm_proj(x2d, mod_l, g_norm1[l][None], w32[l], w16[l], aux32, aux16, seq, tm)
        ya = _gmlp(p32, w_tril[l], gmlp_bias[l], tm)
        yb = _gla(p32, p16, w2p[l], gla_b_gate[l][None], gla_norm_g[l][None], bsz, seq, tm)
        yc = _dsa(p16, p32, bias_tiles, bsz, seq)
        x1, h2, lg = _merge(ya, yb, yc, p32, x2d, mod_l, wbr[l], wout[l], g_norm2[l][None],
                            wr_hi[l], wr_lo[l], br[l], seq, tm)
        x2d = _moe(h2, lg, x1, mod_l, wg[l], wu[l], wd[l], seq, tm)
    return x2d.reshape(bsz, seq, d)
```

```python
import functools
import math

import numpy as np
import jax
import jax.numpy as jnp
from jax import lax
from jax.experimental import pallas as pl
from jax.experimental.pallas import tpu as pltpu

F32 = jnp.float32
BF16 = jnp.bfloat16

D_MODEL = 1024
D_BRANCH = 512
EPS = 1e-6
GMLP_CHUNK = 128
GMLP_GROUPS = 4
GLA_HEADS = 4
GLA_DK = 64
GLA_DV = 128
GLA_GATE_RANK = 16
GLA_GATE_TAU = 16.0
GLA_CHUNK = 64
GLA_SUB = 16
DSA_HEADS = 4
DSA_HDIM = 128
DSA_IDX_HEADS = 4
DSA_IDX_DIM = 64
DSA_QBLOCK = 128
DSA_KTILE = 512
DSA_TOPK_MAX = 256
N_BUCKETS = 32
MAX_DISTANCE = 128
N_GROUPS = 4
EXPERTS_PER_GROUP = 4
N_EXPERTS = 16
D_EXPERT = 256

LANES = 128
COL_TILE = 512
VMEM_LIMIT = 56 * 1024 * 1024
INT_MIN = -(2 ** 31)
NEG_BIG = -1e30

C32_U, C32_V, C32_GQ, C32_GK, C32_R, C32_GATES, C32_SMALL = 0, 512, 1024, 1280, 1536, 2048, 5120
C32_GA, C32_IW = C32_SMALL, C32_SMALL + GLA_GATE_RANK
N32 = C32_SMALL + LANES
C16_Q, C16_K, C16_V, C16_GV, C16_IQ, C16_IK = 0, 512, 1024, 1536, 2048, 2304
N16 = 2560


def _dot(a, b):
    return jnp.dot(a, b, preferred_element_type=F32)


def _dot_nt(a, b):
    return lax.dot_general(a, b, (((1,), (1,)), ((), ())), preferred_element_type=F32)


def _dot_tn(a, b):
    return lax.dot_general(a, b, (((0,), (0,)), ((), ())), preferred_element_type=F32)


def _split2(a):
    hi = a.astype(BF16)
    lo = (a - hi.astype(F32)).astype(BF16)
    return hi, lo


def _dot3(a, w_hi, w_lo):
    a_hi, a_lo = _split2(a)
    return _dot(a_hi, w_hi) + (_dot(a_lo, w_hi) + _dot(a_hi, w_lo))


def _head_rms(y, g, scale):
    outs = []
    for h in range(y.shape[1] // LANES):
        yh = y[:, h * LANES:(h + 1) * LANES]
        ms = jnp.mean(yh * yh, axis=-1, keepdims=True)
        o = yh * lax.rsqrt(ms + EPS) * g
        if scale != 1.0:
            o = o * scale
        outs.append(o)
    return jnp.concatenate(outs, axis=1)


def _mod_kernel(c_ref, w_ref, b_ref, o_ref):
    a = jax.nn.silu(c_ref[...])
    w_hi, w_lo = _split2(w_ref[...])
    o_ref[...] = _dot3(a, w_hi, w_lo) + b_ref[...]


def _modulation(c, w_mod, b_mod):
    L, d, n = w_mod.shape
    bsz = c.shape[0]
    rows = 8 * pl.cdiv(bsz, 8)
    c_pad = jnp.zeros((rows, d), F32).at[:bsz].set(c)
    tn = 1536
    out = pl.pallas_call(
        _mod_kernel,
        grid=(L, n // tn),
        in_specs=[
            pl.BlockSpec((rows, d), lambda l, j: (0, 0)),
            pl.BlockSpec((None, d, tn), lambda l, j: (l, 0, j)),
            pl.BlockSpec((None, 1, tn), lambda l, j: (l, 0, j)),
        ],
        out_specs=pl.BlockSpec((None, rows, tn), lambda l, j: (l, 0, j)),
        out_shape=jax.ShapeDtypeStruct((L, rows, n), F32),
        compiler_params=pltpu.CompilerParams(
            dimension_semantics=("arbitrary", "arbitrary"), vmem_limit_bytes=VMEM_LIMIT),
        name="adaln_modulation",
    )(c_pad, w_mod, b_mod.reshape(L, 1, n))
    return out[:, :bsz].reshape(L, bsz, 6, d)


def _t5_bucket_table():
    n = np.arange(2 * DSA_QBLOCK)
    max_exact = N_BUCKETS // 2
    large = max_exact + (
        np.log(np.maximum(n, max_exact).astype(np.float32) / max_exact)
        / math.log(MAX_DISTANCE / max_exact) * (N_BUCKETS - max_exact)).astype(np.int32)
    large = np.minimum(large, N_BUCKETS - 1)
    return np.where(n < max_exact, n, large).astype(np.int32)


def _bias_kernel(rb_ref, bucket_ref, o_ref):
    for t in range(bucket_ref.shape[0]):
        bucket = bucket_ref[t]
        for h in range(DSA_HEADS):
            acc = jnp.zeros(bucket.shape, F32)
            for b in range(N_BUCKETS):
                acc = jnp.where(bucket == b, rb_ref[b, h], acc)
            o_ref[t, h] = acc


def _bias_tiles(rel_bias):
    table = _t5_bucket_table()
    assert (table[MAX_DISTANCE:] == N_BUCKETS - 1).all()
    t = np.arange(DSA_QBLOCK)[None, :]
    s = np.arange(DSA_QBLOCK)[:, None]
    diag = table[np.maximum(t - s, 0)]
    near = table[DSA_QBLOCK + t - s]
    far = np.full_like(diag, N_BUCKETS - 1)
    buckets = jnp.asarray(np.stack([diag, near, far]).astype(np.int32))
    return pl.pallas_call(
        _bias_kernel,
        in_specs=[pl.BlockSpec(memory_space=pltpu.SMEM), pl.BlockSpec(memory_space=pltpu.VMEM)],
        out_specs=pl.BlockSpec(memory_space=pltpu.VMEM),
        out_shape=jax.ShapeDtypeStruct((3, DSA_HEADS, DSA_QBLOCK, DSA_QBLOCK), F32),
        name="t5_bias_tiles",
    )(rel_bias, buckets)


def _proj_kernel(x_ref, mod_ref, gn_ref, w32_ref, w16_ref, *rest):
    aux32, aux16 = rest[:3], rest[3:5]
    o32_ref, o16_ref = rest[5], rest[6]
    x = x_ref[...]
    y = x * lax.rsqrt(jnp.mean(x * x, axis=-1, keepdims=True) + EPS) * gn_ref[...]
    h = (y * (1.0 + mod_ref[0, 1:2, :]) + mod_ref[0, 0:1, :]).astype(BF16)
    for w_ref, o_ref, epilogues, aux in ((w32_ref, o32_ref, EPILOGUES_32, aux32),
                                         (w16_ref, o16_ref, EPILOGUES_16, aux16)):
        col = 0
        for width, epi in epilogues:
            cols = slice(col, col + width)
            o_ref[:, cols] = epi(_dot(h, w_ref[:, cols]), aux).astype(o_ref.dtype)
            col += width
        assert col == o_ref.shape[1]


def _norm_proj(x2d, mod_l, gn, w32, w16, aux32, aux16, seq, tm):
    tokens, d = x2d.shape
    tiles_per_seq = seq // tm
    const = lambda a: pl.BlockSpec(a.shape, lambda i: (0, 0))
    resident = lambda a: pl.BlockSpec(a.shape, lambda i: (0, 0), pipeline_mode=pl.Buffered(1))
    return pl.pallas_call(
        _proj_kernel,
        grid=(tokens // tm,),
        in_specs=[
            pl.BlockSpec((tm, d), lambda i: (i, 0)),
            pl.BlockSpec((1, 6, d), lambda i: (i // tiles_per_seq, 0, 0)),
            const(gn), resident(w32), resident(w16),
        ] + [const(a) for a in aux32 + aux16],
        out_specs=[pl.BlockSpec((tm, N32), lambda i: (i, 0)), pl.BlockSpec((tm, N16), lambda i: (i, 0))],
        out_shape=[jax.ShapeDtypeStruct((tokens, N32), F32), jax.ShapeDtypeStruct((tokens, N16), BF16)],
        compiler_params=pltpu.CompilerParams(
            dimension_semantics=("parallel",), vmem_limit_bytes=VMEM_LIMIT),
        name="norm_proj",
    )(x2d, mod_l, gn, w32, w16, *aux32, *aux16)


def _epi_raw(y, aux):
    return y


def _epi_gelu(y, aux):
    return jax.nn.gelu(y)


def _epi_gelu_ln(y, aux):
    v = jax.nn.gelu(y)
    mu = jnp.mean(v, axis=-1, keepdims=True)
    var = jnp.mean(jnp.square(v - mu), axis=-1, keepdims=True)
    return (v - mu) * lax.rsqrt(var + EPS) * aux[0][...] + aux[1][...]


def _epi_silu(y, aux):
    return jax.nn.silu(y)


def _epi_gate(k):
    def epi(y, aux):
        return jax.nn.sigmoid(y + aux[2][:, k * COL_TILE:(k + 1) * COL_TILE])
    return epi


def _epi_qnorm(y, aux):
    return _head_rms(y, aux[0][...], DSA_HDIM ** -0.5)


def _epi_knorm(y, aux):
    return _head_rms(y, aux[1][...], 1.0)


EPILOGUES_32 = ([(COL_TILE, e) for e in (_epi_gelu, _epi_gelu_ln, _epi_raw, _epi_silu)]
                + [(COL_TILE, _epi_gate(k)) for k in range(6)] + [(LANES, _epi_raw)])
EPILOGUES_16 = [(COL_TILE, e) for e in (_epi_qnorm, _epi_knorm, _epi_raw, _epi_raw, _epi_raw)]


def _prep_w_in(w_in):
    sizes = (512, 512, 256, 256, 512, 512, 16, 512, 512, 512, 256, 64, 4, 3072)
    offs = np.concatenate([[0], np.cumsum(sizes)])
    seg = lambda k: w_in[:, :, offs[k]:offs[k + 1]]
    (a_u, a_v, g_q, g_k, g_v, g_r, g_a, d_q, d_k, d_v, d_iq, d_ik, d_iw, gates) = [seg(k) for k in range(14)]
    L, d, _ = w_in.shape
    zeros = lambda n: jnp.zeros((L, d, n), w_in.dtype)
    w32 = jnp.concatenate([a_u, a_v, g_q, g_k, g_r, gates, g_a, d_iw,
                           zeros(N32 - C32_SMALL - 20)], axis=-1).astype(BF16)
    w16 = jnp.concatenate([d_q, d_k, d_v, g_v, d_iq, d_ik, d_ik, zeros(N16 - C16_IK - 128)],
                          axis=-1).astype(BF16)
    return w32, w16


def _gmlp_kernel(u_ref, v_ref, w_ref, b_ref, o_ref):
    tm = u_ref.shape[0]
    for c in range(tm // GMLP_CHUNK):
        rows = slice(c * GMLP_CHUNK, (c + 1) * GMLP_CHUNK)
        for g in range(GMLP_GROUPS):
            cols = slice(g * LANES, (g + 1) * LANES)
            mixed = _dot(w_ref[g], v_ref[rows, cols].astype(BF16)) + b_ref[:, cols]
            o_ref[rows, cols] = (u_ref[rows, cols] * mixed).astype(o_ref.dtype)


def _gmlp(p32, w_tril, bias_full, tm):
    tokens = p32.shape[0]
    return pl.pallas_call(
        _gmlp_kernel,
        grid=(tokens // tm,),
        in_specs=[
            pl.BlockSpec((tm, D_BRANCH), lambda i: (i, C32_U // D_BRANCH)),
            pl.BlockSpec((tm, D_BRANCH), lambda i: (i, C32_V // D_BRANCH)),
            pl.BlockSpec(w_tril.shape, lambda i: (0, 0, 0)),
            pl.BlockSpec(bias_full.shape, lambda i: (0, 0)),
        ],
        out_specs=pl.BlockSpec((tm, D_BRANCH), lambda i: (i, 0)),
        out_shape=jax.ShapeDtypeStruct((tokens, D_BRANCH), BF16),
        compiler_params=pltpu.CompilerParams(
            dimension_semantics=("parallel",), vmem_limit_bytes=VMEM_LIMIT),
        name="gmlp_branch",
    )(p32, p32, w_tril, bias_full)


def _gla_kernel(qk_ref, v_ref, r_ref, ga_ref, w2_ref, b2_ref, ng_ref, tril_ref, exp_ref,
                o_ref, st_ref, d_scr):
    C, SUB, H, DK, DV = GLA_CHUNK, GLA_SUB, GLA_HEADS, GLA_DK, GLA_DV
    HK = H * DK

    @pl.when(pl.program_id(1) == 0)
    def _():
        st_ref[...] = jnp.zeros_like(st_ref)

    lane = lax.broadcasted_iota(jnp.int32, (1, HK), 1)
    head_mask = [(lane >= h * DK) & (lane < (h + 1) * DK) for h in range(H)]
    row = lax.broadcasted_iota(jnp.int32, (C, C), 0)
    col = lax.broadcasted_iota(jnp.int32, (C, C), 1)
    sub_shift = SUB.bit_length() - 1
    blk_lower = (row >> sub_shift) > (col >> sub_shift)
    sub_t = lax.broadcasted_iota(jnp.int32, (SUB, 1), 0)

    def chunk(ci, carry):
        r0 = pl.multiple_of(ci * C, C)
        rows = pl.ds(r0, C)
        q = qk_ref[rows, 0:HK] * (DK ** -0.5)
        k = qk_ref[rows, HK:2 * HK]
        v = v_ref[rows, :]
        graw = _dot(ga_ref[rows, 0:LANES].astype(BF16), w2_ref[...]) + b2_ref[...]
        g = jax.nn.log_sigmoid(graw) / GLA_GATE_TAU
        g_hi = g.astype(BF16)
        g_r1 = g - g_hi.astype(F32)
        g_mid = g_r1.astype(BF16)
        g_lo = (g_r1 - g_mid.astype(F32)).astype(BF16)
        tril = tril_ref[...]
        b = _dot(tril, g_hi) + (_dot(tril, g_mid) + _dot(tril, g_lo))
        b_last = b[C - 1:C, :]
        st = st_ref[...]

        q_in = q * jnp.exp(b)
        k_dec = (k * jnp.exp(b_last - b)).astype(BF16)

        a_off = [jnp.zeros((C, C), F32) for _ in range(H)]
        for j in range(C // SUB - 1):
            bj = b[(j + 1) * SUB - 1:(j + 1) * SUB, :]
            qj = q * jnp.exp(jnp.minimum(b - bj, 0.0))
            in_blk = (lax.broadcasted_iota(jnp.int32, (C, 1), 0) >> sub_shift) == j
            kj = jnp.where(in_blk, k * jnp.exp(jnp.minimum(bj - b, 0.0)), 0.0).astype(BF16)
            for h in range(H):
                a_off[h] = a_off[h] + _dot_nt(jnp.where(head_mask[h], qj, 0.0).astype(BF16), kj)

        for i in range(C // SUB):
            rs = slice(i * SUB, (i + 1) * SUB)
            qi, bi = q[rs, :], b[rs, :]
            for s in range(SUB):
                ks = k[i * SUB + s:i * SUB + s + 1, :]
                bs = b[i * SUB + s:i * SUB + s + 1, :]
                dterm = qi * ks * jnp.exp(jnp.minimum(bi - bs, 0.0))
                dterm = jnp.where(sub_t >= s, dterm, 0.0)
                d_scr[s * SUB:(s + 1) * SUB, :] = dterm.astype(BF16)
            gsum = _dot(d_scr[...], exp_ref[...])
            od = jnp.zeros((SUB, H * DV), F32)
            for s in range(SUB):
                vs = v[i * SUB + s:i * SUB + s + 1, :].astype(F32)
                od = od + gsum[s * SUB:(s + 1) * SUB, :] * vs
            o_ref[pl.ds(r0 + i * SUB, SUB), :] = od

        outs = []
        for h in range(H):
            vh = v[:, h * DV:(h + 1) * DV]
            o_inter = _dot_nt(jnp.where(head_mask[h], q_in, 0.0).astype(BF16), st.astype(BF16))
            a_h = jnp.where(blk_lower, a_off[h], 0.0).astype(BF16)
            outs.append(o_inter + _dot(a_h, vh))
        o = o_ref[rows, :] + jnp.concatenate(outs, axis=1)

        upd = _dot_tn(v, k_dec)
        new_st = st * jnp.exp(b_last)
        for h in range(H):
            new_st = new_st + jnp.where(head_mask[h], upd[h * DV:(h + 1) * DV, :], 0.0)
        st_ref[...] = new_st

        y = _head_rms(o, ng_ref[...], 1.0) * r_ref[rows, :]
        o_ref[rows, :] = y
        return carry

    lax.fori_loop(0, qk_ref.shape[0] // C, chunk, 0)


def _gla(p32, p16, w2p, b2, norm_g, bsz, seq, ts):
    C, SUB, H, DK, DV = GLA_CHUNK, GLA_SUB, GLA_HEADS, GLA_DK, GLA_DV
    tokens = p32.shape[0]
    blocks_per_seq = seq // ts
    tril = jnp.asarray(np.tril(np.ones((C, C), np.float32))).astype(BF16)
    expand = np.zeros((H * DK, H * DV), np.float32)
    for h in range(H):
        expand[h * DK:(h + 1) * DK, h * DV:(h + 1) * DV] = 1.0
    expand = jnp.asarray(expand).astype(BF16)
    row_map = lambda cb: (lambda b, i: (b * blocks_per_seq + i, cb))
    const2 = lambda b, i: (0, 0)
    out = pl.pallas_call(
        _gla_kernel,
        grid=(bsz, blocks_per_seq),
        in_specs=[
            pl.BlockSpec((ts, 2 * H * DK), row_map(C32_GQ // (2 * H * DK))),
            pl.BlockSpec((ts, H * DV), row_map(C16_GV // (H * DV))),
            pl.BlockSpec((ts, H * DV), row_map(C32_R // (H * DV))),
            pl.BlockSpec((ts, LANES), row_map(C32_SMALL // LANES)),
            pl.BlockSpec(w2p.shape, const2),
            pl.BlockSpec(b2.shape, const2),
            pl.BlockSpec(norm_g.shape, const2),
            pl.BlockSpec(tril.shape, const2),
            pl.BlockSpec(expand.shape, const2),
        ],
        out_specs=pl.BlockSpec((ts, H * DV), lambda b, i: (b * blocks_per_seq + i, 0)),
        out_shape=jax.ShapeDtypeStruct((tokens, H * DV), F32),
        scratch_shapes=[pltpu.VMEM((DV, H * DK), F32), pltpu.VMEM((SUB * SUB, H * DK), BF16)],
        compiler_params=pltpu.CompilerParams(
            dimension_semantics=("parallel", "arbitrary"), vmem_limit_bytes=VMEM_LIMIT),
        name="gla_branch",
    )(p32, p16, p32, p32, w2p, b2, norm_g, tril, expand)
    return out


def _bit_transpose32(words):
    a = list(words)
    shift, mask = 16, 0x0000FFFF
    while shift:
        m = jnp.int32(np.uint32(mask).astype(np.int32))
        for k in range(32):
            if k & shift == 0:
                t = (a[k] ^ lax.shift_right_logical(a[k + shift], jnp.int32(shift))) & m
                a[k] = a[k] ^ t
                a[k + shift] = a[k + shift] ^ lax.shift_left(t, jnp.int32(shift))
        shift >>= 1
        mask = (mask ^ (mask << shift)) & 0xFFFFFFFF
    return a


def _dsa_kernel(q_ref, iq_ref, iwt_ref, k_ref, vt_ref, ik_ref, bias_ref, ltri_ref,
                o_ref, key_scr, plane_scr, s_scr, acc_scr, cnt_scr, *, topk):
    T, TK = DSA_QBLOCK, DSA_KTILE
    SUBS = TK // T
    H, HD = DSA_HEADS, DSA_HDIM
    qb = pl.program_id(1)
    kt_last = qb // SUBS
    n_kt = kt_last + 1
    key_minus_query = (lax.broadcasted_iota(jnp.int32, (TK, T), 0)
                       - lax.broadcasted_iota(jnp.int32, (TK, T), 1))

    def fold(x, op):
        return op(x.reshape(x.shape[0] // 8, 8, T), axis=0)

    def rows_to_one(x, op):
        return op(x, axis=0, keepdims=True)

    def key_rows(kt):
        return pl.ds(pl.multiple_of(kt * TK, TK), TK)

    lo_mask = lax.broadcasted_iota(jnp.int32, (T, LANES), 1) < DSA_IDX_DIM
    iq = [iq_ref[:, 0:LANES], iq_ref[:, LANES:2 * LANES]]
    iq_h = [jnp.where(lo_mask, iq[0], 0), jnp.where(lo_mask, 0, iq[0]),
            jnp.where(lo_mask, iq[1], 0), jnp.where(lo_mask, 0, iq[1])]
    iw_h = [iwt_ref[h:h + 1, :] for h in range(DSA_IDX_HEADS)]

    GRP_ROWS = 32 * 8
    GRPS = TK // GRP_ROWS
    n_grp = plane_scr.shape[0]

    @pl.when(qb == 0)
    def _():
        plane_scr[...] = jnp.zeros_like(plane_scr)

    def score_tile(kt):
        ik2 = ik_ref[key_rows(kt), :]
        score = jnp.zeros((TK, T), F32)
        for h in range(DSA_IDX_HEADS):
            score = score + iw_h[h] * jnp.maximum(_dot_nt(ik2, iq_h[h]), 0.0)
        bits = lax.bitcast_convert_type(score, jnp.int32)
        return jnp.where(score == 0.0, 0, jnp.where(bits >= 0, bits, bits ^ jnp.int32(0x7FFFFFFF)))

    def store_planes(kt, key):
        unsigned_order = key ^ jnp.int32(INT_MIN)
        for g in range(GRPS):
            words = [unsigned_order[g * GRP_ROWS + 8 * j:g * GRP_ROWS + 8 * j + 8, :] for j in range(32)]
            planes = _bit_transpose32(words)
            for b in range(32):
                plane_scr[kt * GRPS + g, b] = planes[b]

    def score_body(kt, c):
        key = score_tile(kt)
        key_scr[kt] = key
        store_planes(kt, key)
        return c

    lax.fori_loop(0, kt_last, score_body, 0)
    admissible = key_minus_query <= qb * T - kt_last * TK
    key = score_tile(kt_last)
    key_scr[kt_last] = jnp.where(admissible, key, jnp.int32(INT_MIN))
    store_planes(kt_last, key)
    alive_last = []
    for g in range(GRPS):
        word = jnp.zeros((8, T), jnp.int32)
        for j in range(32):
            adm = admissible[g * GRP_ROWS + 8 * j:g * GRP_ROWS + 8 * j + 8, :]
            word = word | jnp.where(adm, jnp.int32(np.int32(np.uint32(1 << (31 - j)))), 0)
        alive_last.append(word)

    alive = []
    for g in range(n_grp):
        kt = g // GRPS
        full = jnp.broadcast_to(jnp.where(kt < kt_last, jnp.int32(-1), jnp.int32(0)), (8, T))
        alive.append(jnp.where(kt == kt_last, alive_last[g % GRPS], full))

    def bit_body(b, carry):
        above, tau_u, alive = carry
        planes = [plane_scr[g, b] for g in range(n_grp)]
        ones = jnp.zeros((8, T), jnp.int32)
        for g in range(n_grp):
            ones = ones + lax.population_count(alive[g] & planes[g])
        ones = rows_to_one(ones, jnp.sum)
        take = above + ones >= topk
        flip = jnp.where(take, jnp.int32(0), jnp.int32(-1))
        alive = tuple(alive[g] & (planes[g] ^ flip) for g in range(n_grp))
        above = jnp.where(take, above, above + ones)
        tau_u = tau_u | jnp.where(take, lax.shift_left(jnp.int32(1), 31 - b), 0)
        return above, tau_u, alive

    zero_row = jnp.zeros((1, T), jnp.int32)
    above, tau_u, alive = lax.fori_loop(0, 32, bit_body, (zero_row, zero_row, tuple(alive)))
    n_tied = jnp.zeros((8, T), jnp.int32)
    for g in range(n_grp):
        n_tied = n_tied + lax.population_count(alive[g])
    n_ge = above + rows_to_one(n_tied, jnp.sum)
    tau = jnp.maximum(tau_u ^ jnp.int32(INT_MIN), jnp.int32(INT_MIN + 1))

    @pl.when(jnp.max(n_ge) > topk)
    def _():
        need = (topk - above).astype(F32)
        cnt_scr[...] = jnp.zeros_like(cnt_scr)

        def tie_body(kt, c):
            key = key_scr[kt]
            eq = key == tau
            eq_f = jnp.where(eq, 1.0, 0.0)
            pref = _dot(ltri_ref[...], eq_f.astype(BF16)) + cnt_scr[...]
            key_scr[kt] = jnp.where(eq & (pref > need), tau - 1, key)
            cnt_scr[...] = cnt_scr[...] + rows_to_one(eq_f, jnp.sum)
            return c

        lax.fori_loop(0, n_kt, tie_body, 0)

    q_h = [q_ref[:, h * HD:(h + 1) * HD] for h in range(H)]

    def logits_body(kt, m_run):
        sel = key_scr[kt] >= tau
        kinds = [jnp.clip(qb - (kt * SUBS + j), 0, 2) for j in range(SUBS)]
        new_m = []
        for h in range(H):
            bias = jnp.concatenate([bias_ref[kinds[j], h] for j in range(SUBS)], axis=0)
            s = _dot_nt(k_ref[key_rows(kt), h * HD:(h + 1) * HD], q_h[h]) + bias
            s = jnp.where(sel, s, NEG_BIG)
            s_scr[h, kt] = s
            new_m.append(jnp.maximum(m_run[h], fold(s, jnp.max)))
        return tuple(new_m)

    m_run = lax.fori_loop(0, n_kt, logits_body,
                          tuple(jnp.full((8, T), NEG_BIG, F32) for _ in range(H)))
    m_h = [rows_to_one(m, jnp.max) for m in m_run]

    for h in range(H):
        acc_scr[h] = jnp.zeros((HD, T), F32)

    def pv_body(kt, l_run):
        new_l = []
        for h in range(H):
            p = jnp.exp(s_scr[h, kt] - m_h[h])
            new_l.append(l_run[h] + fold(p, jnp.sum))
            acc_scr[h] = acc_scr[h] + _dot(vt_ref[kt, h * HD:(h + 1) * HD, :], p.astype(BF16))
        return tuple(new_l)

    l_run = lax.fori_loop(0, n_kt, pv_body, tuple(jnp.zeros((8, T), F32) for _ in range(H)))

    for h in range(H):
        out_t = acc_scr[h] / rows_to_one(l_run[h], jnp.sum)
        o_ref[:, h * HD:(h + 1) * HD] = out_t.T.astype(o_ref.dtype)


def _dsa(p16, p32, bias_tiles, bsz, seq):
    T = DSA_QBLOCK
    tokens = p16.shape[0]
    nqb = seq // T
    topk = min(DSA_TOPK_MAX, seq // 4)
    TK = min(DSA_KTILE, seq)
    assert TK == DSA_KTILE and seq % TK == 0
    nkt = seq // TK
    ltri = jnp.asarray(np.tril(np.ones((TK, TK), np.float32))).astype(BF16)
    w = D_BRANCH
    v_t = jnp.swapaxes(p16[:, C16_V:C16_V + w].reshape(bsz, nkt, TK, w), 2, 3)
    iw_t = jnp.zeros((8, tokens), F32).at[:DSA_IDX_HEADS].set(
        p32[:, C32_IW:C32_IW + DSA_IDX_HEADS].T)
    qmap = lambda cb: (lambda b, i: (b * nqb + i, cb))
    return pl.pallas_call(
        functools.partial(_dsa_kernel, topk=topk),
        grid=(bsz, nqb),
        in_specs=[
            pl.BlockSpec((T, w), qmap(C16_Q // w)),
            pl.BlockSpec((T, 2 * LANES), qmap(C16_IQ // (2 * LANES))),
            pl.BlockSpec((8, T), lambda b, i: (0, b * nqb + i)),
            pl.BlockSpec((seq, w), lambda b, i: (b, C16_K // w)),
            pl.BlockSpec((None, nkt, w, TK), lambda b, i: (b, 0, 0, 0)),
            pl.BlockSpec((seq, LANES), lambda b, i: (b, C16_IK // LANES)),
            pl.BlockSpec(bias_tiles.shape, lambda b, i: (0, 0, 0, 0)),
            pl.BlockSpec(ltri.shape, lambda b, i: (0, 0)),
        ],
        out_specs=pl.BlockSpec((T, w), lambda b, i: (b * nqb + i, 0)),
        out_shape=jax.ShapeDtypeStruct((tokens, w), BF16),
        scratch_shapes=[
            pltpu.VMEM((nkt, TK, T), jnp.int32),
            pltpu.VMEM((nkt * TK // 256, 32, 8, T), jnp.int32),
            pltpu.VMEM((DSA_HEADS, nkt, TK, T), F32),
            pltpu.VMEM((DSA_HEADS, DSA_HDIM, T), F32),
            pltpu.VMEM((1, T), F32),
        ],
        compiler_params=pltpu.CompilerParams(
            dimension_semantics=("parallel", "arbitrary"), vmem_limit_bytes=VMEM_LIMIT),
        name="dsa_branch",
    )(p16, p16, iw_t, p16, v_t, p16, bias_tiles, ltri)


def _merge_kernel(ya_ref, yb_ref, yc_ref, ga_ref, gb_ref, gc_ref, x_ref, mod_ref, wbr_ref, wout_ref,
                  gn2_ref, wr_hi_ref, wr_lo_ref, br_ref, x1_ref, h2_ref, lg_ref):
    merged = ga_ref[...] * _dot(ya_ref[...], wbr_ref[0])
    merged = merged + gb_ref[...] * _dot(yb_ref[...].astype(BF16), wbr_ref[1])
    merged = merged + gc_ref[...] * _dot(yc_ref[...], wbr_ref[2])
    mix = _dot(merged.astype(BF16), wout_ref[...])
    x1 = x_ref[...] + mod_ref[0, 2:3, :] * mix
    x1_ref[...] = x1
    y = x1 * lax.rsqrt(jnp.mean(x1 * x1, axis=-1, keepdims=True) + EPS) * gn2_ref[...]
    h2 = y * (1.0 + mod_ref[0, 4:5, :]) + mod_ref[0, 3:4, :]
    h2_ref[...] = h2.astype(BF16)
    lg_ref[...] = _dot3(h2, wr_hi_ref[...], wr_lo_ref[...]) + br_ref[...]


def _merge(ya, yb, yc, p32, x2d, mod_l, wbr, wout, gn2, wr_hi, wr_lo, br, seq, tm):
    tokens, d = x2d.shape
    tiles_per_seq = seq // tm
    row = lambda i: (i, 0)
    gate = lambda k: (lambda i: (i, C32_GATES // d + k))
    c2 = lambda i: (0, 0)
    return pl.pallas_call(
        _merge_kernel,
        grid=(tokens // tm,),
        in_specs=[
            pl.BlockSpec((tm, D_BRANCH), row), pl.BlockSpec((tm, D_BRANCH), row),
            pl.BlockSpec((tm, D_BRANCH), row),
            pl.BlockSpec((tm, d), gate(0)), pl.BlockSpec((tm, d), gate(1)), pl.BlockSpec((tm, d), gate(2)),
            pl.BlockSpec((tm, d), row),
            pl.BlockSpec((1, 6, d), lambda i: (i // tiles_per_seq, 0, 0)),
            pl.BlockSpec(wbr.shape, lambda i: (0, 0, 0)),
            pl.BlockSpec(wout.shape, c2),
            pl.BlockSpec(gn2.shape, c2),
            pl.BlockSpec(wr_hi.shape, c2), pl.BlockSpec(wr_lo.shape, c2), pl.BlockSpec(br.shape, c2),
        ],
        out_specs=[pl.BlockSpec((tm, d), row), pl.BlockSpec((tm, d), row), pl.BlockSpec((tm, LANES), row)],
        out_shape=[jax.ShapeDtypeStruct((tokens, d), F32), jax.ShapeDtypeStruct((tokens, d), BF16),
                   jax.ShapeDtypeStruct((tokens, LANES), F32)],
        compiler_params=pltpu.CompilerParams(
            dimension_semantics=("parallel",), vmem_limit_bytes=VMEM_LIMIT),
        name="merge_norm_router",
    )(ya, yb, yc, p32, p32, p32, x2d, mod_l, wbr, wout, gn2, wr_hi, wr_lo, br)


def _route(lg):
    lane = lax.broadcasted_iota(jnp.int32, lg.shape, 1)
    big = jnp.int32(10 ** 6)
    is_grp = lane < N_GROUPS
    gl = jnp.where(is_grp, lg, -jnp.inf)
    gmax = jnp.max(gl, axis=1, keepdims=True)
    gsum = jnp.sum(jnp.where(is_grp, jnp.exp(lg - gmax), 0.0), axis=1, keepdims=True)
    p_g = 1.0 / gsum
    g_idx = jnp.min(jnp.where(gl == gmax, lane, big), axis=1, keepdims=True)
    first = N_GROUPS + g_idx * EXPERTS_PER_GROUP
    in_grp = (lane >= first) & (lane < first + EXPERTS_PER_GROUP)
    e1 = jnp.where(in_grp, lg, -jnp.inf)
    v1 = jnp.max(e1, axis=1, keepdims=True)
    i1 = jnp.min(jnp.where(e1 == v1, lane, big), axis=1, keepdims=True)
    e2 = jnp.where(in_grp & (lane != i1), lg, -jnp.inf)
    v2 = jnp.max(e2, axis=1, keepdims=True)
    i2 = jnp.min(jnp.where(e2 == v2, lane, big), axis=1, keepdims=True)
    t = jnp.exp(v2 - v1)
    w1 = p_g * (1.0 / (1.0 + t))
    w2 = p_g * (t / (1.0 + t))
    return jnp.where(lane == i1, w1, jnp.where(lane == i2, w2, 0.0))


def _moe_kernel(h_ref, lg_ref, x1_ref, mod_ref, wg_ref, wu_ref, wd_ref, o_ref, comb_scr, acc_scr):
    e = pl.program_id(1)

    @pl.when(e == 0)
    def _():
        comb_scr[...] = _route(lg_ref[...])
        acc_scr[...] = jnp.zeros_like(acc_scr)

    lane = lax.broadcasted_iota(jnp.int32, comb_scr.shape, 1)
    cw = jnp.sum(jnp.where(lane == e + N_GROUPS, comb_scr[...], 0.0), axis=1, keepdims=True)
    h = h_ref[...]
    act = jax.nn.silu(_dot(h, wg_ref[...])) * _dot(h, wu_ref[...]) * cw
    acc_scr[...] += _dot(act.astype(BF16), wd_ref[...])

    @pl.when(e == pl.num_programs(1) - 1)
    def _():
        o_ref[...] = x1_ref[...] + mod_ref[0, 5:6, :] * acc_scr[...]


def _moe(h2, lg, x1, mod_l, wg, wu, wd, seq, tm):
    tokens, d = x1.shape
    tiles_per_seq = seq // tm
    row = lambda i, e: (i, 0)
    return pl.pallas_call(
        _moe_kernel,
        grid=(tokens // tm, N_EXPERTS),
        in_specs=[
            pl.BlockSpec((tm, d), row), pl.BlockSpec((tm, LANES), row), pl.BlockSpec((tm, d), row),
            pl.BlockSpec((1, 6, d), lambda i, e: (i // tiles_per_seq, 0, 0)),
            pl.BlockSpec((None, d, D_EXPERT), lambda i, e: (e, 0, 0)),
            pl.BlockSpec((None, d, D_EXPERT), lambda i, e: (e, 0, 0)),
            pl.BlockSpec((None, D_EXPERT, d), lambda i, e: (e, 0, 0)),
        ],
        out_specs=pl.BlockSpec((tm, d), row),
        out_shape=jax.ShapeDtypeStruct((tokens, d), F32),
        scratch_shapes=[pltpu.VMEM((tm, LANES), F32), pltpu.VMEM((tm, d), F32)],
        compiler_params=pltpu.CompilerParams(
            dimension_semantics=("parallel", "arbitrary"), vmem_limit_bytes=VMEM_LIMIT),
        name="hier_moe",
    )(h2, lg, x1, mod_l, wg, wu, wd)


def _row_tile(seq, want):
    t = min(want, seq)
    assert seq % t == 0
    return t


def kernel(x, c, w_mod, b_mod, g_norm1, g_norm2, w_in, gmlp_ln_g, gmlp_ln_b, gmlp_w_s, gmlp_b_s,
           gla_w_gate2, gla_b_gate, gla_norm_g, dsa_qnorm_g, dsa_knorm_g, rel_bias, w_branch,
           b_branch_gate, w_out, w_group, b_group, w_router, b_router, w_exp_gate, w_exp_up,
           w_exp_down):
    bsz, seq, d = x.shape
    depth = w_mod.shape[0]
    assert d == D_MODEL and seq % DSA_QBLOCK == 0 and seq % GLA_CHUNK == 0
    tokens = bsz * seq
    tm = _row_tile(seq, 512)

    mods = _modulation(c, w_mod, b_mod)
    bias_tiles = _bias_tiles(rel_bias)
    w32, w16 = _prep_w_in(w_in)

    causal = np.tril(np.ones((GMLP_CHUNK, GMLP_CHUNK), bool))
    w_tril = jnp.where(causal[None, None], gmlp_w_s, 0.0).astype(BF16)
    gmlp_bias = jnp.repeat(jnp.swapaxes(gmlp_b_s, 1, 2), LANES, axis=-1)
    w2p = jnp.zeros((depth, LANES, GLA_HEADS * GLA_DK), F32).at[:, :GLA_GATE_RANK].set(gla_w_gate2)
    w2p = w2p.astype(BF16)
    wr = jnp.zeros((depth, d, LANES), F32)
    wr = wr.at[:, :, :N_GROUPS].set(w_group).at[:, :, N_GROUPS:N_GROUPS + N_EXPERTS].set(w_router)
    wr_hi = wr.astype(BF16)
    wr_lo = (wr - wr_hi.astype(F32)).astype(BF16)
    br = jnp.zeros((depth, 1, LANES), F32)
    br = br.at[:, 0, :N_GROUPS].set(b_group).at[:, 0, N_GROUPS:N_GROUPS + N_EXPERTS].set(b_router)
    wbr = w_branch.astype(BF16)
    wout = w_out.astype(BF16)
    wg, wu, wd = w_exp_gate.astype(BF16), w_exp_up.astype(BF16), w_exp_down.astype(BF16)

    x2d = x.reshape(tokens, d)
    for l in range(depth):
        mod_l = mods[l]
        aux32 = [gmlp_ln_g[l][None], gmlp_ln_b[l][None], b_branch_gate[l].reshape(1, -1)]
        aux16 = [dsa_qnorm_g[l][None], dsa_knorm_g[l][None]]
        p32, p16 = _norm_proj(x2d, mod_l, g_norm1[l][None], w32[l], w16[l], aux32, aux16, seq, tm)
        ya = _gmlp(p32, w_tril[l], gmlp_bias[l], tm)
        yb = _gla(p32, p16, w2p[l], gla_b_gate[l][None], gla_norm_g[l][None], bsz, seq, tm)
        yc = _dsa(p16, p32, bias_tiles, bsz, seq)
        x1, h2, lg = _merge(ya, yb, yc, p32, x2d, mod_l, wbr[l], wout[l], g_norm2[l][None],
                            wr_hi[l], wr_lo[l], br[l], seq, tm)
        x2d = _moe(h2, lg, x1, mod_l, wg[l], wu[l], wd[l], seq, tm)
    return x2d.reshape(bsz, seq, d)
```

```python
import functools
import math

import numpy as np
import jax
import jax.numpy as jnp
from jax import lax
from jax.experimental import pallas as pl
from jax.experimental.pallas import tpu as pltpu

F32 = jnp.float32
BF16 = jnp.bfloat16

D_MODEL = 1024
D_BRANCH = 512
EPS = 1e-6
GMLP_CHUNK = 128
GMLP_GROUPS = 4
GLA_HEADS = 4
GLA_DK = 64
GLA_DV = 128
GLA_GATE_RANK = 16
GLA_GATE_TAU = 16.0
GLA_CHUNK = 64
GLA_SUB = 16
DSA_HEADS = 4
DSA_HDIM = 128
DSA_IDX_HEADS = 4
DSA_IDX_DIM = 64
DSA_QBLOCK = 128
DSA_KTILE = 512
DSA_TOPK_MAX = 256
N_BUCKETS = 32
MAX_DISTANCE = 128
N_GROUPS = 4
EXPERTS_PER_GROUP = 4
N_EXPERTS = 16
D_EXPERT = 256

LANES = 128
COL_TILE = 512
VMEM_LIMIT = 56 * 1024 * 1024
INT_MIN = -(2 ** 31)
NEG_BIG = -1e30

C32_U, C32_V, C32_GQ, C32_GK, C32_R, C32_GATES, C32_SMALL = 0, 512, 1024, 1280, 1536, 2048, 5120
C32_GA, C32_IW = C32_SMALL, C32_SMALL + GLA_GATE_RANK
N32 = C32_SMALL + LANES
C16_Q, C16_K, C16_V, C16_GV, C16_IQ, C16_IK = 0, 512, 1024, 1536, 2048, 2304
N16 = 2560


def _dot(a, b):
    return jnp.dot(a, b, preferred_element_type=F32)


def _dot_nt(a, b):
    return lax.dot_general(a, b, (((1,), (1,)), ((), ())), preferred_element_type=F32)


def _dot_tn(a, b):
    return lax.dot_general(a, b, (((0,), (0,)), ((), ())), preferred_element_type=F32)


def _split2(a):
    hi = a.astype(BF16)
    lo = (a - hi.astype(F32)).astype(BF16)
    return hi, lo


def _dot3(a, w_hi, w_lo):
    a_hi, a_lo = _split2(a)
    return _dot(a_hi, w_hi) + (_dot(a_lo, w_hi) + _dot(a_hi, w_lo))


def _head_rms(y, g, scale):
    outs = []
    for h in range(y.shape[1] // LANES):
        yh = y[:, h * LANES:(h + 1) * LANES]
        ms = jnp.mean(yh * yh, axis=-1, keepdims=True)
        o = yh * lax.rsqrt(ms + EPS) * g
        if scale != 1.0:
            o = o * scale
        outs.append(o)
    return jnp.concatenate(outs, axis=1)


def _mod_kernel(c_ref, w_ref, b_ref, o_ref):
    a = jax.nn.silu(c_ref[...])
    w_hi, w_lo = _split2(w_ref[...])
    o_ref[...] = _dot3(a, w_hi, w_lo) + b_ref[...]


def _modulation(c, w_mod, b_mod):
    L, d, n = w_mod.shape
    bsz = c.shape[0]
    rows = 8 * pl.cdiv(bsz, 8)
    c_pad = jnp.zeros((rows, d), F32).at[:bsz].set(c)
    tn = 1536
    out = pl.pallas_call(
        _mod_kernel,
        grid=(L, n // tn),
        in_specs=[
            pl.BlockSpec((rows, d), lambda l, j: (0, 0)),
            pl.BlockSpec((None, d, tn), lambda l, j: (l, 0, j)),
            pl.BlockSpec((None, 1, tn), lambda l, j: (l, 0, j)),
        ],
        out_specs=pl.BlockSpec((None, rows, tn), lambda l, j: (l, 0, j)),
        out_shape=jax.ShapeDtypeStruct((L, rows, n), F32),
        compiler_params=pltpu.CompilerParams(
            dimension_semantics=("arbitrary", "arbitrary"), vmem_limit_bytes=VMEM_LIMIT),
        name="adaln_modulation",
    )(c_pad, w_mod, b_mod.reshape(L, 1, n))
    return out[:, :bsz].reshape(L, bsz, 6, d)


def _t5_bucket_table():
    n = np.arange(2 * DSA_QBLOCK)
    max_exact = N_BUCKETS // 2
    large = max_exact + (
        np.log(np.maximum(n, max_exact).astype(np.float32) / max_exact)
        / math.log(MAX_DISTANCE / max_exact) * (N_BUCKETS - max_exact)).astype(np.int32)
    large = np.minimum(large, N_BUCKETS - 1)
    return np.where(n < max_exact, n, large).astype(np.int32)


def _bias_kernel(rb_ref, bucket_ref, o_ref):
    for t in range(bucket_ref.shape[0]):
        bucket = bucket_ref[t]
        for h in range(DSA_HEADS):
            acc = jnp.zeros(bucket.shape, F32)
            for b in range(N_BUCKETS):
                acc = jnp.where(bucket == b, rb_ref[b, h], acc)
            o_ref[t, h] = acc


def _bias_tiles(rel_bias):
    table = _t5_bucket_table()
    assert (table[MAX_DISTANCE:] == N_BUCKETS - 1).all()
    t = np.arange(DSA_QBLOCK)[None, :]
    s = np.arange(DSA_QBLOCK)[:, None]
    diag = table[np.maximum(t - s, 0)]
    near = table[DSA_QBLOCK + t - s]
    far = np.full_like(diag, N_BUCKETS - 1)
    buckets = jnp.asarray(np.stack([diag, near, far]).astype(np.int32))
    return pl.pallas_call(
        _bias_kernel,
        in_specs=[pl.BlockSpec(memory_space=pltpu.SMEM), pl.BlockSpec(memory_space=pltpu.VMEM)],
        out_specs=pl.BlockSpec(memory_space=pltpu.VMEM),
        out_shape=jax.ShapeDtypeStruct((3, DSA_HEADS, DSA_QBLOCK, DSA_QBLOCK), F32),
        name="t5_bias_tiles",
    )(rel_bias, buckets)


def _proj_kernel(x_ref, mod_ref, gn_ref, w32_ref, w16_ref, *rest):
    aux32, aux16 = rest[:3], rest[3:5]
    o32_ref, o16_ref = rest[5], rest[6]
    x = x_ref[...]
    y = x * lax.rsqrt(jnp.mean(x * x, axis=-1, keepdims=True) + EPS) * gn_ref[...]
    h = (y * (1.0 + mod_ref[0, 1:2, :]) + mod_ref[0, 0:1, :]).astype(BF16)
    for w_ref, o_ref, epilogues, aux in ((w32_ref, o32_ref, EPILOGUES_32, aux32),
                                         (w16_ref, o16_ref, EPILOGUES_16, aux16)):
        col = 0
        for width, epi in epilogues:
            cols = slice(col, col + width)
            o_ref[:, cols] = epi(_dot(h, w_ref[:, cols]), aux).astype(o_ref.dtype)
            col += width
        assert col == o_ref.shape[1]


def _norm_proj(x2d, mod_l, gn, w32, w16, layer, aux32, aux16, seq, tm):
    tokens, d = x2d.shape
    tiles_per_seq = seq // tm
    const = lambda a: pl.BlockSpec(a.shape, lambda i: (0, 0))
    resident = lambda a: pl.BlockSpec((None,) + a.shape[1:], lambda i: (layer, 0, 0),
                                      pipeline_mode=pl.Buffered(1))
    return pl.pallas_call(
        _proj_kernel,
        grid=(tokens // tm,),
        in_specs=[
            pl.BlockSpec((tm, d), lambda i: (i, 0)),
            pl.BlockSpec((1, 6, d), lambda i: (i // tiles_per_seq, 0, 0)),
            const(gn), resident(w32), resident(w16),
        ] + [const(a) for a in aux32 + aux16],
        out_specs=[pl.BlockSpec((tm, N32), lambda i: (i, 0)), pl.BlockSpec((tm, N16), lambda i: (i, 0))],
        out_shape=[jax.ShapeDtypeStruct((tokens, N32), F32), jax.ShapeDtypeStruct((tokens, N16), BF16)],
        compiler_params=pltpu.CompilerParams(
            dimension_semantics=("parallel",), vmem_limit_bytes=VMEM_LIMIT),
        name="norm_proj",
    )(x2d, mod_l, gn, w32, w16, *aux32, *aux16)


def _epi_raw(y, aux):
    return y


def _epi_gelu(y, aux):
    return jax.nn.gelu(y)


def _epi_gelu_ln(y, aux):
    v = jax.nn.gelu(y)
    mu = jnp.mean(v, axis=-1, keepdims=True)
    var = jnp.mean(jnp.square(v - mu), axis=-1, keepdims=True)
    return (v - mu) * lax.rsqrt(var + EPS) * aux[0][...] + aux[1][...]


def _epi_silu(y, aux):
    return jax.nn.silu(y)


def _epi_gate(k):
    def epi(y, aux):
        return jax.nn.sigmoid(y + aux[2][:, k * COL_TILE:(k + 1) * COL_TILE])
    return epi


def _epi_qnorm(y, aux):
    return _head_rms(y, aux[0][...], DSA_HDIM ** -0.5)


def _epi_knorm(y, aux):
    return _head_rms(y, aux[1][...], 1.0)


EPILOGUES_32 = ([(COL_TILE, e) for e in (_epi_gelu, _epi_gelu_ln, _epi_raw, _epi_silu)]
                + [(COL_TILE, _epi_gate(k)) for k in range(6)] + [(LANES, _epi_raw)])
EPILOGUES_16 = [(COL_TILE, e) for e in (_epi_qnorm, _epi_knorm, _epi_raw, _epi_raw, _epi_raw)]


def _prep_w_in(w_in):
    sizes = (512, 512, 256, 256, 512, 512, 16, 512, 512, 512, 256, 64, 4, 3072)
    offs = np.concatenate([[0], np.cumsum(sizes)])
    seg = lambda k: w_in[:, :, offs[k]:offs[k + 1]]
    (a_u, a_v, g_q, g_k, g_v, g_r, g_a, d_q, d_k, d_v, d_iq, d_ik, d_iw, gates) = [seg(k) for k in range(14)]
    L, d, _ = w_in.shape
    zeros = lambda n: jnp.zeros((L, d, n), w_in.dtype)
    w32 = jnp.concatenate([a_u, a_v, g_q, g_k, g_r, gates, g_a, d_iw,
                           zeros(N32 - C32_SMALL - 20)], axis=-1).astype(BF16)
    w16 = jnp.concatenate([d_q, d_k, d_v, g_v, d_iq, d_ik, d_ik, zeros(N16 - C16_IK - 128)],
                          axis=-1).astype(BF16)
    return w32, w16


def _gmlp_kernel(u_ref, v_ref, w_ref, b_ref, o_ref):
    tm = u_ref.shape[0]
    for c in range(tm // GMLP_CHUNK):
        rows = slice(c * GMLP_CHUNK, (c + 1) * GMLP_CHUNK)
        for g in range(GMLP_GROUPS):
            cols = slice(g * LANES, (g + 1) * LANES)
            mixed = _dot(w_ref[g], v_ref[rows, cols].astype(BF16)) + b_ref[:, cols]
            o_ref[rows, cols] = (u_ref[rows, cols] * mixed).astype(o_ref.dtype)


def _gmlp(p32, w_tril, bias_full, tm):
    tokens = p32.shape[0]
    return pl.pallas_call(
        _gmlp_kernel,
        grid=(tokens // tm,),
        in_specs=[
            pl.BlockSpec((tm, D_BRANCH), lambda i: (i, C32_U // D_BRANCH)),
            pl.BlockSpec((tm, D_BRANCH), lambda i: (i, C32_V // D_BRANCH)),
            pl.BlockSpec(w_tril.shape, lambda i: (0, 0, 0)),
            pl.BlockSpec(bias_full.shape, lambda i: (0, 0)),
        ],
        out_specs=pl.BlockSpec((tm, D_BRANCH), lambda i: (i, 0)),
        out_shape=jax.ShapeDtypeStruct((tokens, D_BRANCH), BF16),
        compiler_params=pltpu.CompilerParams(
            dimension_semantics=("parallel",), vmem_limit_bytes=VMEM_LIMIT),
        name="gmlp_branch",
    )(p32, p32, w_tril, bias_full)


def _gla_kernel(qk_ref, v_ref, r_ref, ga_ref, w2_ref, b2_ref, ng_ref, tril_ref, exp_ref,
                o_ref, st_ref, d_scr):
    C, SUB, H, DK, DV = GLA_CHUNK, GLA_SUB, GLA_HEADS, GLA_DK, GLA_DV
    HK = H * DK

    @pl.when(pl.program_id(1) == 0)
    def _():
        st_ref[...] = jnp.zeros_like(st_ref)

    lane = lax.broadcasted_iota(jnp.int32, (1, HK), 1)
    head_mask = [(lane >= h * DK) & (lane < (h + 1) * DK) for h in range(H)]
    row = lax.broadcasted_iota(jnp.int32, (C, C), 0)
    col = lax.broadcasted_iota(jnp.int32, (C, C), 1)
    sub_shift = SUB.bit_length() - 1
    blk_lower = (row >> sub_shift) > (col >> sub_shift)
    sub_t = lax.broadcasted_iota(jnp.int32, (SUB, 1), 0)

    def chunk(ci, carry):
        r0 = pl.multiple_of(ci * C, C)
        rows = pl.ds(r0, C)
        q = qk_ref[rows, 0:HK] * (DK ** -0.5)
        k = qk_ref[rows, HK:2 * HK]
        v = v_ref[rows, :]
        graw = _dot(ga_ref[rows, 0:LANES].astype(BF16), w2_ref[...]) + b2_ref[...]
        g = jax.nn.log_sigmoid(graw) / GLA_GATE_TAU
        g_hi = g.astype(BF16)
        g_r1 = g - g_hi.astype(F32)
        g_mid = g_r1.astype(BF16)
        g_lo = (g_r1 - g_mid.astype(F32)).astype(BF16)
        tril = tril_ref[...]
        b = _dot(tril, g_hi) + (_dot(tril, g_mid) + _dot(tril, g_lo))
        b_last = b[C - 1:C, :]
        st = st_ref[...]

        q_in = q * jnp.exp(b)
        k_dec = (k * jnp.exp(b_last - b)).astype(BF16)

        a_off = [jnp.zeros((C, C), F32) for _ in range(H)]
        for j in range(C // SUB - 1):
            bj = b[(j + 1) * SUB - 1:(j + 1) * SUB, :]
            qj = q * jnp.exp(jnp.minimum(b - bj, 0.0))
            in_blk = (lax.broadcasted_iota(jnp.int32, (C, 1), 0) >> sub_shift) == j
            kj = jnp.where(in_blk, k * jnp.exp(jnp.minimum(bj - b, 0.0)), 0.0).astype(BF16)
            for h in range(H):
                a_off[h] = a_off[h] + _dot_nt(jnp.where(head_mask[h], qj, 0.0).astype(BF16), kj)

        for i in range(C // SUB):
            rs = slice(i * SUB, (i + 1) * SUB)
            qi, bi = q[rs, :], b[rs, :]
            for s in range(SUB):
                ks = k[i * SUB + s:i * SUB + s + 1, :]
                bs = b[i * SUB + s:i * SUB + s + 1, :]
                dterm = qi * ks * jnp.exp(jnp.minimum(bi - bs, 0.0))
                dterm = jnp.where(sub_t >= s, dterm, 0.0)
                d_scr[s * SUB:(s + 1) * SUB, :] = dterm.astype(BF16)
            gsum = _dot(d_scr[...], exp_ref[...])
            od = jnp.zeros((SUB, H * DV), F32)
            for s in range(SUB):
                vs = v[i * SUB + s:i * SUB + s + 1, :].astype(F32)
                od = od + gsum[s * SUB:(s + 1) * SUB, :] * vs
            o_ref[pl.ds(r0 + i * SUB, SUB), :] = od

        outs = []
        for h in range(H):
            vh = v[:, h * DV:(h + 1) * DV]
            o_inter = _dot_nt(jnp.where(head_mask[h], q_in, 0.0).astype(BF16), st.astype(BF16))
            a_h = jnp.where(blk_lower, a_off[h], 0.0).astype(BF16)
            outs.append(o_inter + _dot(a_h, vh))
        o = o_ref[rows, :] + jnp.concatenate(outs, axis=1)

        upd = _dot_tn(v, k_dec)
        new_st = st * jnp.exp(b_last)
        for h in range(H):
            new_st = new_st + jnp.where(head_mask[h], upd[h * DV:(h + 1) * DV, :], 0.0)
        st_ref[...] = new_st

        y = _head_rms(o, ng_ref[...], 1.0) * r_ref[rows, :]
        o_ref[rows, :] = y
        return carry

    lax.fori_loop(0, qk_ref.shape[0] // C, chunk, 0)


def _gla(p32, p16, w2p, b2, norm_g, bsz, seq, ts):
    C, SUB, H, DK, DV = GLA_CHUNK, GLA_SUB, GLA_HEADS, GLA_DK, GLA_DV
    tokens = p32.shape[0]
    blocks_per_seq = seq // ts
    tril = jnp.asarray(np.tril(np.ones((C, C), np.float32))).astype(BF16)
    expand = np.zeros((H * DK, H * DV), np.float32)
    for h in range(H):
        expand[h * DK:(h + 1) * DK, h * DV:(h + 1) * DV] = 1.0
    expand = jnp.asarray(expand).astype(BF16)
    row_map = lambda cb: (lambda b, i: (b * blocks_per_seq + i, cb))
    const2 = lambda b, i: (0, 0)
    out = pl.pallas_call(
        _gla_kernel,
        grid=(bsz, blocks_per_seq),
        in_specs=[
            pl.BlockSpec((ts, 2 * H * DK), row_map(C32_GQ // (2 * H * DK))),
            pl.BlockSpec((ts, H * DV), row_map(C16_GV // (H * DV))),
            pl.BlockSpec((ts, H * DV), row_map(C32_R // (H * DV))),
            pl.BlockSpec((ts, LANES), row_map(C32_SMALL // LANES)),
            pl.BlockSpec(w2p.shape, const2),
            pl.BlockSpec(b2.shape, const2),
            pl.BlockSpec(norm_g.shape, const2),
            pl.BlockSpec(tril.shape, const2),
            pl.BlockSpec(expand.shape, const2),
        ],
        out_specs=pl.BlockSpec((ts, H * DV), lambda b, i: (b * blocks_per_seq + i, 0)),
        out_shape=jax.ShapeDtypeStruct((tokens, H * DV), F32),
        scratch_shapes=[pltpu.VMEM((DV, H * DK), F32), pltpu.VMEM((SUB * SUB, H * DK), BF16)],
        compiler_params=pltpu.CompilerParams(
            dimension_semantics=("parallel", "arbitrary"), vmem_limit_bytes=VMEM_LIMIT),
        name="gla_branch",
    )(p32, p16, p32, p32, w2p, b2, norm_g, tril, expand)
    return out


def _bit_transpose32(words):
    a = list(words)
    shift, mask = 16, 0x0000FFFF
    while shift:
        m = jnp.int32(np.uint32(mask).astype(np.int32))
        for k in range(32):
            if k & shift == 0:
                t = (a[k] ^ lax.shift_right_logical(a[k + shift], jnp.int32(shift))) & m
                a[k] = a[k] ^ t
                a[k + shift] = a[k + shift] ^ lax.shift_left(t, jnp.int32(shift))
        shift >>= 1
        mask = (mask ^ (mask << shift)) & 0xFFFFFFFF
    return a


def _dsa_kernel(q_ref, iq_ref, iwt_ref, k_ref, vt_ref, ik_ref, bias_ref, ltri_ref,
                o_ref, key_scr, plane_scr, s_scr, acc_scr, cnt_scr, *, topk):
    T, TK = DSA_QBLOCK, DSA_KTILE
    SUBS = TK // T
    H, HD = DSA_HEADS, DSA_HDIM
    qb = pl.program_id(1)
    kt_last = qb // SUBS
    n_kt = kt_last + 1
    key_minus_query = (lax.broadcasted_iota(jnp.int32, (TK, T), 0)
                       - lax.broadcasted_iota(jnp.int32, (TK, T), 1))

    def fold(x, op):
        return op(x.reshape(x.shape[0] // 8, 8, T), axis=0)

    def rows_to_one(x, op):
        return op(x, axis=0, keepdims=True)

    def key_rows(kt):
        return pl.ds(pl.multiple_of(kt * TK, TK), TK)

    lo_mask = lax.broadcasted_iota(jnp.int32, (T, LANES), 1) < DSA_IDX_DIM
    iq = [iq_ref[:, 0:LANES], iq_ref[:, LANES:2 * LANES]]
    iq_h = [jnp.where(lo_mask, iq[0], 0), jnp.where(lo_mask, 0, iq[0]),
            jnp.where(lo_mask, iq[1], 0), jnp.where(lo_mask, 0, iq[1])]
    iw_h = [iwt_ref[h:h + 1, :] for h in range(DSA_IDX_HEADS)]

    GRP_ROWS = 32 * 8
    GRPS = TK // GRP_ROWS
    n_grp = plane_scr.shape[0]

    @pl.when(qb == 0)
    def _():
        plane_scr[...] = jnp.zeros_like(plane_scr)

    def score_tile(kt):
        ik2 = ik_ref[key_rows(kt), :]
        score = jnp.zeros((TK, T), F32)
        for h in range(DSA_IDX_HEADS):
            score = score + iw_h[h] * jnp.maximum(_dot_nt(ik2, iq_h[h]), 0.0)
        bits = lax.bitcast_convert_type(score, jnp.int32)
        return jnp.where(score == 0.0, 0, jnp.where(bits >= 0, bits, bits ^ jnp.int32(0x7FFFFFFF)))

    def store_planes(kt, key):
        unsigned_order = key ^ jnp.int32(INT_MIN)
        for g in range(GRPS):
            words = [unsigned_order[g * GRP_ROWS + 8 * j:g * GRP_ROWS + 8 * j + 8, :] for j in range(32)]
            planes = _bit_transpose32(words)
            for b in range(32):
                plane_scr[kt * GRPS + g, b] = planes[b]

    def score_body(kt, c):
        key = score_tile(kt)
        key_scr[kt] = key
        store_planes(kt, key)
        return c

    lax.fori_loop(0, kt_last, score_body, 0)
    admissible = key_minus_query <= qb * T - kt_last * TK
    key = score_tile(kt_last)
    key_scr[kt_last] = jnp.where(admissible, key, jnp.int32(INT_MIN))
    store_planes(kt_last, key)
    alive_last = []
    for g in range(GRPS):
        word = jnp.zeros((8, T), jnp.int32)
        for j in range(32):
            adm = admissible[g * GRP_ROWS + 8 * j:g * GRP_ROWS + 8 * j + 8, :]
            word = word | jnp.where(adm, jnp.int32(np.int32(np.uint32(1 << (31 - j)))), 0)
        alive_last.append(word)

    alive = []
    for g in range(n_grp):
        kt = g // GRPS
        full = jnp.broadcast_to(jnp.where(kt < kt_last, jnp.int32(-1), jnp.int32(0)), (8, T))
        alive.append(jnp.where(kt == kt_last, alive_last[g % GRPS], full))

    def bit_body(b, carry):
        above, tau_u, alive = carry
        planes = [plane_scr[g, b] for g in range(n_grp)]
        ones = jnp.zeros((8, T), jnp.int32)
        for g in range(n_grp):
            ones = ones + lax.population_count(alive[g] & planes[g])
        ones = rows_to_one(ones, jnp.sum)
        take = above + ones >= topk
        flip = jnp.where(take, jnp.int32(0), jnp.int32(-1))
        alive = tuple(alive[g] & (planes[g] ^ flip) for g in range(n_grp))
        above = jnp.where(take, above, above + ones)
        tau_u = tau_u | jnp.where(take, lax.shift_left(jnp.int32(1), 31 - b), 0)
        return above, tau_u, alive

    zero_row = jnp.zeros((1, T), jnp.int32)
    above, tau_u, alive = lax.fori_loop(0, 32, bit_body, (zero_row, zero_row, tuple(alive)))
    n_tied = jnp.zeros((8, T), jnp.int32)
    for g in range(n_grp):
        n_tied = n_tied + lax.population_count(alive[g])
    n_ge = above + rows_to_one(n_tied, jnp.sum)
    tau = jnp.maximum(tau_u ^ jnp.int32(INT_MIN), jnp.int32(INT_MIN + 1))

    @pl.when(jnp.max(n_ge) > topk)
    def _():
        need = (topk - above).astype(F32)
        cnt_scr[...] = jnp.zeros_like(cnt_scr)

        def tie_body(kt, c):
            key = key_scr[kt]
            eq = key == tau
            eq_f = jnp.where(eq, 1.0, 0.0)
            pref = _dot(ltri_ref[...], eq_f.astype(BF16)) + cnt_scr[...]
            key_scr[kt] = jnp.where(eq & (pref > need), tau - 1, key)
            cnt_scr[...] = cnt_scr[...] + rows_to_one(eq_f, jnp.sum)
            return c

        lax.fori_loop(0, n_kt, tie_body, 0)

    q_h = [q_ref[:, h * HD:(h + 1) * HD] for h in range(H)]

    def logits_body(kt, m_run):
        sel = key_scr[kt] >= tau
        kinds = [jnp.clip(qb - (kt * SUBS + j), 0, 2) for j in range(SUBS)]
        new_m = []
        for h in range(H):
            bias = jnp.concatenate([bias_ref[kinds[j], h] for j in range(SUBS)], axis=0)
            s = _dot_nt(k_ref[key_rows(kt), h * HD:(h + 1) * HD], q_h[h]) + bias
            s = jnp.where(sel, s, NEG_BIG)
            s_scr[h, kt] = s
            new_m.append(jnp.maximum(m_run[h], fold(s, jnp.max)))
        return tuple(new_m)

    m_run = lax.fori_loop(0, n_kt, logits_body,
                          tuple(jnp.full((8, T), NEG_BIG, F32) for _ in range(H)))
    m_h = [rows_to_one(m, jnp.max) for m in m_run]

    for h in range(H):
        acc_scr[h] = jnp.zeros((HD, T), F32)

    def pv_body(kt, l_run):
        new_l = []
        for h in range(H):
            p = jnp.exp(s_scr[h, kt] - m_h[h])
            new_l.append(l_run[h] + fold(p, jnp.sum))
            acc_scr[h] = acc_scr[h] + _dot(vt_ref[kt, h * HD:(h + 1) * HD, :], p.astype(BF16))
        return tuple(new_l)

    l_run = lax.fori_loop(0, n_kt, pv_body, tuple(jnp.zeros((8, T), F32) for _ in range(H)))

    for h in range(H):
        out_t = acc_scr[h] / rows_to_one(l_run[h], jnp.sum)
        o_ref[:, h * HD:(h + 1) * HD] = out_t.T.astype(o_ref.dtype)


def _dsa(p16, p32, bias_tiles, bsz, seq):
    T = DSA_QBLOCK
    tokens = p16.shape[0]
    nqb = seq // T
    topk = min(DSA_TOPK_MAX, seq // 4)
    TK = min(DSA_KTILE, seq)
    assert TK == DSA_KTILE and seq % TK == 0
    nkt = seq // TK
    ltri = jnp.asarray(np.tril(np.ones((TK, TK), np.float32))).astype(BF16)
    w = D_BRANCH
    v_t = jnp.swapaxes(p16[:, C16_V:C16_V + w].reshape(bsz, nkt, TK, w), 2, 3)
    iw_t = jnp.zeros((8, tokens), F32).at[:DSA_IDX_HEADS].set(
        p32[:, C32_IW:C32_IW + DSA_IDX_HEADS].T)
    qmap = lambda cb: (lambda b, i: (b * nqb + i, cb))
    return pl.pallas_call(
        functools.partial(_dsa_kernel, topk=topk),
        grid=(bsz, nqb),
        in_specs=[
            pl.BlockSpec((T, w), qmap(C16_Q // w)),
            pl.BlockSpec((T, 2 * LANES), qmap(C16_IQ // (2 * LANES))),
            pl.BlockSpec((8, T), lambda b, i: (0, b * nqb + i)),
            pl.BlockSpec((seq, w), lambda b, i: (b, C16_K // w)),
            pl.BlockSpec((None, nkt, w, TK), lambda b, i: (b, 0, 0, 0)),
            pl.BlockSpec((seq, LANES), lambda b, i: (b, C16_IK // LANES)),
            pl.BlockSpec(bias_tiles.shape, lambda b, i: (0, 0, 0, 0)),
            pl.BlockSpec(ltri.shape, lambda b, i: (0, 0)),
        ],
        out_specs=pl.BlockSpec((T, w), lambda b, i: (b * nqb + i, 0)),
        out_shape=jax.ShapeDtypeStruct((tokens, w), BF16),
        scratch_shapes=[
            pltpu.VMEM((nkt, TK, T), jnp.int32),
            pltpu.VMEM((nkt * TK // 256, 32, 8, T), jnp.int32),
            pltpu.VMEM((DSA_HEADS, nkt, TK, T), F32),
            pltpu.VMEM((DSA_HEADS, DSA_HDIM, T), F32),
            pltpu.VMEM((1, T), F32),
        ],
        compiler_params=pltpu.CompilerParams(
            dimension_semantics=("parallel", "arbitrary"), vmem_limit_bytes=VMEM_LIMIT),
        name="dsa_branch",
    )(p16, p16, iw_t, p16, v_t, p16, bias_tiles, ltri)


def _merge_kernel(ya_ref, yb_ref, yc_ref, ga_ref, gb_ref, gc_ref, x_ref, mod_ref, wbr_ref, wout_ref,
                  gn2_ref, wr_hi_ref, wr_lo_ref, br_ref, x1_ref, h2_ref, lg_ref):
    merged = ga_ref[...] * _dot(ya_ref[...], wbr_ref[0])
    merged = merged + gb_ref[...] * _dot(yb_ref[...].astype(BF16), wbr_ref[1])
    merged = merged + gc_ref[...] * _dot(yc_ref[...], wbr_ref[2])
    mix = _dot(merged.astype(BF16), wout_ref[...])
    x1 = x_ref[...] + mod_ref[0, 2:3, :] * mix
    x1_ref[...] = x1
    y = x1 * lax.rsqrt(jnp.mean(x1 * x1, axis=-1, keepdims=True) + EPS) * gn2_ref[...]
    h2 = y * (1.0 + mod_ref[0, 4:5, :]) + mod_ref[0, 3:4, :]
    h2_ref[...] = h2.astype(BF16)
    lg_ref[...] = _dot3(h2, wr_hi_ref[...], wr_lo_ref[...]) + br_ref[...]


def _merge(ya, yb, yc, p32, x2d, mod_l, wbr, wout, layer, gn2, wr_hi, wr_lo, br, seq, tm):
    tokens, d = x2d.shape
    tiles_per_seq = seq // tm
    row = lambda i: (i, 0)
    gate = lambda k: (lambda i: (i, C32_GATES // d + k))
    c2 = lambda i: (0, 0)
    of_layer = lambda a: pl.BlockSpec((None,) + a.shape[1:], lambda i: (layer,) + (0,) * (a.ndim - 1))
    return pl.pallas_call(
        _merge_kernel,
        grid=(tokens // tm,),
        in_specs=[
            pl.BlockSpec((tm, D_BRANCH), row), pl.BlockSpec((tm, D_BRANCH), row),
            pl.BlockSpec((tm, D_BRANCH), row),
            pl.BlockSpec((tm, d), gate(0)), pl.BlockSpec((tm, d), gate(1)), pl.BlockSpec((tm, d), gate(2)),
            pl.BlockSpec((tm, d), row),
            pl.BlockSpec((1, 6, d), lambda i: (i // tiles_per_seq, 0, 0)),
            of_layer(wbr), of_layer(wout),
            pl.BlockSpec(gn2.shape, c2),
            of_layer(wr_hi), of_layer(wr_lo), of_layer(br),
        ],
        out_specs=[pl.BlockSpec((tm, d), row), pl.BlockSpec((tm, d), row), pl.BlockSpec((tm, LANES), row)],
        out_shape=[jax.ShapeDtypeStruct((tokens, d), F32), jax.ShapeDtypeStruct((tokens, d), BF16),
                   jax.ShapeDtypeStruct((tokens, LANES), F32)],
        compiler_params=pltpu.CompilerParams(
            dimension_semantics=("parallel",), vmem_limit_bytes=VMEM_LIMIT),
        name="merge_norm_router",
    )(ya, yb, yc, p32, p32, p32, x2d, mod_l, wbr, wout, gn2, wr_hi, wr_lo, br)


def _route(lg):
    lane = lax.broadcasted_iota(jnp.int32, lg.shape, 1)
    big = jnp.int32(10 ** 6)
    is_grp = lane < N_GROUPS
    gl = jnp.where(is_grp, lg, -jnp.inf)
    gmax = jnp.max(gl, axis=1, keepdims=True)
    gsum = jnp.sum(jnp.where(is_grp, jnp.exp(lg - gmax), 0.0), axis=1, keepdims=True)
    p_g = 1.0 / gsum
    g_idx = jnp.min(jnp.where(gl == gmax, lane, big), axis=1, keepdims=True)
    first = N_GROUPS + g_idx * EXPERTS_PER_GROUP
    in_grp = (lane >= first) & (lane < first + EXPERTS_PER_GROUP)
    e1 = jnp.where(in_grp, lg, -jnp.inf)
    v1 = jnp.max(e1, axis=1, keepdims=True)
    i1 = jnp.min(jnp.where(e1 == v1, lane, big), axis=1, keepdims=True)
    e2 = jnp.where(in_grp & (lane != i1), lg, -jnp.inf)
    v2 = jnp.max(e2, axis=1, keepdims=True)
    i2 = jnp.min(jnp.where(e2 == v2, lane, big), axis=1, keepdims=True)
    t = jnp.exp(v2 - v1)
    w1 = p_g * (1.0 / (1.0 + t))
    w2 = p_g * (t / (1.0 + t))
    return jnp.where(lane == i1, w1, jnp.where(lane == i2, w2, 0.0))


def _moe_kernel(h_ref, lg_ref, x1_ref, mod_ref, wgu_ref, wd_ref, o_ref, comb_scr, acc_scr):
    grp = pl.program_id(1)

    @pl.when(grp == 0)
    def _():
        comb = _route(lg_ref[...])
        hi = comb.astype(BF16)
        rest = comb - hi.astype(F32)
        mid = rest.astype(BF16)
        comb_scr[0] = hi
        comb_scr[1] = mid
        comb_scr[2] = (rest - mid.astype(F32)).astype(BF16)
        acc_scr[...] = jnp.zeros_like(acc_scr)

    h = h_ref[...]
    row_id = lax.broadcasted_iota(jnp.int32, (LANES, LANES), 0)
    acts = []
    for j in range(EXPERTS_PER_GROUP):
        lane_of_expert = N_GROUPS + grp * EXPERTS_PER_GROUP + j
        pick = jnp.where(row_id == lane_of_expert, 1.0, 0.0).astype(BF16)
        cw = _dot(comb_scr[0], pick) + (_dot(comb_scr[1], pick) + _dot(comb_scr[2], pick))
        hgu = _dot(h, wgu_ref[j])
        act = jax.nn.silu(hgu[:, :D_EXPERT]) * hgu[:, D_EXPERT:] * jnp.concatenate([cw, cw], axis=1)
        acts.append(act.astype(BF16))
    acc_scr[...] += _dot(jnp.concatenate(acts, axis=1), wd_ref[...])

    @pl.when(grp == pl.num_programs(1) - 1)
    def _():
        o_ref[...] = x1_ref[...] + mod_ref[0, 5:6, :] * acc_scr[...]


def _moe(h2, lg, x1, mod_l, wgu, wd, layer, seq, tm):
    tokens, d = x1.shape
    tiles_per_seq = seq // tm
    row = lambda i, g: (i, 0)
    return pl.pallas_call(
        _moe_kernel,
        grid=(tokens // tm, N_GROUPS),
        in_specs=[
            pl.BlockSpec((tm, d), row), pl.BlockSpec((tm, LANES), row), pl.BlockSpec((tm, d), row),
            pl.BlockSpec((1, 6, d), lambda i, g: (i // tiles_per_seq, 0, 0)),
            pl.BlockSpec((None, None) + wgu.shape[2:], lambda i, g: (layer, g, 0, 0, 0)),
            pl.BlockSpec((None, None) + wd.shape[2:], lambda i, g: (layer, g, 0, 0)),
        ],
        out_specs=pl.BlockSpec((tm, d), row),
        out_shape=jax.ShapeDtypeStruct((tokens, d), F32),
        scratch_shapes=[pltpu.VMEM((3, tm, LANES), BF16), pltpu.VMEM((tm, d), F32)],
        compiler_params=pltpu.CompilerParams(
            dimension_semantics=("parallel", "arbitrary"), vmem_limit_bytes=VMEM_LIMIT),
        name="hier_moe",
    )(h2, lg, x1, mod_l, wgu, wd)


def _row_tile(seq, want):
    t = min(want, seq)
    assert seq % t == 0
    return t


def kernel(x, c, w_mod, b_mod, g_norm1, g_norm2, w_in, gmlp_ln_g, gmlp_ln_b, gmlp_w_s, gmlp_b_s,
           gla_w_gate2, gla_b_gate, gla_norm_g, dsa_qnorm_g, dsa_knorm_g, rel_bias, w_branch,
           b_branch_gate, w_out, w_group, b_group, w_router, b_router, w_exp_gate, w_exp_up,
           w_exp_down):
    bsz, seq, d = x.shape
    depth = w_mod.shape[0]
    assert d == D_MODEL and seq % DSA_QBLOCK == 0 and seq % GLA_CHUNK == 0
    tokens = bsz * seq
    tm = _row_tile(seq, 512)

    mods = _modulation(c, w_mod, b_mod)
    bias_tiles = _bias_tiles(rel_bias)
    w32, w16 = _prep_w_in(w_in)

    causal = np.tril(np.ones((GMLP_CHUNK, GMLP_CHUNK), bool))
    w_tril = jnp.where(causal[None, None], gmlp_w_s, 0.0).astype(BF16)
    gmlp_bias = jnp.repeat(jnp.swapaxes(gmlp_b_s, 1, 2), LANES, axis=-1)
    w2p = jnp.zeros((depth, LANES, GLA_HEADS * GLA_DK), F32).at[:, :GLA_GATE_RANK].set(gla_w_gate2)
    w2p = w2p.astype(BF16)
    wr = jnp.zeros((depth, d, LANES), F32)
    wr = wr.at[:, :, :N_GROUPS].set(w_group).at[:, :, N_GROUPS:N_GROUPS + N_EXPERTS].set(w_router)
    wr_hi = wr.astype(BF16)
    wr_lo = (wr - wr_hi.astype(F32)).astype(BF16)
    br = jnp.zeros((depth, 1, LANES), F32)
    br = br.at[:, 0, :N_GROUPS].set(b_group).at[:, 0, N_GROUPS:N_GROUPS + N_EXPERTS].set(b_router)
    wbr = w_branch.astype(BF16)
    wout = w_out.astype(BF16)
    wgu = jnp.concatenate([w_exp_gate, w_exp_up], axis=-1).astype(BF16).reshape(
        depth, N_GROUPS, EXPERTS_PER_GROUP, d, 2 * D_EXPERT)
    wd = w_exp_down.astype(BF16).reshape(depth, N_GROUPS, EXPERTS_PER_GROUP * D_EXPERT, d)
    tm_moe = _row_tile(seq, 1024)

    x2d = x.reshape(tokens, d)
    for l in range(depth):
        mod_l = mods[l]
        aux32 = [gmlp_ln_g[l][None], gmlp_ln_b[l][None], b_branch_gate[l].reshape(1, -1)]
        aux16 = [dsa_qnorm_g[l][None], dsa_knorm_g[l][None]]
        p32, p16 = _norm_proj(x2d, mod_l, g_norm1[l][None], w32, w16, l, aux32, aux16, seq, tm)
        ya = _gmlp(p32, w_tril[l], gmlp_bias[l], tm)
        yb = _gla(p32, p16, w2p[l], gla_b_gate[l][None], gla_norm_g[l][None], bsz, seq, tm)
        yc = _dsa(p16, p32, bias_tiles, bsz, seq)
        x1, h2, lg = _merge(ya, yb, yc, p32, x2d, mod_l, wbr, wout, l, g_norm2[l][None],
                            wr_hi, wr_lo, br, seq, tm)
        x2d = _moe(h2, lg, x1, mod_l, wgu, wd, l, seq, tm_moe)
    return x2d.reshape(bsz, seq, d)
```

```python
import functools
import math

import numpy as np
import jax
import jax.numpy as jnp
from jax import lax
from jax.experimental import pallas as pl
from jax.experimental.pallas import tpu as pltpu

F32 = jnp.float32
BF16 = jnp.bfloat16

D_MODEL = 1024
D_BRANCH = 512
EPS = 1e-6
GMLP_CHUNK = 128
GMLP_GROUPS = 4
GLA_HEADS = 4
GLA_DK = 64
GLA_DV = 128
GLA_GATE_RANK = 16
GLA_GATE_TAU = 16.0
GLA_CHUNK = 128
GLA_SUB = 16
GLA_MILD_DECAY = -60.0
DSA_HEADS = 4
DSA_HDIM = 128
DSA_IDX_HEADS = 4
DSA_IDX_DIM = 64
DSA_QBLOCK = 128
DSA_KTILE = 512
DSA_TOPK_MAX = 256
N_BUCKETS = 32
MAX_DISTANCE = 128
N_GROUPS = 4
EXPERTS_PER_GROUP = 4
N_EXPERTS = 16
D_EXPERT = 256

LANES = 128
COL_TILE = 512
VMEM_LIMIT = 56 * 1024 * 1024
INT_MIN = -(2 ** 31)
NEG_BIG = -1e30

C32_U, C32_V, C32_GQ, C32_GK, C32_R, C32_GATES, C32_SMALL = 0, 512, 1024, 1280, 1536, 2048, 5120
C32_GA, C32_IW = C32_SMALL, C32_SMALL + GLA_GATE_RANK
N32 = C32_SMALL + LANES
C16_Q, C16_K, C16_GV, C16_IQ, C16_IK = 0, 512, 1024, 1536, 1792
N16 = C16_IK + LANES


def _dot(a, b):
    return jnp.dot(a, b, preferred_element_type=F32)


def _dot_nt(a, b):
    return lax.dot_general(a, b, (((1,), (1,)), ((), ())), preferred_element_type=F32)


def _dot_tn(a, b):
    return lax.dot_general(a, b, (((0,), (0,)), ((), ())), preferred_element_type=F32)


def _split2(a):
    hi = a.astype(BF16)
    lo = (a - hi.astype(F32)).astype(BF16)
    return hi, lo


def _dot3(a, w_hi, w_lo):
    a_hi, a_lo = _split2(a)
    return _dot(a_hi, w_hi) + (_dot(a_lo, w_hi) + _dot(a_hi, w_lo))


def _head_rms(y, g, scale):
    outs = []
    for h in range(y.shape[1] // LANES):
        yh = y[:, h * LANES:(h + 1) * LANES]
        ms = jnp.mean(yh * yh, axis=-1, keepdims=True)
        o = yh * lax.rsqrt(ms + EPS) * g
        if scale != 1.0:
            o = o * scale
        outs.append(o)
    return jnp.concatenate(outs, axis=1)


def _mod_kernel(c_ref, w_ref, b_ref, o_ref):
    a = jax.nn.silu(c_ref[...])
    w_hi, w_lo = _split2(w_ref[...])
    o_ref[...] = _dot3(a, w_hi, w_lo) + b_ref[...]


def _modulation(c, w_mod, b_mod):
    L, d, n = w_mod.shape
    bsz = c.shape[0]
    rows = 8 * pl.cdiv(bsz, 8)
    c_pad = jnp.zeros((rows, d), F32).at[:bsz].set(c)
    tn = 1536
    out = pl.pallas_call(
        _mod_kernel,
        grid=(L, n // tn),
        in_specs=[
            pl.BlockSpec((rows, d), lambda l, j: (0, 0)),
            pl.BlockSpec((None, d, tn), lambda l, j: (l, 0, j)),
            pl.BlockSpec((None, 1, tn), lambda l, j: (l, 0, j)),
        ],
        out_specs=pl.BlockSpec((None, rows, tn), lambda l, j: (l, 0, j)),
        out_shape=jax.ShapeDtypeStruct((L, rows, n), F32),
        compiler_params=pltpu.CompilerParams(
            dimension_semantics=("arbitrary", "arbitrary"), vmem_limit_bytes=VMEM_LIMIT),
        name="adaln_modulation",
    )(c_pad, w_mod, b_mod.reshape(L, 1, n))
    return out[:, :bsz].reshape(L, bsz, 6, d)


def _t5_bucket_table():
    n = np.arange(2 * DSA_QBLOCK)
    max_exact = N_BUCKETS // 2
    large = max_exact + (
        np.log(np.maximum(n, max_exact).astype(np.float32) / max_exact)
        / math.log(MAX_DISTANCE / max_exact) * (N_BUCKETS - max_exact)).astype(np.int32)
    large = np.minimum(large, N_BUCKETS - 1)
    return np.where(n < max_exact, n, large).astype(np.int32)


def _bias_kernel(rb_ref, bucket_ref, o_ref):
    for t in range(bucket_ref.shape[0]):
        bucket = bucket_ref[t]
        for h in range(DSA_HEADS):
            acc = jnp.zeros(bucket.shape, F32)
            for b in range(N_BUCKETS):
                acc = jnp.where(bucket == b, rb_ref[b, h], acc)
            o_ref[t, h] = acc


def _bias_tiles(rel_bias):
    table = _t5_bucket_table()
    assert (table[MAX_DISTANCE:] == N_BUCKETS - 1).all()
    t = np.arange(DSA_QBLOCK)[None, :]
    s = np.arange(DSA_QBLOCK)[:, None]
    diag = table[np.maximum(t - s, 0)]
    near = table[DSA_QBLOCK + t - s]
    far = np.full_like(diag, N_BUCKETS - 1)
    buckets = jnp.asarray(np.stack([diag, near, far]).astype(np.int32))
    return pl.pallas_call(
        _bias_kernel,
        in_specs=[pl.BlockSpec(memory_space=pltpu.SMEM), pl.BlockSpec(memory_space=pltpu.VMEM)],
        out_specs=pl.BlockSpec(memory_space=pltpu.VMEM),
        out_shape=jax.ShapeDtypeStruct((3, DSA_HEADS, DSA_QBLOCK, DSA_QBLOCK), F32),
        name="t5_bias_tiles",
    )(rel_bias, buckets)


def _proj_kernel(x_ref, mod_ref, gn_ref, w32_ref, w16_ref, *rest):
    aux32, aux16 = rest[:3], rest[3:5]
    o32_ref, o16_ref, vt_ref, iwt_ref = rest[5:9]
    x = x_ref[...]
    y = x * lax.rsqrt(jnp.mean(x * x, axis=-1, keepdims=True) + EPS) * gn_ref[...]
    h = (y * (1.0 + mod_ref[0, 1:2, :]) + mod_ref[0, 0:1, :]).astype(BF16)
    for w_ref, o_ref, epilogues, aux in ((w32_ref, o32_ref, EPILOGUES_32, aux32),
                                         (w16_ref, o16_ref, EPILOGUES_16, aux16)):
        wcol = ocol = 0
        for width, epi in epilogues:
            y = epi(_dot(h, w_ref[:, wcol:wcol + width]), aux)
            wcol += width
            if epi is _epi_value_t:
                vt_ref[0] = y.T.astype(vt_ref.dtype)
                continue
            if epi is _epi_small:
                iwt_ref[...] = y.T[GLA_GATE_RANK:GLA_GATE_RANK + 8, :]
            o_ref[:, ocol:ocol + width] = y.astype(o_ref.dtype)
            ocol += width
        assert wcol == w_ref.shape[1] and ocol == o_ref.shape[1]


def _norm_proj(x2d, mod_l, gn, w32, w16, layer, aux32, aux16, seq, tm):
    tokens, d = x2d.shape
    tiles_per_seq = seq // tm
    assert tm == DSA_KTILE
    const = lambda a: pl.BlockSpec(a.shape, lambda i: (0, 0))
    resident = lambda a: pl.BlockSpec((None,) + a.shape[1:], lambda i: (layer, 0, 0),
                                      pipeline_mode=pl.Buffered(1))
    return pl.pallas_call(
        _proj_kernel,
        grid=(tokens // tm,),
        in_specs=[
            pl.BlockSpec((tm, d), lambda i: (i, 0)),
            pl.BlockSpec((1, 6, d), lambda i: (i // tiles_per_seq, 0, 0)),
            const(gn), resident(w32), resident(w16),
        ] + [const(a) for a in aux32 + aux16],
        out_specs=[pl.BlockSpec((tm, N32), lambda i: (i, 0)), pl.BlockSpec((tm, N16), lambda i: (i, 0)),
                   pl.BlockSpec((1, D_BRANCH, tm), lambda i: (i, 0, 0)),
                   pl.BlockSpec((8, tm), lambda i: (0, i))],
        out_shape=[jax.ShapeDtypeStruct((tokens, N32), F32), jax.ShapeDtypeStruct((tokens, N16), BF16),
                   jax.ShapeDtypeStruct((tokens // tm, D_BRANCH, tm), BF16),
                   jax.ShapeDtypeStruct((8, tokens), F32)],
        compiler_params=pltpu.CompilerParams(
            dimension_semantics=("parallel",), vmem_limit_bytes=VMEM_LIMIT),
        name="norm_proj",
    )(x2d, mod_l, gn, w32, w16, *aux32, *aux16)


def _epi_raw(y, aux):
    return y


def _epi_gelu(y, aux):
    return jax.nn.gelu(y)


def _epi_gelu_ln(y, aux):
    v = jax.nn.gelu(y)
    mu = jnp.mean(v, axis=-1, keepdims=True)
    var = jnp.mean(jnp.square(v - mu), axis=-1, keepdims=True)
    return (v - mu) * lax.rsqrt(var + EPS) * aux[0][...] + aux[1][...]


def _epi_silu(y, aux):
    return jax.nn.silu(y)


def _epi_gate(k):
    def epi(y, aux):
        return jax.nn.sigmoid(y + aux[2][:, k * COL_TILE:(k + 1) * COL_TILE])
    return epi


def _epi_qnorm(y, aux):
    return _head_rms(y, aux[0][...], DSA_HDIM ** -0.5)


def _epi_knorm(y, aux):
    return _head_rms(y, aux[1][...], 1.0)


def _epi_value_t(y, aux):
    return y


def _epi_small(y, aux):
    return y


EPILOGUES_32 = ([(COL_TILE, e) for e in (_epi_gelu, _epi_gelu_ln, _epi_raw, _epi_silu)]
                + [(COL_TILE, _epi_gate(k)) for k in range(6)] + [(LANES, _epi_small)])
EPILOGUES_16 = [(COL_TILE, e) for e in (_epi_qnorm, _epi_knorm, _epi_value_t, _epi_raw)] + [
    (2 * LANES, _epi_raw), (LANES, _epi_raw)]


def _prep_w_in(w_in):
    sizes = (512, 512, 256, 256, 512, 512, 16, 512, 512, 512, 256, 64, 4, 3072)
    offs = np.concatenate([[0], np.cumsum(sizes)])
    seg = lambda k: w_in[:, :, offs[k]:offs[k + 1]]
    (a_u, a_v, g_q, g_k, g_v, g_r, g_a, d_q, d_k, d_v, d_iq, d_ik, d_iw, gates) = [seg(k) for k in range(14)]
    L, d, _ = w_in.shape
    zeros = lambda n: jnp.zeros((L, d, n), w_in.dtype)
    w32 = jnp.concatenate([a_u, a_v, g_q, g_k, g_r, gates, g_a, d_iw,
                           zeros(N32 - C32_SMALL - 20)], axis=-1).astype(BF16)
    w16 = jnp.concatenate([d_q, d_k, d_v, g_v, d_iq, d_ik, d_ik], axis=-1).astype(BF16)
    return w32, w16


def _gmlp_kernel(u_ref, v_ref, w_ref, b_ref, o_ref):
    tm = u_ref.shape[0]
    for c in range(tm // GMLP_CHUNK):
        rows = slice(c * GMLP_CHUNK, (c + 1) * GMLP_CHUNK)
        for g in range(GMLP_GROUPS):
            cols = slice(g * LANES, (g + 1) * LANES)
            mixed = _dot(w_ref[g], v_ref[rows, cols].astype(BF16)) + b_ref[:, cols]
            o_ref[rows, cols] = (u_ref[rows, cols] * mixed).astype(o_ref.dtype)


def _gmlp(p32, w_tril, bias_full, tm):
    tokens = p32.shape[0]
    return pl.pallas_call(
        _gmlp_kernel,
        grid=(tokens // tm,),
        in_specs=[
            pl.BlockSpec((tm, D_BRANCH), lambda i: (i, C32_U // D_BRANCH)),
            pl.BlockSpec((tm, D_BRANCH), lambda i: (i, C32_V // D_BRANCH)),
            pl.BlockSpec(w_tril.shape, lambda i: (0, 0, 0)),
            pl.BlockSpec(bias_full.shape, lambda i: (0, 0)),
        ],
        out_specs=pl.BlockSpec((tm, D_BRANCH), lambda i: (i, 0)),
        out_shape=jax.ShapeDtypeStruct((tokens, D_BRANCH), BF16),
        compiler_params=pltpu.CompilerParams(
            dimension_semantics=("parallel",), vmem_limit_bytes=VMEM_LIMIT),
        name="gmlp_branch",
    )(p32, p32, w_tril, bias_full)


def _gla_kernel(qk_ref, v_ref, r_ref, ga_ref, w2_ref, b2_ref, ng_ref, tril_ref, exp_ref,
                o_ref, st_ref, d_scr, g_scr):
    C, SUB, H, DK, DV = GLA_CHUNK, GLA_SUB, GLA_HEADS, GLA_DK, GLA_DV
    HK = H * DK
    n_chunks = qk_ref.shape[0] // C

    @pl.when(pl.program_id(1) == 0)
    def _():
        st_ref[...] = jnp.zeros_like(st_ref)

    lane = lax.broadcasted_iota(jnp.int32, (1, HK), 1)
    head_mask = [(lane >= h * DK) & (lane < (h + 1) * DK) for h in range(H)]
    row = lax.broadcasted_iota(jnp.int32, (C, C), 0)
    col = lax.broadcasted_iota(jnp.int32, (C, C), 1)
    sub_shift = SUB.bit_length() - 1
    blk_lower = (row >> sub_shift) > (col >> sub_shift)
    sub_t = lax.broadcasted_iota(jnp.int32, (SUB, 1), 0)

    graw = _dot(ga_ref[...].astype(BF16), w2_ref[...]) + b2_ref[...]
    g_all = jax.nn.log_sigmoid(graw) / GLA_GATE_TAU
    g_scr[...] = g_all
    chunk_decay = jnp.sum(g_all.reshape(n_chunks, C, HK), axis=1)
    mild = jnp.min(chunk_decay) >= GLA_MILD_DECAY

    def cumulative_decay(rows):
        g = g_scr[rows, :]
        g_hi = g.astype(BF16)
        g_r1 = g - g_hi.astype(F32)
        g_mid = g_r1.astype(BF16)
        g_lo = (g_r1 - g_mid.astype(F32)).astype(BF16)
        tril = tril_ref[...]
        return _dot(tril, g_hi) + (_dot(tril, g_mid) + _dot(tril, g_lo))

    def finish(rows, o, v, k_dec, b_last, st):
        upd = _dot_tn(v, k_dec)
        new_st = st * jnp.exp(b_last)
        for h in range(H):
            new_st = new_st + jnp.where(head_mask[h], upd[h * DV:(h + 1) * DV, :], 0.0)
        st_ref[...] = new_st
        o_ref[rows, :] = _head_rms(o, ng_ref[...], 1.0) * r_ref[rows, :]

    def chunk_mild(ci, carry):
        rows = pl.ds(pl.multiple_of(ci * C, C), C)
        q = qk_ref[rows, 0:HK] * (DK ** -0.5)
        k = qk_ref[rows, HK:2 * HK]
        v = v_ref[rows, :]
        b = cumulative_decay(rows)
        b_last = b[C - 1:C, :]
        st = st_ref[...]
        q_in = q * jnp.exp(b)
        k_out = (k * jnp.exp(-b)).astype(BF16)
        k_dec = (k * jnp.exp(b_last - b)).astype(BF16)
        outs = []
        for h in range(H):
            qm = jnp.where(head_mask[h], q_in, 0.0).astype(BF16)
            a_h = jnp.where(row >= col, _dot_nt(qm, k_out), 0.0).astype(BF16)
            outs.append(_dot_nt(qm, st.astype(BF16)) + _dot(a_h, v[:, h * DV:(h + 1) * DV]))
        finish(rows, jnp.concatenate(outs, axis=1), v, k_dec, b_last, st)
        return carry

    def chunk(ci, carry):
        r0 = pl.multiple_of(ci * C, C)
        rows = pl.ds(r0, C)
        q = qk_ref[rows, 0:HK] * (DK ** -0.5)
        k = qk_ref[rows, HK:2 * HK]
        v = v_ref[rows, :]
        b = cumulative_decay(rows)
        b_last = b[C - 1:C, :]
        st = st_ref[...]

        q_in = q * jnp.exp(b)
        k_dec = (k * jnp.exp(b_last - b)).astype(BF16)

        a_off = [jnp.zeros((C, C), F32) for _ in range(H)]
        for j in range(C // SUB - 1):
            bj = b[(j + 1) * SUB - 1:(j + 1) * SUB, :]
            qj = q * jnp.exp(jnp.minimum(b - bj, 0.0))
            in_blk = (lax.broadcasted_iota(jnp.int32, (C, 1), 0) >> sub_shift) == j
            kj = jnp.where(in_blk, k * jnp.exp(jnp.minimum(bj - b, 0.0)), 0.0).astype(BF16)
            for h in range(H):
                a_off[h] = a_off[h] + _dot_nt(jnp.where(head_mask[h], qj, 0.0).astype(BF16), kj)

        for i in range(C // SUB):
            rs = slice(i * SUB, (i + 1) * SUB)
            qi, bi = q[rs, :], b[rs, :]
            for s in range(SUB):
                ks = k[i * SUB + s:i * SUB + s + 1, :]
                bs = b[i * SUB + s:i * SUB + s + 1, :]
                dterm = qi * ks * jnp.exp(jnp.minimum(bi - bs, 0.0))
                dterm = jnp.where(sub_t >= s, dterm, 0.0)
                d_scr[s * SUB:(s + 1) * SUB, :] = dterm.astype(BF16)
            gsum = _dot(d_scr[...], exp_ref[...])
            od = jnp.zeros((SUB, H * DV), F32)
            for s in range(SUB):
                vs = v[i * SUB + s:i * SUB + s + 1, :].astype(F32)
                od = od + gsum[s * SUB:(s + 1) * SUB, :] * vs
            o_ref[pl.ds(r0 + i * SUB, SUB), :] = od

        outs = []
        for h in range(H):
            vh = v[:, h * DV:(h + 1) * DV]
            o_inter = _dot_nt(jnp.where(head_mask[h], q_in, 0.0).astype(BF16), st.astype(BF16))
            a_h = jnp.where(blk_lower, a_off[h], 0.0).astype(BF16)
            outs.append(o_inter + _dot(a_h, vh))
        o = o_ref[rows, :] + jnp.concatenate(outs, axis=1)
        finish(rows, o, v, k_dec, b_last, st)
        return carry

    @pl.when(mild)
    def _():
        lax.fori_loop(0, n_chunks, chunk_mild, 0, unroll=4)

    @pl.when(jnp.logical_not(mild))
    def _():
        lax.fori_loop(0, n_chunks, chunk, 0)


def _gla(p32, p16, w2p, b2, norm_g, bsz, seq, ts):
    C, SUB, H, DK, DV = GLA_CHUNK, GLA_SUB, GLA_HEADS, GLA_DK, GLA_DV
    tokens = p32.shape[0]
    blocks_per_seq = seq // ts
    tril = jnp.asarray(np.tril(np.ones((C, C), np.float32))).astype(BF16)
    expand = np.zeros((H * DK, H * DV), np.float32)
    for h in range(H):
        expand[h * DK:(h + 1) * DK, h * DV:(h + 1) * DV] = 1.0
    expand = jnp.asarray(expand).astype(BF16)
    row_map = lambda cb: (lambda b, i: (b * blocks_per_seq + i, cb))
    const2 = lambda b, i: (0, 0)
    out = pl.pallas_call(
        _gla_kernel,
        grid=(bsz, blocks_per_seq),
        in_specs=[
            pl.BlockSpec((ts, 2 * H * DK), row_map(C32_GQ // (2 * H * DK))),
            pl.BlockSpec((ts, H * DV), row_map(C16_GV // (H * DV))),
            pl.BlockSpec((ts, H * DV), row_map(C32_R // (H * DV))),
            pl.BlockSpec((ts, LANES), row_map(C32_SMALL // LANES)),
            pl.BlockSpec(w2p.shape, const2),
            pl.BlockSpec(b2.shape, const2),
            pl.BlockSpec(norm_g.shape, const2),
            pl.BlockSpec(tril.shape, const2),
            pl.BlockSpec(expand.shape, const2),
        ],
        out_specs=pl.BlockSpec((ts, H * DV), lambda b, i: (b * blocks_per_seq + i, 0)),
        out_shape=jax.ShapeDtypeStruct((tokens, H * DV), F32),
        scratch_shapes=[pltpu.VMEM((DV, H * DK), F32), pltpu.VMEM((SUB * SUB, H * DK), BF16),
                        pltpu.VMEM((ts, H * DK), F32)],
        compiler_params=pltpu.CompilerParams(
            dimension_semantics=("parallel", "arbitrary"), vmem_limit_bytes=VMEM_LIMIT),
        name="gla_branch",
    )(p32, p16, p32, p32, w2p, b2, norm_g, tril, expand)
    return out


def _bit_transpose32(words):
    a = list(words)
    shift, mask = 16, 0x0000FFFF
    while shift:
        m = jnp.int32(np.uint32(mask).astype(np.int32))
        for k in range(32):
            if k & shift == 0:
                t = (a[k] ^ lax.shift_right_logical(a[k + shift], jnp.int32(shift))) & m
                a[k] = a[k] ^ t
                a[k + shift] = a[k + shift] ^ lax.shift_left(t, jnp.int32(shift))
        shift >>= 1
        mask = (mask ^ (mask << shift)) & 0xFFFFFFFF
    return a


def _dsa_kernel(q_ref, iq_ref, iwt_ref, k_ref, vt_ref, ik_ref, bias_ref, ltri_ref,
                o_ref, key_scr, plane_scr, s_scr, acc_scr, cnt_scr, *, topk):
    T, TK = DSA_QBLOCK, DSA_KTILE
    SUBS = TK // T
    H, HD = DSA_HEADS, DSA_HDIM
    qb = pl.program_id(1)
    kt_last = qb // SUBS
    n_kt = kt_last + 1
    key_minus_query = (lax.broadcasted_iota(jnp.int32, (TK, T), 0)
                       - lax.broadcasted_iota(jnp.int32, (TK, T), 1))

    def fold(x, op):
        return op(x.reshape(x.shape[0] // 8, 8, T), axis=0)

    def rows_to_one(x, op):
        return op(x, axis=0, keepdims=True)

    def key_rows(kt):
        return pl.ds(pl.multiple_of(kt * TK, TK), TK)

    lo_mask = lax.broadcasted_iota(jnp.int32, (T, LANES), 1) < DSA_IDX_DIM
    iq = [iq_ref[:, 0:LANES], iq_ref[:, LANES:2 * LANES]]
    iq_h = [jnp.where(lo_mask, iq[0], 0), jnp.where(lo_mask, 0, iq[0]),
            jnp.where(lo_mask, iq[1], 0), jnp.where(lo_mask, 0, iq[1])]
    iw_h = [iwt_ref[h:h + 1, :] for h in range(DSA_IDX_HEADS)]

    GRP_ROWS = 32 * 8
    GRPS = TK // GRP_ROWS
    n_grp = plane_scr.shape[0]

    @pl.when(qb == 0)
    def _():
        plane_scr[...] = jnp.zeros_like(plane_scr)

    def score_tile(kt):
        ik2 = ik_ref[key_rows(kt), :]
        score = jnp.zeros((TK, T), F32)
        for h in range(DSA_IDX_HEADS):
            score = score + iw_h[h] * jnp.maximum(_dot_nt(ik2, iq_h[h]), 0.0)
        bits = lax.bitcast_convert_type(score, jnp.int32)
        return jnp.where(score == 0.0, 0, jnp.where(bits >= 0, bits, bits ^ jnp.int32(0x7FFFFFFF)))

    def store_planes(kt, key):
        unsigned_order = key ^ jnp.int32(INT_MIN)
        for g in range(GRPS):
            words = [unsigned_order[g * GRP_ROWS + 8 * j:g * GRP_ROWS + 8 * j + 8, :] for j in range(32)]
            planes = _bit_transpose32(words)
            for b in range(32):
                plane_scr[kt * GRPS + g, b] = planes[b]

    def score_body(kt, c):
        key = score_tile(kt)
        key_scr[kt] = key
        store_planes(kt, key)
        return c

    lax.fori_loop(0, kt_last, score_body, 0)
    admissible = key_minus_query <= qb * T - kt_last * TK
    key = score_tile(kt_last)
    key_scr[kt_last] = jnp.where(admissible, key, jnp.int32(INT_MIN))
    store_planes(kt_last, key)
    alive_last = []
    for g in range(GRPS):
        word = jnp.zeros((8, T), jnp.int32)
        for j in range(32):
            adm = admissible[g * GRP_ROWS + 8 * j:g * GRP_ROWS + 8 * j + 8, :]
            word = word | jnp.where(adm, jnp.int32(np.int32(np.uint32(1 << (31 - j)))), 0)
        alive_last.append(word)

    alive = []
    for g in range(n_grp):
        kt = g // GRPS
        full = jnp.broadcast_to(jnp.where(kt < kt_last, jnp.int32(-1), jnp.int32(0)), (8, T))
        alive.append(jnp.where(kt == kt_last, alive_last[g % GRPS], full))

    def bit_body(b, carry):
        above, tau_u, alive = carry
        planes = [plane_scr[g, b] for g in range(n_grp)]
        ones = jnp.zeros((8, T), jnp.int32)
        for g in range(n_grp):
            ones = ones + lax.population_count(alive[g] & planes[g])
        ones = rows_to_one(ones, jnp.sum)
        take = above + ones >= topk
        flip = jnp.where(take, jnp.int32(0), jnp.int32(-1))
        alive = tuple(alive[g] & (planes[g] ^ flip) for g in range(n_grp))
        above = jnp.where(take, above, above + ones)
        tau_u = tau_u | jnp.where(take, lax.shift_left(jnp.int32(1), 31 - b), 0)
        return above, tau_u, alive

    zero_row = jnp.zeros((1, T), jnp.int32)
    above, tau_u, alive = lax.fori_loop(0, 32, bit_body, (zero_row, zero_row, tuple(alive)))
    n_tied = jnp.zeros((8, T), jnp.int32)
    for g in range(n_grp):
        n_tied = n_tied + lax.population_count(alive[g])
    n_ge = above + rows_to_one(n_tied, jnp.sum)
    tau = jnp.maximum(tau_u ^ jnp.int32(INT_MIN), jnp.int32(INT_MIN + 1))

    @pl.when(jnp.max(n_ge) > topk)
    def _():
        need = (topk - above).astype(F32)
        cnt_scr[...] = jnp.zeros_like(cnt_scr)

        def tie_body(kt, c):
            key = key_scr[kt]
            eq = key == tau
            eq_f = jnp.where(eq, 1.0, 0.0)
            pref = _dot(ltri_ref[...], eq_f.astype(BF16)) + cnt_scr[...]
            key_scr[kt] = jnp.where(eq & (pref > need), tau - 1, key)
            cnt_scr[...] = cnt_scr[...] + rows_to_one(eq_f, jnp.sum)
            return c

        lax.fori_loop(0, n_kt, tie_body, 0)

    q_h = [q_ref[:, h * HD:(h + 1) * HD] for h in range(H)]

    def logits_body(kt, m_run):
        sel = key_scr[kt] >= tau
        kinds = [jnp.clip(qb - (kt * SUBS + j), 0, 2) for j in range(SUBS)]
        new_m = []
        for h in range(H):
            bias = jnp.concatenate([bias_ref[kinds[j], h] for j in range(SUBS)], axis=0)
            s = _dot_nt(k_ref[key_rows(kt), h * HD:(h + 1) * HD], q_h[h]) + bias
            s = jnp.where(sel, s, NEG_BIG)
            s_scr[h, kt] = s
            new_m.append(jnp.maximum(m_run[h], fold(s, jnp.max)))
        return tuple(new_m)

    m_run = lax.fori_loop(0, n_kt, logits_body,
                          tuple(jnp.full((8, T), NEG_BIG, F32) for _ in range(H)))
    m_h = [rows_to_one(m, jnp.max) for m in m_run]

    for h in range(H):
        acc_scr[h] = jnp.zeros((HD, T), F32)

    def pv_body(kt, l_run):
        new_l = []
        for h in range(H):
            p = jnp.exp(s_scr[h, kt] - m_h[h])
            new_l.append(l_run[h] + fold(p, jnp.sum))
            acc_scr[h] = acc_scr[h] + _dot(vt_ref[kt, h * HD:(h + 1) * HD, :], p.astype(BF16))
        return tuple(new_l)

    l_run = lax.fori_loop(0, n_kt, pv_body, tuple(jnp.zeros((8, T), F32) for _ in range(H)))

    for h in range(H):
        out_t = acc_scr[h] / rows_to_one(l_run[h], jnp.sum)
        o_ref[:, h * HD:(h + 1) * HD] = out_t.T.astype(o_ref.dtype)


def _dsa(p16, v_t, iw_t, bias_tiles, bsz, seq):
    T = DSA_QBLOCK
    tokens = p16.shape[0]
    nqb = seq // T
    topk = min(DSA_TOPK_MAX, seq // 4)
    TK = min(DSA_KTILE, seq)
    assert TK == DSA_KTILE and seq % TK == 0
    nkt = seq // TK
    ltri = jnp.asarray(np.tril(np.ones((TK, TK), np.float32))).astype(BF16)
    w = D_BRANCH
    v_t = v_t.reshape(bsz, nkt, w, TK)
    qmap = lambda cb: (lambda b, i: (b * nqb + i, cb))
    return pl.pallas_call(
        functools.partial(_dsa_kernel, topk=topk),
        grid=(bsz, nqb),
        in_specs=[
            pl.BlockSpec((T, w), qmap(C16_Q // w)),
            pl.BlockSpec((T, 2 * LANES), qmap(C16_IQ // (2 * LANES))),
            pl.BlockSpec((8, T), lambda b, i: (0, b * nqb + i)),
            pl.BlockSpec((seq, w), lambda b, i: (b, C16_K // w)),
            pl.BlockSpec((None, nkt, w, TK), lambda b, i: (b, 0, 0, 0)),
            pl.BlockSpec((seq, LANES), lambda b, i: (b, C16_IK // LANES)),
            pl.BlockSpec(bias_tiles.shape, lambda b, i: (0, 0, 0, 0)),
            pl.BlockSpec(ltri.shape, lambda b, i: (0, 0)),
        ],
        out_specs=pl.BlockSpec((T, w), lambda b, i: (b * nqb + i, 0)),
        out_shape=jax.ShapeDtypeStruct((tokens, w), BF16),
        scratch_shapes=[
            pltpu.VMEM((nkt, TK, T), jnp.int32),
            pltpu.VMEM((nkt * TK // 256, 32, 8, T), jnp.int32),
            pltpu.VMEM((DSA_HEADS, nkt, TK, T), F32),
            pltpu.VMEM((DSA_HEADS, DSA_HDIM, T), F32),
            pltpu.VMEM((1, T), F32),
        ],
        compiler_params=pltpu.CompilerParams(
            dimension_semantics=("parallel", "arbitrary"), vmem_limit_bytes=VMEM_LIMIT),
        name="dsa_branch",
    )(p16, p16, iw_t, p16, v_t, p16, bias_tiles, ltri)


def _merge_kernel(ya_ref, yb_ref, yc_ref, ga_ref, gb_ref, gc_ref, x_ref, mod_ref, wbr_ref, wout_ref,
                  gn2_ref, wr_hi_ref, wr_lo_ref, br_ref, x1_ref, h2_ref, lg_ref):
    merged = ga_ref[...] * _dot(ya_ref[...], wbr_ref[0])
    merged = merged + gb_ref[...] * _dot(yb_ref[...].astype(BF16), wbr_ref[1])
    merged = merged + gc_ref[...] * _dot(yc_ref[...], wbr_ref[2])
    mix = _dot(merged.astype(BF16), wout_ref[...])
    x1 = x_ref[...] + mod_ref[0, 2:3, :] * mix
    x1_ref[...] = x1
    y = x1 * lax.rsqrt(jnp.mean(x1 * x1, axis=-1, keepdims=True) + EPS) * gn2_ref[...]
    h2 = y * (1.0 + mod_ref[0, 4:5, :]) + mod_ref[0, 3:4, :]
    h2_ref[...] = h2.astype(BF16)
    lg_ref[...] = _dot3(h2, wr_hi_ref[...], wr_lo_ref[...]) + br_ref[...]


def _merge(ya, yb, yc, p32, x2d, mod_l, wbr, wout, layer, gn2, wr_hi, wr_lo, br, seq, tm):
    tokens, d = x2d.shape
    tiles_per_seq = seq // tm
    row = lambda i: (i, 0)
    gate = lambda k: (lambda i: (i, C32_GATES // d + k))
    c2 = lambda i: (0, 0)
    of_layer = lambda a: pl.BlockSpec((None,) + a.shape[1:], lambda i: (layer,) + (0,) * (a.ndim - 1))
    return pl.pallas_call(
        _merge_kernel,
        grid=(tokens // tm,),
        in_specs=[
            pl.BlockSpec((tm, D_BRANCH), row), pl.BlockSpec((tm, D_BRANCH), row),
            pl.BlockSpec((tm, D_BRANCH), row),
            pl.BlockSpec((tm, d), gate(0)), pl.BlockSpec((tm, d), gate(1)), pl.BlockSpec((tm, d), gate(2)),
            pl.BlockSpec((tm, d), row),
            pl.BlockSpec((1, 6, d), lambda i: (i // tiles_per_seq, 0, 0)),
            of_layer(wbr), of_layer(wout),
            pl.BlockSpec(gn2.shape, c2),
            of_layer(wr_hi), of_layer(wr_lo), of_layer(br),
        ],
        out_specs=[pl.BlockSpec((tm, d), row), pl.BlockSpec((tm, d), row), pl.BlockSpec((tm, LANES), row)],
        out_shape=[jax.ShapeDtypeStruct((tokens, d), F32), jax.ShapeDtypeStruct((tokens, d), BF16),
                   jax.ShapeDtypeStruct((tokens, LANES), F32)],
        compiler_params=pltpu.CompilerParams(
            dimension_semantics=("parallel",), vmem_limit_bytes=VMEM_LIMIT),
        name="merge_norm_router",
    )(ya, yb, yc, p32, p32, p32, x2d, mod_l, wbr, wout, gn2, wr_hi, wr_lo, br)


def _route(lg):
    lane = lax.broadcasted_iota(jnp.int32, lg.shape, 1)
    big = jnp.int32(10 ** 6)
    is_grp = lane < N_GROUPS
    gl = jnp.where(is_grp, lg, -jnp.inf)
    gmax = jnp.max(gl, axis=1, keepdims=True)
    gsum = jnp.sum(jnp.where(is_grp, jnp.exp(lg - gmax), 0.0), axis=1, keepdims=True)
    p_g = 1.0 / gsum
    g_idx = jnp.min(jnp.where(gl == gmax, lane, big), axis=1, keepdims=True)
    first = N_GROUPS + g_idx * EXPERTS_PER_GROUP
    in_grp = (lane >= first) & (lane < first + EXPERTS_PER_GROUP)
    e1 = jnp.where(in_grp, lg, -jnp.inf)
    v1 = jnp.max(e1, axis=1, keepdims=True)
    i1 = jnp.min(jnp.where(e1 == v1, lane, big), axis=1, keepdims=True)
    e2 = jnp.where(in_grp & (lane != i1), lg, -jnp.inf)
    v2 = jnp.max(e2, axis=1, keepdims=True)
    i2 = jnp.min(jnp.where(e2 == v2, lane, big), axis=1, keepdims=True)
    t = jnp.exp(v2 - v1)
    w1 = p_g * (1.0 / (1.0 + t))
    w2 = p_g * (t / (1.0 + t))
    return jnp.where(lane == i1, w1, jnp.where(lane == i2, w2, 0.0))


def _moe_kernel(h_ref, lg_ref, x1_ref, mod_ref, wgu_ref, wd_ref, o_ref, comb_scr, acc_scr):
    grp = pl.program_id(1)

    @pl.when(grp == 0)
    def _():
        comb = _route(lg_ref[...])
        hi = comb.astype(BF16)
        rest = comb - hi.astype(F32)
        mid = rest.astype(BF16)
        comb_scr[0] = hi
        comb_scr[1] = mid
        comb_scr[2] = (rest - mid.astype(F32)).astype(BF16)
        acc_scr[...] = jnp.zeros_like(acc_scr)

    h = h_ref[...]
    row_id = lax.broadcasted_iota(jnp.int32, (LANES, LANES), 0)
    acts = []
    for j in range(EXPERTS_PER_GROUP):
        lane_of_expert = N_GROUPS + grp * EXPERTS_PER_GROUP + j
        pick = jnp.where(row_id == lane_of_expert, 1.0, 0.0).astype(BF16)
        cw = _dot(comb_scr[0], pick) + (_dot(comb_scr[1], pick) + _dot(comb_scr[2], pick))
        hgu = _dot(h, wgu_ref[j])
        act = jax.nn.silu(hgu[:, :D_EXPERT]) * hgu[:, D_EXPERT:] * jnp.concatenate([cw, cw], axis=1)
        acts.append(act.astype(BF16))
    acc_scr[...] += _dot(jnp.concatenate(acts, axis=1), wd_ref[...])

    @pl.when(grp == pl.num_programs(1) - 1)
    def _():
        o_ref[...] = x1_ref[...] + mod_ref[0, 5:6, :] * acc_scr[...]


def _moe(h2, lg, x1, mod_l, wgu, wd, layer, seq, tm):
    tokens, d = x1.shape
    tiles_per_seq = seq // tm
    row = lambda i, g: (i, 0)
    return pl.pallas_call(
        _moe_kernel,
        grid=(tokens // tm, N_GROUPS),
        in_specs=[
            pl.BlockSpec((tm, d), row), pl.BlockSpec((tm, LANES), row), pl.BlockSpec((tm, d), row),
            pl.BlockSpec((1, 6, d), lambda i, g: (i // tiles_per_seq, 0, 0)),
            pl.BlockSpec((None, None) + wgu.shape[2:], lambda i, g: (layer, g, 0, 0, 0)),
            pl.BlockSpec((None, None) + wd.shape[2:], lambda i, g: (layer, g, 0, 0)),
        ],
        out_specs=pl.BlockSpec((tm, d), row),
        out_shape=jax.ShapeDtypeStruct((tokens, d), F32),
        scratch_shapes=[pltpu.VMEM((3, tm, LANES), BF16), pltpu.VMEM((tm, d), F32)],
        compiler_params=pltpu.CompilerParams(
            dimension_semantics=("parallel", "arbitrary"), vmem_limit_bytes=VMEM_LIMIT),
        name="hier_moe",
    )(h2, lg, x1, mod_l, wgu, wd)


def _row_tile(seq, want):
    t = min(want, seq)
    assert seq % t == 0
    return t


def kernel(x, c, w_mod, b_mod, g_norm1, g_norm2, w_in, gmlp_ln_g, gmlp_ln_b, gmlp_w_s, gmlp_b_s,
           gla_w_gate2, gla_b_gate, gla_norm_g, dsa_qnorm_g, dsa_knorm_g, rel_bias, w_branch,
           b_branch_gate, w_out, w_group, b_group, w_router, b_router, w_exp_gate, w_exp_up,
           w_exp_down):
    bsz, seq, d = x.shape
    depth = w_mod.shape[0]
    assert d == D_MODEL and seq % DSA_QBLOCK == 0 and seq % GLA_CHUNK == 0
    tokens = bsz * seq
    tm = _row_tile(seq, 512)

    mods = _modulation(c, w_mod, b_mod)
    bias_tiles = _bias_tiles(rel_bias)
    w32, w16 = _prep_w_in(w_in)

    causal = np.tril(np.ones((GMLP_CHUNK, GMLP_CHUNK), bool))
    w_tril = jnp.where(causal[None, None], gmlp_w_s, 0.0).astype(BF16)
    gmlp_bias = jnp.repeat(jnp.swapaxes(gmlp_b_s, 1, 2), LANES, axis=-1)
    w2p = jnp.zeros((depth, LANES, GLA_HEADS * GLA_DK), F32).at[:, :GLA_GATE_RANK].set(gla_w_gate2)
    w2p = w2p.astype(BF16)
    wr = jnp.zeros((depth, d, LANES), F32)
    wr = wr.at[:, :, :N_GROUPS].set(w_group).at[:, :, N_GROUPS:N_GROUPS + N_EXPERTS].set(w_router)
    wr_hi = wr.astype(BF16)
    wr_lo = (wr - wr_hi.astype(F32)).astype(BF16)
    br = jnp.zeros((depth, 1, LANES), F32)
    br = br.at[:, 0, :N_GROUPS].set(b_group).at[:, 0, N_GROUPS:N_GROUPS + N_EXPERTS].set(b_router)
    wbr = w_branch.astype(BF16)
    wout = w_out.astype(BF16)
    wgu = jnp.concatenate([w_exp_gate, w_exp_up], axis=-1).astype(BF16).reshape(
        depth, N_GROUPS, EXPERTS_PER_GROUP, d, 2 * D_EXPERT)
    wd = w_exp_down.astype(BF16).reshape(depth, N_GROUPS, EXPERTS_PER_GROUP * D_EXPERT, d)
    tm_moe = _row_tile(seq, 1024)

    x2d = x.reshape(tokens, d)
    for l in range(depth):
        mod_l = mods[l]
        aux32 = [gmlp_ln_g[l][None], gmlp_ln_b[l][None], b_branch_gate[l].reshape(1, -1)]
        aux16 = [dsa_qnorm_g[l][None], dsa_knorm_g[l][None]]
        p32, p16, v_t, iw_t = _norm_proj(x2d, mod_l, g_norm1[l][None], w32, w16, l, aux32, aux16,
                                         seq, tm)
        ya = _gmlp(p32, w_tril[l], gmlp_bias[l], tm)
        yb = _gla(p32, p16, w2p[l], gla_b_gate[l][None], gla_norm_g[l][None], bsz, seq, tm)
        yc = _dsa(p16, v_t, iw_t, bias_tiles, bsz, seq)
        x1, h2, lg = _merge(ya, yb, yc, p32, x2d, mod_l, wbr, wout, l, g_norm2[l][None],
                            wr_hi, wr_lo, br, seq, tm)
        x2d = _moe(h2, lg, x1, mod_l, wgu, wd, l, seq, tm_moe)
    return x2d.reshape(bsz, seq, d)
```

```python
import functools
import math

import numpy as np
import jax
import jax.numpy as jnp
from jax import lax
from jax.experimental import pallas as pl
from jax.experimental.pallas import tpu as pltpu

F32 = jnp.float32
BF16 = jnp.bfloat16

D_MODEL = 1024
D_BRANCH = 512
EPS = 1e-6
GMLP_CHUNK = 128
GMLP_GROUPS = 4
GLA_HEADS = 4
GLA_DK = 64
GLA_DV = 128
GLA_GATE_RANK = 16
GLA_GATE_TAU = 16.0
GLA_CHUNK = 128
GLA_SUB = 16
GLA_MILD_DECAY = -60.0
DSA_HEADS = 4
DSA_HDIM = 128
DSA_IDX_HEADS = 4
DSA_IDX_DIM = 64
DSA_QBLOCK = 128
DSA_KTILE = 512
DSA_TOPK_MAX = 256
N_BUCKETS = 32
MAX_DISTANCE = 128
N_GROUPS = 4
EXPERTS_PER_GROUP = 4
N_EXPERTS = 16
D_EXPERT = 256

LANES = 128
COL_TILE = 512
VMEM_LIMIT = 56 * 1024 * 1024
INT_MIN = -(2 ** 31)
NEG_BIG = -1e30

C32_U, C32_V, C32_GQ, C32_GK, C32_R, C32_GATES, C32_SMALL = 0, 512, 1024, 1280, 1536, 2048, 5120
C32_GA, C32_IW = C32_SMALL, C32_SMALL + GLA_GATE_RANK
N32 = C32_SMALL + LANES
C16_Q, C16_K, C16_GV, C16_IQ, C16_IK = 0, 512, 1024, 1536, 1792
N16 = C16_IK + LANES


def _dot(a, b):
    return jnp.dot(a, b, preferred_element_type=F32)


def _dot_nt(a, b):
    return lax.dot_general(a, b, (((1,), (1,)), ((), ())), preferred_element_type=F32)


def _dot_tn(a, b):
    return lax.dot_general(a, b, (((0,), (0,)), ((), ())), preferred_element_type=F32)


def _split2(a):
    hi = a.astype(BF16)
    lo = (a - hi.astype(F32)).astype(BF16)
    return hi, lo


def _dot3(a, w_hi, w_lo):
    a_hi, a_lo = _split2(a)
    return _dot(a_hi, w_hi) + (_dot(a_lo, w_hi) + _dot(a_hi, w_lo))


def _head_rms(y, g, scale):
    outs = []
    for h in range(y.shape[1] // LANES):
        yh = y[:, h * LANES:(h + 1) * LANES]
        ms = jnp.mean(yh * yh, axis=-1, keepdims=True)
        o = yh * lax.rsqrt(ms + EPS) * g
        if scale != 1.0:
            o = o * scale
        outs.append(o)
    return jnp.concatenate(outs, axis=1)


def _mod_kernel(c_ref, w_ref, b_ref, o_ref):
    a = jax.nn.silu(c_ref[...])
    w_hi, w_lo = _split2(w_ref[...])
    o_ref[...] = _dot3(a, w_hi, w_lo) + b_ref[...]


def _modulation(c, w_mod, b_mod):
    L, d, n = w_mod.shape
    bsz = c.shape[0]
    rows = 8 * pl.cdiv(bsz, 8)
    c_pad = jnp.zeros((rows, d), F32).at[:bsz].set(c)
    tn = 1536
    out = pl.pallas_call(
        _mod_kernel,
        grid=(L, n // tn),
        in_specs=[
            pl.BlockSpec((rows, d), lambda l, j: (0, 0)),
            pl.BlockSpec((None, d, tn), lambda l, j: (l, 0, j)),
            pl.BlockSpec((None, 1, tn), lambda l, j: (l, 0, j)),
        ],
        out_specs=pl.BlockSpec((None, rows, tn), lambda l, j: (l, 0, j)),
        out_shape=jax.ShapeDtypeStruct((L, rows, n), F32),
        compiler_params=pltpu.CompilerParams(
            dimension_semantics=("arbitrary", "arbitrary"), vmem_limit_bytes=VMEM_LIMIT),
        name="adaln_modulation",
    )(c_pad, w_mod, b_mod.reshape(L, 1, n))
    return out[:, :bsz].reshape(L, bsz, 6, d)


def _t5_bucket_table():
    n = np.arange(2 * DSA_QBLOCK)
    max_exact = N_BUCKETS // 2
    large = max_exact + (
        np.log(np.maximum(n, max_exact).astype(np.float32) / max_exact)
        / math.log(MAX_DISTANCE / max_exact) * (N_BUCKETS - max_exact)).astype(np.int32)
    large = np.minimum(large, N_BUCKETS - 1)
    return np.where(n < max_exact, n, large).astype(np.int32)


def _bias_kernel(rb_ref, bucket_ref, o_ref):
    for t in range(bucket_ref.shape[0]):
        bucket = bucket_ref[t]
        for h in range(DSA_HEADS):
            acc = jnp.zeros(bucket.shape, F32)
            for b in range(N_BUCKETS):
                acc = jnp.where(bucket == b, rb_ref[b, h], acc)
            o_ref[t, h] = acc


def _bias_tiles(rel_bias):
    table = _t5_bucket_table()
    assert (table[MAX_DISTANCE:] == N_BUCKETS - 1).all()
    t = np.arange(DSA_QBLOCK)[None, :]
    s = np.arange(DSA_QBLOCK)[:, None]
    diag = table[np.maximum(t - s, 0)]
    near = table[DSA_QBLOCK + t - s]
    far = np.full_like(diag, N_BUCKETS - 1)
    buckets = jnp.asarray(np.stack([diag, near, far]).astype(np.int32))
    return pl.pallas_call(
        _bias_kernel,
        in_specs=[pl.BlockSpec(memory_space=pltpu.SMEM), pl.BlockSpec(memory_space=pltpu.VMEM)],
        out_specs=pl.BlockSpec(memory_space=pltpu.VMEM),
        out_shape=jax.ShapeDtypeStruct((3, DSA_HEADS, DSA_QBLOCK, DSA_QBLOCK), F32),
        name="t5_bias_tiles",
    )(rel_bias, buckets)


def _proj_kernel(x_ref, mod_ref, gn_ref, w32_ref, w16_ref, *rest):
    aux32, aux16 = rest[:3], rest[3:5]
    o32_ref, o16_ref, vt_ref, iwt_ref = rest[5:9]
    x = x_ref[...]
    y = x * lax.rsqrt(jnp.mean(x * x, axis=-1, keepdims=True) + EPS) * gn_ref[...]
    h = (y * (1.0 + mod_ref[0, 1:2, :]) + mod_ref[0, 0:1, :]).astype(BF16)
    for w_ref, o_ref, epilogues, aux in ((w32_ref, o32_ref, EPILOGUES_32, aux32),
                                         (w16_ref, o16_ref, EPILOGUES_16, aux16)):
        wcol = ocol = 0
        for width, epi in epilogues:
            y = epi(_dot(h, w_ref[:, wcol:wcol + width]), aux)
            wcol += width
            if epi is _epi_value_t:
                vt_ref[0] = y.T.astype(vt_ref.dtype)
                continue
            if epi is _epi_small:
                iwt_ref[...] = y.T[GLA_GATE_RANK:GLA_GATE_RANK + 8, :]
            o_ref[:, ocol:ocol + width] = y.astype(o_ref.dtype)
            ocol += width
        assert wcol == w_ref.shape[1] and ocol == o_ref.shape[1]


def _norm_proj(x2d, mod_l, gn, w32, w16, layer, aux32, aux16, seq, tm):
    tokens, d = x2d.shape
    tiles_per_seq = seq // tm
    assert tm == DSA_KTILE
    const = lambda a: pl.BlockSpec(a.shape, lambda i: (0, 0))
    resident = lambda a: pl.BlockSpec((None,) + a.shape[1:], lambda i: (layer, 0, 0),
                                      pipeline_mode=pl.Buffered(1))
    return pl.pallas_call(
        _proj_kernel,
        grid=(tokens // tm,),
        in_specs=[
            pl.BlockSpec((tm, d), lambda i: (i, 0)),
            pl.BlockSpec((1, 6, d), lambda i: (i // tiles_per_seq, 0, 0)),
            const(gn), resident(w32), resident(w16),
        ] + [const(a) for a in aux32 + aux16],
        out_specs=[pl.BlockSpec((tm, N32), lambda i: (i, 0)), pl.BlockSpec((tm, N16), lambda i: (i, 0)),
                   pl.BlockSpec((1, D_BRANCH, tm), lambda i: (i, 0, 0)),
                   pl.BlockSpec((8, tm), lambda i: (0, i))],
        out_shape=[jax.ShapeDtypeStruct((tokens, N32), F32), jax.ShapeDtypeStruct((tokens, N16), BF16),
                   jax.ShapeDtypeStruct((tokens // tm, D_BRANCH, tm), BF16),
                   jax.ShapeDtypeStruct((8, tokens), F32)],
        compiler_params=pltpu.CompilerParams(
            dimension_semantics=("parallel",), vmem_limit_bytes=VMEM_LIMIT),
        name="norm_proj",
    )(x2d, mod_l, gn, w32, w16, *aux32, *aux16)


def _epi_raw(y, aux):
    return y


def _epi_gelu(y, aux):
    return jax.nn.gelu(y)


def _epi_gelu_ln(y, aux):
    v = jax.nn.gelu(y)
    mu = jnp.mean(v, axis=-1, keepdims=True)
    var = jnp.mean(jnp.square(v - mu), axis=-1, keepdims=True)
    return (v - mu) * lax.rsqrt(var + EPS) * aux[0][...] + aux[1][...]


def _epi_silu(y, aux):
    return jax.nn.silu(y)


def _epi_gate(k):
    def epi(y, aux):
        return jax.nn.sigmoid(y + aux[2][:, k * COL_TILE:(k + 1) * COL_TILE])
    return epi


def _epi_qnorm(y, aux):
    return _head_rms(y, aux[0][...], DSA_HDIM ** -0.5)


def _epi_knorm(y, aux):
    return _head_rms(y, aux[1][...], 1.0)


def _epi_value_t(y, aux):
    return y


def _epi_small(y, aux):
    return y


EPILOGUES_32 = ([(COL_TILE, e) for e in (_epi_gelu, _epi_gelu_ln, _epi_raw, _epi_silu)]
                + [(COL_TILE, _epi_gate(k)) for k in range(6)] + [(LANES, _epi_small)])
EPILOGUES_16 = [(COL_TILE, e) for e in (_epi_qnorm, _epi_knorm, _epi_value_t, _epi_raw)] + [
    (2 * LANES, _epi_raw), (LANES, _epi_raw)]


def _prep_w_in(w_in):
    sizes = (512, 512, 256, 256, 512, 512, 16, 512, 512, 512, 256, 64, 4, 3072)
    offs = np.concatenate([[0], np.cumsum(sizes)])
    seg = lambda k: w_in[:, :, offs[k]:offs[k + 1]]
    (a_u, a_v, g_q, g_k, g_v, g_r, g_a, d_q, d_k, d_v, d_iq, d_ik, d_iw, gates) = [seg(k) for k in range(14)]
    L, d, _ = w_in.shape
    zeros = lambda n: jnp.zeros((L, d, n), w_in.dtype)
    w32 = jnp.concatenate([a_u, a_v, g_q, g_k, g_r, gates, g_a, d_iw,
                           zeros(N32 - C32_SMALL - 20)], axis=-1).astype(BF16)
    w16 = jnp.concatenate([d_q, d_k, d_v, g_v, d_iq, d_ik, d_ik], axis=-1).astype(BF16)
    return w32, w16


def _gmlp_kernel(u_ref, v_ref, w_ref, b_ref, o_ref):
    tm = u_ref.shape[0]
    for c in range(tm // GMLP_CHUNK):
        rows = slice(c * GMLP_CHUNK, (c + 1) * GMLP_CHUNK)
        for g in range(GMLP_GROUPS):
            cols = slice(g * LANES, (g + 1) * LANES)
            mixed = _dot(w_ref[g], v_ref[rows, cols].astype(BF16)) + b_ref[:, cols]
            o_ref[rows, cols] = (u_ref[rows, cols] * mixed).astype(o_ref.dtype)


def _gmlp(p32, w_tril, bias_full, tm):
    tokens = p32.shape[0]
    return pl.pallas_call(
        _gmlp_kernel,
        grid=(tokens // tm,),
        in_specs=[
            pl.BlockSpec((tm, D_BRANCH), lambda i: (i, C32_U // D_BRANCH)),
            pl.BlockSpec((tm, D_BRANCH), lambda i: (i, C32_V // D_BRANCH)),
            pl.BlockSpec(w_tril.shape, lambda i: (0, 0, 0)),
            pl.BlockSpec(bias_full.shape, lambda i: (0, 0)),
        ],
        out_specs=pl.BlockSpec((tm, D_BRANCH), lambda i: (i, 0)),
        out_shape=jax.ShapeDtypeStruct((tokens, D_BRANCH), BF16),
        compiler_params=pltpu.CompilerParams(
            dimension_semantics=("parallel",), vmem_limit_bytes=VMEM_LIMIT),
        name="gmlp_branch",
    )(p32, p32, w_tril, bias_full)


def _gla_kernel(qk_ref, v_ref, r_ref, ga_ref, w2_ref, b2_ref, ng_ref, tril_ref, exp_ref,
                o_ref, st_ref, d_scr, g_scr):
    C, SUB, H, DK, DV = GLA_CHUNK, GLA_SUB, GLA_HEADS, GLA_DK, GLA_DV
    HK = H * DK
    n_chunks = qk_ref.shape[0] // C

    @pl.when(pl.program_id(1) == 0)
    def _():
        st_ref[...] = jnp.zeros_like(st_ref)

    lane = lax.broadcasted_iota(jnp.int32, (1, HK), 1)
    head_mask = [(lane >= h * DK) & (lane < (h + 1) * DK) for h in range(H)]
    row = lax.broadcasted_iota(jnp.int32, (C, C), 0)
    col = lax.broadcasted_iota(jnp.int32, (C, C), 1)
    sub_shift = SUB.bit_length() - 1
    blk_lower = (row >> sub_shift) > (col >> sub_shift)
    sub_t = lax.broadcasted_iota(jnp.int32, (SUB, 1), 0)

    graw = _dot(ga_ref[...].astype(BF16), w2_ref[...]) + b2_ref[...]
    g_all = jax.nn.log_sigmoid(graw) / GLA_GATE_TAU
    g_scr[...] = g_all
    chunk_decay = jnp.sum(g_all.reshape(n_chunks, C, HK), axis=1)
    mild = jnp.min(chunk_decay) >= GLA_MILD_DECAY

    def cumulative_decay(rows):
        g = g_scr[rows, :]
        g_hi = g.astype(BF16)
        g_r1 = g - g_hi.astype(F32)
        g_mid = g_r1.astype(BF16)
        g_lo = (g_r1 - g_mid.astype(F32)).astype(BF16)
        tril = tril_ref[...]
        return _dot(tril, g_hi) + (_dot(tril, g_mid) + _dot(tril, g_lo))

    def finish(rows, o, v, k_dec, b_last, st):
        upd = _dot_tn(v, k_dec)
        new_st = st * jnp.exp(b_last)
        for h in range(H):
            new_st = new_st + jnp.where(head_mask[h], upd[h * DV:(h + 1) * DV, :], 0.0)
        st_ref[...] = new_st
        o_ref[rows, :] = _head_rms(o, ng_ref[...], 1.0) * r_ref[rows, :]

    def chunk_mild(ci, carry):
        rows = pl.ds(pl.multiple_of(ci * C, C), C)
        q = qk_ref[rows, 0:HK] * (DK ** -0.5)
        k = qk_ref[rows, HK:2 * HK]
        v = v_ref[rows, :]
        b = cumulative_decay(rows)
        b_last = b[C - 1:C, :]
        st = st_ref[...]
        q_in = q * jnp.exp(b)
        k_out = (k * jnp.exp(-b)).astype(BF16)
        k_dec = (k * jnp.exp(b_last - b)).astype(BF16)
        outs = []
        for h in range(H):
            qm = jnp.where(head_mask[h], q_in, 0.0).astype(BF16)
            a_h = jnp.where(row >= col, _dot_nt(qm, k_out), 0.0).astype(BF16)
            outs.append(_dot_nt(qm, st.astype(BF16)) + _dot(a_h, v[:, h * DV:(h + 1) * DV]))
        finish(rows, jnp.concatenate(outs, axis=1), v, k_dec, b_last, st)
        return carry

    def chunk(ci, carry):
        r0 = pl.multiple_of(ci * C, C)
        rows = pl.ds(r0, C)
        q = qk_ref[rows, 0:HK] * (DK ** -0.5)
        k = qk_ref[rows, HK:2 * HK]
        v = v_ref[rows, :]
        b = cumulative_decay(rows)
        b_last = b[C - 1:C, :]
        st = st_ref[...]

        q_in = q * jnp.exp(b)
        k_dec = (k * jnp.exp(b_last - b)).astype(BF16)

        a_off = [jnp.zeros((C, C), F32) for _ in range(H)]
        for j in range(C // SUB - 1):
            bj = b[(j + 1) * SUB - 1:(j + 1) * SUB, :]
            qj = q * jnp.exp(jnp.minimum(b - bj, 0.0))
            in_blk = (lax.broadcasted_iota(jnp.int32, (C, 1), 0) >> sub_shift) == j
            kj = jnp.where(in_blk, k * jnp.exp(jnp.minimum(bj - b, 0.0)), 0.0).astype(BF16)
            for h in range(H):
                a_off[h] = a_off[h] + _dot_nt(jnp.where(head_mask[h], qj, 0.0).astype(BF16), kj)

        for i in range(C // SUB):
            rs = slice(i * SUB, (i + 1) * SUB)
            qi, bi = q[rs, :], b[rs, :]
            for s in range(SUB):
                ks = k[i * SUB + s:i * SUB + s + 1, :]
                bs = b[i * SUB + s:i * SUB + s + 1, :]
                dterm = qi * ks * jnp.exp(jnp.minimum(bi - bs, 0.0))
                dterm = jnp.where(sub_t >= s, dterm, 0.0)
                d_scr[s * SUB:(s + 1) * SUB, :] = dterm.astype(BF16)
            gsum = _dot(d_scr[...], exp_ref[...])
            od = jnp.zeros((SUB, H * DV), F32)
            for s in range(SUB):
                vs = v[i * SUB + s:i * SUB + s + 1, :].astype(F32)
                od = od + gsum[s * SUB:(s + 1) * SUB, :] * vs
            o_ref[pl.ds(r0 + i * SUB, SUB), :] = od

        outs = []
        for h in range(H):
            vh = v[:, h * DV:(h + 1) * DV]
            o_inter = _dot_nt(jnp.where(head_mask[h], q_in, 0.0).astype(BF16), st.astype(BF16))
            a_h = jnp.where(blk_lower, a_off[h], 0.0).astype(BF16)
            outs.append(o_inter + _dot(a_h, vh))
        o = o_ref[rows, :] + jnp.concatenate(outs, axis=1)
        finish(rows, o, v, k_dec, b_last, st)
        return carry

    @pl.when(mild)
    def _():
        lax.fori_loop(0, n_chunks, chunk_mild, 0, unroll=4)

    @pl.when(jnp.logical_not(mild))
    def _():
        lax.fori_loop(0, n_chunks, chunk, 0)


def _gla(p32, p16, w2p, b2, norm_g, bsz, seq, ts):
    C, SUB, H, DK, DV = GLA_CHUNK, GLA_SUB, GLA_HEADS, GLA_DK, GLA_DV
    tokens = p32.shape[0]
    blocks_per_seq = seq // ts
    tril = jnp.asarray(np.tril(np.ones((C, C), np.float32))).astype(BF16)
    expand = np.zeros((H * DK, H * DV), np.float32)
    for h in range(H):
        expand[h * DK:(h + 1) * DK, h * DV:(h + 1) * DV] = 1.0
    expand = jnp.asarray(expand).astype(BF16)
    row_map = lambda cb: (lambda b, i: (b * blocks_per_seq + i, cb))
    const2 = lambda b, i: (0, 0)
    out = pl.pallas_call(
        _gla_kernel,
        grid=(bsz, blocks_per_seq),
        in_specs=[
            pl.BlockSpec((ts, 2 * H * DK), row_map(C32_GQ // (2 * H * DK))),
            pl.BlockSpec((ts, H * DV), row_map(C16_GV // (H * DV))),
            pl.BlockSpec((ts, H * DV), row_map(C32_R // (H * DV))),
            pl.BlockSpec((ts, LANES), row_map(C32_SMALL // LANES)),
            pl.BlockSpec(w2p.shape, const2),
            pl.BlockSpec(b2.shape, const2),
            pl.BlockSpec(norm_g.shape, const2),
            pl.BlockSpec(tril.shape, const2),
            pl.BlockSpec(expand.shape, const2),
        ],
        out_specs=pl.BlockSpec((ts, H * DV), lambda b, i: (b * blocks_per_seq + i, 0)),
        out_shape=jax.ShapeDtypeStruct((tokens, H * DV), F32),
        scratch_shapes=[pltpu.VMEM((DV, H * DK), F32), pltpu.VMEM((SUB * SUB, H * DK), BF16),
                        pltpu.VMEM((ts, H * DK), F32)],
        compiler_params=pltpu.CompilerParams(
            dimension_semantics=("parallel", "arbitrary"), vmem_limit_bytes=VMEM_LIMIT),
        name="gla_branch",
    )(p32, p16, p32, p32, w2p, b2, norm_g, tril, expand)
    return out


def _bit_transpose32(words):
    a = list(words)
    shift, mask = 16, 0x0000FFFF
    while shift:
        m = jnp.int32(np.uint32(mask).astype(np.int32))
        for k in range(32):
            if k & shift == 0:
                t = (a[k] ^ lax.shift_right_logical(a[k + shift], jnp.int32(shift))) & m
                a[k] = a[k] ^ t
                a[k + shift] = a[k + shift] ^ lax.shift_left(t, jnp.int32(shift))
        shift >>= 1
        mask = (mask ^ (mask << shift)) & 0xFFFFFFFF
    return a


def _dsa_kernel(q_ref, iq_ref, iwt_ref, k_ref, vt_ref, ik_ref, bias_ref, ltri_ref,
                o_ref, key_scr, plane_scr, s_scr, acc_scr, *, topk):
    T, TK = DSA_QBLOCK, DSA_KTILE
    SUBS = TK // T
    H, HD = DSA_HEADS, DSA_HDIM
    qb = pl.program_id(1)
    kt_last = qb // SUBS
    n_kt = kt_last + 1
    key_minus_query = (lax.broadcasted_iota(jnp.int32, (TK, T), 0)
                       - lax.broadcasted_iota(jnp.int32, (TK, T), 1))

    def fold(x, op):
        return op(x.reshape(x.shape[0] // 8, 8, T), axis=0)

    def rows_to_one(x, op):
        return op(x, axis=0, keepdims=True)

    def key_rows(kt):
        return pl.ds(pl.multiple_of(kt * TK, TK), TK)

    def tile_loop(n, body, init):
        carry = lax.fori_loop(0, n // 2, lambda i, c: body(2 * i + 1, body(2 * i, c)), init)
        return lax.cond(n % 2 == 1, lambda c: body(n - 1, c), lambda c: c, carry)

    lo_mask = lax.broadcasted_iota(jnp.int32, (T, LANES), 1) < DSA_IDX_DIM
    iq = [iq_ref[:, 0:LANES], iq_ref[:, LANES:2 * LANES]]
    iq_h = [jnp.where(lo_mask, iq[0], 0), jnp.where(lo_mask, 0, iq[0]),
            jnp.where(lo_mask, iq[1], 0), jnp.where(lo_mask, 0, iq[1])]
    iw_h = [iwt_ref[h:h + 1, :] for h in range(DSA_IDX_HEADS)]

    GRP_ROWS = 32 * 8
    GRPS = TK // GRP_ROWS
    n_grp = plane_scr.shape[0]

    @pl.when(qb == 0)
    def _():
        plane_scr[...] = jnp.zeros_like(plane_scr)

    def score_tile(kt):
        ik2 = ik_ref[key_rows(kt), :]
        score = jnp.zeros((TK, T), F32)
        for h in range(DSA_IDX_HEADS):
            score = score + iw_h[h] * jnp.maximum(_dot_nt(ik2, iq_h[h]), 0.0)
        bits = lax.bitcast_convert_type(score, jnp.int32)
        return jnp.where(score == 0.0, 0, jnp.where(bits >= 0, bits, bits ^ jnp.int32(0x7FFFFFFF)))

    def store_planes(kt, key):
        unsigned_order = key ^ jnp.int32(INT_MIN)
        for g in range(GRPS):
            words = [unsigned_order[g * GRP_ROWS + 8 * j:g * GRP_ROWS + 8 * j + 8, :] for j in range(32)]
            planes = _bit_transpose32(words)
            for b in range(32):
                plane_scr[kt * GRPS + g, b] = planes[b]

    def score_body(kt, c):
        key = score_tile(kt)
        key_scr[kt] = key
        store_planes(kt, key)
        return c

    tile_loop(kt_last, score_body, 0)
    admissible = key_minus_query <= qb * T - kt_last * TK
    key = score_tile(kt_last)
    key_scr[kt_last] = jnp.where(admissible, key, jnp.int32(INT_MIN))
    store_planes(kt_last, key)
    alive_last = []
    for g in range(GRPS):
        word = jnp.zeros((8, T), jnp.int32)
        for j in range(32):
            adm = admissible[g * GRP_ROWS + 8 * j:g * GRP_ROWS + 8 * j + 8, :]
            word = word | jnp.where(adm, jnp.int32(np.int32(np.uint32(1 << (31 - j)))), 0)
        alive_last.append(word)

    alive = []
    for g in range(n_grp):
        kt = g // GRPS
        full = jnp.broadcast_to(jnp.where(kt < kt_last, jnp.int32(-1), jnp.int32(0)), (8, T))
        alive.append(jnp.where(kt == kt_last, alive_last[g % GRPS], full))

    def bit_body(b, carry):
        above, tau_u, alive = carry
        planes = [plane_scr[g, b] for g in range(n_grp)]
        ones = jnp.zeros((8, T), jnp.int32)
        for g in range(n_grp):
            ones = ones + lax.population_count(alive[g] & planes[g])
        ones = rows_to_one(ones, jnp.sum)
        take = above + ones >= topk
        flip = jnp.where(take, jnp.int32(0), jnp.int32(-1))
        alive = tuple(alive[g] & (planes[g] ^ flip) for g in range(n_grp))
        above = jnp.where(take, above, above + ones)
        tau_u = tau_u | jnp.where(take, lax.shift_left(jnp.int32(1), 31 - b), 0)
        return above, tau_u, alive

    zero_row = jnp.zeros((1, T), jnp.int32)
    above, tau_u, alive = lax.fori_loop(0, 32, bit_body, (zero_row, zero_row, tuple(alive)))
    n_tied = jnp.zeros((8, T), jnp.int32)
    for g in range(n_grp):
        n_tied = n_tied + lax.population_count(alive[g])
    n_ge = above + rows_to_one(n_tied, jnp.sum)
    tau = jnp.maximum(tau_u ^ jnp.int32(INT_MIN), jnp.int32(INT_MIN + 1))

    @pl.when(jnp.max(n_ge) > topk)
    def _():
        need = (topk - above).astype(F32)

        def tie_body(kt, seen):
            for j in range(SUBS):
                rows = slice(j * T, (j + 1) * T)
                key = key_scr[kt, rows, :]
                eq = key == tau
                eq_f = jnp.where(eq, 1.0, 0.0)
                pref = _dot(ltri_ref[...], eq_f.astype(BF16)) + seen
                key_scr[kt, rows, :] = jnp.where(eq & (pref > need), tau - 1, key)
                seen = seen + rows_to_one(eq_f, jnp.sum)
            return seen

        lax.fori_loop(0, n_kt, tie_body, jnp.zeros((1, T), F32))

    q_h = [q_ref[:, h * HD:(h + 1) * HD] for h in range(H)]

    def logits_body(kt, m_run):
        sel = key_scr[kt] >= tau
        kinds = [jnp.clip(qb - (kt * SUBS + j), 0, 2) for j in range(SUBS)]
        new_m = []
        for h in range(H):
            bias = jnp.concatenate([bias_ref[kinds[j], h] for j in range(SUBS)], axis=0)
            s = _dot_nt(k_ref[key_rows(kt), h * HD:(h + 1) * HD], q_h[h]) + bias
            s = jnp.where(sel, s, NEG_BIG)
            s_scr[h, kt] = s
            new_m.append(jnp.maximum(m_run[h], fold(s, jnp.max)))
        return tuple(new_m)

    m_run = tile_loop(n_kt, logits_body, tuple(jnp.full((8, T), NEG_BIG, F32) for _ in range(H)))
    m_h = [rows_to_one(m, jnp.max) for m in m_run]

    for h in range(H):
        acc_scr[h] = jnp.zeros((HD, T), F32)

    def pv_body(kt, l_run):
        new_l = []
        for h in range(H):
            p = jnp.exp(s_scr[h, kt] - m_h[h])
            new_l.append(l_run[h] + fold(p, jnp.sum))
            acc_scr[h] = acc_scr[h] + _dot(vt_ref[kt, h * HD:(h + 1) * HD, :], p.astype(BF16))
        return tuple(new_l)

    l_run = tile_loop(n_kt, pv_body, tuple(jnp.zeros((8, T), F32) for _ in range(H)))

    for h in range(H):
        out_t = acc_scr[h] / rows_to_one(l_run[h], jnp.sum)
        o_ref[:, h * HD:(h + 1) * HD] = out_t.T.astype(o_ref.dtype)


def _dsa(p16, v_t, iw_t, bias_tiles, bsz, seq):
    T = DSA_QBLOCK
    tokens = p16.shape[0]
    nqb = seq // T
    topk = min(DSA_TOPK_MAX, seq // 4)
    TK = min(DSA_KTILE, seq)
    assert TK == DSA_KTILE and seq % TK == 0
    nkt = seq // TK
    ltri = jnp.asarray(np.tril(np.ones((T, T), np.float32))).astype(BF16)
    w = D_BRANCH
    v_t = v_t.reshape(bsz, nkt, w, TK)
    qmap = lambda cb: (lambda b, i: (b * nqb + i, cb))
    return pl.pallas_call(
        functools.partial(_dsa_kernel, topk=topk),
        grid=(bsz, nqb),
        in_specs=[
            pl.BlockSpec((T, w), qmap(C16_Q // w)),
            pl.BlockSpec((T, 2 * LANES), qmap(C16_IQ // (2 * LANES))),
            pl.BlockSpec((8, T), lambda b, i: (0, b * nqb + i)),
            pl.BlockSpec((seq, w), lambda b, i: (b, C16_K // w)),
            pl.BlockSpec((None, nkt, w, TK), lambda b, i: (b, 0, 0, 0)),
            pl.BlockSpec((seq, LANES), lambda b, i: (b, C16_IK // LANES)),
            pl.BlockSpec(bias_tiles.shape, lambda b, i: (0, 0, 0, 0)),
            pl.BlockSpec(ltri.shape, lambda b, i: (0, 0)),
        ],
        out_specs=pl.BlockSpec((T, w), lambda b, i: (b * nqb + i, 0)),
        out_shape=jax.ShapeDtypeStruct((tokens, w), BF16),
        scratch_shapes=[
            pltpu.VMEM((nkt, TK, T), jnp.int32),
            pltpu.VMEM((nkt * TK // 256, 32, 8, T), jnp.int32),
            pltpu.VMEM((DSA_HEADS, nkt, TK, T), F32),
            pltpu.VMEM((DSA_HEADS, DSA_HDIM, T), F32),
        ],
        compiler_params=pltpu.CompilerParams(
            dimension_semantics=("parallel", "arbitrary"), vmem_limit_bytes=VMEM_LIMIT),
        name="dsa_branch",
    )(p16, p16, iw_t, p16, v_t, p16, bias_tiles, ltri)


def _merge_kernel(ya_ref, yb_ref, yc_ref, ga_ref, gb_ref, gc_ref, x_ref, mod_ref, wbr_ref, wout_ref,
                  gn2_ref, wr_hi_ref, wr_lo_ref, br_ref, x1_ref, h2_ref, lg_ref):
    merged = ga_ref[...] * _dot(ya_ref[...], wbr_ref[0])
    merged = merged + gb_ref[...] * _dot(yb_ref[...].astype(BF16), wbr_ref[1])
    merged = merged + gc_ref[...] * _dot(yc_ref[...], wbr_ref[2])
    mix = _dot(merged.astype(BF16), wout_ref[...])
    x1 = x_ref[...] + mod_ref[0, 2:3, :] * mix
    x1_ref[...] = x1
    y = x1 * lax.rsqrt(jnp.mean(x1 * x1, axis=-1, keepdims=True) + EPS) * gn2_ref[...]
    h2 = y * (1.0 + mod_ref[0, 4:5, :]) + mod_ref[0, 3:4, :]
    h2_ref[...] = h2.astype(BF16)
    lg_ref[...] = _dot3(h2, wr_hi_ref[...], wr_lo_ref[...]) + br_ref[...]


def _merge(ya, yb, yc, p32, x2d, mod_l, wbr, wout, layer, gn2, wr_hi, wr_lo, br, seq, tm):
    tokens, d = x2d.shape
    tiles_per_seq = seq // tm
    row = lambda i: (i, 0)
    gate = lambda k: (lambda i: (i, C32_GATES // d + k))
    c2 = lambda i: (0, 0)
    of_layer = lambda a: pl.BlockSpec((None,) + a.shape[1:], lambda i: (layer,) + (0,) * (a.ndim - 1))
    return pl.pallas_call(
        _merge_kernel,
        grid=(tokens // tm,),
        in_specs=[
            pl.BlockSpec((tm, D_BRANCH), row), pl.BlockSpec((tm, D_BRANCH), row),
            pl.BlockSpec((tm, D_BRANCH), row),
            pl.BlockSpec((tm, d), gate(0)), pl.BlockSpec((tm, d), gate(1)), pl.BlockSpec((tm, d), gate(2)),
            pl.BlockSpec((tm, d), row),
            pl.BlockSpec((1, 6, d), lambda i: (i // tiles_per_seq, 0, 0)),
            of_layer(wbr), of_layer(wout),
            pl.BlockSpec(gn2.shape, c2),
            of_layer(wr_hi), of_layer(wr_lo), of_layer(br),
        ],
        out_specs=[pl.BlockSpec((tm, d), row), pl.BlockSpec((tm, d), row), pl.BlockSpec((tm, LANES), row)],
        out_shape=[jax.ShapeDtypeStruct((tokens, d), F32), jax.ShapeDtypeStruct((tokens, d), BF16),
                   jax.ShapeDtypeStruct((tokens, LANES), F32)],
        compiler_params=pltpu.CompilerParams(
            dimension_semantics=("parallel",), vmem_limit_bytes=VMEM_LIMIT),
        name="merge_norm_router",
    )(ya, yb, yc, p32, p32, p32, x2d, mod_l, wbr, wout, gn2, wr_hi, wr_lo, br)


def _route(lg):
    lane = lax.broadcasted_iota(jnp.int32, lg.shape, 1)
    big = jnp.int32(10 ** 6)
    is_grp = lane < N_GROUPS
    gl = jnp.where(is_grp, lg, -jnp.inf)
    gmax = jnp.max(gl, axis=1, keepdims=True)
    gsum = jnp.sum(jnp.where(is_grp, jnp.exp(lg - gmax), 0.0), axis=1, keepdims=True)
    p_g = 1.0 / gsum
    g_idx = jnp.min(jnp.where(gl == gmax, lane, big), axis=1, keepdims=True)
    first = N_GROUPS + g_idx * EXPERTS_PER_GROUP
    in_grp = (lane >= first) & (lane < first + EXPERTS_PER_GROUP)
    e1 = jnp.where(in_grp, lg, -jnp.inf)
    v1 = jnp.max(e1, axis=1, keepdims=True)
    i1 = jnp.min(jnp.where(e1 == v1, lane, big), axis=1, keepdims=True)
    e2 = jnp.where(in_grp & (lane != i1), lg, -jnp.inf)
    v2 = jnp.max(e2, axis=1, keepdims=True)
    i2 = jnp.min(jnp.where(e2 == v2, lane, big), axis=1, keepdims=True)
    t = jnp.exp(v2 - v1)
    w1 = p_g * (1.0 / (1.0 + t))
    w2 = p_g * (t / (1.0 + t))
    return jnp.where(lane == i1, w1, jnp.where(lane == i2, w2, 0.0))


def _moe_kernel(h_ref, lg_ref, x1_ref, mod_ref, wg_ref, wu_ref, wd_ref, o_ref, comb_scr, acc_scr):
    grp = pl.program_id(1)

    @pl.when(grp == 0)
    def _():
        comb = _route(lg_ref[...])
        hi = comb.astype(BF16)
        rest = comb - hi.astype(F32)
        mid = rest.astype(BF16)
        comb_scr[0] = hi
        comb_scr[1] = mid
        comb_scr[2] = (rest - mid.astype(F32)).astype(BF16)
        acc_scr[...] = jnp.zeros_like(acc_scr)

    h = h_ref[...]
    row_id = lax.broadcasted_iota(jnp.int32, (LANES, LANES), 0)
    acts = []
    for j in range(EXPERTS_PER_GROUP):
        lane_of_expert = N_GROUPS + grp * EXPERTS_PER_GROUP + j
        pick = jnp.where(row_id == lane_of_expert, 1.0, 0.0).astype(BF16)
        cw = _dot(comb_scr[0], pick) + (_dot(comb_scr[1], pick) + _dot(comb_scr[2], pick))
        act = (jax.nn.silu(_dot(h, wg_ref[j])) * _dot(h, wu_ref[j])
               * jnp.concatenate([cw, cw], axis=1))
        acts.append(act.astype(BF16))
    acc_scr[...] += _dot(jnp.concatenate(acts, axis=1), wd_ref[...])

    @pl.when(grp == pl.num_programs(1) - 1)
    def _():
        o_ref[...] = x1_ref[...] + mod_ref[0, 5:6, :] * acc_scr[...]


def _moe(h2, lg, x1, mod_l, wg, wu, wd, layer, seq, tm):
    tokens, d = x1.shape
    tiles_per_seq = seq // tm
    row = lambda i, g: (i, 0)
    return pl.pallas_call(
        _moe_kernel,
        grid=(tokens // tm, N_GROUPS),
        in_specs=[
            pl.BlockSpec((tm, d), row), pl.BlockSpec((tm, LANES), row), pl.BlockSpec((tm, d), row),
            pl.BlockSpec((1, 6, d), lambda i, g: (i // tiles_per_seq, 0, 0)),
            pl.BlockSpec((None, None) + wg.shape[2:], lambda i, g: (layer, g, 0, 0, 0)),
            pl.BlockSpec((None, None) + wu.shape[2:], lambda i, g: (layer, g, 0, 0, 0)),
            pl.BlockSpec((None, None) + wd.shape[2:], lambda i, g: (layer, g, 0, 0)),
        ],
        out_specs=pl.BlockSpec((tm, d), row),
        out_shape=jax.ShapeDtypeStruct((tokens, d), F32),
        scratch_shapes=[pltpu.VMEM((3, tm, LANES), BF16), pltpu.VMEM((tm, d), F32)],
        compiler_params=pltpu.CompilerParams(
            dimension_semantics=("parallel", "arbitrary"), vmem_limit_bytes=VMEM_LIMIT),
        name="hier_moe",
    )(h2, lg, x1, mod_l, wg, wu, wd)


def _row_tile(seq, want):
    t = min(want, seq)
    assert seq % t == 0
    return t


def kernel(x, c, w_mod, b_mod, g_norm1, g_norm2, w_in, gmlp_ln_g, gmlp_ln_b, gmlp_w_s, gmlp_b_s,
           gla_w_gate2, gla_b_gate, gla_norm_g, dsa_qnorm_g, dsa_knorm_g, rel_bias, w_branch,
           b_branch_gate, w_out, w_group, b_group, w_router, b_router, w_exp_gate, w_exp_up,
           w_exp_down):
    bsz, seq, d = x.shape
    depth = w_mod.shape[0]
    assert d == D_MODEL and seq % DSA_QBLOCK == 0 and seq % GLA_CHUNK == 0
    tokens = bsz * seq
    tm = _row_tile(seq, 512)

    mods = _modulation(c, w_mod, b_mod)
    bias_tiles = _bias_tiles(rel_bias)
    w32, w16 = _prep_w_in(w_in)

    causal = np.tril(np.ones((GMLP_CHUNK, GMLP_CHUNK), bool))
    w_tril = jnp.where(causal[None, None], gmlp_w_s, 0.0).astype(BF16)
    gmlp_bias = jnp.repeat(jnp.swapaxes(gmlp_b_s, 1, 2), LANES, axis=-1)
    w2p = jnp.zeros((depth, LANES, GLA_HEADS * GLA_DK), F32).at[:, :GLA_GATE_RANK].set(gla_w_gate2)
    w2p = w2p.astype(BF16)
    wr = jnp.zeros((depth, d, LANES), F32)
    wr = wr.at[:, :, :N_GROUPS].set(w_group).at[:, :, N_GROUPS:N_GROUPS + N_EXPERTS].set(w_router)
    wr_hi = wr.astype(BF16)
    wr_lo = (wr - wr_hi.astype(F32)).astype(BF16)
    br = jnp.zeros((depth, 1, LANES), F32)
    br = br.at[:, 0, :N_GROUPS].set(b_group).at[:, 0, N_GROUPS:N_GROUPS + N_EXPERTS].set(b_router)
    wbr = w_branch.astype(BF16)
    wout = w_out.astype(BF16)
    grouped = (depth, N_GROUPS, EXPERTS_PER_GROUP, d, D_EXPERT)
    wg, wu = w_exp_gate.astype(BF16).reshape(grouped), w_exp_up.astype(BF16).reshape(grouped)
    wd = w_exp_down.astype(BF16).reshape(depth, N_GROUPS, EXPERTS_PER_GROUP * D_EXPERT, d)
    tm_moe = _row_tile(seq, 1024)

    x2d = x.reshape(tokens, d)
    for l in range(depth):
        mod_l = mods[l]
        aux32 = [gmlp_ln_g[l][None], gmlp_ln_b[l][None], b_branch_gate[l].reshape(1, -1)]
        aux16 = [dsa_qnorm_g[l][None], dsa_knorm_g[l][None]]
        p32, p16, v_t, iw_t = _norm_proj(x2d, mod_l, g_norm1[l][None], w32, w16, l, aux32, aux16,
                                         seq, tm)
        ya = _gmlp(p32, w_tril[l], gmlp_bias[l], tm)
        yb = _gla(p32, p16, w2p[l], gla_b_gate[l][None], gla_norm_g[l][None], bsz, seq, tm)
        yc = _dsa(p16, v_t, iw_t, bias_tiles, bsz, seq)
        x1, h2, lg = _merge(ya, yb, yc, p32, x2d, mod_l, wbr, wout, l, g_norm2[l][None],
                            wr_hi, wr_lo, br, seq, tm)
        x2d = _moe(h2, lg, x1, mod_l, wg, wu, wd, l, seq, tm_moe)
    return x2d.reshape(bsz, seq, d)
```

```python
import functools
import math

import numpy as np
import jax
import jax.numpy as jnp
from jax import lax
from jax.experimental import pallas as pl
from jax.experimental.pallas import tpu as pltpu

F32 = jnp.float32
BF16 = jnp.bfloat16

D_MODEL = 1024
D_BRANCH = 512
EPS = 1e-6
GMLP_CHUNK = 128
GMLP_GROUPS = 4
GLA_HEADS = 4
GLA_DK = 64
GLA_DV = 128
GLA_GATE_RANK = 16
GLA_GATE_TAU = 16.0
GLA_CHUNK = 128
GLA_SUB = 16
GLA_MILD_DECAY = -60.0
DSA_HEADS = 4
DSA_HDIM = 128
DSA_IDX_HEADS = 4
DSA_IDX_DIM = 64
DSA_QBLOCK = 128
DSA_KTILE = 512
DSA_TOPK_MAX = 256
N_BUCKETS = 32
MAX_DISTANCE = 128
N_GROUPS = 4
EXPERTS_PER_GROUP = 4
N_EXPERTS = 16
D_EXPERT = 256

LANES = 128
COL_TILE = 512
VMEM_LIMIT = 56 * 1024 * 1024
INT_MIN = -(2 ** 31)
NEG_BIG = -1e30

C32_U, C32_V, C32_GQ, C32_GK, C32_R, C32_SMALL = 0, 512, 1024, 1280, 1536, 2048
N32 = C32_SMALL + LANES
C16_GATES, C16_Q, C16_K, C16_GV, C16_IQ, C16_IK = 0, 3072, 3584, 4096, 4608, 4864
N16 = C16_IK + LANES


def _dot(a, b):
    return jnp.dot(a, b, preferred_element_type=F32)


def _dot_nt(a, b):
    return lax.dot_general(a, b, (((1,), (1,)), ((), ())), preferred_element_type=F32)


def _dot_tn(a, b):
    return lax.dot_general(a, b, (((0,), (0,)), ((), ())), preferred_element_type=F32)


def _split2(a):
    hi = a.astype(BF16)
    lo = (a - hi.astype(F32)).astype(BF16)
    return hi, lo


def _dot3(a, w_hi, w_lo):
    a_hi, a_lo = _split2(a)
    return _dot(a_hi, w_hi) + (_dot(a_lo, w_hi) + _dot(a_hi, w_lo))


def _head_rms(y, g, scale):
    outs = []
    for h in range(y.shape[1] // LANES):
        yh = y[:, h * LANES:(h + 1) * LANES]
        ms = jnp.mean(yh * yh, axis=-1, keepdims=True)
        o = yh * lax.rsqrt(ms + EPS) * g
        if scale != 1.0:
            o = o * scale
        outs.append(o)
    return jnp.concatenate(outs, axis=1)


def _mod_kernel(c_ref, w_ref, b_ref, o_ref):
    a = jax.nn.silu(c_ref[...])
    w_hi, w_lo = _split2(w_ref[...])
    o_ref[...] = _dot3(a, w_hi, w_lo) + b_ref[...]


def _modulation(c, w_mod, b_mod):
    L, d, n = w_mod.shape
    bsz = c.shape[0]
    rows = 8 * pl.cdiv(bsz, 8)
    c_pad = jnp.zeros((rows, d), F32).at[:bsz].set(c)
    tn = 1536
    out = pl.pallas_call(
        _mod_kernel,
        grid=(L, n // tn),
        in_specs=[
            pl.BlockSpec((rows, d), lambda l, j: (0, 0)),
            pl.BlockSpec((None, d, tn), lambda l, j: (l, 0, j)),
            pl.BlockSpec((None, 1, tn), lambda l, j: (l, 0, j)),
        ],
        out_specs=pl.BlockSpec((None, rows, tn), lambda l, j: (l, 0, j)),
        out_shape=jax.ShapeDtypeStruct((L, rows, n), F32),
        compiler_params=pltpu.CompilerParams(
            dimension_semantics=("arbitrary", "arbitrary"), vmem_limit_bytes=VMEM_LIMIT),
        name="adaln_modulation",
    )(c_pad, w_mod, b_mod.reshape(L, 1, n))
    return out[:, :bsz].reshape(L, bsz, 6, d)


def _t5_bucket_table():
    n = np.arange(2 * DSA_QBLOCK)
    max_exact = N_BUCKETS // 2
    large = max_exact + (
        np.log(np.maximum(n, max_exact).astype(np.float32) / max_exact)
        / math.log(MAX_DISTANCE / max_exact) * (N_BUCKETS - max_exact)).astype(np.int32)
    large = np.minimum(large, N_BUCKETS - 1)
    return np.where(n < max_exact, n, large).astype(np.int32)


def _bias_kernel(rb_ref, bucket_ref, o_ref):
    for t in range(bucket_ref.shape[0]):
        bucket = bucket_ref[t]
        for h in range(DSA_HEADS):
            acc = jnp.zeros(bucket.shape, F32)
            for b in range(N_BUCKETS):
                acc = jnp.where(bucket == b, rb_ref[b, h], acc)
            o_ref[t, h] = acc


def _bias_tiles(rel_bias):
    table = _t5_bucket_table()
    assert (table[MAX_DISTANCE:] == N_BUCKETS - 1).all()
    t = np.arange(DSA_QBLOCK)[None, :]
    s = np.arange(DSA_QBLOCK)[:, None]
    diag = table[np.maximum(t - s, 0)]
    near = table[DSA_QBLOCK + t - s]
    far = np.full_like(diag, N_BUCKETS - 1)
    buckets = jnp.asarray(np.stack([diag, near, far]).astype(np.int32))
    return pl.pallas_call(
        _bias_kernel,
        in_specs=[pl.BlockSpec(memory_space=pltpu.SMEM), pl.BlockSpec(memory_space=pltpu.VMEM)],
        out_specs=pl.BlockSpec(memory_space=pltpu.VMEM),
        out_shape=jax.ShapeDtypeStruct((3, DSA_HEADS, DSA_QBLOCK, DSA_QBLOCK), F32),
        name="t5_bias_tiles",
    )(rel_bias, buckets)


def _proj_kernel(x_ref, mod_ref, gn_ref, w32_ref, w16_ref, *rest):
    aux32, aux16 = rest[:2], rest[2:5]
    o32_ref, o16_ref, vt_ref, iwt_ref = rest[5:9]
    x = x_ref[...]
    y = x * lax.rsqrt(jnp.mean(x * x, axis=-1, keepdims=True) + EPS) * gn_ref[...]
    h = (y * (1.0 + mod_ref[0, 1:2, :]) + mod_ref[0, 0:1, :]).astype(BF16)
    for w_ref, o_ref, epilogues, aux in ((w32_ref, o32_ref, EPILOGUES_32, aux32),
                                         (w16_ref, o16_ref, EPILOGUES_16, aux16)):
        wcol = ocol = 0
        for width, epi in epilogues:
            y = epi(_dot(h, w_ref[:, wcol:wcol + width]), aux)
            wcol += width
            if epi is _epi_value_t:
                vt_ref[0] = y.T.astype(vt_ref.dtype)
                continue
            if epi is _epi_small:
                iwt_ref[...] = y.T[GLA_GATE_RANK:GLA_GATE_RANK + 8, :]
            o_ref[:, ocol:ocol + width] = y.astype(o_ref.dtype)
            ocol += width
        assert wcol == w_ref.shape[1] and ocol == o_ref.shape[1]


def _norm_proj(x2d, mod_l, gn, w32, w16, layer, aux32, aux16, seq, tm):
    tokens, d = x2d.shape
    tiles_per_seq = seq // tm
    assert tm == DSA_KTILE
    const = lambda a: pl.BlockSpec(a.shape, lambda i: (0, 0))
    resident = lambda a: pl.BlockSpec((None,) + a.shape[1:], lambda i: (layer, 0, 0),
                                      pipeline_mode=pl.Buffered(1))
    return pl.pallas_call(
        _proj_kernel,
        grid=(tokens // tm,),
        in_specs=[
            pl.BlockSpec((tm, d), lambda i: (i, 0)),
            pl.BlockSpec((1, 6, d), lambda i: (i // tiles_per_seq, 0, 0)),
            const(gn), resident(w32), resident(w16),
        ] + [const(a) for a in aux32 + aux16],
        out_specs=[pl.BlockSpec((tm, N32), lambda i: (i, 0)), pl.BlockSpec((tm, N16), lambda i: (i, 0)),
                   pl.BlockSpec((1, D_BRANCH, tm), lambda i: (i, 0, 0)),
                   pl.BlockSpec((8, tm), lambda i: (0, i))],
        out_shape=[jax.ShapeDtypeStruct((tokens, N32), F32), jax.ShapeDtypeStruct((tokens, N16), BF16),
                   jax.ShapeDtypeStruct((tokens // tm, D_BRANCH, tm), BF16),
                   jax.ShapeDtypeStruct((8, tokens), F32)],
        compiler_params=pltpu.CompilerParams(
            dimension_semantics=("parallel",), vmem_limit_bytes=VMEM_LIMIT),
        name="norm_proj",
    )(x2d, mod_l, gn, w32, w16, *aux32, *aux16)


def _epi_raw(y, aux):
    return y


def _epi_gelu(y, aux):
    return jax.nn.gelu(y)


def _epi_gelu_ln(y, aux):
    v = jax.nn.gelu(y)
    mu = jnp.mean(v, axis=-1, keepdims=True)
    var = jnp.mean(jnp.square(v - mu), axis=-1, keepdims=True)
    return (v - mu) * lax.rsqrt(var + EPS) * aux[0][...] + aux[1][...]


def _epi_silu(y, aux):
    return jax.nn.silu(y)


def _epi_gate(k):
    def epi(y, aux):
        return jax.nn.sigmoid(y + aux[2][:, k * COL_TILE:(k + 1) * COL_TILE])
    return epi


def _epi_qnorm(y, aux):
    return _head_rms(y, aux[0][...], DSA_HDIM ** -0.5)


def _epi_knorm(y, aux):
    return _head_rms(y, aux[1][...], 1.0)


def _epi_value_t(y, aux):
    return y


def _epi_small(y, aux):
    return y


EPILOGUES_32 = ([(COL_TILE, e) for e in (_epi_gelu, _epi_gelu_ln, _epi_raw, _epi_silu)]
                + [(LANES, _epi_small)])
EPILOGUES_16 = ([(COL_TILE, _epi_gate(k)) for k in range(6)]
                + [(COL_TILE, e) for e in (_epi_qnorm, _epi_knorm, _epi_value_t, _epi_raw)]
                + [(2 * LANES, _epi_raw), (LANES, _epi_raw)])


def _prep_w_in(w_in):
    sizes = (512, 512, 256, 256, 512, 512, 16, 512, 512, 512, 256, 64, 4, 3072)
    offs = np.concatenate([[0], np.cumsum(sizes)])
    seg = lambda k: w_in[:, :, offs[k]:offs[k + 1]]
    (a_u, a_v, g_q, g_k, g_v, g_r, g_a, d_q, d_k, d_v, d_iq, d_ik, d_iw, gates) = [seg(k) for k in range(14)]
    L, d, _ = w_in.shape
    zeros = lambda n: jnp.zeros((L, d, n), w_in.dtype)
    w32 = jnp.concatenate([a_u, a_v, g_q, g_k, g_r, g_a, d_iw,
                           zeros(N32 - C32_SMALL - 20)], axis=-1).astype(BF16)
    w16 = jnp.concatenate([gates, d_q, d_k, d_v, g_v, d_iq, d_ik, d_ik], axis=-1).astype(BF16)
    return w32, w16


def _gmlp_kernel(u_ref, v_ref, w_ref, b_ref, o_ref):
    tm = u_ref.shape[0]
    for c in range(tm // GMLP_CHUNK):
        rows = slice(c * GMLP_CHUNK, (c + 1) * GMLP_CHUNK)
        for g in range(GMLP_GROUPS):
            cols = slice(g * LANES, (g + 1) * LANES)
            mixed = _dot(w_ref[g], v_ref[rows, cols].astype(BF16)) + b_ref[:, cols]
            o_ref[rows, cols] = (u_ref[rows, cols] * mixed).astype(o_ref.dtype)


def _gmlp(p32, w_tril, bias_full, tm):
    tokens = p32.shape[0]
    return pl.pallas_call(
        _gmlp_kernel,
        grid=(tokens // tm,),
        in_specs=[
            pl.BlockSpec((tm, D_BRANCH), lambda i: (i, C32_U // D_BRANCH)),
            pl.BlockSpec((tm, D_BRANCH), lambda i: (i, C32_V // D_BRANCH)),
            pl.BlockSpec(w_tril.shape, lambda i: (0, 0, 0)),
            pl.BlockSpec(bias_full.shape, lambda i: (0, 0)),
        ],
        out_specs=pl.BlockSpec((tm, D_BRANCH), lambda i: (i, 0)),
        out_shape=jax.ShapeDtypeStruct((tokens, D_BRANCH), BF16),
        compiler_params=pltpu.CompilerParams(
            dimension_semantics=("parallel",), vmem_limit_bytes=VMEM_LIMIT),
        name="gmlp_branch",
    )(p32, p32, w_tril, bias_full)


def _gla_kernel(qk_ref, v_ref, r_ref, ga_ref, w2_ref, b2_ref, ng_ref, tril_ref, exp_ref,
                o_ref, st_ref, d_scr, g_scr):
    C, SUB, H, DK, DV = GLA_CHUNK, GLA_SUB, GLA_HEADS, GLA_DK, GLA_DV
    HK = H * DK
    n_chunks = qk_ref.shape[0] // C

    @pl.when(pl.program_id(1) == 0)
    def _():
        st_ref[...] = jnp.zeros_like(st_ref)

    lane = lax.broadcasted_iota(jnp.int32, (1, HK), 1)
    head_mask = [(lane >= h * DK) & (lane < (h + 1) * DK) for h in range(H)]
    row = lax.broadcasted_iota(jnp.int32, (C, C), 0)
    col = lax.broadcasted_iota(jnp.int32, (C, C), 1)
    sub_shift = SUB.bit_length() - 1
    blk_lower = (row >> sub_shift) > (col >> sub_shift)
    sub_t = lax.broadcasted_iota(jnp.int32, (SUB, 1), 0)

    graw = _dot(ga_ref[...].astype(BF16), w2_ref[...]) + b2_ref[...]
    g_all = jax.nn.log_sigmoid(graw) / GLA_GATE_TAU
    g_scr[...] = g_all
    chunk_decay = jnp.sum(g_all.reshape(n_chunks, C, HK), axis=1)
    mild = jnp.min(chunk_decay) >= GLA_MILD_DECAY

    def cumulative_decay(rows):
        g = g_scr[rows, :]
        g_hi = g.astype(BF16)
        g_r1 = g - g_hi.astype(F32)
        g_mid = g_r1.astype(BF16)
        g_lo = (g_r1 - g_mid.astype(F32)).astype(BF16)
        tril = tril_ref[...]
        return _dot(tril, g_hi) + (_dot(tril, g_mid) + _dot(tril, g_lo))

    def finish(rows, o, v, k_dec, b_last, st):
        upd = _dot_tn(v, k_dec)
        new_st = st * jnp.exp(b_last)
        for h in range(H):
            new_st = new_st + jnp.where(head_mask[h], upd[h * DV:(h + 1) * DV, :], 0.0)
        st_ref[...] = new_st
        o_ref[rows, :] = _head_rms(o, ng_ref[...], 1.0) * r_ref[rows, :]

    def chunk_mild(ci, carry):
        rows = pl.ds(pl.multiple_of(ci * C, C), C)
        q = qk_ref[rows, 0:HK] * (DK ** -0.5)
        k = qk_ref[rows, HK:2 * HK]
        v = v_ref[rows, :]
        b = cumulative_decay(rows)
        b_last = b[C - 1:C, :]
        st = st_ref[...]
        q_in = q * jnp.exp(b)
        k_out = (k * jnp.exp(-b)).astype(BF16)
        k_dec = (k * jnp.exp(b_last - b)).astype(BF16)
        outs = []
        for h in range(H):
            qm = jnp.where(head_mask[h], q_in, 0.0).astype(BF16)
            a_h = jnp.where(row >= col, _dot_nt(qm, k_out), 0.0).astype(BF16)
            outs.append(_dot_nt(qm, st.astype(BF16)) + _dot(a_h, v[:, h * DV:(h + 1) * DV]))
        finish(rows, jnp.concatenate(outs, axis=1), v, k_dec, b_last, st)
        return carry

    def chunk(ci, carry):
        r0 = pl.multiple_of(ci * C, C)
        rows = pl.ds(r0, C)
        q = qk_ref[rows, 0:HK] * (DK ** -0.5)
        k = qk_ref[rows, HK:2 * HK]
        v = v_ref[rows, :]
        b = cumulative_decay(rows)
        b_last = b[C - 1:C, :]
        st = st_ref[...]

        q_in = q * jnp.exp(b)
        k_dec = (k * jnp.exp(b_last - b)).astype(BF16)

        a_off = [jnp.zeros((C, C), F32) for _ in range(H)]
        for j in range(C // SUB - 1):
            bj = b[(j + 1) * SUB - 1:(j + 1) * SUB, :]
            qj = q * jnp.exp(jnp.minimum(b - bj, 0.0))
            in_blk = (lax.broadcasted_iota(jnp.int32, (C, 1), 0) >> sub_shift) == j
            kj = jnp.where(in_blk, k * jnp.exp(jnp.minimum(bj - b, 0.0)), 0.0).astype(BF16)
            for h in range(H):
                a_off[h] = a_off[h] + _dot_nt(jnp.where(head_mask[h], qj, 0.0).astype(BF16), kj)

        for i in range(C // SUB):
            rs = slice(i * SUB, (i + 1) * SUB)
            qi, bi = q[rs, :], b[rs, :]
            for s in range(SUB):
                ks = k[i * SUB + s:i * SUB + s + 1, :]
                bs = b[i * SUB + s:i * SUB + s + 1, :]
                dterm = qi * ks * jnp.exp(jnp.minimum(bi - bs, 0.0))
                dterm = jnp.where(sub_t >= s, dterm, 0.0)
                d_scr[s * SUB:(s + 1) * SUB, :] = dterm.astype(BF16)
            gsum = _dot(d_scr[...], exp_ref[...])
            od = jnp.zeros((SUB, H * DV), F32)
            for s in range(SUB):
                vs = v[i * SUB + s:i * SUB + s + 1, :].astype(F32)
                od = od + gsum[s * SUB:(s + 1) * SUB, :] * vs
            o_ref[pl.ds(r0 + i * SUB, SUB), :] = od

        outs = []
        for h in range(H):
            vh = v[:, h * DV:(h + 1) * DV]
            o_inter = _dot_nt(jnp.where(head_mask[h], q_in, 0.0).astype(BF16), st.astype(BF16))
            a_h = jnp.where(blk_lower, a_off[h], 0.0).astype(BF16)
            outs.append(o_inter + _dot(a_h, vh))
        o = o_ref[rows, :] + jnp.concatenate(outs, axis=1)
        finish(rows, o, v, k_dec, b_last, st)
        return carry

    @pl.when(mild)
    def _():
        lax.fori_loop(0, n_chunks, chunk_mild, 0, unroll=4)

    @pl.when(jnp.logical_not(mild))
    def _():
        lax.fori_loop(0, n_chunks, chunk, 0)


def _gla(p32, p16, w2p, b2, norm_g, bsz, seq, ts):
    C, SUB, H, DK, DV = GLA_CHUNK, GLA_SUB, GLA_HEADS, GLA_DK, GLA_DV
    tokens = p32.shape[0]
    blocks_per_seq = seq // ts
    tril = jnp.asarray(np.tril(np.ones((C, C), np.float32))).astype(BF16)
    expand = np.zeros((H * DK, H * DV), np.float32)
    for h in range(H):
        expand[h * DK:(h + 1) * DK, h * DV:(h + 1) * DV] = 1.0
    expand = jnp.asarray(expand).astype(BF16)
    row_map = lambda cb: (lambda b, i: (b * blocks_per_seq + i, cb))
    const2 = lambda b, i: (0, 0)
    out = pl.pallas_call(
        _gla_kernel,
        grid=(bsz, blocks_per_seq),
        in_specs=[
            pl.BlockSpec((ts, 2 * H * DK), row_map(C32_GQ // (2 * H * DK))),
            pl.BlockSpec((ts, H * DV), row_map(C16_GV // (H * DV))),
            pl.BlockSpec((ts, H * DV), row_map(C32_R // (H * DV))),
            pl.BlockSpec((ts, LANES), row_map(C32_SMALL // LANES)),
            pl.BlockSpec(w2p.shape, const2),
            pl.BlockSpec(b2.shape, const2),
            pl.BlockSpec(norm_g.shape, const2),
            pl.BlockSpec(tril.shape, const2),
            pl.BlockSpec(expand.shape, const2),
        ],
        out_specs=pl.BlockSpec((ts, H * DV), lambda b, i: (b * blocks_per_seq + i, 0)),
        out_shape=jax.ShapeDtypeStruct((tokens, H * DV), F32),
        scratch_shapes=[pltpu.VMEM((DV, H * DK), F32), pltpu.VMEM((SUB * SUB, H * DK), BF16),
                        pltpu.VMEM((ts, H * DK), F32)],
        compiler_params=pltpu.CompilerParams(
            dimension_semantics=("parallel", "arbitrary"), vmem_limit_bytes=VMEM_LIMIT),
        name="gla_branch",
    )(p32, p16, p32, p32, w2p, b2, norm_g, tril, expand)
    return out


def _bit_transpose32(words):
    a = list(words)
    shift, mask = 16, 0x0000FFFF
    while shift:
        m = jnp.int32(np.uint32(mask).astype(np.int32))
        for k in range(32):
            if k & shift == 0:
                t = (a[k] ^ lax.shift_right_logical(a[k + shift], jnp.int32(shift))) & m
                a[k] = a[k] ^ t
                a[k + shift] = a[k + shift] ^ lax.shift_left(t, jnp.int32(shift))
        shift >>= 1
        mask = (mask ^ (mask << shift)) & 0xFFFFFFFF
    return a


def _dsa_kernel(q_ref, iq_ref, iwt_ref, k_ref, vt_ref, ik_ref, bias_ref, ltri_ref,
                o_ref, key_scr, plane_scr, s_scr, acc_scr, *, topk):
    T, TK = DSA_QBLOCK, DSA_KTILE
    SUBS = TK // T
    H, HD = DSA_HEADS, DSA_HDIM
    qb = pl.program_id(1)
    kt_last = qb // SUBS
    n_kt = kt_last + 1
    key_minus_query = (lax.broadcasted_iota(jnp.int32, (TK, T), 0)
                       - lax.broadcasted_iota(jnp.int32, (TK, T), 1))

    def fold(x, op):
        return op(x.reshape(x.shape[0] // 8, 8, T), axis=0)

    def rows_to_one(x, op):
        return op(x, axis=0, keepdims=True)

    def key_rows(kt):
        return pl.ds(pl.multiple_of(kt * TK, TK), TK)

    def tile_loop(n, body, init):
        carry = lax.fori_loop(0, n // 2, lambda i, c: body(2 * i + 1, body(2 * i, c)), init)
        return lax.cond(n % 2 == 1, lambda c: body(n - 1, c), lambda c: c, carry)

    lo_mask = lax.broadcasted_iota(jnp.int32, (T, LANES), 1) < DSA_IDX_DIM
    iq = [iq_ref[:, 0:LANES], iq_ref[:, LANES:2 * LANES]]
    iq_h = [jnp.where(lo_mask, iq[0], 0), jnp.where(lo_mask, 0, iq[0]),
            jnp.where(lo_mask, iq[1], 0), jnp.where(lo_mask, 0, iq[1])]
    iq_all = jnp.concatenate(iq_h, axis=0)
    iw_h = [iwt_ref[h:h + 1, :] for h in range(DSA_IDX_HEADS)]

    GRP_ROWS = 32 * 8
    GRPS = TK // GRP_ROWS
    n_grp = plane_scr.shape[0]

    @pl.when(qb == 0)
    def _():
        plane_scr[...] = jnp.zeros_like(plane_scr)

    def score_tile(kt):
        ik2 = ik_ref[key_rows(kt), :]
        raw = _dot_nt(ik2, iq_all)
        score = jnp.zeros((TK, T), F32)
        for h in range(DSA_IDX_HEADS):
            score = score + iw_h[h] * jnp.maximum(raw[:, h * T:(h + 1) * T], 0.0)
        bits = lax.bitcast_convert_type(score, jnp.int32)
        return jnp.where(score == 0.0, 0, jnp.where(bits >= 0, bits, bits ^ jnp.int32(0x7FFFFFFF)))

    def store_planes(kt, key):
        unsigned_order = key ^ jnp.int32(INT_MIN)
        for g in range(GRPS):
            words = [unsigned_order[g * GRP_ROWS + 8 * j:g * GRP_ROWS + 8 * j + 8, :] for j in range(32)]
            planes = _bit_transpose32(words)
            for b in range(32):
                plane_scr[kt * GRPS + g, b] = planes[b]

    def score_body(kt, c):
        key = score_tile(kt)
        key_scr[kt] = key
        store_planes(kt, key)
        return c

    tile_loop(kt_last, score_body, 0)
    admissible = key_minus_query <= qb * T - kt_last * TK
    key = score_tile(kt_last)
    key_scr[kt_last] = jnp.where(admissible, key, jnp.int32(INT_MIN))
    store_planes(kt_last, key)
    alive_last = []
    for g in range(GRPS):
        word = jnp.zeros((8, T), jnp.int32)
        for j in range(32):
            adm = admissible[g * GRP_ROWS + 8 * j:g * GRP_ROWS + 8 * j + 8, :]
            word = word | jnp.where(adm, jnp.int32(np.int32(np.uint32(1 << (31 - j)))), 0)
        alive_last.append(word)

    alive = []
    for g in range(n_grp):
        kt = g // GRPS
        full = jnp.broadcast_to(jnp.where(kt < kt_last, jnp.int32(-1), jnp.int32(0)), (8, T))
        alive.append(jnp.where(kt == kt_last, alive_last[g % GRPS], full))

    def bits_body(i, carry):
        above, tau_u, alive = carry
        hi_planes = [plane_scr[g, 2 * i] for g in range(n_grp)]
        lo_planes = [plane_scr[g, 2 * i + 1] for g in range(n_grp)]
        n11 = n10 = n01 = jnp.zeros((8, T), jnp.int32)
        for g in range(n_grp):
            with_hi = alive[g] & hi_planes[g]
            both = with_hi & lo_planes[g]
            n11 = n11 + lax.population_count(both)
            n10 = n10 + lax.population_count(with_hi ^ both)
            n01 = n01 + lax.population_count((alive[g] & lo_planes[g]) ^ both)
        c11 = above + rows_to_one(n11, jnp.sum)
        c10 = c11 + rows_to_one(n10, jnp.sum)
        c01 = c10 + rows_to_one(n01, jnp.sum)
        is11, is10, is01 = c11 >= topk, c10 >= topk, c01 >= topk
        hi_bit = is10
        lo_bit = is11 | (is01 & ~is10)
        above = jnp.where(is11, above, jnp.where(is10, c11, jnp.where(is01, c10, c01)))
        flip_hi = jnp.where(hi_bit, jnp.int32(0), jnp.int32(-1))
        flip_lo = jnp.where(lo_bit, jnp.int32(0), jnp.int32(-1))
        alive = tuple(alive[g] & (hi_planes[g] ^ flip_hi) & (lo_planes[g] ^ flip_lo)
                      for g in range(n_grp))
        tau_u = (tau_u | jnp.where(hi_bit, lax.shift_left(jnp.int32(1), 31 - 2 * i), 0)
                 | jnp.where(lo_bit, lax.shift_left(jnp.int32(1), 30 - 2 * i), 0))
        return above, tau_u, alive

    zero_row = jnp.zeros((1, T), jnp.int32)
    above, tau_u, alive = lax.fori_loop(0, 16, bits_body, (zero_row, zero_row, tuple(alive)))
    n_tied = jnp.zeros((8, T), jnp.int32)
    for g in range(n_grp):
        n_tied = n_tied + lax.population_count(alive[g])
    n_ge = above + rows_to_one(n_tied, jnp.sum)
    tau = jnp.maximum(tau_u ^ jnp.int32(INT_MIN), jnp.int32(INT_MIN + 1))

    @pl.when(jnp.max(n_ge) > topk)
    def _():
        need = (topk - above).astype(F32)

        def tie_body(kt, seen):
            for j in range(SUBS):
                rows = slice(j * T, (j + 1) * T)
                key = key_scr[kt, rows, :]
                eq = key == tau
                eq_f = jnp.where(eq, 1.0, 0.0)
                pref = _dot(ltri_ref[...], eq_f.astype(BF16)) + seen
                key_scr[kt, rows, :] = jnp.where(eq & (pref > need), tau - 1, key)
                seen = seen + rows_to_one(eq_f, jnp.sum)
            return seen

        tile_loop(n_kt, tie_body, jnp.zeros((1, T), F32))

    q_h = [q_ref[:, h * HD:(h + 1) * HD] for h in range(H)]

    def logits_body(kt, m_run):
        sel = key_scr[kt] >= tau
        kinds = [jnp.clip(qb - (kt * SUBS + j), 0, 2) for j in range(SUBS)]
        new_m = []
        for h in range(H):
            bias = jnp.concatenate([bias_ref[kinds[j], h] for j in range(SUBS)], axis=0)
            s = _dot_nt(k_ref[key_rows(kt), h * HD:(h + 1) * HD], q_h[h]) + bias
            s = jnp.where(sel, s, NEG_BIG)
            s_scr[h, kt] = s
            new_m.append(jnp.maximum(m_run[h], fold(s, jnp.max)))
        return tuple(new_m)

    m_run = tile_loop(n_kt, logits_body, tuple(jnp.full((8, T), NEG_BIG, F32) for _ in range(H)))
    m_h = [rows_to_one(m, jnp.max) for m in m_run]

    for h in range(H):
        acc_scr[h] = jnp.zeros((HD, T), F32)

    def pv_body(kt, l_run):
        new_l = []
        for h in range(H):
            p = jnp.exp(s_scr[h, kt] - m_h[h])
            new_l.append(l_run[h] + fold(p, jnp.sum))
            acc_scr[h] = acc_scr[h] + _dot(vt_ref[kt, h * HD:(h + 1) * HD, :], p.astype(BF16))
        return tuple(new_l)

    l_run = tile_loop(n_kt, pv_body, tuple(jnp.zeros((8, T), F32) for _ in range(H)))

    for h in range(H):
        out_t = acc_scr[h] / rows_to_one(l_run[h], jnp.sum)
        o_ref[:, h * HD:(h + 1) * HD] = out_t.T.astype(o_ref.dtype)


def _dsa(p16, v_t, iw_t, bias_tiles, bsz, seq):
    T = DSA_QBLOCK
    tokens = p16.shape[0]
    nqb = seq // T
    topk = min(DSA_TOPK_MAX, seq // 4)
    TK = min(DSA_KTILE, seq)
    assert TK == DSA_KTILE and seq % TK == 0
    nkt = seq // TK
    ltri = jnp.asarray(np.tril(np.ones((T, T), np.float32))).astype(BF16)
    w = D_BRANCH
    v_t = v_t.reshape(bsz, nkt, w, TK)
    qmap = lambda cb: (lambda b, i: (b * nqb + i, cb))
    return pl.pallas_call(
        functools.partial(_dsa_kernel, topk=topk),
        grid=(bsz, nqb),
        in_specs=[
            pl.BlockSpec((T, w), qmap(C16_Q // w)),
            pl.BlockSpec((T, 2 * LANES), qmap(C16_IQ // (2 * LANES))),
            pl.BlockSpec((8, T), lambda b, i: (0, b * nqb + i)),
            pl.BlockSpec((seq, w), lambda b, i: (b, C16_K // w)),
            pl.BlockSpec((None, nkt, w, TK), lambda b, i: (b, 0, 0, 0)),
            pl.BlockSpec((seq, LANES), lambda b, i: (b, C16_IK // LANES)),
            pl.BlockSpec(bias_tiles.shape, lambda b, i: (0, 0, 0, 0)),
            pl.BlockSpec(ltri.shape, lambda b, i: (0, 0)),
        ],
        out_specs=pl.BlockSpec((T, w), lambda b, i: (b * nqb + i, 0)),
        out_shape=jax.ShapeDtypeStruct((tokens, w), BF16),
        scratch_shapes=[
            pltpu.VMEM((nkt, TK, T), jnp.int32),
            pltpu.VMEM((nkt * TK // 256, 32, 8, T), jnp.int32),
            pltpu.VMEM((DSA_HEADS, nkt, TK, T), F32),
            pltpu.VMEM((DSA_HEADS, DSA_HDIM, T), F32),
        ],
        compiler_params=pltpu.CompilerParams(
            dimension_semantics=("parallel", "arbitrary"), vmem_limit_bytes=VMEM_LIMIT),
        name="dsa_branch",
    )(p16, p16, iw_t, p16, v_t, p16, bias_tiles, ltri)


def _merge_kernel(ya_ref, yb_ref, yc_ref, ga_ref, gb_ref, gc_ref, x_ref, mod_ref, wbr_ref, wout_ref,
                  gn2_ref, wr_hi_ref, wr_lo_ref, br_ref, x1_ref, h2_ref, lg_ref):
    merged = ga_ref[...] * _dot(ya_ref[...], wbr_ref[0])
    merged = merged + gb_ref[...] * _dot(yb_ref[...].astype(BF16), wbr_ref[1])
    merged = merged + gc_ref[...] * _dot(yc_ref[...], wbr_ref[2])
    mix = _dot(merged.astype(BF16), wout_ref[...])
    x1 = x_ref[...] + mod_ref[0, 2:3, :] * mix
    x1_ref[...] = x1
    y = x1 * lax.rsqrt(jnp.mean(x1 * x1, axis=-1, keepdims=True) + EPS) * gn2_ref[...]
    h2 = y * (1.0 + mod_ref[0, 4:5, :]) + mod_ref[0, 3:4, :]
    h2_ref[...] = h2.astype(BF16)
    lg_ref[...] = _dot3(h2, wr_hi_ref[...], wr_lo_ref[...]) + br_ref[...]


def _merge(ya, yb, yc, p16, x2d, mod_l, wbr, wout, layer, gn2, wr_hi, wr_lo, br, seq, tm):
    tokens, d = x2d.shape
    tiles_per_seq = seq // tm
    row = lambda i: (i, 0)
    gate = lambda k: (lambda i: (i, C16_GATES // d + k))
    c2 = lambda i: (0, 0)
    of_layer = lambda a: pl.BlockSpec((None,) + a.shape[1:], lambda i: (layer,) + (0,) * (a.ndim - 1))
    return pl.pallas_call(
        _merge_kernel,
        grid=(tokens // tm,),
        in_specs=[
            pl.BlockSpec((tm, D_BRANCH), row), pl.BlockSpec((tm, D_BRANCH), row),
            pl.BlockSpec((tm, D_BRANCH), row),
            pl.BlockSpec((tm, d), gate(0)), pl.BlockSpec((tm, d), gate(1)), pl.BlockSpec((tm, d), gate(2)),
            pl.BlockSpec((tm, d), row),
            pl.BlockSpec((1, 6, d), lambda i: (i // tiles_per_seq, 0, 0)),
            of_layer(wbr), of_layer(wout),
            pl.BlockSpec(gn2.shape, c2),
            of_layer(wr_hi), of_layer(wr_lo), of_layer(br),
        ],
        out_specs=[pl.BlockSpec((tm, d), row), pl.BlockSpec((tm, d), row), pl.BlockSpec((tm, LANES), row)],
        out_shape=[jax.ShapeDtypeStruct((tokens, d), F32), jax.ShapeDtypeStruct((tokens, d), BF16),
                   jax.ShapeDtypeStruct((tokens, LANES), F32)],
        compiler_params=pltpu.CompilerParams(
            dimension_semantics=("parallel",), vmem_limit_bytes=VMEM_LIMIT),
        name="merge_norm_router",
    )(ya, yb, yc, p16, p16, p16, x2d, mod_l, wbr, wout, gn2, wr_hi, wr_lo, br)


def _route(lg):
    lane = lax.broadcasted_iota(jnp.int32, lg.shape, 1)
    big = jnp.int32(10 ** 6)
    is_grp = lane < N_GROUPS
    gl = jnp.where(is_grp, lg, -jnp.inf)
    gmax = jnp.max(gl, axis=1, keepdims=True)
    gsum = jnp.sum(jnp.where(is_grp, jnp.exp(lg - gmax), 0.0), axis=1, keepdims=True)
    p_g = 1.0 / gsum
    g_idx = jnp.min(jnp.where(gl == gmax, lane, big), axis=1, keepdims=True)
    first = N_GROUPS + g_idx * EXPERTS_PER_GROUP
    in_grp = (lane >= first) & (lane < first + EXPERTS_PER_GROUP)
    e1 = jnp.where(in_grp, lg, -jnp.inf)
    v1 = jnp.max(e1, axis=1, keepdims=True)
    i1 = jnp.min(jnp.where(e1 == v1, lane, big), axis=1, keepdims=True)
    e2 = jnp.where(in_grp & (lane != i1), lg, -jnp.inf)
    v2 = jnp.max(e2, axis=1, keepdims=True)
    i2 = jnp.min(jnp.where(e2 == v2, lane, big), axis=1, keepdims=True)
    t = jnp.exp(v2 - v1)
    w1 = p_g * (1.0 / (1.0 + t))
    w2 = p_g * (t / (1.0 + t))
    return jnp.where(lane == i1, w1, jnp.where(lane == i2, w2, 0.0))


def _moe_kernel(h_ref, lg_ref, x1_ref, mod_ref, wg_ref, wu_ref, wd_ref, o_ref, comb_scr, acc_scr):
    grp = pl.program_id(1)

    @pl.when(grp == 0)
    def _():
        comb = _route(lg_ref[...])
        hi = comb.astype(BF16)
        rest = comb - hi.astype(F32)
        mid = rest.astype(BF16)
        comb_scr[0] = hi
        comb_scr[1] = mid
        comb_scr[2] = (rest - mid.astype(F32)).astype(BF16)
        acc_scr[...] = jnp.zeros_like(acc_scr)

    h = h_ref[...]
    row_id = lax.broadcasted_iota(jnp.int32, (LANES, LANES), 0)
    acts = []
    for j in range(EXPERTS_PER_GROUP):
        lane_of_expert = N_GROUPS + grp * EXPERTS_PER_GROUP + j
        pick = jnp.where(row_id == lane_of_expert, 1.0, 0.0).astype(BF16)
        cw = _dot(comb_scr[0], pick) + (_dot(comb_scr[1], pick) + _dot(comb_scr[2], pick))
        act = (jax.nn.silu(_dot(h, wg_ref[j])) * _dot(h, wu_ref[j])
               * jnp.concatenate([cw, cw], axis=1))
        acts.append(act.astype(BF16))
    acc_scr[...] += _dot(jnp.concatenate(acts, axis=1), wd_ref[...])

    @pl.when(grp == pl.num_programs(1) - 1)
    def _():
        o_ref[...] = x1_ref[...] + mod_ref[0, 5:6, :] * acc_scr[...]


def _moe(h2, lg, x1, mod_l, wg, wu, wd, layer, seq, tm):
    tokens, d = x1.shape
    tiles_per_seq = seq // tm
    row = lambda i, g: (i, 0)
    return pl.pallas_call(
        _moe_kernel,
        grid=(tokens // tm, N_GROUPS),
        in_specs=[
            pl.BlockSpec((tm, d), row), pl.BlockSpec((tm, LANES), row), pl.BlockSpec((tm, d), row),
            pl.BlockSpec((1, 6, d), lambda i, g: (i // tiles_per_seq, 0, 0)),
            pl.BlockSpec((None, None) + wg.shape[2:], lambda i, g: (layer, g, 0, 0, 0)),
            pl.BlockSpec((None, None) + wu.shape[2:], lambda i, g: (layer, g, 0, 0, 0)),
            pl.BlockSpec((None, None) + wd.shape[2:], lambda i, g: (layer, g, 0, 0)),
        ],
        out_specs=pl.BlockSpec((tm, d), row),
        out_shape=jax.ShapeDtypeStruct((tokens, d), F32),
        scratch_shapes=[pltpu.VMEM((3, tm, LANES), BF16), pltpu.VMEM((tm, d), F32)],
        compiler_params=pltpu.CompilerParams(
            dimension_semantics=("parallel", "arbitrary"), vmem_limit_bytes=VMEM_LIMIT),
        name="hier_moe",
    )(h2, lg, x1, mod_l, wg, wu, wd)


def _row_tile(seq, want):
    t = min(want, seq)
    assert seq % t == 0
    return t


def kernel(x, c, w_mod, b_mod, g_norm1, g_norm2, w_in, gmlp_ln_g, gmlp_ln_b, gmlp_w_s, gmlp_b_s,
           gla_w_gate2, gla_b_gate, gla_norm_g, dsa_qnorm_g, dsa_knorm_g, rel_bias, w_branch,
           b_branch_gate, w_out, w_group, b_group, w_router, b_router, w_exp_gate, w_exp_up,
           w_exp_down):
    bsz, seq, d = x.shape
    depth = w_mod.shape[0]
    assert d == D_MODEL and seq % DSA_QBLOCK == 0 and seq % GLA_CHUNK == 0
    tokens = bsz * seq
    tm = _row_tile(seq, 512)

    mods = _modulation(c, w_mod, b_mod)
    bias_tiles = _bias_tiles(rel_bias)
    w32, w16 = _prep_w_in(w_in)

    causal = np.tril(np.ones((GMLP_CHUNK, GMLP_CHUNK), bool))
    w_tril = jnp.where(causal[None, None], gmlp_w_s, 0.0).astype(BF16)
    gmlp_bias = jnp.repeat(jnp.swapaxes(gmlp_b_s, 1, 2), LANES, axis=-1)
    w2p = jnp.zeros((depth, LANES, GLA_HEADS * GLA_DK), F32).at[:, :GLA_GATE_RANK].set(gla_w_gate2)
    w2p = w2p.astype(BF16)
    wr = jnp.zeros((depth, d, LANES), F32)
    wr = wr.at[:, :, :N_GROUPS].set(w_group).at[:, :, N_GROUPS:N_GROUPS + N_EXPERTS].set(w_router)
    wr_hi = wr.astype(BF16)
    wr_lo = (wr - wr_hi.astype(F32)).astype(BF16)
    br = jnp.zeros((depth, 1, LANES), F32)
    br = br.at[:, 0, :N_GROUPS].set(b_group).at[:, 0, N_GROUPS:N_GROUPS + N_EXPERTS].set(b_router)
    wbr = w_branch.astype(BF16)
    wout = w_out.astype(BF16)
    grouped = (depth, N_GROUPS, EXPERTS_PER_GROUP, d, D_EXPERT)
    wg, wu = w_exp_gate.astype(BF16).reshape(grouped), w_exp_up.astype(BF16).reshape(grouped)
    wd = w_exp_down.astype(BF16).reshape(depth, N_GROUPS, EXPERTS_PER_GROUP * D_EXPERT, d)
    tm_moe = _row_tile(seq, 1024)

    x2d = x.reshape(tokens, d)
    for l in range(depth):
        mod_l = mods[l]
        aux32 = [gmlp_ln_g[l][None], gmlp_ln_b[l][None]]
        aux16 = [dsa_qnorm_g[l][None], dsa_knorm_g[l][None], b_branch_gate[l].reshape(1, -1)]
        p32, p16, v_t, iw_t = _norm_proj(x2d, mod_l, g_norm1[l][None], w32, w16, l, aux32, aux16,
                                         seq, tm)
        ya = _gmlp(p32, w_tril[l], gmlp_bias[l], tm)
        yb = _gla(p32, p16, w2p[l], gla_b_gate[l][None], gla_norm_g[l][None], bsz, seq, tm)
        yc = _dsa(p16, v_t, iw_t, bias_tiles, bsz, seq)
        x1, h2, lg = _merge(ya, yb, yc, p16, x2d, mod_l, wbr, wout, l, g_norm2[l][None],
                            wr_hi, wr_lo, br, seq, tm)
        x2d = _moe(h2, lg, x1, mod_l, wg, wu, wd, l, seq, tm_moe)
    return x2d.reshape(bsz, seq, d)
```

```python
import functools
import math

import numpy as np
import jax
import jax.numpy as jnp
from jax import lax
from jax.experimental import pallas as pl
from jax.experimental.pallas import tpu as pltpu

F32 = jnp.float32
BF16 = jnp.bfloat16

D_MODEL = 1024
D_BRANCH = 512
EPS = 1e-6
GMLP_CHUNK = 128
GMLP_GROUPS = 4
GLA_HEADS = 4
GLA_DK = 64
GLA_DV = 128
GLA_GATE_RANK = 16
GLA_GATE_TAU = 16.0
GLA_CHUNK = 128
GLA_SUB = 16
GLA_MILD_DECAY = -60.0
DSA_HEADS = 4
DSA_HDIM = 128
DSA_IDX_HEADS = 4
DSA_IDX_DIM = 64
DSA_QBLOCK = 128
DSA_KTILE = 512
DSA_TOPK_MAX = 256
N_BUCKETS = 32
MAX_DISTANCE = 128
N_GROUPS = 4
EXPERTS_PER_GROUP = 4
N_EXPERTS = 16
D_EXPERT = 256

LANES = 128
COL_TILE = 512
VMEM_LIMIT = 56 * 1024 * 1024
INT_MIN = -(2 ** 31)
NEG_BIG = -1e30

C32_U, C32_V, C32_GQ, C32_GK, C32_R, C32_SMALL = 0, 512, 1024, 1280, 1536, 2048
N32 = C32_SMALL + LANES
C16_GATES, C16_Q, C16_K, C16_GV, C16_IQ, C16_IK = 0, 3072, 3584, 4096, 4608, 4864
N16 = C16_IK + LANES


def _dot(a, b):
    return jnp.dot(a, b, preferred_element_type=F32)


def _dot_nt(a, b):
    return lax.dot_general(a, b, (((1,), (1,)), ((), ())), preferred_element_type=F32)


def _dot_tn(a, b):
    return lax.dot_general(a, b, (((0,), (0,)), ((), ())), preferred_element_type=F32)


def _split2(a):
    hi = a.astype(BF16)
    lo = (a - hi.astype(F32)).astype(BF16)
    return hi, lo


def _dot3(a, w_hi, w_lo):
    a_hi, a_lo = _split2(a)
    return _dot(a_hi, w_hi) + (_dot(a_lo, w_hi) + _dot(a_hi, w_lo))


def _head_rms(y, g, scale):
    outs = []
    for h in range(y.shape[1] // LANES):
        yh = y[:, h * LANES:(h + 1) * LANES]
        ms = jnp.mean(yh * yh, axis=-1, keepdims=True)
        o = yh * lax.rsqrt(ms + EPS) * g
        if scale != 1.0:
            o = o * scale
        outs.append(o)
    return jnp.concatenate(outs, axis=1)


def _mod_kernel(c_ref, w_ref, b_ref, o_ref):
    a = jax.nn.silu(c_ref[...])
    w_hi, w_lo = _split2(w_ref[...])
    o_ref[...] = _dot3(a, w_hi, w_lo) + b_ref[...]


def _modulation(c, w_mod, b_mod):
    L, d, n = w_mod.shape
    bsz = c.shape[0]
    rows = 8 * pl.cdiv(bsz, 8)
    c_pad = jnp.zeros((rows, d), F32).at[:bsz].set(c)
    tn = 1536
    out = pl.pallas_call(
        _mod_kernel,
        grid=(L, n // tn),
        in_specs=[
            pl.BlockSpec((rows, d), lambda l, j: (0, 0)),
            pl.BlockSpec((None, d, tn), lambda l, j: (l, 0, j)),
            pl.BlockSpec((None, 1, tn), lambda l, j: (l, 0, j)),
        ],
        out_specs=pl.BlockSpec((None, rows, tn), lambda l, j: (l, 0, j)),
        out_shape=jax.ShapeDtypeStruct((L, rows, n), F32),
        compiler_params=pltpu.CompilerParams(
            dimension_semantics=("arbitrary", "arbitrary"), vmem_limit_bytes=VMEM_LIMIT),
        name="adaln_modulation",
    )(c_pad, w_mod, b_mod.reshape(L, 1, n))
    return out[:, :bsz].reshape(L, bsz, 6, d)


def _t5_bucket_table():
    n = np.arange(2 * DSA_QBLOCK)
    max_exact = N_BUCKETS // 2
    large = max_exact + (
        np.log(np.maximum(n, max_exact).astype(np.float32) / max_exact)
        / math.log(MAX_DISTANCE / max_exact) * (N_BUCKETS - max_exact)).astype(np.int32)
    large = np.minimum(large, N_BUCKETS - 1)
    return np.where(n < max_exact, n, large).astype(np.int32)


def _bias_kernel(rb_ref, bucket_ref, o_ref):
    for t in range(bucket_ref.shape[0]):
        bucket = bucket_ref[t]
        for h in range(DSA_HEADS):
            acc = jnp.zeros(bucket.shape, F32)
            for b in range(N_BUCKETS):
                acc = jnp.where(bucket == b, rb_ref[b, h], acc)
            o_ref[t, h] = acc


def _bias_tiles(rel_bias):
    table = _t5_bucket_table()
    assert (table[MAX_DISTANCE:] == N_BUCKETS - 1).all()
    t = np.arange(DSA_QBLOCK)[None, :]
    s = np.arange(DSA_QBLOCK)[:, None]
    diag = table[np.maximum(t - s, 0)]
    near = table[DSA_QBLOCK + t - s]
    far = np.full_like(diag, N_BUCKETS - 1)
    buckets = jnp.asarray(np.stack([diag, near, far]).astype(np.int32))
    return pl.pallas_call(
        _bias_kernel,
        in_specs=[pl.BlockSpec(memory_space=pltpu.SMEM), pl.BlockSpec(memory_space=pltpu.VMEM)],
        out_specs=pl.BlockSpec(memory_space=pltpu.VMEM),
        out_shape=jax.ShapeDtypeStruct((3, DSA_HEADS, DSA_QBLOCK, DSA_QBLOCK), F32),
        name="t5_bias_tiles",
    )(rel_bias, buckets)


def _proj_kernel(x_ref, mod_ref, gn_ref, w32_ref, w16_ref, *rest):
    aux32, aux16 = rest[:2], rest[2:5]
    o32_ref, o16_ref, vt_ref, iwt_ref = rest[5:9]
    x = x_ref[...]
    y = x * lax.rsqrt(jnp.mean(x * x, axis=-1, keepdims=True) + EPS) * gn_ref[...]
    h = (y * (1.0 + mod_ref[0, 1:2, :]) + mod_ref[0, 0:1, :]).astype(BF16)
    for w_ref, o_ref, epilogues, aux in ((w32_ref, o32_ref, EPILOGUES_32, aux32),
                                         (w16_ref, o16_ref, EPILOGUES_16, aux16)):
        wcol = ocol = 0
        for width, epi in epilogues:
            y = epi(_dot(h, w_ref[:, wcol:wcol + width]), aux)
            wcol += width
            if epi is _epi_value_t:
                vt_ref[0] = y.T.astype(vt_ref.dtype)
                continue
            if epi is _epi_small:
                iwt_ref[...] = y.T[GLA_GATE_RANK:GLA_GATE_RANK + 8, :]
            o_ref[:, ocol:ocol + width] = y.astype(o_ref.dtype)
            ocol += width
        assert wcol == w_ref.shape[1] and ocol == o_ref.shape[1]


def _norm_proj(x2d, mod_l, gn, w32, w16, layer, aux32, aux16, seq, tm):
    tokens, d = x2d.shape
    tiles_per_seq = seq // tm
    assert tm == DSA_KTILE
    const = lambda a: pl.BlockSpec(a.shape, lambda i: (0, 0))
    resident = lambda a: pl.BlockSpec((None,) + a.shape[1:], lambda i: (layer, 0, 0),
                                      pipeline_mode=pl.Buffered(1))
    return pl.pallas_call(
        _proj_kernel,
        grid=(tokens // tm,),
        in_specs=[
            pl.BlockSpec((tm, d), lambda i: (i, 0)),
            pl.BlockSpec((1, 6, d), lambda i: (i // tiles_per_seq, 0, 0)),
            const(gn), resident(w32), resident(w16),
        ] + [const(a) for a in aux32 + aux16],
        out_specs=[pl.BlockSpec((tm, N32), lambda i: (i, 0)), pl.BlockSpec((tm, N16), lambda i: (i, 0)),
                   pl.BlockSpec((1, D_BRANCH, tm), lambda i: (i, 0, 0)),
                   pl.BlockSpec((8, tm), lambda i: (0, i))],
        out_shape=[jax.ShapeDtypeStruct((tokens, N32), F32), jax.ShapeDtypeStruct((tokens, N16), BF16),
                   jax.ShapeDtypeStruct((tokens // tm, D_BRANCH, tm), BF16),
                   jax.ShapeDtypeStruct((8, tokens), F32)],
        compiler_params=pltpu.CompilerParams(
            dimension_semantics=("parallel",), vmem_limit_bytes=VMEM_LIMIT),
        name="norm_proj",
    )(x2d, mod_l, gn, w32, w16, *aux32, *aux16)


def _epi_raw(y, aux):
    return y


def _epi_gelu(y, aux):
    return jax.nn.gelu(y)


def _epi_gelu_ln(y, aux):
    v = jax.nn.gelu(y)
    mu = jnp.mean(v, axis=-1, keepdims=True)
    var = jnp.mean(jnp.square(v - mu), axis=-1, keepdims=True)
    return (v - mu) * lax.rsqrt(var + EPS) * aux[0][...] + aux[1][...]


def _epi_silu(y, aux):
    return jax.nn.silu(y)


def _epi_gate(k):
    def epi(y, aux):
        return jax.nn.sigmoid(y + aux[2][:, k * COL_TILE:(k + 1) * COL_TILE])
    return epi


def _epi_qnorm(y, aux):
    return _head_rms(y, aux[0][...], DSA_HDIM ** -0.5)


def _epi_knorm(y, aux):
    return _head_rms(y, aux[1][...], 1.0)


def _epi_value_t(y, aux):
    return y


def _epi_small(y, aux):
    return y


EPILOGUES_32 = ([(COL_TILE, e) for e in (_epi_gelu, _epi_gelu_ln, _epi_raw, _epi_silu)]
                + [(LANES, _epi_small)])
EPILOGUES_16 = ([(COL_TILE, _epi_gate(k)) for k in range(6)]
                + [(COL_TILE, e) for e in (_epi_qnorm, _epi_knorm, _epi_value_t, _epi_raw)]
                + [(2 * LANES, _epi_raw), (LANES, _epi_raw)])


def _prep_w_in(w_in):
    sizes = (512, 512, 256, 256, 512, 512, 16, 512, 512, 512, 256, 64, 4, 3072)
    offs = np.concatenate([[0], np.cumsum(sizes)])
    seg = lambda k: w_in[:, :, offs[k]:offs[k + 1]]
    (a_u, a_v, g_q, g_k, g_v, g_r, g_a, d_q, d_k, d_v, d_iq, d_ik, d_iw, gates) = [seg(k) for k in range(14)]
    L, d, _ = w_in.shape
    zeros = lambda n: jnp.zeros((L, d, n), w_in.dtype)
    w32 = jnp.concatenate([a_u, a_v, g_q, g_k, g_r, g_a, d_iw,
                           zeros(N32 - C32_SMALL - 20)], axis=-1).astype(BF16)
    w16 = jnp.concatenate([gates, d_q, d_k, d_v, g_v, d_iq, d_ik, d_ik], axis=-1).astype(BF16)
    return w32, w16


def _gla_kernel(qk_ref, v_ref, r_ref, ga_ref, w2_ref, b2_ref, ng_ref, tril_ref, exp_ref,
                o_ref, st_ref, d_scr, g_scr):
    C, SUB, H, DK, DV = GLA_CHUNK, GLA_SUB, GLA_HEADS, GLA_DK, GLA_DV
    HK = H * DK
    n_chunks = qk_ref.shape[0] // C

    @pl.when(pl.program_id(1) == 0)
    def _():
        st_ref[...] = jnp.zeros_like(st_ref)

    lane = lax.broadcasted_iota(jnp.int32, (1, HK), 1)
    head_mask = [(lane >= h * DK) & (lane < (h + 1) * DK) for h in range(H)]
    row = lax.broadcasted_iota(jnp.int32, (C, C), 0)
    col = lax.broadcasted_iota(jnp.int32, (C, C), 1)
    sub_shift = SUB.bit_length() - 1
    blk_lower = (row >> sub_shift) > (col >> sub_shift)
    sub_t = lax.broadcasted_iota(jnp.int32, (SUB, 1), 0)

    graw = _dot(ga_ref[...].astype(BF16), w2_ref[...]) + b2_ref[...]
    g_all = jax.nn.log_sigmoid(graw) / GLA_GATE_TAU
    g_scr[...] = g_all
    chunk_decay = jnp.sum(g_all.reshape(n_chunks, C, HK), axis=1)
    mild = jnp.min(chunk_decay) >= GLA_MILD_DECAY

    def cumulative_decay(rows):
        g = g_scr[rows, :]
        g_hi = g.astype(BF16)
        g_r1 = g - g_hi.astype(F32)
        g_mid = g_r1.astype(BF16)
        g_lo = (g_r1 - g_mid.astype(F32)).astype(BF16)
        tril = tril_ref[...]
        return _dot(tril, g_hi) + (_dot(tril, g_mid) + _dot(tril, g_lo))

    def finish(rows, o, v, k_dec, b_last, st):
        upd = _dot_tn(v, k_dec)
        new_st = st * jnp.exp(b_last)
        for h in range(H):
            new_st = new_st + jnp.where(head_mask[h], upd[h * DV:(h + 1) * DV, :], 0.0)
        st_ref[...] = new_st
        o_ref[rows, :] = _head_rms(o, ng_ref[...], 1.0) * r_ref[rows, :]

    def chunk_mild(ci, carry):
        rows = pl.ds(pl.multiple_of(ci * C, C), C)
        q = qk_ref[rows, 0:HK] * (DK ** -0.5)
        k = qk_ref[rows, HK:2 * HK]
        v = v_ref[rows, :]
        b = cumulative_decay(rows)
        b_last = b[C - 1:C, :]
        st = st_ref[...]
        q_in = q * jnp.exp(b)
        k_out = (k * jnp.exp(-b)).astype(BF16)
        k_dec = (k * jnp.exp(b_last - b)).astype(BF16)
        outs = []
        for h in range(H):
            qm = jnp.where(head_mask[h], q_in, 0.0).astype(BF16)
            a_h = jnp.where(row >= col, _dot_nt(qm, k_out), 0.0).astype(BF16)
            outs.append(_dot_nt(qm, st.astype(BF16)) + _dot(a_h, v[:, h * DV:(h + 1) * DV]))
        finish(rows, jnp.concatenate(outs, axis=1), v, k_dec, b_last, st)
        return carry

    def chunk(ci, carry):
        r0 = pl.multiple_of(ci * C, C)
        rows = pl.ds(r0, C)
        q = qk_ref[rows, 0:HK] * (DK ** -0.5)
        k = qk_ref[rows, HK:2 * HK]
        v = v_ref[rows, :]
        b = cumulative_decay(rows)
        b_last = b[C - 1:C, :]
        st = st_ref[...]

        q_in = q * jnp.exp(b)
        k_dec = (k * jnp.exp(b_last - b)).astype(BF16)

        a_off = [jnp.zeros((C, C), F32) for _ in range(H)]
        for j in range(C // SUB - 1):
            bj = b[(j + 1) * SUB - 1:(j + 1) * SUB, :]
            qj = q * jnp.exp(jnp.minimum(b - bj, 0.0))
            in_blk = (lax.broadcasted_iota(jnp.int32, (C, 1), 0) >> sub_shift) == j
            kj = jnp.where(in_blk, k * jnp.exp(jnp.minimum(bj - b, 0.0)), 0.0).astype(BF16)
            for h in range(H):
                a_off[h] = a_off[h] + _dot_nt(jnp.where(head_mask[h], qj, 0.0).astype(BF16), kj)

        for i in range(C // SUB):
            rs = slice(i * SUB, (i + 1) * SUB)
            qi, bi = q[rs, :], b[rs, :]
            for s in range(SUB):
                ks = k[i * SUB + s:i * SUB + s + 1, :]
                bs = b[i * SUB + s:i * SUB + s + 1, :]
                dterm = qi * ks * jnp.exp(jnp.minimum(bi - bs, 0.0))
                dterm = jnp.where(sub_t >= s, dterm, 0.0)
                d_scr[s * SUB:(s + 1) * SUB, :] = dterm.astype(BF16)
            gsum = _dot(d_scr[...], exp_ref[...])
            od = jnp.zeros((SUB, H * DV), F32)
            for s in range(SUB):
                vs = v[i * SUB + s:i * SUB + s + 1, :].astype(F32)
                od = od + gsum[s * SUB:(s + 1) * SUB, :] * vs
            o_ref[pl.ds(r0 + i * SUB, SUB), :] = od

        outs = []
        for h in range(H):
            vh = v[:, h * DV:(h + 1) * DV]
            o_inter = _dot_nt(jnp.where(head_mask[h], q_in, 0.0).astype(BF16), st.astype(BF16))
            a_h = jnp.where(blk_lower, a_off[h], 0.0).astype(BF16)
            outs.append(o_inter + _dot(a_h, vh))
        o = o_ref[rows, :] + jnp.concatenate(outs, axis=1)
        finish(rows, o, v, k_dec, b_last, st)
        return carry

    @pl.when(mild)
    def _():
        lax.fori_loop(0, n_chunks, chunk_mild, 0, unroll=4)

    @pl.when(jnp.logical_not(mild))
    def _():
        lax.fori_loop(0, n_chunks, chunk, 0)


def _gla(p32, p16, w2p, b2, norm_g, bsz, seq, ts):
    C, SUB, H, DK, DV = GLA_CHUNK, GLA_SUB, GLA_HEADS, GLA_DK, GLA_DV
    tokens = p32.shape[0]
    blocks_per_seq = seq // ts
    tril = jnp.asarray(np.tril(np.ones((C, C), np.float32))).astype(BF16)
    expand = np.zeros((H * DK, H * DV), np.float32)
    for h in range(H):
        expand[h * DK:(h + 1) * DK, h * DV:(h + 1) * DV] = 1.0
    expand = jnp.asarray(expand).astype(BF16)
    row_map = lambda cb: (lambda b, i: (b * blocks_per_seq + i, cb))
    const2 = lambda b, i: (0, 0)
    out = pl.pallas_call(
        _gla_kernel,
        grid=(bsz, blocks_per_seq),
        in_specs=[
            pl.BlockSpec((ts, 2 * H * DK), row_map(C32_GQ // (2 * H * DK))),
            pl.BlockSpec((ts, H * DV), row_map(C16_GV // (H * DV))),
            pl.BlockSpec((ts, H * DV), row_map(C32_R // (H * DV))),
            pl.BlockSpec((ts, LANES), row_map(C32_SMALL // LANES)),
            pl.BlockSpec(w2p.shape, const2),
            pl.BlockSpec(b2.shape, const2),
            pl.BlockSpec(norm_g.shape, const2),
            pl.BlockSpec(tril.shape, const2),
            pl.BlockSpec(expand.shape, const2),
        ],
        out_specs=pl.BlockSpec((ts, H * DV), lambda b, i: (b * blocks_per_seq + i, 0)),
        out_shape=jax.ShapeDtypeStruct((tokens, H * DV), F32),
        scratch_shapes=[pltpu.VMEM((DV, H * DK), F32), pltpu.VMEM((SUB * SUB, H * DK), BF16),
                        pltpu.VMEM((ts, H * DK), F32)],
        compiler_params=pltpu.CompilerParams(
            dimension_semantics=("parallel", "arbitrary"), vmem_limit_bytes=VMEM_LIMIT),
        name="gla_branch",
    )(p32, p16, p32, p32, w2p, b2, norm_g, tril, expand)
    return out


def _bit_transpose32(words):
    a = list(words)
    shift, mask = 16, 0x0000FFFF
    while shift:
        m = jnp.int32(np.uint32(mask).astype(np.int32))
        for k in range(32):
            if k & shift == 0:
                t = (a[k] ^ lax.shift_right_logical(a[k + shift], jnp.int32(shift))) & m
                a[k] = a[k] ^ t
                a[k + shift] = a[k + shift] ^ lax.shift_left(t, jnp.int32(shift))
        shift >>= 1
        mask = (mask ^ (mask << shift)) & 0xFFFFFFFF
    return a


def _dsa_kernel(q_ref, iq_ref, iwt_ref, k_ref, vt_ref, ik_ref, bias_ref, ltri_ref,
                o_ref, key_scr, plane_scr, s_scr, acc_scr, *, topk):
    T, TK = DSA_QBLOCK, DSA_KTILE
    SUBS = TK // T
    H, HD = DSA_HEADS, DSA_HDIM
    qb = pl.program_id(1)
    kt_last = qb // SUBS
    n_kt = kt_last + 1
    key_minus_query = (lax.broadcasted_iota(jnp.int32, (TK, T), 0)
                       - lax.broadcasted_iota(jnp.int32, (TK, T), 1))

    def fold(x, op):
        return op(x.reshape(x.shape[0] // 8, 8, T), axis=0)

    def rows_to_one(x, op):
        return op(x, axis=0, keepdims=True)

    def key_rows(kt):
        return pl.ds(pl.multiple_of(kt * TK, TK), TK)

    def tile_loop(n, body, init):
        carry = lax.fori_loop(0, n // 2, lambda i, c: body(2 * i + 1, body(2 * i, c)), init)
        return lax.cond(n % 2 == 1, lambda c: body(n - 1, c), lambda c: c, carry)

    lo_mask = lax.broadcasted_iota(jnp.int32, (T, LANES), 1) < DSA_IDX_DIM
    iq = [iq_ref[:, 0:LANES], iq_ref[:, LANES:2 * LANES]]
    iq_h = [jnp.where(lo_mask, iq[0], 0), jnp.where(lo_mask, 0, iq[0]),
            jnp.where(lo_mask, iq[1], 0), jnp.where(lo_mask, 0, iq[1])]
    iq_all = jnp.concatenate(iq_h, axis=0)
    iw_h = [iwt_ref[h:h + 1, :] for h in range(DSA_IDX_HEADS)]

    GRP_ROWS = 32 * 8
    GRPS = TK // GRP_ROWS
    n_grp = plane_scr.shape[0]

    @pl.when(qb == 0)
    def _():
        plane_scr[...] = jnp.zeros_like(plane_scr)

    def score_tile(kt):
        ik2 = ik_ref[key_rows(kt), :]
        raw = _dot_nt(ik2, iq_all)
        score = jnp.zeros((TK, T), F32)
        for h in range(DSA_IDX_HEADS):
            score = score + iw_h[h] * jnp.maximum(raw[:, h * T:(h + 1) * T], 0.0)
        bits = lax.bitcast_convert_type(score, jnp.int32)
        return jnp.where(score == 0.0, 0, jnp.where(bits >= 0, bits, bits ^ jnp.int32(0x7FFFFFFF)))

    def store_planes(kt, key):
        unsigned_order = key ^ jnp.int32(INT_MIN)
        for g in range(GRPS):
            words = [unsigned_order[g * GRP_ROWS + 8 * j:g * GRP_ROWS + 8 * j + 8, :] for j in range(32)]
            planes = _bit_transpose32(words)
            for b in range(32):
                plane_scr[kt * GRPS + g, b] = planes[b]

    def score_body(kt, c):
        key = score_tile(kt)
        key_scr[kt] = key
        store_planes(kt, key)
        return c

    tile_loop(kt_last, score_body, 0)
    admissible = key_minus_query <= qb * T - kt_last * TK
    key = score_tile(kt_last)
    key_scr[kt_last] = jnp.where(admissible, key, jnp.int32(INT_MIN))
    store_planes(kt_last, key)
    alive_last = []
    for g in range(GRPS):
        word = jnp.zeros((8, T), jnp.int32)
        for j in range(32):
            adm = admissible[g * GRP_ROWS + 8 * j:g * GRP_ROWS + 8 * j + 8, :]
            word = word | jnp.where(adm, jnp.int32(np.int32(np.uint32(1 << (31 - j)))), 0)
        alive_last.append(word)

    alive = []
    for g in range(n_grp):
        kt = g // GRPS
        full = jnp.broadcast_to(jnp.where(kt < kt_last, jnp.int32(-1), jnp.int32(0)), (8, T))
        alive.append(jnp.where(kt == kt_last, alive_last[g % GRPS], full))

    def bits_body(i, carry):
        above, tau_u, alive = carry
        hi_planes = [plane_scr[g, 2 * i] for g in range(n_grp)]
        lo_planes = [plane_scr[g, 2 * i + 1] for g in range(n_grp)]
        n11 = n10 = n01 = jnp.zeros((8, T), jnp.int32)
        for g in range(n_grp):
            with_hi = alive[g] & hi_planes[g]
            both = with_hi & lo_planes[g]
            n11 = n11 + lax.population_count(both)
            n10 = n10 + lax.population_count(with_hi ^ both)
            n01 = n01 + lax.population_count((alive[g] & lo_planes[g]) ^ both)
        c11 = above + rows_to_one(n11, jnp.sum)
        c10 = c11 + rows_to_one(n10, jnp.sum)
        c01 = c10 + rows_to_one(n01, jnp.sum)
        is11, is10, is01 = c11 >= topk, c10 >= topk, c01 >= topk
        hi_bit = is10
        lo_bit = is11 | (is01 & ~is10)
        above = jnp.where(is11, above, jnp.where(is10, c11, jnp.where(is01, c10, c01)))
        flip_hi = jnp.where(hi_bit, jnp.int32(0), jnp.int32(-1))
        flip_lo = jnp.where(lo_bit, jnp.int32(0), jnp.int32(-1))
        alive = tuple(alive[g] & (hi_planes[g] ^ flip_hi) & (lo_planes[g] ^ flip_lo)
                      for g in range(n_grp))
        tau_u = (tau_u | jnp.where(hi_bit, lax.shift_left(jnp.int32(1), 31 - 2 * i), 0)
                 | jnp.where(lo_bit, lax.shift_left(jnp.int32(1), 30 - 2 * i), 0))
        return above, tau_u, alive

    zero_row = jnp.zeros((1, T), jnp.int32)
    above, tau_u, alive = lax.fori_loop(0, 16, bits_body, (zero_row, zero_row, tuple(alive)))
    n_tied = jnp.zeros((8, T), jnp.int32)
    for g in range(n_grp):
        n_tied = n_tied + lax.population_count(alive[g])
    n_ge = above + rows_to_one(n_tied, jnp.sum)
    tau = jnp.maximum(tau_u ^ jnp.int32(INT_MIN), jnp.int32(INT_MIN + 1))

    @pl.when(jnp.max(n_ge) > topk)
    def _():
        need = (topk - above).astype(F32)

        def tie_body(kt, seen):
            for j in range(SUBS):
                rows = slice(j * T, (j + 1) * T)
                key = key_scr[kt, rows, :]
                eq = key == tau
                eq_f = jnp.where(eq, 1.0, 0.0)
                pref = _dot(ltri_ref[...], eq_f.astype(BF16)) + seen
                key_scr[kt, rows, :] = jnp.where(eq & (pref > need), tau - 1, key)
                seen = seen + rows_to_one(eq_f, jnp.sum)
            return seen

        tile_loop(n_kt, tie_body, jnp.zeros((1, T), F32))

    q_h = [q_ref[:, h * HD:(h + 1) * HD] for h in range(H)]

    def logits_body(kt, m_run):
        sel = key_scr[kt] >= tau
        kinds = [jnp.clip(qb - (kt * SUBS + j), 0, 2) for j in range(SUBS)]
        new_m = []
        for h in range(H):
            bias = jnp.concatenate([bias_ref[kinds[j], h] for j in range(SUBS)], axis=0)
            s = _dot_nt(k_ref[key_rows(kt), h * HD:(h + 1) * HD], q_h[h]) + bias
            s = jnp.where(sel, s, NEG_BIG)
            s_scr[h, kt] = s
            new_m.append(jnp.maximum(m_run[h], fold(s, jnp.max)))
        return tuple(new_m)

    m_run = tile_loop(n_kt, logits_body, tuple(jnp.full((8, T), NEG_BIG, F32) for _ in range(H)))
    m_h = [rows_to_one(m, jnp.max) for m in m_run]

    for h in range(H):
        acc_scr[h] = jnp.zeros((HD, T), F32)

    def pv_body(kt, l_run):
        new_l = []
        for h in range(H):
            p = jnp.exp(s_scr[h, kt] - m_h[h])
            new_l.append(l_run[h] + fold(p, jnp.sum))
            acc_scr[h] = acc_scr[h] + _dot(vt_ref[kt, h * HD:(h + 1) * HD, :], p.astype(BF16))
        return tuple(new_l)

    l_run = tile_loop(n_kt, pv_body, tuple(jnp.zeros((8, T), F32) for _ in range(H)))

    for h in range(H):
        out_t = acc_scr[h] / rows_to_one(l_run[h], jnp.sum)
        o_ref[:, h * HD:(h + 1) * HD] = out_t.T.astype(o_ref.dtype)


def _dsa(p16, v_t, iw_t, bias_tiles, bsz, seq):
    T = DSA_QBLOCK
    tokens = p16.shape[0]
    nqb = seq // T
    topk = min(DSA_TOPK_MAX, seq // 4)
    TK = min(DSA_KTILE, seq)
    assert TK == DSA_KTILE and seq % TK == 0
    nkt = seq // TK
    ltri = jnp.asarray(np.tril(np.ones((T, T), np.float32))).astype(BF16)
    w = D_BRANCH
    v_t = v_t.reshape(bsz, nkt, w, TK)
    qmap = lambda cb: (lambda b, i: (b * nqb + i, cb))
    return pl.pallas_call(
        functools.partial(_dsa_kernel, topk=topk),
        grid=(bsz, nqb),
        in_specs=[
            pl.BlockSpec((T, w), qmap(C16_Q // w)),
            pl.BlockSpec((T, 2 * LANES), qmap(C16_IQ // (2 * LANES))),
            pl.BlockSpec((8, T), lambda b, i: (0, b * nqb + i)),
            pl.BlockSpec((seq, w), lambda b, i: (b, C16_K // w)),
            pl.BlockSpec((None, nkt, w, TK), lambda b, i: (b, 0, 0, 0)),
            pl.BlockSpec((seq, LANES), lambda b, i: (b, C16_IK // LANES)),
            pl.BlockSpec(bias_tiles.shape, lambda b, i: (0, 0, 0, 0)),
            pl.BlockSpec(ltri.shape, lambda b, i: (0, 0)),
        ],
        out_specs=pl.BlockSpec((T, w), lambda b, i: (b * nqb + i, 0)),
        out_shape=jax.ShapeDtypeStruct((tokens, w), BF16),
        scratch_shapes=[
            pltpu.VMEM((nkt, TK, T), jnp.int32),
            pltpu.VMEM((nkt * TK // 256, 32, 8, T), jnp.int32),
            pltpu.VMEM((DSA_HEADS, nkt, TK, T), F32),
            pltpu.VMEM((DSA_HEADS, DSA_HDIM, T), F32),
        ],
        compiler_params=pltpu.CompilerParams(
            dimension_semantics=("parallel", "arbitrary"), vmem_limit_bytes=VMEM_LIMIT),
        name="dsa_branch",
    )(p16, p16, iw_t, p16, v_t, p16, bias_tiles, ltri)


def _merge_kernel(u_ref, v_ref, ws_ref, bs_ref, yb_ref, yc_ref, ga_ref, gb_ref, gc_ref, x_ref, mod_ref,
                  wbr_ref, wout_ref, gn2_ref, wr_hi_ref, wr_lo_ref, br_ref, x1_ref, h2_ref, comb_ref):
    tm = u_ref.shape[0]
    ya_rows = []
    for c in range(tm // GMLP_CHUNK):
        rows = slice(c * GMLP_CHUNK, (c + 1) * GMLP_CHUNK)
        groups = []
        for g in range(GMLP_GROUPS):
            cols = slice(g * LANES, (g + 1) * LANES)
            mixed = _dot(ws_ref[g], v_ref[rows, cols].astype(BF16)) + bs_ref[:, cols]
            groups.append((u_ref[rows, cols] * mixed).astype(BF16))
        ya_rows.append(jnp.concatenate(groups, axis=1))
    ya = jnp.concatenate(ya_rows, axis=0)

    merged = ga_ref[...] * _dot(ya, wbr_ref[0])
    merged = merged + gb_ref[...] * _dot(yb_ref[...].astype(BF16), wbr_ref[1])
    merged = merged + gc_ref[...] * _dot(yc_ref[...], wbr_ref[2])
    mix = _dot(merged.astype(BF16), wout_ref[...])
    x1 = x_ref[...] + mod_ref[0, 2:3, :] * mix
    x1_ref[...] = x1
    y = x1 * lax.rsqrt(jnp.mean(x1 * x1, axis=-1, keepdims=True) + EPS) * gn2_ref[...]
    h2 = y * (1.0 + mod_ref[0, 4:5, :]) + mod_ref[0, 3:4, :]
    h2_ref[...] = h2.astype(BF16)
    comb = _route(_dot3(h2, wr_hi_ref[...], wr_lo_ref[...]) + br_ref[...])
    hi = comb.astype(BF16)
    rest = comb - hi.astype(F32)
    mid = rest.astype(BF16)
    comb_ref[0] = hi
    comb_ref[1] = mid
    comb_ref[2] = (rest - mid.astype(F32)).astype(BF16)


def _merge(p32, w_tril, gmlp_bias, yb, yc, p16, x2d, mod_l, wbr, wout, layer, gn2, wr_hi, wr_lo, br,
           seq, tm):
    tokens, d = x2d.shape
    tiles_per_seq = seq // tm
    row = lambda i: (i, 0)
    gate = lambda k: (lambda i: (i, C16_GATES // d + k))
    c2 = lambda i: (0, 0)
    of_layer = lambda a: pl.BlockSpec((None,) + a.shape[1:], lambda i: (layer,) + (0,) * (a.ndim - 1))
    return pl.pallas_call(
        _merge_kernel,
        grid=(tokens // tm,),
        in_specs=[
            pl.BlockSpec((tm, D_BRANCH), lambda i: (i, C32_U // D_BRANCH)),
            pl.BlockSpec((tm, D_BRANCH), lambda i: (i, C32_V // D_BRANCH)),
            of_layer(w_tril), of_layer(gmlp_bias),
            pl.BlockSpec((tm, D_BRANCH), row), pl.BlockSpec((tm, D_BRANCH), row),
            pl.BlockSpec((tm, d), gate(0)), pl.BlockSpec((tm, d), gate(1)), pl.BlockSpec((tm, d), gate(2)),
            pl.BlockSpec((tm, d), row),
            pl.BlockSpec((1, 6, d), lambda i: (i // tiles_per_seq, 0, 0)),
            of_layer(wbr), of_layer(wout),
            pl.BlockSpec(gn2.shape, c2),
            of_layer(wr_hi), of_layer(wr_lo), of_layer(br),
        ],
        out_specs=[pl.BlockSpec((tm, d), row), pl.BlockSpec((tm, d), row),
                   pl.BlockSpec((3, tm, LANES), lambda i: (0, i, 0))],
        out_shape=[jax.ShapeDtypeStruct((tokens, d), F32), jax.ShapeDtypeStruct((tokens, d), BF16),
                   jax.ShapeDtypeStruct((3, tokens, LANES), BF16)],
        compiler_params=pltpu.CompilerParams(
            dimension_semantics=("parallel",), vmem_limit_bytes=VMEM_LIMIT),
        name="gmlp_merge_norm_router",
    )(p32, p32, w_tril, gmlp_bias, yb, yc, p16, p16, p16, x2d, mod_l, wbr, wout, gn2, wr_hi, wr_lo, br)


def _route(lg):
    lane = lax.broadcasted_iota(jnp.int32, lg.shape, 1)
    big = jnp.int32(10 ** 6)
    is_grp = lane < N_GROUPS
    gl = jnp.where(is_grp, lg, -jnp.inf)
    gmax = jnp.max(gl, axis=1, keepdims=True)
    gsum = jnp.sum(jnp.where(is_grp, jnp.exp(lg - gmax), 0.0), axis=1, keepdims=True)
    p_g = 1.0 / gsum
    g_idx = jnp.min(jnp.where(gl == gmax, lane, big), axis=1, keepdims=True)
    first = N_GROUPS + g_idx * EXPERTS_PER_GROUP
    in_grp = (lane >= first) & (lane < first + EXPERTS_PER_GROUP)
    e1 = jnp.where(in_grp, lg, -jnp.inf)
    v1 = jnp.max(e1, axis=1, keepdims=True)
    i1 = jnp.min(jnp.where(e1 == v1, lane, big), axis=1, keepdims=True)
    e2 = jnp.where(in_grp & (lane != i1), lg, -jnp.inf)
    v2 = jnp.max(e2, axis=1, keepdims=True)
    i2 = jnp.min(jnp.where(e2 == v2, lane, big), axis=1, keepdims=True)
    t = jnp.exp(v2 - v1)
    w1 = p_g * (1.0 / (1.0 + t))
    w2 = p_g * (t / (1.0 + t))
    return jnp.where(lane == i1, w1, jnp.where(lane == i2, w2, 0.0))


def _moe_kernel(h_ref, comb_ref, x1_ref, mod_ref, wg_ref, wu_ref, wd_ref, o_ref, acc_scr):
    grp = pl.program_id(1)

    @pl.when(grp == 0)
    def _():
        acc_scr[...] = jnp.zeros_like(acc_scr)

    h = h_ref[...]
    row_id = lax.broadcasted_iota(jnp.int32, (LANES, LANES), 0)
    acts = []
    for j in range(EXPERTS_PER_GROUP):
        lane_of_expert = N_GROUPS + grp * EXPERTS_PER_GROUP + j
        pick = jnp.where(row_id == lane_of_expert, 1.0, 0.0).astype(BF16)
        cw = _dot(comb_ref[0], pick) + (_dot(comb_ref[1], pick) + _dot(comb_ref[2], pick))
        act = (jax.nn.silu(_dot(h, wg_ref[j])) * _dot(h, wu_ref[j])
               * jnp.concatenate([cw, cw], axis=1))
        acts.append(act.astype(BF16))
    acc_scr[...] += _dot(jnp.concatenate(acts, axis=1), wd_ref[...])

    @pl.when(grp == pl.num_programs(1) - 1)
    def _():
        o_ref[...] = x1_ref[...] + mod_ref[0, 5:6, :] * acc_scr[...]


def _moe(h2, comb, x1, mod_l, wg, wu, wd, layer, seq, tm):
    tokens, d = x1.shape
    tiles_per_seq = seq // tm
    row = lambda i, g: (i, 0)
    return pl.pallas_call(
        _moe_kernel,
        grid=(tokens // tm, N_GROUPS),
        in_specs=[
            pl.BlockSpec((tm, d), row), pl.BlockSpec((3, tm, LANES), lambda i, g: (0, i, 0)),
            pl.BlockSpec((tm, d), row),
            pl.BlockSpec((1, 6, d), lambda i, g: (i // tiles_per_seq, 0, 0)),
            pl.BlockSpec((None, None) + wg.shape[2:], lambda i, g: (layer, g, 0, 0, 0)),
            pl.BlockSpec((None, None) + wu.shape[2:], lambda i, g: (layer, g, 0, 0, 0)),
            pl.BlockSpec((None, None) + wd.shape[2:], lambda i, g: (layer, g, 0, 0)),
        ],
        out_specs=pl.BlockSpec((tm, d), row),
        out_shape=jax.ShapeDtypeStruct((tokens, d), F32),
        scratch_shapes=[pltpu.VMEM((tm, d), F32)],
        compiler_params=pltpu.CompilerParams(
            dimension_semantics=("parallel", "arbitrary"), vmem_limit_bytes=VMEM_LIMIT),
        name="hier_moe",
    )(h2, comb, x1, mod_l, wg, wu, wd)


def _row_tile(seq, want):
    t = min(want, seq)
    assert seq % t == 0
    return t


def kernel(x, c, w_mod, b_mod, g_norm1, g_norm2, w_in, gmlp_ln_g, gmlp_ln_b, gmlp_w_s, gmlp_b_s,
           gla_w_gate2, gla_b_gate, gla_norm_g, dsa_qnorm_g, dsa_knorm_g, rel_bias, w_branch,
           b_branch_gate, w_out, w_group, b_group, w_router, b_router, w_exp_gate, w_exp_up,
           w_exp_down):
    bsz, seq, d = x.shape
    depth = w_mod.shape[0]
    assert d == D_MODEL and seq % DSA_QBLOCK == 0 and seq % GLA_CHUNK == 0
    tokens = bsz * seq
    tm = _row_tile(seq, 512)

    mods = _modulation(c, w_mod, b_mod)
    bias_tiles = _bias_tiles(rel_bias)
    w32, w16 = _prep_w_in(w_in)

    causal = np.tril(np.ones((GMLP_CHUNK, GMLP_CHUNK), bool))
    w_tril = jnp.where(causal[None, None], gmlp_w_s, 0.0).astype(BF16)
    gmlp_bias = jnp.repeat(jnp.swapaxes(gmlp_b_s, 1, 2), LANES, axis=-1)
    w2p = jnp.zeros((depth, LANES, GLA_HEADS * GLA_DK), F32).at[:, :GLA_GATE_RANK].set(gla_w_gate2)
    w2p = w2p.astype(BF16)
    wr = jnp.zeros((depth, d, LANES), F32)
    wr = wr.at[:, :, :N_GROUPS].set(w_group).at[:, :, N_GROUPS:N_GROUPS + N_EXPERTS].set(w_router)
    wr_hi = wr.astype(BF16)
    wr_lo = (wr - wr_hi.astype(F32)).astype(BF16)
    br = jnp.zeros((depth, 1, LANES), F32)
    br = br.at[:, 0, :N_GROUPS].set(b_group).at[:, 0, N_GROUPS:N_GROUPS + N_EXPERTS].set(b_router)
    wbr = w_branch.astype(BF16)
    wout = w_out.astype(BF16)
    grouped = (depth, N_GROUPS, EXPERTS_PER_GROUP, d, D_EXPERT)
    wg, wu = w_exp_gate.astype(BF16).reshape(grouped), w_exp_up.astype(BF16).reshape(grouped)
    wd = w_exp_down.astype(BF16).reshape(depth, N_GROUPS, EXPERTS_PER_GROUP * D_EXPERT, d)
    tm_moe = _row_tile(seq, 1024)

    x2d = x.reshape(tokens, d)
    for l in range(depth):
        mod_l = mods[l]
        aux32 = [gmlp_ln_g[l][None], gmlp_ln_b[l][None]]
        aux16 = [dsa_qnorm_g[l][None], dsa_knorm_g[l][None], b_branch_gate[l].reshape(1, -1)]
        p32, p16, v_t, iw_t = _norm_proj(x2d, mod_l, g_norm1[l][None], w32, w16, l, aux32, aux16,
                                         seq, tm)
        yb = _gla(p32, p16, w2p[l], gla_b_gate[l][None], gla_norm_g[l][None], bsz, seq, tm)
        yc = _dsa(p16, v_t, iw_t, bias_tiles, bsz, seq)
        x1, h2, comb = _merge(p32, w_tril, gmlp_bias, yb, yc, p16, x2d, mod_l, wbr, wout, l,
                              g_norm2[l][None], wr_hi, wr_lo, br, seq, tm)
        x2d = _moe(h2, comb, x1, mod_l, wg, wu, wd, l, seq, tm_moe)
    return x2d.reshape(bsz, seq, d)
```

```python
import functools
import math

import numpy as np
import jax
import jax.numpy as jnp
from jax import lax
from jax.experimental import pallas as pl
from jax.experimental.pallas import tpu as pltpu

F32 = jnp.float32
BF16 = jnp.bfloat16

D_MODEL = 1024
D_BRANCH = 512
EPS = 1e-6
GMLP_CHUNK = 128
GMLP_GROUPS = 4
GLA_HEADS = 4
GLA_DK = 64
GLA_DV = 128
GLA_GATE_RANK = 16
GLA_GATE_TAU = 16.0
GLA_CHUNK = 128
GLA_SUB = 16
GLA_MILD_DECAY = -60.0
DSA_HEADS = 4
DSA_HDIM = 128
DSA_IDX_HEADS = 4
DSA_IDX_DIM = 64
DSA_QBLOCK = 128
DSA_KTILE = 512
DSA_TOPK_MAX = 256
N_BUCKETS = 32
MAX_DISTANCE = 128
N_GROUPS = 4
EXPERTS_PER_GROUP = 4
N_EXPERTS = 16
D_EXPERT = 256

LANES = 128
COL_TILE = 512
VMEM_LIMIT = 56 * 1024 * 1024
INT_MIN = -(2 ** 31)
NEG_BIG = -1e30

C32_U, C32_V, C32_GQ, C32_GK, C32_R, C32_SMALL = 0, 512, 1024, 1280, 1536, 2048
N32 = C32_SMALL + LANES
C16_GATES, C16_Q, C16_K, C16_GV, C16_IQ, C16_IK = 0, 3072, 3584, 4096, 4608, 4864
N16 = C16_IK + LANES


def _dot(a, b):
    return jnp.dot(a, b, preferred_element_type=F32)


def _dot_nt(a, b):
    return lax.dot_general(a, b, (((1,), (1,)), ((), ())), preferred_element_type=F32)


def _dot_tn(a, b):
    return lax.dot_general(a, b, (((0,), (0,)), ((), ())), preferred_element_type=F32)


def _split2(a):
    hi = a.astype(BF16)
    lo = (a - hi.astype(F32)).astype(BF16)
    return hi, lo


def _dot3(a, w_hi, w_lo):
    a_hi, a_lo = _split2(a)
    return _dot(a_hi, w_hi) + (_dot(a_lo, w_hi) + _dot(a_hi, w_lo))


def _head_rms(y, g, scale):
    outs = []
    for h in range(y.shape[1] // LANES):
        yh = y[:, h * LANES:(h + 1) * LANES]
        ms = jnp.mean(yh * yh, axis=-1, keepdims=True)
        o = yh * lax.rsqrt(ms + EPS) * g
        if scale != 1.0:
            o = o * scale
        outs.append(o)
    return jnp.concatenate(outs, axis=1)


def _mod_kernel(c_ref, w_ref, b_ref, o_ref):
    a = jax.nn.silu(c_ref[...])
    w_hi, w_lo = _split2(w_ref[...])
    o_ref[...] = _dot3(a, w_hi, w_lo) + b_ref[...]


def _modulation(c, w_mod, b_mod):
    L, d, n = w_mod.shape
    bsz = c.shape[0]
    rows = 8 * pl.cdiv(bsz, 8)
    c_pad = jnp.zeros((rows, d), F32).at[:bsz].set(c)
    tn = 1536
    out = pl.pallas_call(
        _mod_kernel,
        grid=(L, n // tn),
        in_specs=[
            pl.BlockSpec((rows, d), lambda l, j: (0, 0)),
            pl.BlockSpec((None, d, tn), lambda l, j: (l, 0, j)),
            pl.BlockSpec((None, 1, tn), lambda l, j: (l, 0, j)),
        ],
        out_specs=pl.BlockSpec((None, rows, tn), lambda l, j: (l, 0, j)),
        out_shape=jax.ShapeDtypeStruct((L, rows, n), F32),
        compiler_params=pltpu.CompilerParams(
            dimension_semantics=("arbitrary", "arbitrary"), vmem_limit_bytes=VMEM_LIMIT),
        name="adaln_modulation",
    )(c_pad, w_mod, b_mod.reshape(L, 1, n))
    return out[:, :bsz].reshape(L, bsz, 6, d)


def _t5_bucket_table():
    n = np.arange(2 * DSA_QBLOCK)
    max_exact = N_BUCKETS // 2
    large = max_exact + (
        np.log(np.maximum(n, max_exact).astype(np.float32) / max_exact)
        / math.log(MAX_DISTANCE / max_exact) * (N_BUCKETS - max_exact)).astype(np.int32)
    large = np.minimum(large, N_BUCKETS - 1)
    return np.where(n < max_exact, n, large).astype(np.int32)


def _bias_kernel(rb_ref, bucket_ref, o_ref):
    for t in range(bucket_ref.shape[0]):
        bucket = bucket_ref[t]
        for h in range(DSA_HEADS):
            acc = jnp.zeros(bucket.shape, F32)
            for b in range(N_BUCKETS):
                acc = jnp.where(bucket == b, rb_ref[b, h], acc)
            o_ref[t, h] = acc


def _bias_tiles(rel_bias):
    table = _t5_bucket_table()
    assert (table[MAX_DISTANCE:] == N_BUCKETS - 1).all()
    t = np.arange(DSA_QBLOCK)[None, :]
    s = np.arange(DSA_QBLOCK)[:, None]
    diag = table[np.maximum(t - s, 0)]
    near = table[DSA_QBLOCK + t - s]
    far = np.full_like(diag, N_BUCKETS - 1)
    buckets = jnp.asarray(np.stack([diag, near, far]).astype(np.int32))
    return pl.pallas_call(
        _bias_kernel,
        in_specs=[pl.BlockSpec(memory_space=pltpu.SMEM), pl.BlockSpec(memory_space=pltpu.VMEM)],
        out_specs=pl.BlockSpec(memory_space=pltpu.VMEM),
        out_shape=jax.ShapeDtypeStruct((3, DSA_HEADS, DSA_QBLOCK, DSA_QBLOCK), F32),
        name="t5_bias_tiles",
    )(rel_bias, buckets)


def _proj_kernel(x_ref, mod_ref, gn_ref, w32_ref, w16_ref, *rest):
    aux32, aux16 = rest[:2], rest[2:5]
    o32_ref, o16_ref, vt_ref, iwt_ref = rest[5:9]
    x = x_ref[...]
    y = x * lax.rsqrt(jnp.mean(x * x, axis=-1, keepdims=True) + EPS) * gn_ref[...]
    h = (y * (1.0 + mod_ref[0, 1:2, :]) + mod_ref[0, 0:1, :]).astype(BF16)
    for w_ref, o_ref, epilogues, aux in ((w32_ref, o32_ref, EPILOGUES_32, aux32),
                                         (w16_ref, o16_ref, EPILOGUES_16, aux16)):
        wcol = ocol = 0
        for width, epi in epilogues:
            y = epi(_dot(h, w_ref[:, wcol:wcol + width]), aux)
            wcol += width
            if epi is _epi_value_t:
                vt_ref[0] = y.T.astype(vt_ref.dtype)
                continue
            if epi is _epi_small:
                iwt_ref[...] = y.T[GLA_GATE_RANK:GLA_GATE_RANK + 8, :]
            o_ref[:, ocol:ocol + width] = y.astype(o_ref.dtype)
            ocol += width
        assert wcol == w_ref.shape[1] and ocol == o_ref.shape[1]


def _norm_proj(x2d, mod_l, gn, w32, w16, layer, aux32, aux16, seq, tm):
    tokens, d = x2d.shape
    tiles_per_seq = seq // tm
    assert tm == DSA_KTILE
    const = lambda a: pl.BlockSpec(a.shape, lambda i: (0, 0))
    resident = lambda a: pl.BlockSpec((None,) + a.shape[1:], lambda i: (layer, 0, 0),
                                      pipeline_mode=pl.Buffered(1))
    return pl.pallas_call(
        _proj_kernel,
        grid=(tokens // tm,),
        in_specs=[
            pl.BlockSpec((tm, d), lambda i: (i, 0)),
            pl.BlockSpec((1, 6, d), lambda i: (i // tiles_per_seq, 0, 0)),
            const(gn), resident(w32), resident(w16),
        ] + [const(a) for a in aux32 + aux16],
        out_specs=[pl.BlockSpec((tm, N32), lambda i: (i, 0)), pl.BlockSpec((tm, N16), lambda i: (i, 0)),
                   pl.BlockSpec((1, D_BRANCH, tm), lambda i: (i, 0, 0)),
                   pl.BlockSpec((8, tm), lambda i: (0, i))],
        out_shape=[jax.ShapeDtypeStruct((tokens, N32), F32), jax.ShapeDtypeStruct((tokens, N16), BF16),
                   jax.ShapeDtypeStruct((tokens // tm, D_BRANCH, tm), BF16),
                   jax.ShapeDtypeStruct((8, tokens), F32)],
        compiler_params=pltpu.CompilerParams(
            dimension_semantics=("parallel",), vmem_limit_bytes=VMEM_LIMIT),
        name="norm_proj",
    )(x2d, mod_l, gn, w32, w16, *aux32, *aux16)


def _epi_raw(y, aux):
    return y


def _epi_gelu(y, aux):
    return jax.nn.gelu(y)


def _epi_gelu_ln(y, aux):
    v = jax.nn.gelu(y)
    mu = jnp.mean(v, axis=-1, keepdims=True)
    var = jnp.mean(jnp.square(v - mu), axis=-1, keepdims=True)
    return (v - mu) * lax.rsqrt(var + EPS) * aux[0][...] + aux[1][...]


def _epi_silu(y, aux):
    return jax.nn.silu(y)


def _epi_gate(k):
    def epi(y, aux):
        return jax.nn.sigmoid(y + aux[2][:, k * COL_TILE:(k + 1) * COL_TILE])
    return epi


def _epi_qnorm(y, aux):
    return _head_rms(y, aux[0][...], DSA_HDIM ** -0.5)


def _epi_knorm(y, aux):
    return _head_rms(y, aux[1][...], 1.0)


def _epi_value_t(y, aux):
    return y


def _epi_small(y, aux):
    return y


EPILOGUES_32 = ([(COL_TILE, e) for e in (_epi_gelu, _epi_gelu_ln, _epi_raw, _epi_silu)]
                + [(LANES, _epi_small)])
EPILOGUES_16 = ([(COL_TILE, _epi_gate(k)) for k in range(6)]
                + [(COL_TILE, e) for e in (_epi_qnorm, _epi_knorm, _epi_value_t, _epi_raw)]
                + [(2 * LANES, _epi_raw), (LANES, _epi_raw)])


def _prep_w_in(w_in):
    sizes = (512, 512, 256, 256, 512, 512, 16, 512, 512, 512, 256, 64, 4, 3072)
    offs = np.concatenate([[0], np.cumsum(sizes)])
    seg = lambda k: w_in[:, :, offs[k]:offs[k + 1]]
    (a_u, a_v, g_q, g_k, g_v, g_r, g_a, d_q, d_k, d_v, d_iq, d_ik, d_iw, gates) = [seg(k) for k in range(14)]
    L, d, _ = w_in.shape
    zeros = lambda n: jnp.zeros((L, d, n), w_in.dtype)
    w32 = jnp.concatenate([a_u, a_v, g_q, g_k, g_r, g_a, d_iw,
                           zeros(N32 - C32_SMALL - 20)], axis=-1).astype(BF16)
    w16 = jnp.concatenate([gates, d_q, d_k, d_v, g_v, d_iq, d_ik, d_ik], axis=-1).astype(BF16)
    return w32, w16


def _gla_kernel(qk_ref, v_ref, r_ref, ga_ref, w2_ref, b2_ref, ng_ref, tril_ref, exp_ref,
                o_ref, st_ref, d_scr, g_scr):
    C, SUB, H, DK, DV = GLA_CHUNK, GLA_SUB, GLA_HEADS, GLA_DK, GLA_DV
    HK = H * DK
    n_chunks = qk_ref.shape[0] // C

    @pl.when(pl.program_id(1) == 0)
    def _():
        st_ref[...] = jnp.zeros_like(st_ref)

    lane = lax.broadcasted_iota(jnp.int32, (1, HK), 1)
    head_mask = [(lane >= h * DK) & (lane < (h + 1) * DK) for h in range(H)]
    row = lax.broadcasted_iota(jnp.int32, (C, C), 0)
    col = lax.broadcasted_iota(jnp.int32, (C, C), 1)
    sub_shift = SUB.bit_length() - 1
    blk_lower = (row >> sub_shift) > (col >> sub_shift)
    sub_t = lax.broadcasted_iota(jnp.int32, (SUB, 1), 0)

    graw = _dot(ga_ref[...].astype(BF16), w2_ref[...]) + b2_ref[...]
    g_all = jax.nn.log_sigmoid(graw) / GLA_GATE_TAU
    g_scr[...] = g_all
    chunk_decay = jnp.sum(g_all.reshape(n_chunks, C, HK), axis=1)
    mild = jnp.min(chunk_decay) >= GLA_MILD_DECAY

    def cumulative_decay(rows):
        g = g_scr[rows, :]
        g_hi = g.astype(BF16)
        g_r1 = g - g_hi.astype(F32)
        g_mid = g_r1.astype(BF16)
        g_lo = (g_r1 - g_mid.astype(F32)).astype(BF16)
        tril = tril_ref[...]
        return _dot(tril, g_hi) + (_dot(tril, g_mid) + _dot(tril, g_lo))

    def finish(rows, o, v, k_dec, b_last, st):
        upd = _dot_tn(v, k_dec)
        new_st = st * jnp.exp(b_last)
        for h in range(H):
            new_st = new_st + jnp.where(head_mask[h], upd[h * DV:(h + 1) * DV, :], 0.0)
        st_ref[...] = new_st
        o_ref[rows, :] = _head_rms(o, ng_ref[...], 1.0) * r_ref[rows, :]

    def chunk_mild(ci, carry):
        rows = pl.ds(pl.multiple_of(ci * C, C), C)
        q = qk_ref[rows, 0:HK] * (DK ** -0.5)
        k = qk_ref[rows, HK:2 * HK]
        v = v_ref[rows, :]
        b = cumulative_decay(rows)
        b_last = b[C - 1:C, :]
        st = st_ref[...]
        q_in = q * jnp.exp(b)
        k_out = (k * jnp.exp(-b)).astype(BF16)
        k_dec = (k * jnp.exp(b_last - b)).astype(BF16)
        outs = []
        for h in range(H):
            qm = jnp.where(head_mask[h], q_in, 0.0).astype(BF16)
            a_h = jnp.where(row >= col, _dot_nt(qm, k_out), 0.0).astype(BF16)
            outs.append(_dot_nt(qm, st.astype(BF16)) + _dot(a_h, v[:, h * DV:(h + 1) * DV]))
        finish(rows, jnp.concatenate(outs, axis=1), v, k_dec, b_last, st)
        return carry

    def chunk(ci, carry):
        r0 = pl.multiple_of(ci * C, C)
        rows = pl.ds(r0, C)
        q = qk_ref[rows, 0:HK] * (DK ** -0.5)
        k = qk_ref[rows, HK:2 * HK]
        v = v_ref[rows, :]
        b = cumulative_decay(rows)
        b_last = b[C - 1:C, :]
        st = st_ref[...]

        q_in = q * jnp.exp(b)
        k_dec = (k * jnp.exp(b_last - b)).astype(BF16)

        a_off = [jnp.zeros((C, C), F32) for _ in range(H)]
        for j in range(C // SUB - 1):
            bj = b[(j + 1) * SUB - 1:(j + 1) * SUB, :]
            qj = q * jnp.exp(jnp.minimum(b - bj, 0.0))
            in_blk = (lax.broadcasted_iota(jnp.int32, (C, 1), 0) >> sub_shift) == j
            kj = jnp.where(in_blk, k * jnp.exp(jnp.minimum(bj - b, 0.0)), 0.0).astype(BF16)
            for h in range(H):
                a_off[h] = a_off[h] + _dot_nt(jnp.where(head_mask[h], qj, 0.0).astype(BF16), kj)

        for i in range(C // SUB):
            rs = slice(i * SUB, (i + 1) * SUB)
            qi, bi = q[rs, :], b[rs, :]
            for s in range(SUB):
                ks = k[i * SUB + s:i * SUB + s + 1, :]
                bs = b[i * SUB + s:i * SUB + s + 1, :]
                dterm = qi * ks * jnp.exp(jnp.minimum(bi - bs, 0.0))
                dterm = jnp.where(sub_t >= s, dterm, 0.0)
                d_scr[s * SUB:(s + 1) * SUB, :] = dterm.astype(BF16)
            gsum = _dot(d_scr[...], exp_ref[...])
            od = jnp.zeros((SUB, H * DV), F32)
            for s in range(SUB):
                vs = v[i * SUB + s:i * SUB + s + 1, :].astype(F32)
                od = od + gsum[s * SUB:(s + 1) * SUB, :] * vs
            o_ref[pl.ds(r0 + i * SUB, SUB), :] = od

        outs = []
        for h in range(H):
            vh = v[:, h * DV:(h + 1) * DV]
            o_inter = _dot_nt(jnp.where(head_mask[h], q_in, 0.0).astype(BF16), st.astype(BF16))
            a_h = jnp.where(blk_lower, a_off[h], 0.0).astype(BF16)
            outs.append(o_inter + _dot(a_h, vh))
        o = o_ref[rows, :] + jnp.concatenate(outs, axis=1)
        finish(rows, o, v, k_dec, b_last, st)
        return carry

    @pl.when(mild)
    def _():
        lax.fori_loop(0, n_chunks, chunk_mild, 0, unroll=4)

    @pl.when(jnp.logical_not(mild))
    def _():
        lax.fori_loop(0, n_chunks, chunk, 0)


def _gla(p32, p16, w2p, b2, norm_g, bsz, seq, ts):
    C, SUB, H, DK, DV = GLA_CHUNK, GLA_SUB, GLA_HEADS, GLA_DK, GLA_DV
    tokens = p32.shape[0]
    blocks_per_seq = seq // ts
    tril = jnp.asarray(np.tril(np.ones((C, C), np.float32))).astype(BF16)
    expand = np.zeros((H * DK, H * DV), np.float32)
    for h in range(H):
        expand[h * DK:(h + 1) * DK, h * DV:(h + 1) * DV] = 1.0
    expand = jnp.asarray(expand).astype(BF16)
    row_map = lambda cb: (lambda b, i: (b * blocks_per_seq + i, cb))
    const2 = lambda b, i: (0, 0)
    out = pl.pallas_call(
        _gla_kernel,
        grid=(bsz, blocks_per_seq),
        in_specs=[
            pl.BlockSpec((ts, 2 * H * DK), row_map(C32_GQ // (2 * H * DK))),
            pl.BlockSpec((ts, H * DV), row_map(C16_GV // (H * DV))),
            pl.BlockSpec((ts, H * DV), row_map(C32_R // (H * DV))),
            pl.BlockSpec((ts, LANES), row_map(C32_SMALL // LANES)),
            pl.BlockSpec(w2p.shape, const2),
            pl.BlockSpec(b2.shape, const2),
            pl.BlockSpec(norm_g.shape, const2),
            pl.BlockSpec(tril.shape, const2),
            pl.BlockSpec(expand.shape, const2),
        ],
        out_specs=pl.BlockSpec((ts, H * DV), lambda b, i: (b * blocks_per_seq + i, 0)),
        out_shape=jax.ShapeDtypeStruct((tokens, H * DV), F32),
        scratch_shapes=[pltpu.VMEM((DV, H * DK), F32), pltpu.VMEM((SUB * SUB, H * DK), BF16),
                        pltpu.VMEM((ts, H * DK), F32)],
        compiler_params=pltpu.CompilerParams(
            dimension_semantics=("parallel", "arbitrary"), vmem_limit_bytes=VMEM_LIMIT),
        name="gla_branch",
    )(p32, p16, p32, p32, w2p, b2, norm_g, tril, expand)
    return out


def _bit_transpose32(words):
    a = list(words)
    shift, mask = 16, 0x0000FFFF
    while shift:
        m = jnp.int32(np.uint32(mask).astype(np.int32))
        for k in range(32):
            if k & shift == 0:
                t = (a[k] ^ lax.shift_right_logical(a[k + shift], jnp.int32(shift))) & m
                a[k] = a[k] ^ t
                a[k + shift] = a[k + shift] ^ lax.shift_left(t, jnp.int32(shift))
        shift >>= 1
        mask = (mask ^ (mask << shift)) & 0xFFFFFFFF
    return a


def _dsa_kernel(q_ref, iq_ref, iwt_ref, k_ref, vt_ref, ik_ref, bias_ref, ltri_ref,
                o_ref, key_scr, plane_scr, s_scr, acc_scr, *, topk):
    T, TK = DSA_QBLOCK, DSA_KTILE
    SUBS = TK // T
    H, HD = DSA_HEADS, DSA_HDIM
    qb = pl.program_id(1)
    kt_last = qb // SUBS
    n_kt = kt_last + 1
    key_minus_query = (lax.broadcasted_iota(jnp.int32, (TK, T), 0)
                       - lax.broadcasted_iota(jnp.int32, (TK, T), 1))

    def fold(x, op):
        return op(x.reshape(x.shape[0] // 8, 8, T), axis=0)

    def rows_to_one(x, op):
        return op(x, axis=0, keepdims=True)

    def key_rows(kt):
        return pl.ds(pl.multiple_of(kt * TK, TK), TK)

    def tile_loop(n, body, init):
        carry = lax.fori_loop(0, n // 2, lambda i, c: body(2 * i + 1, body(2 * i, c)), init)
        return lax.cond(n % 2 == 1, lambda c: body(n - 1, c), lambda c: c, carry)

    lo_mask = lax.broadcasted_iota(jnp.int32, (T, LANES), 1) < DSA_IDX_DIM
    iq = [iq_ref[:, 0:LANES], iq_ref[:, LANES:2 * LANES]]
    iq_h = [jnp.where(lo_mask, iq[0], 0), jnp.where(lo_mask, 0, iq[0]),
            jnp.where(lo_mask, iq[1], 0), jnp.where(lo_mask, 0, iq[1])]
    iq_all = jnp.concatenate(iq_h, axis=0)
    iw_h = [iwt_ref[h:h + 1, :] for h in range(DSA_IDX_HEADS)]

    GRP_ROWS = 32 * 8
    GRPS = TK // GRP_ROWS
    n_grp = plane_scr.shape[0]

    @pl.when(qb == 0)
    def _():
        plane_scr[...] = jnp.zeros_like(plane_scr)

    n_keys = key_scr.shape[0] * TK
    minus_one_minus_row = -1 - lax.broadcasted_iota(jnp.int32, (TK, T), 0)

    def score_tile(kt):
        ik2 = ik_ref[key_rows(kt), :]
        raw = _dot_nt(ik2, iq_all)
        score = jnp.zeros((TK, T), F32)
        for h in range(DSA_IDX_HEADS):
            score = score + iw_h[h] * jnp.maximum(raw[:, h * T:(h + 1) * T], 0.0)
        bits = lax.bitcast_convert_type(score, jnp.int32)
        zero_key = minus_one_minus_row - kt * TK
        negative_key = (bits ^ jnp.int32(0x7FFFFFFF)) - n_keys
        return jnp.where(score == 0.0, zero_key, jnp.where(bits >= 0, bits, negative_key))

    def store_planes(kt, key):
        unsigned_order = key ^ jnp.int32(INT_MIN)
        for g in range(GRPS):
            words = [unsigned_order[g * GRP_ROWS + 8 * j:g * GRP_ROWS + 8 * j + 8, :] for j in range(32)]
            planes = _bit_transpose32(words)
            for b in range(32):
                plane_scr[kt * GRPS + g, b] = planes[b]

    def score_body(kt, c):
        key = score_tile(kt)
        key_scr[kt] = key
        store_planes(kt, key)
        return c

    tile_loop(kt_last, score_body, 0)
    admissible = key_minus_query <= qb * T - kt_last * TK
    key = score_tile(kt_last)
    key_scr[kt_last] = jnp.where(admissible, key, jnp.int32(INT_MIN))
    store_planes(kt_last, key)
    alive_last = []
    for g in range(GRPS):
        word = jnp.zeros((8, T), jnp.int32)
        for j in range(32):
            adm = admissible[g * GRP_ROWS + 8 * j:g * GRP_ROWS + 8 * j + 8, :]
            word = word | jnp.where(adm, jnp.int32(np.int32(np.uint32(1 << (31 - j)))), 0)
        alive_last.append(word)

    alive = []
    for g in range(n_grp):
        kt = g // GRPS
        full = jnp.broadcast_to(jnp.where(kt < kt_last, jnp.int32(-1), jnp.int32(0)), (8, T))
        alive.append(jnp.where(kt == kt_last, alive_last[g % GRPS], full))

    def bits_body(i, carry):
        above, tau_u, alive = carry
        hi_planes = [plane_scr[g, 2 * i] for g in range(n_grp)]
        lo_planes = [plane_scr[g, 2 * i + 1] for g in range(n_grp)]
        n11 = n10 = n01 = jnp.zeros((8, T), jnp.int32)
        for g in range(n_grp):
            with_hi = alive[g] & hi_planes[g]
            both = with_hi & lo_planes[g]
            n11 = n11 + lax.population_count(both)
            n10 = n10 + lax.population_count(with_hi ^ both)
            n01 = n01 + lax.population_count((alive[g] & lo_planes[g]) ^ both)
        c11 = above + rows_to_one(n11, jnp.sum)
        c10 = c11 + rows_to_one(n10, jnp.sum)
        c01 = c10 + rows_to_one(n01, jnp.sum)
        is11, is10, is01 = c11 >= topk, c10 >= topk, c01 >= topk
        hi_bit = is10
        lo_bit = is11 | (is01 & ~is10)
        above = jnp.where(is11, above, jnp.where(is10, c11, jnp.where(is01, c10, c01)))
        flip_hi = jnp.where(hi_bit, jnp.int32(0), jnp.int32(-1))
        flip_lo = jnp.where(lo_bit, jnp.int32(0), jnp.int32(-1))
        alive = tuple(alive[g] & (hi_planes[g] ^ flip_hi) & (lo_planes[g] ^ flip_lo)
                      for g in range(n_grp))
        tau_u = (tau_u | jnp.where(hi_bit, lax.shift_left(jnp.int32(1), 31 - 2 * i), 0)
                 | jnp.where(lo_bit, lax.shift_left(jnp.int32(1), 30 - 2 * i), 0))
        return above, tau_u, alive

    zero_row = jnp.zeros((1, T), jnp.int32)
    above, tau_u, alive = lax.fori_loop(0, 16, bits_body, (zero_row, zero_row, tuple(alive)))
    n_tied = jnp.zeros((8, T), jnp.int32)
    for g in range(n_grp):
        n_tied = n_tied + lax.population_count(alive[g])
    n_ge = above + rows_to_one(n_tied, jnp.sum)
    tau = jnp.maximum(tau_u ^ jnp.int32(INT_MIN), jnp.int32(INT_MIN + 1))

    @pl.when(jnp.max(n_ge) > topk)
    def _():
        need = (topk - above).astype(F32)

        def tie_body(kt, seen):
            for j in range(SUBS):
                rows = slice(j * T, (j + 1) * T)
                key = key_scr[kt, rows, :]
                eq = key == tau
                eq_f = jnp.where(eq, 1.0, 0.0)
                pref = _dot(ltri_ref[...], eq_f.astype(BF16)) + seen
                key_scr[kt, rows, :] = jnp.where(eq & (pref > need), tau - 1, key)
                seen = seen + rows_to_one(eq_f, jnp.sum)
            return seen

        tile_loop(n_kt, tie_body, jnp.zeros((1, T), F32))

    q_h = [q_ref[:, h * HD:(h + 1) * HD] for h in range(H)]

    def logits_body(kt, m_run):
        sel = key_scr[kt] >= tau
        kinds = [jnp.clip(qb - (kt * SUBS + j), 0, 2) for j in range(SUBS)]
        new_m = []
        for h in range(H):
            bias = jnp.concatenate([bias_ref[kinds[j], h] for j in range(SUBS)], axis=0)
            s = _dot_nt(k_ref[key_rows(kt), h * HD:(h + 1) * HD], q_h[h]) + bias
            s = jnp.where(sel, s, NEG_BIG)
            s_scr[h, kt] = s
            new_m.append(jnp.maximum(m_run[h], fold(s, jnp.max)))
        return tuple(new_m)

    m_run = tile_loop(n_kt, logits_body, tuple(jnp.full((8, T), NEG_BIG, F32) for _ in range(H)))
    m_h = [rows_to_one(m, jnp.max) for m in m_run]

    for h in range(H):
        acc_scr[h] = jnp.zeros((HD, T), F32)

    def pv_body(kt, l_run):
        new_l = []
        for h in range(H):
            p = jnp.exp(s_scr[h, kt] - m_h[h])
            new_l.append(l_run[h] + fold(p, jnp.sum))
            acc_scr[h] = acc_scr[h] + _dot(vt_ref[kt, h * HD:(h + 1) * HD, :], p.astype(BF16))
        return tuple(new_l)

    l_run = tile_loop(n_kt, pv_body, tuple(jnp.zeros((8, T), F32) for _ in range(H)))

    for h in range(H):
        out_t = acc_scr[h] / rows_to_one(l_run[h], jnp.sum)
        o_ref[:, h * HD:(h + 1) * HD] = out_t.T.astype(o_ref.dtype)


def _dsa(p16, v_t, iw_t, bias_tiles, bsz, seq):
    T = DSA_QBLOCK
    tokens = p16.shape[0]
    nqb = seq // T
    topk = min(DSA_TOPK_MAX, seq // 4)
    TK = min(DSA_KTILE, seq)
    assert TK == DSA_KTILE and seq % TK == 0
    nkt = seq // TK
    ltri = jnp.asarray(np.tril(np.ones((T, T), np.float32))).astype(BF16)
    w = D_BRANCH
    v_t = v_t.reshape(bsz, nkt, w, TK)
    qmap = lambda cb: (lambda b, i: (b * nqb + i, cb))
    return pl.pallas_call(
        functools.partial(_dsa_kernel, topk=topk),
        grid=(bsz, nqb),
        in_specs=[
            pl.BlockSpec((T, w), qmap(C16_Q // w)),
            pl.BlockSpec((T, 2 * LANES), qmap(C16_IQ // (2 * LANES))),
            pl.BlockSpec((8, T), lambda b, i: (0, b * nqb + i)),
            pl.BlockSpec((seq, w), lambda b, i: (b, C16_K // w)),
            pl.BlockSpec((None, nkt, w, TK), lambda b, i: (b, 0, 0, 0)),
            pl.BlockSpec((seq, LANES), lambda b, i: (b, C16_IK // LANES)),
            pl.BlockSpec(bias_tiles.shape, lambda b, i: (0, 0, 0, 0)),
            pl.BlockSpec(ltri.shape, lambda b, i: (0, 0)),
        ],
        out_specs=pl.BlockSpec((T, w), lambda b, i: (b * nqb + i, 0)),
        out_shape=jax.ShapeDtypeStruct((tokens, w), BF16),
        scratch_shapes=[
            pltpu.VMEM((nkt, TK, T), jnp.int32),
            pltpu.VMEM((nkt * TK // 256, 32, 8, T), jnp.int32),
            pltpu.VMEM((DSA_HEADS, nkt, TK, T), F32),
            pltpu.VMEM((DSA_HEADS, DSA_HDIM, T), F32),
        ],
        compiler_params=pltpu.CompilerParams(
            dimension_semantics=("parallel", "arbitrary"), vmem_limit_bytes=VMEM_LIMIT),
        name="dsa_branch",
    )(p16, p16, iw_t, p16, v_t, p16, bias_tiles, ltri)


def _merge_kernel(u_ref, v_ref, ws_ref, bs_ref, yb_ref, yc_ref, ga_ref, gb_ref, gc_ref, x_ref, mod_ref,
                  wbr_ref, wout_ref, gn2_ref, wr_hi_ref, wr_lo_ref, br_ref, x1_ref, h2_ref, comb_ref):
    tm = u_ref.shape[0]
    ya_rows = []
    for c in range(tm // GMLP_CHUNK):
        rows = slice(c * GMLP_CHUNK, (c + 1) * GMLP_CHUNK)
        groups = []
        for g in range(GMLP_GROUPS):
            cols = slice(g * LANES, (g + 1) * LANES)
            mixed = _dot(ws_ref[g], v_ref[rows, cols].astype(BF16)) + bs_ref[:, cols]
            groups.append((u_ref[rows, cols] * mixed).astype(BF16))
        ya_rows.append(jnp.concatenate(groups, axis=1))
    ya = jnp.concatenate(ya_rows, axis=0)

    merged = ga_ref[...] * _dot(ya, wbr_ref[0])
    merged = merged + gb_ref[...] * _dot(yb_ref[...].astype(BF16), wbr_ref[1])
    merged = merged + gc_ref[...] * _dot(yc_ref[...], wbr_ref[2])
    mix = _dot(merged.astype(BF16), wout_ref[...])
    x1 = x_ref[...] + mod_ref[0, 2:3, :] * mix
    x1_ref[...] = x1
    y = x1 * lax.rsqrt(jnp.mean(x1 * x1, axis=-1, keepdims=True) + EPS) * gn2_ref[...]
    h2 = y * (1.0 + mod_ref[0, 4:5, :]) + mod_ref[0, 3:4, :]
    h2_ref[...] = h2.astype(BF16)
    comb = _route(_dot3(h2, wr_hi_ref[...], wr_lo_ref[...]) + br_ref[...])
    hi = comb.astype(BF16)
    rest = comb - hi.astype(F32)
    mid = rest.astype(BF16)
    comb_ref[0] = hi
    comb_ref[1] = mid
    comb_ref[2] = (rest - mid.astype(F32)).astype(BF16)


def _merge(p32, w_tril, gmlp_bias, yb, yc, p16, x2d, mod_l, wbr, wout, layer, gn2, wr_hi, wr_lo, br,
           seq, tm):
    tokens, d = x2d.shape
    tiles_per_seq = seq // tm
    row = lambda i: (i, 0)
    gate = lambda k: (lambda i: (i, C16_GATES // d + k))
    c2 = lambda i: (0, 0)
    of_layer = lambda a: pl.BlockSpec((None,) + a.shape[1:], lambda i: (layer,) + (0,) * (a.ndim - 1))
    return pl.pallas_call(
        _merge_kernel,
        grid=(tokens // tm,),
        in_specs=[
            pl.BlockSpec((tm, D_BRANCH), lambda i: (i, C32_U // D_BRANCH)),
            pl.BlockSpec((tm, D_BRANCH), lambda i: (i, C32_V // D_BRANCH)),
            of_layer(w_tril), of_layer(gmlp_bias),
            pl.BlockSpec((tm, D_BRANCH), row), pl.BlockSpec((tm, D_BRANCH), row),
            pl.BlockSpec((tm, d), gate(0)), pl.BlockSpec((tm, d), gate(1)), pl.BlockSpec((tm, d), gate(2)),
            pl.BlockSpec((tm, d), row),
            pl.BlockSpec((1, 6, d), lambda i: (i // tiles_per_seq, 0, 0)),
            of_layer(wbr), of_layer(wout),
            pl.BlockSpec(gn2.shape, c2),
            of_layer(wr_hi), of_layer(wr_lo), of_layer(br),
        ],
        out_specs=[pl.BlockSpec((tm, d), row), pl.BlockSpec((tm, d), row),
                   pl.BlockSpec((3, tm, LANES), lambda i: (0, i, 0))],
        out_shape=[jax.ShapeDtypeStruct((tokens, d), F32), jax.ShapeDtypeStruct((tokens, d), BF16),
                   jax.ShapeDtypeStruct((3, tokens, LANES), BF16)],
        compiler_params=pltpu.CompilerParams(
            dimension_semantics=("parallel",), vmem_limit_bytes=VMEM_LIMIT),
        name="gmlp_merge_norm_router",
    )(p32, p32, w_tril, gmlp_bias, yb, yc, p16, p16, p16, x2d, mod_l, wbr, wout, gn2, wr_hi, wr_lo, br)


def _route(lg):
    lane = lax.broadcasted_iota(jnp.int32, lg.shape, 1)
    big = jnp.int32(10 ** 6)
    is_grp = lane < N_GROUPS
    gl = jnp.where(is_grp, lg, -jnp.inf)
    gmax = jnp.max(gl, axis=1, keepdims=True)
    gsum = jnp.sum(jnp.where(is_grp, jnp.exp(lg - gmax), 0.0), axis=1, keepdims=True)
    p_g = 1.0 / gsum
    g_idx = jnp.min(jnp.where(gl == gmax, lane, big), axis=1, keepdims=True)
    first = N_GROUPS + g_idx * EXPERTS_PER_GROUP
    in_grp = (lane >= first) & (lane < first + EXPERTS_PER_GROUP)
    e1 = jnp.where(in_grp, lg, -jnp.inf)
    v1 = jnp.max(e1, axis=1, keepdims=True)
    i1 = jnp.min(jnp.where(e1 == v1, lane, big), axis=1, keepdims=True)
    e2 = jnp.where(in_grp & (lane != i1), lg, -jnp.inf)
    v2 = jnp.max(e2, axis=1, keepdims=True)
    i2 = jnp.min(jnp.where(e2 == v2, lane, big), axis=1, keepdims=True)
    t = jnp.exp(v2 - v1)
    w1 = p_g * (1.0 / (1.0 + t))
    w2 = p_g * (t / (1.0 + t))
    return jnp.where(lane == i1, w1, jnp.where(lane == i2, w2, 0.0))


def _moe_kernel(h_ref, comb_ref, x1_ref, mod_ref, wg_ref, wu_ref, wd_ref, o_ref, acc_scr):
    grp = pl.program_id(1)

    @pl.when(grp == 0)
    def _():
        acc_scr[...] = jnp.zeros_like(acc_scr)

    h = h_ref[...]
    row_id = lax.broadcasted_iota(jnp.int32, (LANES, LANES), 0)
    acts = []
    for j in range(EXPERTS_PER_GROUP):
        lane_of_expert = N_GROUPS + grp * EXPERTS_PER_GROUP + j
        pick = jnp.where(row_id == lane_of_expert, 1.0, 0.0).astype(BF16)
        cw = _dot(comb_ref[0], pick) + (_dot(comb_ref[1], pick) + _dot(comb_ref[2], pick))
        act = (jax.nn.silu(_dot(h, wg_ref[j])) * _dot(h, wu_ref[j])
               * jnp.concatenate([cw, cw], axis=1))
        acts.append(act.astype(BF16))
    acc_scr[...] += _dot(jnp.concatenate(acts, axis=1), wd_ref[...])

    @pl.when(grp == pl.num_programs(1) - 1)
    def _():
        o_ref[...] = x1_ref[...] + mod_ref[0, 5:6, :] * acc_scr[...]


def _moe(h2, comb, x1, mod_l, wg, wu, wd, layer, seq, tm):
    tokens, d = x1.shape
    tiles_per_seq = seq // tm
    row = lambda i, g: (i, 0)
    return pl.pallas_call(
        _moe_kernel,
        grid=(tokens // tm, N_GROUPS),
        in_specs=[
            pl.BlockSpec((tm, d), row), pl.BlockSpec((3, tm, LANES), lambda i, g: (0, i, 0)),
            pl.BlockSpec((tm, d), row),
            pl.BlockSpec((1, 6, d), lambda i, g: (i // tiles_per_seq, 0, 0)),
            pl.BlockSpec((None, None) + wg.shape[2:], lambda i, g: (layer, g, 0, 0, 0)),
            pl.BlockSpec((None, None) + wu.shape[2:], lambda i, g: (layer, g, 0, 0, 0)),
            pl.BlockSpec((None, None) + wd.shape[2:], lambda i, g: (layer, g, 0, 0)),
        ],
        out_specs=pl.BlockSpec((tm, d), row),
        out_shape=jax.ShapeDtypeStruct((tokens, d), F32),
        scratch_shapes=[pltpu.VMEM((tm, d), F32)],
        compiler_params=pltpu.CompilerParams(
            dimension_semantics=("parallel", "arbitrary"), vmem_limit_bytes=VMEM_LIMIT),
        name="hier_moe",
    )(h2, comb, x1, mod_l, wg, wu, wd)


def _row_tile(seq, want):
    t = min(want, seq)
    assert seq % t == 0
    return t


def kernel(x, c, w_mod, b_mod, g_norm1, g_norm2, w_in, gmlp_ln_g, gmlp_ln_b, gmlp_w_s, gmlp_b_s,
           gla_w_gate2, gla_b_gate, gla_norm_g, dsa_qnorm_g, dsa_knorm_g, rel_bias, w_branch,
           b_branch_gate, w_out, w_group, b_group, w_router, b_router, w_exp_gate, w_exp_up,
           w_exp_down):
    bsz, seq, d = x.shape
    depth = w_mod.shape[0]
    assert d == D_MODEL and seq % DSA_QBLOCK == 0 and seq % GLA_CHUNK == 0
    tokens = bsz * seq
    tm = _row_tile(seq, 512)

    mods = _modulation(c, w_mod, b_mod)
    bias_tiles = _bias_tiles(rel_bias)
    w32, w16 = _prep_w_in(w_in)

    causal = np.tril(np.ones((GMLP_CHUNK, GMLP_CHUNK), bool))
    w_tril = jnp.where(causal[None, None], gmlp_w_s, 0.0).astype(BF16)
    gmlp_bias = jnp.repeat(jnp.swapaxes(gmlp_b_s, 1, 2), LANES, axis=-1)
    w2p = jnp.zeros((depth, LANES, GLA_HEADS * GLA_DK), F32).at[:, :GLA_GATE_RANK].set(gla_w_gate2)
    w2p = w2p.astype(BF16)
    wr = jnp.zeros((depth, d, LANES), F32)
    wr = wr.at[:, :, :N_GROUPS].set(w_group).at[:, :, N_GROUPS:N_GROUPS + N_EXPERTS].set(w_router)
    wr_hi = wr.astype(BF16)
    wr_lo = (wr - wr_hi.astype(F32)).astype(BF16)
    br = jnp.zeros((depth, 1, LANES), F32)
    br = br.at[:, 0, :N_GROUPS].set(b_group).at[:, 0, N_GROUPS:N_GROUPS + N_EXPERTS].set(b_router)
    wbr = w_branch.astype(BF16)
    wout = w_out.astype(BF16)
    grouped = (depth, N_GROUPS, EXPERTS_PER_GROUP, d, D_EXPERT)
    wg, wu = w_exp_gate.astype(BF16).reshape(grouped), w_exp_up.astype(BF16).reshape(grouped)
    wd = w_exp_down.astype(BF16).reshape(depth, N_GROUPS, EXPERTS_PER_GROUP * D_EXPERT, d)
    tm_moe = _row_tile(seq, 1024)

    x2d = x.reshape(tokens, d)
    for l in range(depth):
        mod_l = mods[l]
        aux32 = [gmlp_ln_g[l][None], gmlp_ln_b[l][None]]
        aux16 = [dsa_qnorm_g[l][None], dsa_knorm_g[l][None], b_branch_gate[l].reshape(1, -1)]
        p32, p16, v_t, iw_t = _norm_proj(x2d, mod_l, g_norm1[l][None], w32, w16, l, aux32, aux16,
                                         seq, tm)
        yb = _gla(p32, p16, w2p[l], gla_b_gate[l][None], gla_norm_g[l][None], bsz, seq, tm)
        yc = _dsa(p16, v_t, iw_t, bias_tiles, bsz, seq)
        x1, h2, comb = _merge(p32, w_tril, gmlp_bias, yb, yc, p16, x2d, mod_l, wbr, wout, l,
                              g_norm2[l][None], wr_hi, wr_lo, br, seq, tm)
        x2d = _moe(h2, comb, x1, mod_l, wg, wu, wd, l, seq, tm_moe)
    return x2d.reshape(bsz, seq, d)
```

```python
import functools
import math

import numpy as np
import jax
import jax.numpy as jnp
from jax import lax
from jax.experimental import pallas as pl
from jax.experimental.pallas import tpu as pltpu

F32 = jnp.float32
BF16 = jnp.bfloat16

D_MODEL = 1024
D_BRANCH = 512
EPS = 1e-6
GMLP_CHUNK = 128
GMLP_GROUPS = 4
GLA_HEADS = 4
GLA_DK = 64
GLA_DV = 128
GLA_GATE_RANK = 16
GLA_GATE_TAU = 16.0
GLA_CHUNK = 128
GLA_SUB = 16
GLA_MILD_DECAY = -60.0
DSA_HEADS = 4
DSA_HDIM = 128
DSA_IDX_HEADS = 4
DSA_IDX_DIM = 64
DSA_QBLOCK = 128
DSA_KTILE = 512
DSA_TOPK_MAX = 256
N_BUCKETS = 32
MAX_DISTANCE = 128
N_GROUPS = 4
EXPERTS_PER_GROUP = 4
N_EXPERTS = 16
D_EXPERT = 256
MOE_CHUNK = 256
MOE_ALIGN = 16

LANES = 128
COL_TILE = 512
VMEM_LIMIT = 56 * 1024 * 1024
INT_MIN = -(2 ** 31)
NEG_BIG = -1e30

C32_U, C32_V, C32_GQ, C32_GK, C32_R, C32_SMALL = 0, 512, 1024, 1280, 1536, 2048
N32 = C32_SMALL + LANES
C16_GATES, C16_Q, C16_K, C16_GV, C16_IQ, C16_IK = 0, 3072, 3584, 4096, 4608, 4864
N16 = C16_IK + LANES


def _dot(a, b):
    return jnp.dot(a, b, preferred_element_type=F32)


def _dot_nt(a, b):
    return lax.dot_general(a, b, (((1,), (1,)), ((), ())), preferred_element_type=F32)


def _dot_tn(a, b):
    return lax.dot_general(a, b, (((0,), (0,)), ((), ())), preferred_element_type=F32)


def _split2(a):
    hi = a.astype(BF16)
    lo = (a - hi.astype(F32)).astype(BF16)
    return hi, lo


def _dot3(a, w_hi, w_lo):
    a_hi, a_lo = _split2(a)
    return _dot(a_hi, w_hi) + (_dot(a_lo, w_hi) + _dot(a_hi, w_lo))


def _head_rms(y, g, scale):
    outs = []
    for h in range(y.shape[1] // LANES):
        yh = y[:, h * LANES:(h + 1) * LANES]
        ms = jnp.mean(yh * yh, axis=-1, keepdims=True)
        o = yh * lax.rsqrt(ms + EPS) * g
        if scale != 1.0:
            o = o * scale
        outs.append(o)
    return jnp.concatenate(outs, axis=1)


def _mod_kernel(c_ref, w_ref, b_ref, o_ref):
    a = jax.nn.silu(c_ref[...])
    w_hi, w_lo = _split2(w_ref[...])
    o_ref[...] = _dot3(a, w_hi, w_lo) + b_ref[...]


def _modulation(c, w_mod, b_mod):
    L, d, n = w_mod.shape
    bsz = c.shape[0]
    rows = 8 * pl.cdiv(bsz, 8)
    c_pad = jnp.zeros((rows, d), F32).at[:bsz].set(c)
    tn = 1536
    out = pl.pallas_call(
        _mod_kernel,
        grid=(L, n // tn),
        in_specs=[
            pl.BlockSpec((rows, d), lambda l, j: (0, 0)),
            pl.BlockSpec((None, d, tn), lambda l, j: (l, 0, j)),
            pl.BlockSpec((None, 1, tn), lambda l, j: (l, 0, j)),
        ],
        out_specs=pl.BlockSpec((None, rows, tn), lambda l, j: (l, 0, j)),
        out_shape=jax.ShapeDtypeStruct((L, rows, n), F32),
        compiler_params=pltpu.CompilerParams(
            dimension_semantics=("arbitrary", "arbitrary"), vmem_limit_bytes=VMEM_LIMIT),
        name="adaln_modulation",
    )(c_pad, w_mod, b_mod.reshape(L, 1, n))
    return out[:, :bsz].reshape(L, bsz, 6, d)


def _t5_bucket_table():
    n = np.arange(2 * DSA_QBLOCK)
    max_exact = N_BUCKETS // 2
    large = max_exact + (
        np.log(np.maximum(n, max_exact).astype(np.float32) / max_exact)
        / math.log(MAX_DISTANCE / max_exact) * (N_BUCKETS - max_exact)).astype(np.int32)
    large = np.minimum(large, N_BUCKETS - 1)
    return np.where(n < max_exact, n, large).astype(np.int32)


def _bias_kernel(rb_ref, bucket_ref, o_ref):
    for t in range(bucket_ref.shape[0]):
        bucket = bucket_ref[t]
        for h in range(DSA_HEADS):
            acc = jnp.zeros(bucket.shape, F32)
            for b in range(N_BUCKETS):
                acc = jnp.where(bucket == b, rb_ref[b, h], acc)
            o_ref[t, h] = acc


def _bias_tiles(rel_bias):
    table = _t5_bucket_table()
    assert (table[MAX_DISTANCE:] == N_BUCKETS - 1).all()
    t = np.arange(DSA_QBLOCK)[None, :]
    s = np.arange(DSA_QBLOCK)[:, None]
    diag = table[np.maximum(t - s, 0)]
    near = table[DSA_QBLOCK + t - s]
    far = np.full_like(diag, N_BUCKETS - 1)
    buckets = jnp.asarray(np.stack([diag, near, far]).astype(np.int32))
    return pl.pallas_call(
        _bias_kernel,
        in_specs=[pl.BlockSpec(memory_space=pltpu.SMEM), pl.BlockSpec(memory_space=pltpu.VMEM)],
        out_specs=pl.BlockSpec(memory_space=pltpu.VMEM),
        out_shape=jax.ShapeDtypeStruct((3, DSA_HEADS, DSA_QBLOCK, DSA_QBLOCK), F32),
        name="t5_bias_tiles",
    )(rel_bias, buckets)


def _proj_kernel(x_ref, mod_ref, gn_ref, w32_ref, w16_ref, *rest):
    aux32, aux16 = rest[:2], rest[2:5]
    o32_ref, o16_ref, vt_ref, iwt_ref = rest[5:9]
    x = x_ref[...]
    y = x * lax.rsqrt(jnp.mean(x * x, axis=-1, keepdims=True) + EPS) * gn_ref[...]
    h = (y * (1.0 + mod_ref[0, 1:2, :]) + mod_ref[0, 0:1, :]).astype(BF16)
    for w_ref, o_ref, epilogues, aux in ((w32_ref, o32_ref, EPILOGUES_32, aux32),
                                         (w16_ref, o16_ref, EPILOGUES_16, aux16)):
        wcol = ocol = 0
        for width, epi in epilogues:
            y = epi(_dot(h, w_ref[:, wcol:wcol + width]), aux)
            wcol += width
            if epi is _epi_value_t:
                vt_ref[0] = y.T.astype(vt_ref.dtype)
                continue
            if epi is _epi_small:
                iwt_ref[...] = y.T[GLA_GATE_RANK:GLA_GATE_RANK + 8, :]
            o_ref[:, ocol:ocol + width] = y.astype(o_ref.dtype)
            ocol += width
        assert wcol == w_ref.shape[1] and ocol == o_ref.shape[1]


def _norm_proj(x2d, mod_l, gn, w32, w16, layer, aux32, aux16, seq, tm):
    tokens, d = x2d.shape
    tiles_per_seq = seq // tm
    assert tm == DSA_KTILE
    const = lambda a: pl.BlockSpec(a.shape, lambda i: (0, 0))
    resident = lambda a: pl.BlockSpec((None,) + a.shape[1:], lambda i: (layer, 0, 0),
                                      pipeline_mode=pl.Buffered(1))
    return pl.pallas_call(
        _proj_kernel,
        grid=(tokens // tm,),
        in_specs=[
            pl.BlockSpec((tm, d), lambda i: (i, 0)),
            pl.BlockSpec((1, 6, d), lambda i: (i // tiles_per_seq, 0, 0)),
            const(gn), resident(w32), resident(w16),
        ] + [const(a) for a in aux32 + aux16],
        out_specs=[pl.BlockSpec((tm, N32), lambda i: (i, 0)), pl.BlockSpec((tm, N16), lambda i: (i, 0)),
                   pl.BlockSpec((1, D_BRANCH, tm), lambda i: (i, 0, 0)),
                   pl.BlockSpec((8, tm), lambda i: (0, i))],
        out_shape=[jax.ShapeDtypeStruct((tokens, N32), F32), jax.ShapeDtypeStruct((tokens, N16), BF16),
                   jax.ShapeDtypeStruct((tokens // tm, D_BRANCH, tm), BF16),
                   jax.ShapeDtypeStruct((8, tokens), F32)],
        compiler_params=pltpu.CompilerParams(
            dimension_semantics=("parallel",), vmem_limit_bytes=VMEM_LIMIT),
        name="norm_proj",
    )(x2d, mod_l, gn, w32, w16, *aux32, *aux16)


def _epi_raw(y, aux):
    return y


def _epi_gelu(y, aux):
    return jax.nn.gelu(y)


def _epi_gelu_ln(y, aux):
    v = jax.nn.gelu(y)
    mu = jnp.mean(v, axis=-1, keepdims=True)
    var = jnp.mean(jnp.square(v - mu), axis=-1, keepdims=True)
    return (v - mu) * lax.rsqrt(var + EPS) * aux[0][...] + aux[1][...]


def _epi_silu(y, aux):
    return jax.nn.silu(y)


def _epi_gate(k):
    def epi(y, aux):
        return jax.nn.sigmoid(y + aux[2][:, k * COL_TILE:(k + 1) * COL_TILE])
    return epi


def _epi_qnorm(y, aux):
    return _head_rms(y, aux[0][...], DSA_HDIM ** -0.5)


def _epi_knorm(y, aux):
    return _head_rms(y, aux[1][...], 1.0)


def _epi_value_t(y, aux):
    return y


def _epi_small(y, aux):
    return y


EPILOGUES_32 = ([(COL_TILE, e) for e in (_epi_gelu, _epi_gelu_ln, _epi_raw, _epi_silu)]
                + [(LANES, _epi_small)])
EPILOGUES_16 = ([(COL_TILE, _epi_gate(k)) for k in range(6)]
                + [(COL_TILE, e) for e in (_epi_qnorm, _epi_knorm, _epi_value_t, _epi_raw)]
                + [(2 * LANES, _epi_raw), (LANES, _epi_raw)])


def _prep_w_in(w_in):
    sizes = (512, 512, 256, 256, 512, 512, 16, 512, 512, 512, 256, 64, 4, 3072)
    offs = np.concatenate([[0], np.cumsum(sizes)])
    seg = lambda k: w_in[:, :, offs[k]:offs[k + 1]]
    (a_u, a_v, g_q, g_k, g_v, g_r, g_a, d_q, d_k, d_v, d_iq, d_ik, d_iw, gates) = [seg(k) for k in range(14)]
    L, d, _ = w_in.shape
    zeros = lambda n: jnp.zeros((L, d, n), w_in.dtype)
    w32 = jnp.concatenate([a_u, a_v, g_q, g_k, g_r, g_a, d_iw,
                           zeros(N32 - C32_SMALL - 20)], axis=-1).astype(BF16)
    w16 = jnp.concatenate([gates, d_q, d_k, d_v, g_v, d_iq, d_ik, d_ik], axis=-1).astype(BF16)
    return w32, w16


def _gla_kernel(qk_ref, v_ref, r_ref, ga_ref, w2_ref, b2_ref, ng_ref, tril_ref, exp_ref,
                o_ref, st_ref, d_scr, g_scr):
    C, SUB, H, DK, DV = GLA_CHUNK, GLA_SUB, GLA_HEADS, GLA_DK, GLA_DV
    HK = H * DK
    n_chunks = qk_ref.shape[0] // C

    @pl.when(pl.program_id(1) == 0)
    def _():
        st_ref[...] = jnp.zeros_like(st_ref)

    lane = lax.broadcasted_iota(jnp.int32, (1, HK), 1)
    head_mask = [(lane >= h * DK) & (lane < (h + 1) * DK) for h in range(H)]
    row = lax.broadcasted_iota(jnp.int32, (C, C), 0)
    col = lax.broadcasted_iota(jnp.int32, (C, C), 1)
    sub_shift = SUB.bit_length() - 1
    blk_lower = (row >> sub_shift) > (col >> sub_shift)
    sub_t = lax.broadcasted_iota(jnp.int32, (SUB, 1), 0)

    graw = _dot(ga_ref[...].astype(BF16), w2_ref[...]) + b2_ref[...]
    g_all = jax.nn.log_sigmoid(graw) / GLA_GATE_TAU
    g_scr[...] = g_all
    chunk_decay = jnp.sum(g_all.reshape(n_chunks, C, HK), axis=1)
    mild = jnp.min(chunk_decay) >= GLA_MILD_DECAY

    def cumulative_decay(rows):
        g = g_scr[rows, :]
        g_hi = g.astype(BF16)
        g_r1 = g - g_hi.astype(F32)
        g_mid = g_r1.astype(BF16)
        g_lo = (g_r1 - g_mid.astype(F32)).astype(BF16)
        tril = tril_ref[...]
        return _dot(tril, g_hi) + (_dot(tril, g_mid) + _dot(tril, g_lo))

    def finish(rows, o, v, k_dec, b_last, st):
        upd = _dot_tn(v, k_dec)
        new_st = st * jnp.exp(b_last)
        for h in range(H):
            new_st = new_st + jnp.where(head_mask[h], upd[h * DV:(h + 1) * DV, :], 0.0)
        st_ref[...] = new_st
        o_ref[rows, :] = _head_rms(o, ng_ref[...], 1.0) * r_ref[rows, :]

    def chunk_mild(ci, carry):
        rows = pl.ds(pl.multiple_of(ci * C, C), C)
        q = qk_ref[rows, 0:HK] * (DK ** -0.5)
        k = qk_ref[rows, HK:2 * HK]
        v = v_ref[rows, :]
        b = cumulative_decay(rows)
        b_last = b[C - 1:C, :]
        st = st_ref[...]
        q_in = q * jnp.exp(b)
        k_out = (k * jnp.exp(-b)).astype(BF16)
        k_dec = (k * jnp.exp(b_last - b)).astype(BF16)
        outs = []
        for h in range(H):
            qm = jnp.where(head_mask[h], q_in, 0.0).astype(BF16)
            a_h = jnp.where(row >= col, _dot_nt(qm, k_out), 0.0).astype(BF16)
            outs.append(_dot_nt(qm, st.astype(BF16)) + _dot(a_h, v[:, h * DV:(h + 1) * DV]))
        finish(rows, jnp.concatenate(outs, axis=1), v, k_dec, b_last, st)
        return carry

    def chunk(ci, carry):
        r0 = pl.multiple_of(ci * C, C)
        rows = pl.ds(r0, C)
        q = qk_ref[rows, 0:HK] * (DK ** -0.5)
        k = qk_ref[rows, HK:2 * HK]
        v = v_ref[rows, :]
        b = cumulative_decay(rows)
        b_last = b[C - 1:C, :]
        st = st_ref[...]

        q_in = q * jnp.exp(b)
        k_dec = (k * jnp.exp(b_last - b)).astype(BF16)

        a_off = [jnp.zeros((C, C), F32) for _ in range(H)]
        for j in range(C // SUB - 1):
            bj = b[(j + 1) * SUB - 1:(j + 1) * SUB, :]
            qj = q * jnp.exp(jnp.minimum(b - bj, 0.0))
            in_blk = (lax.broadcasted_iota(jnp.int32, (C, 1), 0) >> sub_shift) == j
            kj = jnp.where(in_blk, k * jnp.exp(jnp.minimum(bj - b, 0.0)), 0.0).astype(BF16)
            for h in range(H):
                a_off[h] = a_off[h] + _dot_nt(jnp.where(head_mask[h], qj, 0.0).astype(BF16), kj)

        for i in range(C // SUB):
            rs = slice(i * SUB, (i + 1) * SUB)
            qi, bi = q[rs, :], b[rs, :]
            for s in range(SUB):
                ks = k[i * SUB + s:i * SUB + s + 1, :]
                bs = b[i * SUB + s:i * SUB + s + 1, :]
                dterm = qi * ks * jnp.exp(jnp.minimum(bi - bs, 0.0))
                dterm = jnp.where(sub_t >= s, dterm, 0.0)
                d_scr[s * SUB:(s + 1) * SUB, :] = dterm.astype(BF16)
            gsum = _dot(d_scr[...], exp_ref[...])
            od = jnp.zeros((SUB, H * DV), F32)
            for s in range(SUB):
                vs = v[i * SUB + s:i * SUB + s + 1, :].astype(F32)
                od = od + gsum[s * SUB:(s + 1) * SUB, :] * vs
            o_ref[pl.ds(r0 + i * SUB, SUB), :] = od

        outs = []
        for h in range(H):
            vh = v[:, h * DV:(h + 1) * DV]
            o_inter = _dot_nt(jnp.where(head_mask[h], q_in, 0.0).astype(BF16), st.astype(BF16))
            a_h = jnp.where(blk_lower, a_off[h], 0.0).astype(BF16)
            outs.append(o_inter + _dot(a_h, vh))
        o = o_ref[rows, :] + jnp.concatenate(outs, axis=1)
        finish(rows, o, v, k_dec, b_last, st)
        return carry

    @pl.when(mild)
    def _():
        lax.fori_loop(0, n_chunks, chunk_mild, 0, unroll=4)

    @pl.when(jnp.logical_not(mild))
    def _():
        lax.fori_loop(0, n_chunks, chunk, 0)


def _gla(p32, p16, w2p, b2, norm_g, bsz, seq, ts):
    C, SUB, H, DK, DV = GLA_CHUNK, GLA_SUB, GLA_HEADS, GLA_DK, GLA_DV
    tokens = p32.shape[0]
    blocks_per_seq = seq // ts
    tril = jnp.asarray(np.tril(np.ones((C, C), np.float32))).astype(BF16)
    expand = np.zeros((H * DK, H * DV), np.float32)
    for h in range(H):
        expand[h * DK:(h + 1) * DK, h * DV:(h + 1) * DV] = 1.0
    expand = jnp.asarray(expand).astype(BF16)
    row_map = lambda cb: (lambda b, i: (b * blocks_per_seq + i, cb))
    const2 = lambda b, i: (0, 0)
    out = pl.pallas_call(
        _gla_kernel,
        grid=(bsz, blocks_per_seq),
        in_specs=[
            pl.BlockSpec((ts, 2 * H * DK), row_map(C32_GQ // (2 * H * DK))),
            pl.BlockSpec((ts, H * DV), row_map(C16_GV // (H * DV))),
            pl.BlockSpec((ts, H * DV), row_map(C32_R // (H * DV))),
            pl.BlockSpec((ts, LANES), row_map(C32_SMALL // LANES)),
            pl.BlockSpec(w2p.shape, const2),
            pl.BlockSpec(b2.shape, const2),
            pl.BlockSpec(norm_g.shape, const2),
            pl.BlockSpec(tril.shape, const2),
            pl.BlockSpec(expand.shape, const2),
        ],
        out_specs=pl.BlockSpec((ts, H * DV), lambda b, i: (b * blocks_per_seq + i, 0)),
        out_shape=jax.ShapeDtypeStruct((tokens, H * DV), F32),
        scratch_shapes=[pltpu.VMEM((DV, H * DK), F32), pltpu.VMEM((SUB * SUB, H * DK), BF16),
                        pltpu.VMEM((ts, H * DK), F32)],
        compiler_params=pltpu.CompilerParams(
            dimension_semantics=("parallel", "arbitrary"), vmem_limit_bytes=VMEM_LIMIT),
        name="gla_branch",
    )(p32, p16, p32, p32, w2p, b2, norm_g, tril, expand)
    return out


def _bit_transpose32(words):
    a = list(words)
    shift, mask = 16, 0x0000FFFF
    while shift:
        m = jnp.int32(np.uint32(mask).astype(np.int32))
        for k in range(32):
            if k & shift == 0:
                t = (a[k] ^ lax.shift_right_logical(a[k + shift], jnp.int32(shift))) & m
                a[k] = a[k] ^ t
                a[k + shift] = a[k + shift] ^ lax.shift_left(t, jnp.int32(shift))
        shift >>= 1
        mask = (mask ^ (mask << shift)) & 0xFFFFFFFF
    return a


def _dsa_kernel(q_ref, iq_ref, iwt_ref, k_ref, vt_ref, ik_ref, bias_ref, ltri_ref,
                o_ref, key_scr, plane_scr, s_scr, acc_scr, *, topk):
    T, TK = DSA_QBLOCK, DSA_KTILE
    SUBS = TK // T
    H, HD = DSA_HEADS, DSA_HDIM
    qb = pl.program_id(1)
    kt_last = qb // SUBS
    n_kt = kt_last + 1
    key_minus_query = (lax.broadcasted_iota(jnp.int32, (TK, T), 0)
                       - lax.broadcasted_iota(jnp.int32, (TK, T), 1))

    def fold(x, op):
        return op(x.reshape(x.shape[0] // 8, 8, T), axis=0)

    def rows_to_one(x, op):
        return op(x, axis=0, keepdims=True)

    def key_rows(kt):
        return pl.ds(pl.multiple_of(kt * TK, TK), TK)

    def tile_loop(n, body, init):
        carry = lax.fori_loop(0, n // 2, lambda i, c: body(2 * i + 1, body(2 * i, c)), init)
        return lax.cond(n % 2 == 1, lambda c: body(n - 1, c), lambda c: c, carry)

    lo_mask = lax.broadcasted_iota(jnp.int32, (T, LANES), 1) < DSA_IDX_DIM
    iq = [iq_ref[:, 0:LANES], iq_ref[:, LANES:2 * LANES]]
    iq_h = [jnp.where(lo_mask, iq[0], 0), jnp.where(lo_mask, 0, iq[0]),
            jnp.where(lo_mask, iq[1], 0), jnp.where(lo_mask, 0, iq[1])]
    iq_all = jnp.concatenate(iq_h, axis=0)
    iw_h = [iwt_ref[h:h + 1, :] for h in range(DSA_IDX_HEADS)]

    GRP_ROWS = 32 * 8
    GRPS = TK // GRP_ROWS
    n_grp = plane_scr.shape[0]

    @pl.when(qb == 0)
    def _():
        plane_scr[...] = jnp.zeros_like(plane_scr)

    n_keys = key_scr.shape[0] * TK
    minus_one_minus_row = -1 - lax.broadcasted_iota(jnp.int32, (TK, T), 0)

    def score_tile(kt):
        ik2 = ik_ref[key_rows(kt), :]
        raw = _dot_nt(ik2, iq_all)
        score = jnp.zeros((TK, T), F32)
        for h in range(DSA_IDX_HEADS):
            score = score + iw_h[h] * jnp.maximum(raw[:, h * T:(h + 1) * T], 0.0)
        bits = lax.bitcast_convert_type(score, jnp.int32)
        zero_key = minus_one_minus_row - kt * TK
        negative_key = (bits ^ jnp.int32(0x7FFFFFFF)) - n_keys
        return jnp.where(score == 0.0, zero_key, jnp.where(bits >= 0, bits, negative_key))

    def store_planes(kt, key):
        unsigned_order = key ^ jnp.int32(INT_MIN)
        for g in range(GRPS):
            words = [unsigned_order[g * GRP_ROWS + 8 * j:g * GRP_ROWS + 8 * j + 8, :] for j in range(32)]
            planes = _bit_transpose32(words)
            for b in range(32):
                plane_scr[kt * GRPS + g, b] = planes[b]

    def score_body(kt, c):
        key = score_tile(kt)
        key_scr[kt] = key
        store_planes(kt, key)
        return c

    tile_loop(kt_last, score_body, 0)
    admissible = key_minus_query <= qb * T - kt_last * TK
    key = score_tile(kt_last)
    key_scr[kt_last] = jnp.where(admissible, key, jnp.int32(INT_MIN))
    store_planes(kt_last, key)
    alive_last = []
    for g in range(GRPS):
        word = jnp.zeros((8, T), jnp.int32)
        for j in range(32):
            adm = admissible[g * GRP_ROWS + 8 * j:g * GRP_ROWS + 8 * j + 8, :]
            word = word | jnp.where(adm, jnp.int32(np.int32(np.uint32(1 << (31 - j)))), 0)
        alive_last.append(word)

    alive = []
    for g in range(n_grp):
        kt = g // GRPS
        full = jnp.broadcast_to(jnp.where(kt < kt_last, jnp.int32(-1), jnp.int32(0)), (8, T))
        alive.append(jnp.where(kt == kt_last, alive_last[g % GRPS], full))

    def bits_body(i, carry):
        above, tau_u, alive = carry
        hi_planes = [plane_scr[g, 2 * i] for g in range(n_grp)]
        lo_planes = [plane_scr[g, 2 * i + 1] for g in range(n_grp)]
        n11 = n10 = n01 = jnp.zeros((8, T), jnp.int32)
        for g in range(n_grp):
            with_hi = alive[g] & hi_planes[g]
            both = with_hi & lo_planes[g]
            n11 = n11 + lax.population_count(both)
            n10 = n10 + lax.population_count(with_hi ^ both)
            n01 = n01 + lax.population_count((alive[g] & lo_planes[g]) ^ both)
        c11 = above + rows_to_one(n11, jnp.sum)
        c10 = c11 + rows_to_one(n10, jnp.sum)
        c01 = c10 + rows_to_one(n01, jnp.sum)
        is11, is10, is01 = c11 >= topk, c10 >= topk, c01 >= topk
        hi_bit = is10
        lo_bit = is11 | (is01 & ~is10)
        above = jnp.where(is11, above, jnp.where(is10, c11, jnp.where(is01, c10, c01)))
        flip_hi = jnp.where(hi_bit, jnp.int32(0), jnp.int32(-1))
        flip_lo = jnp.where(lo_bit, jnp.int32(0), jnp.int32(-1))
        alive = tuple(alive[g] & (hi_planes[g] ^ flip_hi) & (lo_planes[g] ^ flip_lo)
                      for g in range(n_grp))
        tau_u = (tau_u | jnp.where(hi_bit, lax.shift_left(jnp.int32(1), 31 - 2 * i), 0)
                 | jnp.where(lo_bit, lax.shift_left(jnp.int32(1), 30 - 2 * i), 0))
        return above, tau_u, alive

    zero_row = jnp.zeros((1, T), jnp.int32)
    above, tau_u, alive = lax.fori_loop(0, 16, bits_body, (zero_row, zero_row, tuple(alive)))
    n_tied = jnp.zeros((8, T), jnp.int32)
    for g in range(n_grp):
        n_tied = n_tied + lax.population_count(alive[g])
    n_ge = above + rows_to_one(n_tied, jnp.sum)
    tau = jnp.maximum(tau_u ^ jnp.int32(INT_MIN), jnp.int32(INT_MIN + 1))

    @pl.when(jnp.max(n_ge) > topk)
    def _():
        need = (topk - above).astype(F32)

        def tie_body(kt, seen):
            for j in range(SUBS):
                rows = slice(j * T, (j + 1) * T)
                key = key_scr[kt, rows, :]
                eq = key == tau
                eq_f = jnp.where(eq, 1.0, 0.0)
                pref = _dot(ltri_ref[...], eq_f.astype(BF16)) + seen
                key_scr[kt, rows, :] = jnp.where(eq & (pref > need), tau - 1, key)
                seen = seen + rows_to_one(eq_f, jnp.sum)
            return seen

        tile_loop(n_kt, tie_body, jnp.zeros((1, T), F32))

    q_h = [q_ref[:, h * HD:(h + 1) * HD] for h in range(H)]

    def logits_body(kt, m_run):
        sel = key_scr[kt] >= tau
        kinds = [jnp.clip(qb - (kt * SUBS + j), 0, 2) for j in range(SUBS)]
        new_m = []
        for h in range(H):
            bias = jnp.concatenate([bias_ref[kinds[j], h] for j in range(SUBS)], axis=0)
            s = _dot_nt(k_ref[key_rows(kt), h * HD:(h + 1) * HD], q_h[h]) + bias
            s = jnp.where(sel, s, NEG_BIG)
            s_scr[h, kt] = s
            new_m.append(jnp.maximum(m_run[h], fold(s, jnp.max)))
        return tuple(new_m)

    m_run = tile_loop(n_kt, logits_body, tuple(jnp.full((8, T), NEG_BIG, F32) for _ in range(H)))
    m_h = [rows_to_one(m, jnp.max) for m in m_run]

    for h in range(H):
        acc_scr[h] = jnp.zeros((HD, T), F32)

    def pv_body(kt, l_run):
        new_l = []
        for h in range(H):
            p = jnp.exp(s_scr[h, kt] - m_h[h])
            new_l.append(l_run[h] + fold(p, jnp.sum))
            acc_scr[h] = acc_scr[h] + _dot(vt_ref[kt, h * HD:(h + 1) * HD, :], p.astype(BF16))
        return tuple(new_l)

    l_run = tile_loop(n_kt, pv_body, tuple(jnp.zeros((8, T), F32) for _ in range(H)))

    for h in range(H):
        out_t = acc_scr[h] / rows_to_one(l_run[h], jnp.sum)
        o_ref[:, h * HD:(h + 1) * HD] = out_t.T.astype(o_ref.dtype)


def _dsa(p16, v_t, iw_t, bias_tiles, bsz, seq):
    T = DSA_QBLOCK
    tokens = p16.shape[0]
    nqb = seq // T
    topk = min(DSA_TOPK_MAX, seq // 4)
    TK = min(DSA_KTILE, seq)
    assert TK == DSA_KTILE and seq % TK == 0
    nkt = seq // TK
    ltri = jnp.asarray(np.tril(np.ones((T, T), np.float32))).astype(BF16)
    w = D_BRANCH
    v_t = v_t.reshape(bsz, nkt, w, TK)
    qmap = lambda cb: (lambda b, i: (b * nqb + i, cb))
    return pl.pallas_call(
        functools.partial(_dsa_kernel, topk=topk),
        grid=(bsz, nqb),
        in_specs=[
            pl.BlockSpec((T, w), qmap(C16_Q // w)),
            pl.BlockSpec((T, 2 * LANES), qmap(C16_IQ // (2 * LANES))),
            pl.BlockSpec((8, T), lambda b, i: (0, b * nqb + i)),
            pl.BlockSpec((seq, w), lambda b, i: (b, C16_K // w)),
            pl.BlockSpec((None, nkt, w, TK), lambda b, i: (b, 0, 0, 0)),
            pl.BlockSpec((seq, LANES), lambda b, i: (b, C16_IK // LANES)),
            pl.BlockSpec(bias_tiles.shape, lambda b, i: (0, 0, 0, 0)),
            pl.BlockSpec(ltri.shape, lambda b, i: (0, 0)),
        ],
        out_specs=pl.BlockSpec((T, w), lambda b, i: (b * nqb + i, 0)),
        out_shape=jax.ShapeDtypeStruct((tokens, w), BF16),
        scratch_shapes=[
            pltpu.VMEM((nkt, TK, T), jnp.int32),
            pltpu.VMEM((nkt * TK // 256, 32, 8, T), jnp.int32),
            pltpu.VMEM((DSA_HEADS, nkt, TK, T), F32),
            pltpu.VMEM((DSA_HEADS, DSA_HDIM, T), F32),
        ],
        compiler_params=pltpu.CompilerParams(
            dimension_semantics=("parallel", "arbitrary"), vmem_limit_bytes=VMEM_LIMIT),
        name="dsa_branch",
    )(p16, p16, iw_t, p16, v_t, p16, bias_tiles, ltri)


def _merge_kernel(u_ref, v_ref, ws_ref, bs_ref, yb_ref, yc_ref, ga_ref, gb_ref, gc_ref, x_ref, mod_ref,
                  wbr_ref, wout_ref, gn2_ref, wr_hi_ref, wr_lo_ref, br_ref, x1_ref, h2_ref, comb_ref):
    tm = u_ref.shape[0]
    ya_rows = []
    for c in range(tm // GMLP_CHUNK):
        rows = slice(c * GMLP_CHUNK, (c + 1) * GMLP_CHUNK)
        groups = []
        for g in range(GMLP_GROUPS):
            cols = slice(g * LANES, (g + 1) * LANES)
            mixed = _dot(ws_ref[g], v_ref[rows, cols].astype(BF16)) + bs_ref[:, cols]
            groups.append((u_ref[rows, cols] * mixed).astype(BF16))
        ya_rows.append(jnp.concatenate(groups, axis=1))
    ya = jnp.concatenate(ya_rows, axis=0)

    merged = ga_ref[...] * _dot(ya, wbr_ref[0])
    merged = merged + gb_ref[...] * _dot(yb_ref[...].astype(BF16), wbr_ref[1])
    merged = merged + gc_ref[...] * _dot(yc_ref[...], wbr_ref[2])
    mix = _dot(merged.astype(BF16), wout_ref[...])
    x1 = x_ref[...] + mod_ref[0, 2:3, :] * mix
    x1_ref[...] = x1
    y = x1 * lax.rsqrt(jnp.mean(x1 * x1, axis=-1, keepdims=True) + EPS) * gn2_ref[...]
    h2 = y * (1.0 + mod_ref[0, 4:5, :]) + mod_ref[0, 3:4, :]
    h2_ref[...] = h2.astype(BF16)
    comb, group_onehot = _route(_dot3(h2, wr_hi_ref[...], wr_lo_ref[...]) + br_ref[...])
    comb_ref[3] = group_onehot.astype(BF16)
    hi = comb.astype(BF16)
    rest = comb - hi.astype(F32)
    mid = rest.astype(BF16)
    comb_ref[0] = hi
    comb_ref[1] = mid
    comb_ref[2] = (rest - mid.astype(F32)).astype(BF16)


def _merge(p32, w_tril, gmlp_bias, yb, yc, p16, x2d, mod_l, wbr, wout, layer, gn2, wr_hi, wr_lo, br,
           seq, tm):
    tokens, d = x2d.shape
    tiles_per_seq = seq // tm
    row = lambda i: (i, 0)
    gate = lambda k: (lambda i: (i, C16_GATES // d + k))
    c2 = lambda i: (0, 0)
    of_layer = lambda a: pl.BlockSpec((None,) + a.shape[1:], lambda i: (layer,) + (0,) * (a.ndim - 1))
    return pl.pallas_call(
        _merge_kernel,
        grid=(tokens // tm,),
        in_specs=[
            pl.BlockSpec((tm, D_BRANCH), lambda i: (i, C32_U // D_BRANCH)),
            pl.BlockSpec((tm, D_BRANCH), lambda i: (i, C32_V // D_BRANCH)),
            of_layer(w_tril), of_layer(gmlp_bias),
            pl.BlockSpec((tm, D_BRANCH), row), pl.BlockSpec((tm, D_BRANCH), row),
            pl.BlockSpec((tm, d), gate(0)), pl.BlockSpec((tm, d), gate(1)), pl.BlockSpec((tm, d), gate(2)),
            pl.BlockSpec((tm, d), row),
            pl.BlockSpec((1, 6, d), lambda i: (i // tiles_per_seq, 0, 0)),
            of_layer(wbr), of_layer(wout),
            pl.BlockSpec(gn2.shape, c2),
            of_layer(wr_hi), of_layer(wr_lo), of_layer(br),
        ],
        out_specs=[pl.BlockSpec((tm, d), row), pl.BlockSpec((tm, d), row),
                   pl.BlockSpec((4, tm, LANES), lambda i: (0, i, 0))],
        out_shape=[jax.ShapeDtypeStruct((tokens, d), F32), jax.ShapeDtypeStruct((tokens, d), BF16),
                   jax.ShapeDtypeStruct((4, tokens, LANES), BF16)],
        compiler_params=pltpu.CompilerParams(
            dimension_semantics=("parallel",), vmem_limit_bytes=VMEM_LIMIT),
        name="gmlp_merge_norm_router",
    )(p32, p32, w_tril, gmlp_bias, yb, yc, p16, p16, p16, x2d, mod_l, wbr, wout, gn2, wr_hi, wr_lo, br)


def _route(lg):
    lane = lax.broadcasted_iota(jnp.int32, lg.shape, 1)
    big = jnp.int32(10 ** 6)
    is_grp = lane < N_GROUPS
    gl = jnp.where(is_grp, lg, -jnp.inf)
    gmax = jnp.max(gl, axis=1, keepdims=True)
    gsum = jnp.sum(jnp.where(is_grp, jnp.exp(lg - gmax), 0.0), axis=1, keepdims=True)
    p_g = 1.0 / gsum
    g_idx = jnp.min(jnp.where(gl == gmax, lane, big), axis=1, keepdims=True)
    first = N_GROUPS + g_idx * EXPERTS_PER_GROUP
    in_grp = (lane >= first) & (lane < first + EXPERTS_PER_GROUP)
    e1 = jnp.where(in_grp, lg, -jnp.inf)
    v1 = jnp.max(e1, axis=1, keepdims=True)
    i1 = jnp.min(jnp.where(e1 == v1, lane, big), axis=1, keepdims=True)
    e2 = jnp.where(in_grp & (lane != i1), lg, -jnp.inf)
    v2 = jnp.max(e2, axis=1, keepdims=True)
    i2 = jnp.min(jnp.where(e2 == v2, lane, big), axis=1, keepdims=True)
    t = jnp.exp(v2 - v1)
    w1 = p_g * (1.0 / (1.0 + t))
    w2 = p_g * (t / (1.0 + t))
    comb = jnp.where(lane == i1, w1, jnp.where(lane == i2, w2, 0.0))
    return comb, jnp.where(lane == g_idx, 1.0, 0.0)


def _moe_kernel(h_ref, comb_ref, x1_ref, mod_ref, wg_ref, wu_ref, wd_ref, tril_ref, o_ref,
                hp_scr, cwp_scr, yp_scr, pt_scr, start_ref, nchunk_ref):
    grp = pl.program_id(1)
    tm = h_ref.shape[0]
    n_sorted = hp_scr.shape[0]
    lane = lax.broadcasted_iota(jnp.int32, (1, LANES), 1)

    @pl.when(grp == 0)
    def _():
        onehot = comb_ref[3]
        incl = _dot(tril_ref[...], onehot)
        counts = incl[tm - 1:tm, :]
        start = jnp.int32(0)
        start_row = jnp.zeros((1, LANES), F32)
        for g in range(N_GROUPS):
            cnt = jnp.sum(jnp.where(lane == g, counts, 0.0)).astype(jnp.int32)
            start_ref[g] = start
            nchunk_ref[g] = lax.shift_right_logical(cnt + (MOE_CHUNK - 1), MOE_CHUNK.bit_length() - 1)
            start_row = jnp.where(lane == g, start.astype(F32), start_row)
            align = MOE_ALIGN.bit_length() - 1
            start = start + lax.shift_left(lax.shift_right_logical(cnt + (MOE_ALIGN - 1), align), align)
        dest = jnp.sum(onehot.astype(F32) * (incl - 1.0 + start_row), axis=1, keepdims=True)
        dest_col = dest.astype(jnp.int32)
        dest_row = jnp.broadcast_to(dest, (tm, LANES)).T[0:1, :].astype(jnp.int32)
        back = lax.broadcasted_iota(jnp.int32, (tm, n_sorted), 1) == dest_col
        pt_scr[...] = jnp.where(back, 1.0, 0.0).astype(BF16)
        fwd = lax.broadcasted_iota(jnp.int32, (n_sorted, tm), 0) == dest_row
        fwd = jnp.where(fwd, 1.0, 0.0).astype(BF16)
        hp_scr[...] = _dot(fwd, h_ref[...]).astype(BF16)
        for part in range(3):
            cwp_scr[part] = _dot(fwd, comb_ref[part]).astype(BF16)
        yp_scr[...] = jnp.zeros_like(yp_scr)

    row_id = lax.broadcasted_iota(jnp.int32, (LANES, LANES), 0)
    start = start_ref[grp]

    def chunk(c, carry):
        rows = pl.ds(pl.multiple_of(start + c * MOE_CHUNK, MOE_ALIGN), MOE_CHUNK)
        h = hp_scr[rows, :]
        acts = []
        for j in range(EXPERTS_PER_GROUP):
            lane_of_expert = N_GROUPS + grp * EXPERTS_PER_GROUP + j
            pick = jnp.where(row_id == lane_of_expert, 1.0, 0.0).astype(BF16)
            cw = (_dot(cwp_scr[0, rows, :], pick)
                  + (_dot(cwp_scr[1, rows, :], pick) + _dot(cwp_scr[2, rows, :], pick)))
            act = (jax.nn.silu(_dot(h, wg_ref[j])) * _dot(h, wu_ref[j])
                   * jnp.concatenate([cw, cw], axis=1))
            acts.append(act.astype(BF16))
        yp_scr[rows, :] = _dot(jnp.concatenate(acts, axis=1), wd_ref[...]).astype(BF16)
        return carry

    lax.fori_loop(0, nchunk_ref[grp], chunk, 0)

    @pl.when(grp == pl.num_programs(1) - 1)
    def _():
        o_ref[...] = x1_ref[...] + mod_ref[0, 5:6, :] * _dot(pt_scr[...], yp_scr[...])


def _moe(h2, comb, x1, mod_l, wg, wu, wd, layer, seq, tm):
    tokens, d = x1.shape
    tiles_per_seq = seq // tm
    n_sorted = tm + N_GROUPS * MOE_ALIGN + MOE_CHUNK
    n_sorted = LANES * pl.cdiv(n_sorted, LANES)
    tril = jnp.asarray(np.tril(np.ones((tm, tm), np.float32))).astype(BF16)
    row = lambda i, g: (i, 0)
    return pl.pallas_call(
        _moe_kernel,
        grid=(tokens // tm, N_GROUPS),
        in_specs=[
            pl.BlockSpec((tm, d), row), pl.BlockSpec((4, tm, LANES), lambda i, g: (0, i, 0)),
            pl.BlockSpec((tm, d), row),
            pl.BlockSpec((1, 6, d), lambda i, g: (i // tiles_per_seq, 0, 0)),
            pl.BlockSpec((None, None) + wg.shape[2:], lambda i, g: (layer, g, 0, 0, 0)),
            pl.BlockSpec((None, None) + wu.shape[2:], lambda i, g: (layer, g, 0, 0, 0)),
            pl.BlockSpec((None, None) + wd.shape[2:], lambda i, g: (layer, g, 0, 0)),
            pl.BlockSpec(tril.shape, lambda i, g: (0, 0), pipeline_mode=pl.Buffered(1)),
        ],
        out_specs=pl.BlockSpec((tm, d), row),
        out_shape=jax.ShapeDtypeStruct((tokens, d), F32),
        scratch_shapes=[pltpu.VMEM((n_sorted, d), BF16), pltpu.VMEM((3, n_sorted, LANES), BF16),
                        pltpu.VMEM((n_sorted, d), BF16), pltpu.VMEM((tm, n_sorted), BF16),
                        pltpu.SMEM((N_GROUPS,), jnp.int32), pltpu.SMEM((N_GROUPS,), jnp.int32)],
        compiler_params=pltpu.CompilerParams(
            dimension_semantics=("parallel", "arbitrary"), vmem_limit_bytes=VMEM_LIMIT),
        name="hier_moe",
    )(h2, comb, x1, mod_l, wg, wu, wd, tril)


def _row_tile(seq, want):
    t = min(want, seq)
    assert seq % t == 0
    return t


def kernel(x, c, w_mod, b_mod, g_norm1, g_norm2, w_in, gmlp_ln_g, gmlp_ln_b, gmlp_w_s, gmlp_b_s,
           gla_w_gate2, gla_b_gate, gla_norm_g, dsa_qnorm_g, dsa_knorm_g, rel_bias, w_branch,
           b_branch_gate, w_out, w_group, b_group, w_router, b_router, w_exp_gate, w_exp_up,
           w_exp_down):
    bsz, seq, d = x.shape
    depth = w_mod.shape[0]
    assert d == D_MODEL and seq % DSA_QBLOCK == 0 and seq % GLA_CHUNK == 0
    tokens = bsz * seq
    tm = _row_tile(seq, 512)

    mods = _modulation(c, w_mod, b_mod)
    bias_tiles = _bias_tiles(rel_bias)
    w32, w16 = _prep_w_in(w_in)

    causal = np.tril(np.ones((GMLP_CHUNK, GMLP_CHUNK), bool))
    w_tril = jnp.where(causal[None, None], gmlp_w_s, 0.0).astype(BF16)
    gmlp_bias = jnp.repeat(jnp.swapaxes(gmlp_b_s, 1, 2), LANES, axis=-1)
    w2p = jnp.zeros((depth, LANES, GLA_HEADS * GLA_DK), F32).at[:, :GLA_GATE_RANK].set(gla_w_gate2)
    w2p = w2p.astype(BF16)
    wr = jnp.zeros((depth, d, LANES), F32)
    wr = wr.at[:, :, :N_GROUPS].set(w_group).at[:, :, N_GROUPS:N_GROUPS + N_EXPERTS].set(w_router)
    wr_hi = wr.astype(BF16)
    wr_lo = (wr - wr_hi.astype(F32)).astype(BF16)
    br = jnp.zeros((depth, 1, LANES), F32)
    br = br.at[:, 0, :N_GROUPS].set(b_group).at[:, 0, N_GROUPS:N_GROUPS + N_EXPERTS].set(b_router)
    wbr = w_branch.astype(BF16)
    wout = w_out.astype(BF16)
    grouped = (depth, N_GROUPS, EXPERTS_PER_GROUP, d, D_EXPERT)
    wg, wu = w_exp_gate.astype(BF16).reshape(grouped), w_exp_up.astype(BF16).reshape(grouped)
    wd = w_exp_down.astype(BF16).reshape(depth, N_GROUPS, EXPERTS_PER_GROUP * D_EXPERT, d)
    tm_moe = _row_tile(seq, 1024)

    x2d = x.reshape(tokens, d)
    for l in range(depth):
        mod_l = mods[l]
        aux32 = [gmlp_ln_g[l][None], gmlp_ln_b[l][None]]
        aux16 = [dsa_qnorm_g[l][None], dsa_knorm_g[l][None], b_branch_gate[l].reshape(1, -1)]
        p32, p16, v_t, iw_t = _norm_proj(x2d, mod_l, g_norm1[l][None], w32, w16, l, aux32, aux16,
                                         seq, tm)
        yb = _gla(p32, p16, w2p[l], gla_b_gate[l][None], gla_norm_g[l][None], bsz, seq, tm)
        yc = _dsa(p16, v_t, iw_t, bias_tiles, bsz, seq)
        x1, h2, comb = _merge(p32, w_tril, gmlp_bias, yb, yc, p16, x2d, mod_l, wbr, wout, l,
                              g_norm2[l][None], wr_hi, wr_lo, br, seq, tm)
        x2d = _moe(h2, comb, x1, mod_l, wg, wu, wd, l, seq, tm_moe)
    return x2d.reshape(bsz, seq, d)
```

```python
import functools
import math

import numpy as np
import jax
import jax.numpy as jnp
from jax import lax
from jax.experimental import pallas as pl
from jax.experimental.pallas import tpu as pltpu

F32 = jnp.float32
BF16 = jnp.bfloat16

D_MODEL = 1024
D_BRANCH = 512
EPS = 1e-6
GMLP_CHUNK = 128
GMLP_GROUPS = 4
GLA_HEADS = 4
GLA_DK = 64
GLA_DV = 128
GLA_GATE_RANK = 16
GLA_GATE_TAU = 16.0
GLA_CHUNK = 128
GLA_SUB = 16
GLA_MILD_DECAY = -60.0
DSA_HEADS = 4
DSA_HDIM = 128
DSA_IDX_HEADS = 4
DSA_IDX_DIM = 64
DSA_QBLOCK = 128
DSA_KTILE = 512
DSA_TOPK_MAX = 256
N_BUCKETS = 32
MAX_DISTANCE = 128
N_GROUPS = 4
EXPERTS_PER_GROUP = 4
N_EXPERTS = 16
D_EXPERT = 256
MOE_CHUNK = 128
MOE_ALIGN = 16

LANES = 128
COL_TILE = 512
VMEM_LIMIT = 56 * 1024 * 1024
INT_MIN = -(2 ** 31)
NEG_BIG = -1e30

C32_U, C32_V, C32_GQ, C32_GK, C32_R, C32_SMALL = 0, 512, 1024, 1280, 1536, 2048
N32 = C32_SMALL + LANES
C16_GATES, C16_Q, C16_K, C16_GV, C16_IQ, C16_IK = 0, 3072, 3584, 4096, 4608, 4864
N16 = C16_IK + LANES


def _dot(a, b):
    return jnp.dot(a, b, preferred_element_type=F32)


def _dot_nt(a, b):
    return lax.dot_general(a, b, (((1,), (1,)), ((), ())), preferred_element_type=F32)


def _dot_tn(a, b):
    return lax.dot_general(a, b, (((0,), (0,)), ((), ())), preferred_element_type=F32)


def _split2(a):
    hi = a.astype(BF16)
    lo = (a - hi.astype(F32)).astype(BF16)
    return hi, lo


def _dot3(a, w_hi, w_lo):
    a_hi, a_lo = _split2(a)
    return _dot(a_hi, w_hi) + (_dot(a_lo, w_hi) + _dot(a_hi, w_lo))


def _head_rms(y, g, scale):
    outs = []
    for h in range(y.shape[1] // LANES):
        yh = y[:, h * LANES:(h + 1) * LANES]
        ms = jnp.mean(yh * yh, axis=-1, keepdims=True)
        o = yh * lax.rsqrt(ms + EPS) * g
        if scale != 1.0:
            o = o * scale
        outs.append(o)
    return jnp.concatenate(outs, axis=1)


def _mod_kernel(c_ref, w_ref, b_ref, o_ref):
    a = jax.nn.silu(c_ref[...])
    w_hi, w_lo = _split2(w_ref[...])
    o_ref[...] = _dot3(a, w_hi, w_lo) + b_ref[...]


def _modulation(c, w_mod, b_mod):
    L, d, n = w_mod.shape
    bsz = c.shape[0]
    rows = 8 * pl.cdiv(bsz, 8)
    c_pad = jnp.zeros((rows, d), F32).at[:bsz].set(c)
    tn = 1536
    out = pl.pallas_call(
        _mod_kernel,
        grid=(L, n // tn),
        in_specs=[
            pl.BlockSpec((rows, d), lambda l, j: (0, 0)),
            pl.BlockSpec((None, d, tn), lambda l, j: (l, 0, j)),
            pl.BlockSpec((None, 1, tn), lambda l, j: (l, 0, j)),
        ],
        out_specs=pl.BlockSpec((None, rows, tn), lambda l, j: (l, 0, j)),
        out_shape=jax.ShapeDtypeStruct((L, rows, n), F32),
        compiler_params=pltpu.CompilerParams(
            dimension_semantics=("arbitrary", "arbitrary"), vmem_limit_bytes=VMEM_LIMIT),
        name="adaln_modulation",
    )(c_pad, w_mod, b_mod.reshape(L, 1, n))
    return out[:, :bsz].reshape(L, bsz, 6, d)


def _t5_bucket_table():
    n = np.arange(2 * DSA_QBLOCK)
    max_exact = N_BUCKETS // 2
    large = max_exact + (
        np.log(np.maximum(n, max_exact).astype(np.float32) / max_exact)
        / math.log(MAX_DISTANCE / max_exact) * (N_BUCKETS - max_exact)).astype(np.int32)
    large = np.minimum(large, N_BUCKETS - 1)
    return np.where(n < max_exact, n, large).astype(np.int32)


def _bias_kernel(rb_ref, bucket_ref, o_ref):
    for t in range(bucket_ref.shape[0]):
        bucket = bucket_ref[t]
        for h in range(DSA_HEADS):
            acc = jnp.zeros(bucket.shape, F32)
            for b in range(N_BUCKETS):
                acc = jnp.where(bucket == b, rb_ref[b, h], acc)
            o_ref[t, h] = acc


def _bias_tiles(rel_bias):
    table = _t5_bucket_table()
    assert (table[MAX_DISTANCE:] == N_BUCKETS - 1).all()
    t = np.arange(DSA_QBLOCK)[None, :]
    s = np.arange(DSA_QBLOCK)[:, None]
    diag = table[np.maximum(t - s, 0)]
    near = table[DSA_QBLOCK + t - s]
    far = np.full_like(diag, N_BUCKETS - 1)
    buckets = jnp.asarray(np.stack([diag, near, far]).astype(np.int32))
    return pl.pallas_call(
        _bias_kernel,
        in_specs=[pl.BlockSpec(memory_space=pltpu.SMEM), pl.BlockSpec(memory_space=pltpu.VMEM)],
        out_specs=pl.BlockSpec(memory_space=pltpu.VMEM),
        out_shape=jax.ShapeDtypeStruct((3, DSA_HEADS, DSA_QBLOCK, DSA_QBLOCK), F32),
        name="t5_bias_tiles",
    )(rel_bias, buckets)


def _proj_kernel(x_ref, mod_ref, gn_ref, w32_ref, w16_ref, *rest):
    aux32, aux16 = rest[:2], rest[2:5]
    o32_ref, o16_ref, vt_ref, iwt_ref = rest[5:9]
    x = x_ref[...]
    y = x * lax.rsqrt(jnp.mean(x * x, axis=-1, keepdims=True) + EPS) * gn_ref[...]
    h = (y * (1.0 + mod_ref[0, 1:2, :]) + mod_ref[0, 0:1, :]).astype(BF16)
    for w_ref, o_ref, epilogues, aux in ((w32_ref, o32_ref, EPILOGUES_32, aux32),
                                         (w16_ref, o16_ref, EPILOGUES_16, aux16)):
        wcol = ocol = 0
        for width, epi in epilogues:
            y = epi(_dot(h, w_ref[:, wcol:wcol + width]), aux)
            wcol += width
            if epi is _epi_value_t:
                vt_ref[0] = y.T.astype(vt_ref.dtype)
                continue
            if epi is _epi_small:
                iwt_ref[...] = y.T[GLA_GATE_RANK:GLA_GATE_RANK + 8, :]
            o_ref[:, ocol:ocol + width] = y.astype(o_ref.dtype)
            ocol += width
        assert wcol == w_ref.shape[1] and ocol == o_ref.shape[1]


def _norm_proj(x2d, mod_l, gn, w32, w16, layer, aux32, aux16, seq, tm):
    tokens, d = x2d.shape
    tiles_per_seq = seq // tm
    assert tm == DSA_KTILE
    const = lambda a: pl.BlockSpec(a.shape, lambda i: (0, 0))
    resident = lambda a: pl.BlockSpec((None,) + a.shape[1:], lambda i: (layer, 0, 0),
                                      pipeline_mode=pl.Buffered(1))
    return pl.pallas_call(
        _proj_kernel,
        grid=(tokens // tm,),
        in_specs=[
            pl.BlockSpec((tm, d), lambda i: (i, 0)),
            pl.BlockSpec((1, 6, d), lambda i: (i // tiles_per_seq, 0, 0)),
            const(gn), resident(w32), resident(w16),
        ] + [const(a) for a in aux32 + aux16],
        out_specs=[pl.BlockSpec((tm, N32), lambda i: (i, 0)), pl.BlockSpec((tm, N16), lambda i: (i, 0)),
                   pl.BlockSpec((1, D_BRANCH, tm), lambda i: (i, 0, 0)),
                   pl.BlockSpec((8, tm), lambda i: (0, i))],
        out_shape=[jax.ShapeDtypeStruct((tokens, N32), F32), jax.ShapeDtypeStruct((tokens, N16), BF16),
                   jax.ShapeDtypeStruct((tokens // tm, D_BRANCH, tm), BF16),
                   jax.ShapeDtypeStruct((8, tokens), F32)],
        compiler_params=pltpu.CompilerParams(
            dimension_semantics=("parallel",), vmem_limit_bytes=VMEM_LIMIT),
        name="norm_proj",
    )(x2d, mod_l, gn, w32, w16, *aux32, *aux16)


def _epi_raw(y, aux):
    return y


def _epi_gelu(y, aux):
    return jax.nn.gelu(y)


def _epi_gelu_ln(y, aux):
    v = jax.nn.gelu(y)
    mu = jnp.mean(v, axis=-1, keepdims=True)
    var = jnp.mean(jnp.square(v - mu), axis=-1, keepdims=True)
    return (v - mu) * lax.rsqrt(var + EPS) * aux[0][...] + aux[1][...]


def _epi_silu(y, aux):
    return jax.nn.silu(y)


def _epi_gate(k):
    def epi(y, aux):
        return jax.nn.sigmoid(y + aux[2][:, k * COL_TILE:(k + 1) * COL_TILE])
    return epi


def _epi_qnorm(y, aux):
    return _head_rms(y, aux[0][...], DSA_HDIM ** -0.5)


def _epi_knorm(y, aux):
    return _head_rms(y, aux[1][...], 1.0)


def _epi_value_t(y, aux):
    return y


def _epi_small(y, aux):
    return y


EPILOGUES_32 = ([(COL_TILE, e) for e in (_epi_gelu, _epi_gelu_ln, _epi_raw, _epi_silu)]
                + [(LANES, _epi_small)])
EPILOGUES_16 = ([(COL_TILE, _epi_gate(k)) for k in range(6)]
                + [(COL_TILE, e) for e in (_epi_qnorm, _epi_knorm, _epi_value_t, _epi_raw)]
                + [(2 * LANES, _epi_raw), (LANES, _epi_raw)])


def _prep_w_in(w_in):
    sizes = (512, 512, 256, 256, 512, 512, 16, 512, 512, 512, 256, 64, 4, 3072)
    offs = np.concatenate([[0], np.cumsum(sizes)])
    seg = lambda k: w_in[:, :, offs[k]:offs[k + 1]]
    (a_u, a_v, g_q, g_k, g_v, g_r, g_a, d_q, d_k, d_v, d_iq, d_ik, d_iw, gates) = [seg(k) for k in range(14)]
    L, d, _ = w_in.shape
    zeros = lambda n: jnp.zeros((L, d, n), w_in.dtype)
    w32 = jnp.concatenate([a_u, a_v, g_q, g_k, g_r, g_a, d_iw,
                           zeros(N32 - C32_SMALL - 20)], axis=-1).astype(BF16)
    w16 = jnp.concatenate([gates, d_q, d_k, d_v, g_v, d_iq, d_ik, d_ik], axis=-1).astype(BF16)
    return w32, w16


def _gla_kernel(qk_ref, v_ref, r_ref, ga_ref, w2_ref, b2_ref, ng_ref, tril_ref, exp_ref,
                o_ref, st_ref, d_scr, g_scr):
    C, SUB, H, DK, DV = GLA_CHUNK, GLA_SUB, GLA_HEADS, GLA_DK, GLA_DV
    HK = H * DK
    n_batch = qk_ref.shape[0]
    n_chunks = qk_ref.shape[1] // C

    @pl.when(pl.program_id(0) == 0)
    def _():
        st_ref[...] = jnp.zeros_like(st_ref)

    lane = lax.broadcasted_iota(jnp.int32, (1, HK), 1)
    head_mask = [(lane >= h * DK) & (lane < (h + 1) * DK) for h in range(H)]
    row = lax.broadcasted_iota(jnp.int32, (C, C), 0)
    col = lax.broadcasted_iota(jnp.int32, (C, C), 1)
    sub_shift = SUB.bit_length() - 1
    blk_lower = (row >> sub_shift) > (col >> sub_shift)
    sub_t = lax.broadcasted_iota(jnp.int32, (SUB, 1), 0)

    mild = None
    for bi in range(n_batch):
        graw = _dot(ga_ref[bi].astype(BF16), w2_ref[...]) + b2_ref[...]
        g_all = jax.nn.log_sigmoid(graw) / GLA_GATE_TAU
        g_scr[bi] = g_all
        chunk_decay = jnp.sum(g_all.reshape(n_chunks, C, HK), axis=1)
        ok = jnp.min(chunk_decay) >= GLA_MILD_DECAY
        mild = ok if mild is None else jnp.logical_and(mild, ok)

    def cumulative_decay(bi, rows):
        g = g_scr[bi, rows, :]
        g_hi = g.astype(BF16)
        g_r1 = g - g_hi.astype(F32)
        g_mid = g_r1.astype(BF16)
        g_lo = (g_r1 - g_mid.astype(F32)).astype(BF16)
        tril = tril_ref[...]
        return _dot(tril, g_hi) + (_dot(tril, g_mid) + _dot(tril, g_lo))

    def finish(bi, rows, o, v, k_dec, b_last, st):
        upd = _dot_tn(v, k_dec)
        new_st = st * jnp.exp(b_last)
        for h in range(H):
            new_st = new_st + jnp.where(head_mask[h], upd[h * DV:(h + 1) * DV, :], 0.0)
        st_ref[bi] = new_st
        o_ref[bi, rows, :] = _head_rms(o, ng_ref[...], 1.0) * r_ref[bi, rows, :]

    def chunk_mild(bi, ci):
        rows = slice(ci * C, (ci + 1) * C)
        q = qk_ref[bi, rows, 0:HK] * (DK ** -0.5)
        k = qk_ref[bi, rows, HK:2 * HK]
        v = v_ref[bi, rows, :]
        b = cumulative_decay(bi, rows)
        b_last = b[C - 1:C, :]
        st = st_ref[bi]
        q_in = q * jnp.exp(b)
        k_out = (k * jnp.exp(-b)).astype(BF16)
        k_dec = (k * jnp.exp(b_last - b)).astype(BF16)
        outs = []
        for h in range(H):
            qm = jnp.where(head_mask[h], q_in, 0.0).astype(BF16)
            a_h = jnp.where(row >= col, _dot_nt(qm, k_out), 0.0).astype(BF16)
            outs.append(_dot_nt(qm, st.astype(BF16)) + _dot(a_h, v[:, h * DV:(h + 1) * DV]))
        finish(bi, rows, jnp.concatenate(outs, axis=1), v, k_dec, b_last, st)

    def chunk(bi, ci):
        r0 = pl.multiple_of(ci * C, C)
        rows = pl.ds(r0, C)
        q = qk_ref[bi, rows, 0:HK] * (DK ** -0.5)
        k = qk_ref[bi, rows, HK:2 * HK]
        v = v_ref[bi, rows, :]
        b = cumulative_decay(bi, rows)
        b_last = b[C - 1:C, :]
        st = st_ref[bi]

        q_in = q * jnp.exp(b)
        k_dec = (k * jnp.exp(b_last - b)).astype(BF16)

        a_off = [jnp.zeros((C, C), F32) for _ in range(H)]
        for j in range(C // SUB - 1):
            bj = b[(j + 1) * SUB - 1:(j + 1) * SUB, :]
            qj = q * jnp.exp(jnp.minimum(b - bj, 0.0))
            in_blk = (lax.broadcasted_iota(jnp.int32, (C, 1), 0) >> sub_shift) == j
            kj = jnp.where(in_blk, k * jnp.exp(jnp.minimum(bj - b, 0.0)), 0.0).astype(BF16)
            for h in range(H):
                a_off[h] = a_off[h] + _dot_nt(jnp.where(head_mask[h], qj, 0.0).astype(BF16), kj)

        for i in range(C // SUB):
            rs = slice(i * SUB, (i + 1) * SUB)
            q_sub, b_sub = q[rs, :], b[rs, :]
            for s in range(SUB):
                ks = k[i * SUB + s:i * SUB + s + 1, :]
                bs = b[i * SUB + s:i * SUB + s + 1, :]
                dterm = q_sub * ks * jnp.exp(jnp.minimum(b_sub - bs, 0.0))
                dterm = jnp.where(sub_t >= s, dterm, 0.0)
                d_scr[s * SUB:(s + 1) * SUB, :] = dterm.astype(BF16)
            gsum = _dot(d_scr[...], exp_ref[...])
            od = jnp.zeros((SUB, H * DV), F32)
            for s in range(SUB):
                vs = v[i * SUB + s:i * SUB + s + 1, :].astype(F32)
                od = od + gsum[s * SUB:(s + 1) * SUB, :] * vs
            o_ref[bi, pl.ds(r0 + i * SUB, SUB), :] = od

        outs = []
        for h in range(H):
            vh = v[:, h * DV:(h + 1) * DV]
            o_inter = _dot_nt(jnp.where(head_mask[h], q_in, 0.0).astype(BF16), st.astype(BF16))
            a_h = jnp.where(blk_lower, a_off[h], 0.0).astype(BF16)
            outs.append(o_inter + _dot(a_h, vh))
        o = o_ref[bi, rows, :] + jnp.concatenate(outs, axis=1)
        finish(bi, rows, o, v, k_dec, b_last, st)

    @pl.when(mild)
    def _():
        for ci in range(n_chunks):
            for bi in range(n_batch):
                chunk_mild(bi, ci)

    @pl.when(jnp.logical_not(mild))
    def _():
        for bi in range(n_batch):
            lax.fori_loop(0, n_chunks, lambda ci, c, bi=bi: (chunk(bi, ci), c)[1], 0)


def _gla(p32, p16, w2p, b2, norm_g, bsz, seq, ts):
    C, SUB, H, DK, DV = GLA_CHUNK, GLA_SUB, GLA_HEADS, GLA_DK, GLA_DV
    tokens = p32.shape[0]
    blocks_per_seq = seq // ts
    tril = jnp.asarray(np.tril(np.ones((C, C), np.float32))).astype(BF16)
    expand = np.zeros((H * DK, H * DV), np.float32)
    for h in range(H):
        expand[h * DK:(h + 1) * DK, h * DV:(h + 1) * DV] = 1.0
    expand = jnp.asarray(expand).astype(BF16)
    p32 = p32.reshape(bsz, seq, -1)
    p16 = p16.reshape(bsz, seq, -1)
    row_map = lambda cb: (lambda i: (0, i, cb))
    const2 = lambda i: (0, 0)
    out = pl.pallas_call(
        _gla_kernel,
        grid=(blocks_per_seq,),
        in_specs=[
            pl.BlockSpec((bsz, ts, 2 * H * DK), row_map(C32_GQ // (2 * H * DK))),
            pl.BlockSpec((bsz, ts, H * DV), row_map(C16_GV // (H * DV))),
            pl.BlockSpec((bsz, ts, H * DV), row_map(C32_R // (H * DV))),
            pl.BlockSpec((bsz, ts, LANES), row_map(C32_SMALL // LANES)),
            pl.BlockSpec(w2p.shape, const2),
            pl.BlockSpec(b2.shape, const2),
            pl.BlockSpec(norm_g.shape, const2),
            pl.BlockSpec(tril.shape, const2),
            pl.BlockSpec(expand.shape, const2),
        ],
        out_specs=pl.BlockSpec((bsz, ts, H * DV), lambda i: (0, i, 0)),
        out_shape=jax.ShapeDtypeStruct((bsz, seq, H * DV), F32),
        scratch_shapes=[pltpu.VMEM((bsz, DV, H * DK), F32), pltpu.VMEM((SUB * SUB, H * DK), BF16),
                        pltpu.VMEM((bsz, ts, H * DK), F32)],
        compiler_params=pltpu.CompilerParams(
            dimension_semantics=("arbitrary",), vmem_limit_bytes=VMEM_LIMIT),
        name="gla_branch",
    )(p32, p16, p32, p32, w2p, b2, norm_g, tril, expand)
    return out.reshape(tokens, H * DV)


def _bit_transpose32(words):
    a = list(words)
    shift, mask = 16, 0x0000FFFF
    while shift:
        m = jnp.int32(np.uint32(mask).astype(np.int32))
        for k in range(32):
            if k & shift == 0:
                t = (a[k] ^ lax.shift_right_logical(a[k + shift], jnp.int32(shift))) & m
                a[k] = a[k] ^ t
                a[k + shift] = a[k + shift] ^ lax.shift_left(t, jnp.int32(shift))
        shift >>= 1
        mask = (mask ^ (mask << shift)) & 0xFFFFFFFF
    return a


def _dsa_kernel(q_ref, iq_ref, iwt_ref, k_ref, vt_ref, ik_ref, bias_ref, ltri_ref,
                o_ref, key_scr, plane_scr, s_scr, acc_scr, *, topk):
    T, TK = DSA_QBLOCK, DSA_KTILE
    SUBS = TK // T
    H, HD = DSA_HEADS, DSA_HDIM
    qb = pl.program_id(1)
    kt_last = qb // SUBS
    n_kt = kt_last + 1
    key_minus_query = (lax.broadcasted_iota(jnp.int32, (TK, T), 0)
                       - lax.broadcasted_iota(jnp.int32, (TK, T), 1))

    def fold(x, op):
        return op(x.reshape(x.shape[0] // 8, 8, T), axis=0)

    def rows_to_one(x, op):
        return op(x, axis=0, keepdims=True)

    def key_rows(kt):
        return pl.ds(pl.multiple_of(kt * TK, TK), TK)

    def tile_loop(n, body, init):
        carry = lax.fori_loop(0, n // 2, lambda i, c: body(2 * i + 1, body(2 * i, c)), init)
        return lax.cond(n % 2 == 1, lambda c: body(n - 1, c), lambda c: c, carry)

    lo_mask = lax.broadcasted_iota(jnp.int32, (T, LANES), 1) < DSA_IDX_DIM
    iq = [iq_ref[:, 0:LANES], iq_ref[:, LANES:2 * LANES]]
    iq_h = [jnp.where(lo_mask, iq[0], 0), jnp.where(lo_mask, 0, iq[0]),
            jnp.where(lo_mask, iq[1], 0), jnp.where(lo_mask, 0, iq[1])]
    iq_all = jnp.concatenate(iq_h, axis=0)
    iw_h = [iwt_ref[h:h + 1, :] for h in range(DSA_IDX_HEADS)]

    GRP_ROWS = 32 * 8
    GRPS = TK // GRP_ROWS
    n_grp = plane_scr.shape[0]

    @pl.when(qb == 0)
    def _():
        plane_scr[...] = jnp.zeros_like(plane_scr)

    n_keys = key_scr.shape[0] * TK
    minus_one_minus_row = -1 - lax.broadcasted_iota(jnp.int32, (TK, T), 0)

    def score_tile(kt):
        ik2 = ik_ref[key_rows(kt), :]
        raw = _dot_nt(ik2, iq_all)
        score = jnp.zeros((TK, T), F32)
        for h in range(DSA_IDX_HEADS):
            score = score + iw_h[h] * jnp.maximum(raw[:, h * T:(h + 1) * T], 0.0)
        bits = lax.bitcast_convert_type(score, jnp.int32)
        zero_key = minus_one_minus_row - kt * TK
        negative_key = (bits ^ jnp.int32(0x7FFFFFFF)) - n_keys
        return jnp.where(score == 0.0, zero_key, jnp.where(bits >= 0, bits, negative_key))

    def store_planes(kt, key):
        unsigned_order = key ^ jnp.int32(INT_MIN)
        for g in range(GRPS):
            words = [unsigned_order[g * GRP_ROWS + 8 * j:g * GRP_ROWS + 8 * j + 8, :] for j in range(32)]
            planes = _bit_transpose32(words)
            for b in range(32):
                plane_scr[kt * GRPS + g, b] = planes[b]

    def score_body(kt, c):
        key = score_tile(kt)
        key_scr[kt] = key
        store_planes(kt, key)
        return c

    tile_loop(kt_last, score_body, 0)
    admissible = key_minus_query <= qb * T - kt_last * TK
    key = score_tile(kt_last)
    key_scr[kt_last] = jnp.where(admissible, key, jnp.int32(INT_MIN))
    store_planes(kt_last, key)
    alive_last = []
    for g in range(GRPS):
        word = jnp.zeros((8, T), jnp.int32)
        for j in range(32):
            adm = admissible[g * GRP_ROWS + 8 * j:g * GRP_ROWS + 8 * j + 8, :]
            word = word | jnp.where(adm, jnp.int32(np.int32(np.uint32(1 << (31 - j)))), 0)
        alive_last.append(word)

    alive = []
    for g in range(n_grp):
        kt = g // GRPS
        full = jnp.broadcast_to(jnp.where(kt < kt_last, jnp.int32(-1), jnp.int32(0)), (8, T))
        alive.append(jnp.where(kt == kt_last, alive_last[g % GRPS], full))

    def bits_body(i, carry):
        above, tau_u, alive = carry
        hi_planes = [plane_scr[g, 2 * i] for g in range(n_grp)]
        lo_planes = [plane_scr[g, 2 * i + 1] for g in range(n_grp)]
        n11 = n10 = n01 = jnp.zeros((8, T), jnp.int32)
        for g in range(n_grp):
            with_hi = alive[g] & hi_planes[g]
            both = with_hi & lo_planes[g]
            n11 = n11 + lax.population_count(both)
            n10 = n10 + lax.population_count(with_hi ^ both)
            n01 = n01 + lax.population_count((alive[g] & lo_planes[g]) ^ both)
        c11 = above + rows_to_one(n11, jnp.sum)
        c10 = c11 + rows_to_one(n10, jnp.sum)
        c01 = c10 + rows_to_one(n01, jnp.sum)
        is11, is10, is01 = c11 >= topk, c10 >= topk, c01 >= topk
        hi_bit = is10
        lo_bit = is11 | (is01 & ~is10)
        above = jnp.where(is11, above, jnp.where(is10, c11, jnp.where(is01, c10, c01)))
        flip_hi = jnp.where(hi_bit, jnp.int32(0), jnp.int32(-1))
        flip_lo = jnp.where(lo_bit, jnp.int32(0), jnp.int32(-1))
        alive = tuple(alive[g] & (hi_planes[g] ^ flip_hi) & (lo_planes[g] ^ flip_lo)
                      for g in range(n_grp))
        tau_u = (tau_u | jnp.where(hi_bit, lax.shift_left(jnp.int32(1), 31 - 2 * i), 0)
                 | jnp.where(lo_bit, lax.shift_left(jnp.int32(1), 30 - 2 * i), 0))
        return above, tau_u, alive

    zero_row = jnp.zeros((1, T), jnp.int32)
    above, tau_u, alive = lax.fori_loop(0, 16, bits_body, (zero_row, zero_row, tuple(alive)))
    n_tied = jnp.zeros((8, T), jnp.int32)
    for g in range(n_grp):
        n_tied = n_tied + lax.population_count(alive[g])
    n_ge = above + rows_to_one(n_tied, jnp.sum)
    tau = jnp.maximum(tau_u ^ jnp.int32(INT_MIN), jnp.int32(INT_MIN + 1))

    @pl.when(jnp.max(n_ge) > topk)
    def _():
        need = (topk - above).astype(F32)

        def tie_body(kt, seen):
            for j in range(SUBS):
                rows = slice(j * T, (j + 1) * T)
                key = key_scr[kt, rows, :]
                eq = key == tau
                eq_f = jnp.where(eq, 1.0, 0.0)
                pref = _dot(ltri_ref[...], eq_f.astype(BF16)) + seen
                key_scr[kt, rows, :] = jnp.where(eq & (pref > need), tau - 1, key)
                seen = seen + rows_to_one(eq_f, jnp.sum)
            return seen

        tile_loop(n_kt, tie_body, jnp.zeros((1, T), F32))

    q_h = [q_ref[:, h * HD:(h + 1) * HD] for h in range(H)]

    def logits_body(kt, m_run):
        sel = key_scr[kt] >= tau
        kinds = [jnp.clip(qb - (kt * SUBS + j), 0, 2) for j in range(SUBS)]
        new_m = []
        for h in range(H):
            bias = jnp.concatenate([bias_ref[kinds[j], h] for j in range(SUBS)], axis=0)
            s = _dot_nt(k_ref[key_rows(kt), h * HD:(h + 1) * HD], q_h[h]) + bias
            s = jnp.where(sel, s, NEG_BIG)
            s_scr[h, kt] = s
            new_m.append(jnp.maximum(m_run[h], fold(s, jnp.max)))
        return tuple(new_m)

    m_run = tile_loop(n_kt, logits_body, tuple(jnp.full((8, T), NEG_BIG, F32) for _ in range(H)))
    m_h = [rows_to_one(m, jnp.max) for m in m_run]

    for h in range(H):
        acc_scr[h] = jnp.zeros((HD, T), F32)

    def pv_body(kt, l_run):
        new_l = []
        for h in range(H):
            p = jnp.exp(s_scr[h, kt] - m_h[h])
            new_l.append(l_run[h] + fold(p, jnp.sum))
            acc_scr[h] = acc_scr[h] + _dot(vt_ref[kt, h * HD:(h + 1) * HD, :], p.astype(BF16))
        return tuple(new_l)

    l_run = tile_loop(n_kt, pv_body, tuple(jnp.zeros((8, T), F32) for _ in range(H)))

    for h in range(H):
        out_t = acc_scr[h] / rows_to_one(l_run[h], jnp.sum)
        o_ref[:, h * HD:(h + 1) * HD] = out_t.T.astype(o_ref.dtype)


def _dsa(p16, v_t, iw_t, bias_tiles, bsz, seq):
    T = DSA_QBLOCK
    tokens = p16.shape[0]
    nqb = seq // T
    topk = min(DSA_TOPK_MAX, seq // 4)
    TK = min(DSA_KTILE, seq)
    assert TK == DSA_KTILE and seq % TK == 0
    nkt = seq // TK
    ltri = jnp.asarray(np.tril(np.ones((T, T), np.float32))).astype(BF16)
    w = D_BRANCH
    v_t = v_t.reshape(bsz, nkt, w, TK)
    qmap = lambda cb: (lambda b, i: (b * nqb + i, cb))
    return pl.pallas_call(
        functools.partial(_dsa_kernel, topk=topk),
        grid=(bsz, nqb),
        in_specs=[
            pl.BlockSpec((T, w), qmap(C16_Q // w)),
            pl.BlockSpec((T, 2 * LANES), qmap(C16_IQ // (2 * LANES))),
            pl.BlockSpec((8, T), lambda b, i: (0, b * nqb + i)),
            pl.BlockSpec((seq, w), lambda b, i: (b, C16_K // w)),
            pl.BlockSpec((None, nkt, w, TK), lambda b, i: (b, 0, 0, 0)),
            pl.BlockSpec((seq, LANES), lambda b, i: (b, C16_IK // LANES)),
            pl.BlockSpec(bias_tiles.shape, lambda b, i: (0, 0, 0, 0)),
            pl.BlockSpec(ltri.shape, lambda b, i: (0, 0)),
        ],
        out_specs=pl.BlockSpec((T, w), lambda b, i: (b * nqb + i, 0)),
        out_shape=jax.ShapeDtypeStruct((tokens, w), BF16),
        scratch_shapes=[
            pltpu.VMEM((nkt, TK, T), jnp.int32),
            pltpu.VMEM((nkt * TK // 256, 32, 8, T), jnp.int32),
            pltpu.VMEM((DSA_HEADS, nkt, TK, T), F32),
            pltpu.VMEM((DSA_HEADS, DSA_HDIM, T), F32),
        ],
        compiler_params=pltpu.CompilerParams(
            dimension_semantics=("parallel", "arbitrary"), vmem_limit_bytes=VMEM_LIMIT),
        name="dsa_branch",
    )(p16, p16, iw_t, p16, v_t, p16, bias_tiles, ltri)


def _merge_kernel(u_ref, v_ref, ws_ref, bs_ref, yb_ref, yc_ref, ga_ref, gb_ref, gc_ref, x_ref, mod_ref,
                  wbr_ref, wout_ref, gn2_ref, wr_hi_ref, wr_lo_ref, br_ref, x1_ref, h2_ref, comb_ref):
    tm = u_ref.shape[0]
    ya_rows = []
    for c in range(tm // GMLP_CHUNK):
        rows = slice(c * GMLP_CHUNK, (c + 1) * GMLP_CHUNK)
        groups = []
        for g in range(GMLP_GROUPS):
            cols = slice(g * LANES, (g + 1) * LANES)
            mixed = _dot(ws_ref[g], v_ref[rows, cols].astype(BF16)) + bs_ref[:, cols]
            groups.append((u_ref[rows, cols] * mixed).astype(BF16))
        ya_rows.append(jnp.concatenate(groups, axis=1))
    ya = jnp.concatenate(ya_rows, axis=0)

    merged = ga_ref[...] * _dot(ya, wbr_ref[0])
    merged = merged + gb_ref[...] * _dot(yb_ref[...].astype(BF16), wbr_ref[1])
    merged = merged + gc_ref[...] * _dot(yc_ref[...], wbr_ref[2])
    mix = _dot(merged.astype(BF16), wout_ref[...])
    x1 = x_ref[...] + mod_ref[0, 2:3, :] * mix
    x1_ref[...] = x1
    y = x1 * lax.rsqrt(jnp.mean(x1 * x1, axis=-1, keepdims=True) + EPS) * gn2_ref[...]
    h2 = y * (1.0 + mod_ref[0, 4:5, :]) + mod_ref[0, 3:4, :]
    h2_ref[...] = h2.astype(BF16)
    comb, group_onehot = _route(_dot3(h2, wr_hi_ref[...], wr_lo_ref[...]) + br_ref[...])
    comb_ref[3] = group_onehot.astype(BF16)
    hi = comb.astype(BF16)
    rest = comb - hi.astype(F32)
    mid = rest.astype(BF16)
    comb_ref[0] = hi
    comb_ref[1] = mid
    comb_ref[2] = (rest - mid.astype(F32)).astype(BF16)


def _merge(p32, w_tril, gmlp_bias, yb, yc, p16, x2d, mod_l, wbr, wout, layer, gn2, wr_hi, wr_lo, br,
           seq, tm):
    tokens, d = x2d.shape
    tiles_per_seq = seq // tm
    row = lambda i: (i, 0)
    gate = lambda k: (lambda i: (i, C16_GATES // d + k))
    c2 = lambda i: (0, 0)
    of_layer = lambda a: pl.BlockSpec((None,) + a.shape[1:], lambda i: (layer,) + (0,) * (a.ndim - 1))
    return pl.pallas_call(
        _merge_kernel,
        grid=(tokens // tm,),
        in_specs=[
            pl.BlockSpec((tm, D_BRANCH), lambda i: (i, C32_U // D_BRANCH)),
            pl.BlockSpec((tm, D_BRANCH), lambda i: (i, C32_V // D_BRANCH)),
            of_layer(w_tril), of_layer(gmlp_bias),
            pl.BlockSpec((tm, D_BRANCH), row), pl.BlockSpec((tm, D_BRANCH), row),
            pl.BlockSpec((tm, d), gate(0)), pl.BlockSpec((tm, d), gate(1)), pl.BlockSpec((tm, d), gate(2)),
            pl.BlockSpec((tm, d), row),
            pl.BlockSpec((1, 6, d), lambda i: (i // tiles_per_seq, 0, 0)),
            of_layer(wbr), of_layer(wout),
            pl.BlockSpec(gn2.shape, c2),
            of_layer(wr_hi), of_layer(wr_lo), of_layer(br),
        ],
        out_specs=[pl.BlockSpec((tm, d), row), pl.BlockSpec((tm, d), row),
                   pl.BlockSpec((4, tm, LANES), lambda i: (0, i, 0))],
        out_shape=[jax.ShapeDtypeStruct((tokens, d), F32), jax.ShapeDtypeStruct((tokens, d), BF16),
                   jax.ShapeDtypeStruct((4, tokens, LANES), BF16)],
        compiler_params=pltpu.CompilerParams(
            dimension_semantics=("parallel",), vmem_limit_bytes=VMEM_LIMIT),
        name="gmlp_merge_norm_router",
    )(p32, p32, w_tril, gmlp_bias, yb, yc, p16, p16, p16, x2d, mod_l, wbr, wout, gn2, wr_hi, wr_lo, br)


def _route(lg):
    lane = lax.broadcasted_iota(jnp.int32, lg.shape, 1)
    big = jnp.int32(10 ** 6)
    is_grp = lane < N_GROUPS
    gl = jnp.where(is_grp, lg, -jnp.inf)
    gmax = jnp.max(gl, axis=1, keepdims=True)
    gsum = jnp.sum(jnp.where(is_grp, jnp.exp(lg - gmax), 0.0), axis=1, keepdims=True)
    p_g = 1.0 / gsum
    g_idx = jnp.min(jnp.where(gl == gmax, lane, big), axis=1, keepdims=True)
    first = N_GROUPS + g_idx * EXPERTS_PER_GROUP
    in_grp = (lane >= first) & (lane < first + EXPERTS_PER_GROUP)
    e1 = jnp.where(in_grp, lg, -jnp.inf)
    v1 = jnp.max(e1, axis=1, keepdims=True)
    i1 = jnp.min(jnp.where(e1 == v1, lane, big), axis=1, keepdims=True)
    e2 = jnp.where(in_grp & (lane != i1), lg, -jnp.inf)
    v2 = jnp.max(e2, axis=1, keepdims=True)
    i2 = jnp.min(jnp.where(e2 == v2, lane, big), axis=1, keepdims=True)
    t = jnp.exp(v2 - v1)
    w1 = p_g * (1.0 / (1.0 + t))
    w2 = p_g * (t / (1.0 + t))
    comb = jnp.where(lane == i1, w1, jnp.where(lane == i2, w2, 0.0))
    return comb, jnp.where(lane == g_idx, 1.0, 0.0)


def _moe_kernel(h_ref, comb_ref, x1_ref, mod_ref, wg_ref, wu_ref, wd_ref, tril_ref, o_ref,
                hp_scr, cwp_scr, yp_scr, pt_scr, start_ref, nchunk_ref):
    grp = pl.program_id(1)
    tm = h_ref.shape[0]
    n_sorted = hp_scr.shape[0]
    lane = lax.broadcasted_iota(jnp.int32, (1, LANES), 1)

    @pl.when(grp == 0)
    def _():
        onehot = comb_ref[3]
        incl = _dot(tril_ref[...], onehot)
        counts = incl[tm - 1:tm, :]
        start = jnp.int32(0)
        start_row = jnp.zeros((1, LANES), F32)
        for g in range(N_GROUPS):
            cnt = jnp.sum(jnp.where(lane == g, counts, 0.0)).astype(jnp.int32)
            start_ref[g] = start
            nchunk_ref[g] = lax.shift_right_logical(cnt + (MOE_CHUNK - 1), MOE_CHUNK.bit_length() - 1)
            start_row = jnp.where(lane == g, start.astype(F32), start_row)
            align = MOE_ALIGN.bit_length() - 1
            start = start + lax.shift_left(lax.shift_right_logical(cnt + (MOE_ALIGN - 1), align), align)
        dest = jnp.sum(onehot.astype(F32) * (incl - 1.0 + start_row), axis=1, keepdims=True)
        dest_col = dest.astype(jnp.int32)
        dest_row = jnp.broadcast_to(dest, (tm, LANES)).T[0:1, :].astype(jnp.int32)
        back = lax.broadcasted_iota(jnp.int32, (tm, n_sorted), 1) == dest_col
        pt_scr[...] = jnp.where(back, 1.0, 0.0).astype(BF16)
        fwd = lax.broadcasted_iota(jnp.int32, (n_sorted, tm), 0) == dest_row
        fwd = jnp.where(fwd, 1.0, 0.0).astype(BF16)
        hp_scr[...] = _dot(fwd, h_ref[...]).astype(BF16)
        for part in range(3):
            cwp_scr[part] = _dot(fwd, comb_ref[part]).astype(BF16)
        yp_scr[...] = jnp.zeros_like(yp_scr)

    row_id = lax.broadcasted_iota(jnp.int32, (LANES, LANES), 0)
    start = start_ref[grp]

    def chunk(c, carry):
        rows = pl.ds(pl.multiple_of(start + c * MOE_CHUNK, MOE_ALIGN), MOE_CHUNK)
        h = hp_scr[rows, :]
        acts = []
        for j in range(EXPERTS_PER_GROUP):
            lane_of_expert = N_GROUPS + grp * EXPERTS_PER_GROUP + j
            pick = jnp.where(row_id == lane_of_expert, 1.0, 0.0).astype(BF16)
            cw = (_dot(cwp_scr[0, rows, :], pick)
                  + (_dot(cwp_scr[1, rows, :], pick) + _dot(cwp_scr[2, rows, :], pick)))
            act = (jax.nn.silu(_dot(h, wg_ref[j])) * _dot(h, wu_ref[j])
                   * jnp.concatenate([cw, cw], axis=1))
            acts.append(act.astype(BF16))
        yp_scr[rows, :] = _dot(jnp.concatenate(acts, axis=1), wd_ref[...]).astype(BF16)
        return carry

    lax.fori_loop(0, nchunk_ref[grp], chunk, 0)

    @pl.when(grp == pl.num_programs(1) - 1)
    def _():
        o_ref[...] = x1_ref[...] + mod_ref[0, 5:6, :] * _dot(pt_scr[...], yp_scr[...])


def _moe(h2, comb, x1, mod_l, wg, wu, wd, layer, seq, tm):
    tokens, d = x1.shape
    tiles_per_seq = seq // tm
    n_sorted = tm + N_GROUPS * MOE_ALIGN + MOE_CHUNK
    n_sorted = LANES * pl.cdiv(n_sorted, LANES)
    tril = jnp.asarray(np.tril(np.ones((tm, tm), np.float32))).astype(BF16)
    row = lambda i, g: (i, 0)
    return pl.pallas_call(
        _moe_kernel,
        grid=(tokens // tm, N_GROUPS),
        in_specs=[
            pl.BlockSpec((tm, d), row), pl.BlockSpec((4, tm, LANES), lambda i, g: (0, i, 0)),
            pl.BlockSpec((tm, d), row),
            pl.BlockSpec((1, 6, d), lambda i, g: (i // tiles_per_seq, 0, 0)),
            pl.BlockSpec((None, None) + wg.shape[2:], lambda i, g: (layer, g, 0, 0, 0)),
            pl.BlockSpec((None, None) + wu.shape[2:], lambda i, g: (layer, g, 0, 0, 0)),
            pl.BlockSpec((None, None) + wd.shape[2:], lambda i, g: (layer, g, 0, 0)),
            pl.BlockSpec(tril.shape, lambda i, g: (0, 0), pipeline_mode=pl.Buffered(1)),
        ],
        out_specs=pl.BlockSpec((tm, d), row),
        out_shape=jax.ShapeDtypeStruct((tokens, d), F32),
        scratch_shapes=[pltpu.VMEM((n_sorted, d), BF16), pltpu.VMEM((3, n_sorted, LANES), BF16),
                        pltpu.VMEM((n_sorted, d), BF16), pltpu.VMEM((tm, n_sorted), BF16),
                        pltpu.SMEM((N_GROUPS,), jnp.int32), pltpu.SMEM((N_GROUPS,), jnp.int32)],
        compiler_params=pltpu.CompilerParams(
            dimension_semantics=("parallel", "arbitrary"), vmem_limit_bytes=VMEM_LIMIT),
        name="hier_moe",
    )(h2, comb, x1, mod_l, wg, wu, wd, tril)


def _row_tile(seq, want):
    t = min(want, seq)
    assert seq % t == 0
    return t


def kernel(x, c, w_mod, b_mod, g_norm1, g_norm2, w_in, gmlp_ln_g, gmlp_ln_b, gmlp_w_s, gmlp_b_s,
           gla_w_gate2, gla_b_gate, gla_norm_g, dsa_qnorm_g, dsa_knorm_g, rel_bias, w_branch,
           b_branch_gate, w_out, w_group, b_group, w_router, b_router, w_exp_gate, w_exp_up,
           w_exp_down):
    bsz, seq, d = x.shape
    depth = w_mod.shape[0]
    assert d == D_MODEL and seq % DSA_QBLOCK == 0 and seq % GLA_CHUNK == 0
    tokens = bsz * seq
    tm = _row_tile(seq, 512)

    mods = _modulation(c, w_mod, b_mod)
    bias_tiles = _bias_tiles(rel_bias)
    w32, w16 = _prep_w_in(w_in)

    causal = np.tril(np.ones((GMLP_CHUNK, GMLP_CHUNK), bool))
    w_tril = jnp.where(causal[None, None], gmlp_w_s, 0.0).astype(BF16)
    gmlp_bias = jnp.repeat(jnp.swapaxes(gmlp_b_s, 1, 2), LANES, axis=-1)
    w2p = jnp.zeros((depth, LANES, GLA_HEADS * GLA_DK), F32).at[:, :GLA_GATE_RANK].set(gla_w_gate2)
    w2p = w2p.astype(BF16)
    wr = jnp.zeros((depth, d, LANES), F32)
    wr = wr.at[:, :, :N_GROUPS].set(w_group).at[:, :, N_GROUPS:N_GROUPS + N_EXPERTS].set(w_router)
    wr_hi = wr.astype(BF16)
    wr_lo = (wr - wr_hi.astype(F32)).astype(BF16)
    br = jnp.zeros((depth, 1, LANES), F32)
    br = br.at[:, 0, :N_GROUPS].set(b_group).at[:, 0, N_GROUPS:N_GROUPS + N_EXPERTS].set(b_router)
    wbr = w_branch.astype(BF16)
    wout = w_out.astype(BF16)
    grouped = (depth, N_GROUPS, EXPERTS_PER_GROUP, d, D_EXPERT)
    wg, wu = w_exp_gate.astype(BF16).reshape(grouped), w_exp_up.astype(BF16).reshape(grouped)
    wd = w_exp_down.astype(BF16).reshape(depth, N_GROUPS, EXPERTS_PER_GROUP * D_EXPERT, d)
    tm_moe = _row_tile(seq, 1024)

    x2d = x.reshape(tokens, d)
    for l in range(depth):
        mod_l = mods[l]
        aux32 = [gmlp_ln_g[l][None], gmlp_ln_b[l][None]]
        aux16 = [dsa_qnorm_g[l][None], dsa_knorm_g[l][None], b_branch_gate[l].reshape(1, -1)]
        p32, p16, v_t, iw_t = _norm_proj(x2d, mod_l, g_norm1[l][None], w32, w16, l, aux32, aux16,
                                         seq, tm)
        yb = _gla(p32, p16, w2p[l], gla_b_gate[l][None], gla_norm_g[l][None], bsz, seq, tm)
        yc = _dsa(p16, v_t, iw_t, bias_tiles, bsz, seq)
        x1, h2, comb = _merge(p32, w_tril, gmlp_bias, yb, yc, p16, x2d, mod_l, wbr, wout, l,
                              g_norm2[l][None], wr_hi, wr_lo, br, seq, tm)
        x2d = _moe(h2, comb, x1, mod_l, wg, wu, wd, l, seq, tm_moe)
    return x2d.reshape(bsz, seq, d)
```

```python
import functools
import math

import numpy as np
import jax
import jax.numpy as jnp
from jax import lax
from jax.experimental import pallas as pl
from jax.experimental.pallas import tpu as pltpu

F32 = jnp.float32
BF16 = jnp.bfloat16

D_MODEL = 1024
D_BRANCH = 512
EPS = 1e-6
GMLP_CHUNK = 128
GMLP_GROUPS = 4
GLA_HEADS = 4
GLA_DK = 64
GLA_DV = 128
GLA_GATE_RANK = 16
GLA_GATE_TAU = 16.0
GLA_CHUNK = 128
GLA_SUB = 16
GLA_MILD_DECAY = -60.0
DSA_HEADS = 4
DSA_HDIM = 128
DSA_IDX_HEADS = 4
DSA_IDX_DIM = 64
DSA_QBLOCK = 128
DSA_KTILE = 512
DSA_TOPK_MAX = 256
N_BUCKETS = 32
MAX_DISTANCE = 128
N_GROUPS = 4
EXPERTS_PER_GROUP = 4
N_EXPERTS = 16
D_EXPERT = 256
MOE_CHUNK = 128
MOE_ALIGN = 16
MOE_PART_LANES = 32

LANES = 128
COL_TILE = 512
VMEM_LIMIT = 56 * 1024 * 1024
INT_MIN = -(2 ** 31)
NEG_BIG = -1e30

C32_U, C32_V, C32_GQ, C32_GK, C32_R, C32_SMALL = 0, 512, 1024, 1280, 1536, 2048
N32 = C32_SMALL + LANES
C16_GATES, C16_Q, C16_K, C16_GV, C16_IQ, C16_IK = 0, 3072, 3584, 4096, 4608, 4864
N16 = C16_IK + LANES


def _dot(a, b):
    return jnp.dot(a, b, preferred_element_type=F32)


def _dot_nt(a, b):
    return lax.dot_general(a, b, (((1,), (1,)), ((), ())), preferred_element_type=F32)


def _dot_tn(a, b):
    return lax.dot_general(a, b, (((0,), (0,)), ((), ())), preferred_element_type=F32)


def _split2(a):
    hi = a.astype(BF16)
    lo = (a - hi.astype(F32)).astype(BF16)
    return hi, lo


def _dot3(a, w_hi, w_lo):
    a_hi, a_lo = _split2(a)
    return _dot(a_hi, w_hi) + (_dot(a_lo, w_hi) + _dot(a_hi, w_lo))


def _head_rms(y, g, scale):
    outs = []
    for h in range(y.shape[1] // LANES):
        yh = y[:, h * LANES:(h + 1) * LANES]
        ms = jnp.mean(yh * yh, axis=-1, keepdims=True)
        o = yh * lax.rsqrt(ms + EPS) * g
        if scale != 1.0:
            o = o * scale
        outs.append(o)
    return jnp.concatenate(outs, axis=1)


def _mod_kernel(c_ref, w_ref, b_ref, o_ref):
    a = jax.nn.silu(c_ref[...])
    w_hi, w_lo = _split2(w_ref[...])
    o_ref[...] = _dot3(a, w_hi, w_lo) + b_ref[...]


def _modulation(c, w_mod, b_mod):
    L, d, n = w_mod.shape
    bsz = c.shape[0]
    rows = 8 * pl.cdiv(bsz, 8)
    c_pad = jnp.zeros((rows, d), F32).at[:bsz].set(c)
    tn = 1536
    out = pl.pallas_call(
        _mod_kernel,
        grid=(L, n // tn),
        in_specs=[
            pl.BlockSpec((rows, d), lambda l, j: (0, 0)),
            pl.BlockSpec((None, d, tn), lambda l, j: (l, 0, j)),
            pl.BlockSpec((None, 1, tn), lambda l, j: (l, 0, j)),
        ],
        out_specs=pl.BlockSpec((None, rows, tn), lambda l, j: (l, 0, j)),
        out_shape=jax.ShapeDtypeStruct((L, rows, n), F32),
        compiler_params=pltpu.CompilerParams(
            dimension_semantics=("arbitrary", "arbitrary"), vmem_limit_bytes=VMEM_LIMIT),
        name="adaln_modulation",
    )(c_pad, w_mod, b_mod.reshape(L, 1, n))
    return out[:, :bsz].reshape(L, bsz, 6, d)


def _t5_bucket_table():
    n = np.arange(2 * DSA_QBLOCK)
    max_exact = N_BUCKETS // 2
    large = max_exact + (
        np.log(np.maximum(n, max_exact).astype(np.float32) / max_exact)
        / math.log(MAX_DISTANCE / max_exact) * (N_BUCKETS - max_exact)).astype(np.int32)
    large = np.minimum(large, N_BUCKETS - 1)
    return np.where(n < max_exact, n, large).astype(np.int32)


def _bias_kernel(rb_ref, bucket_ref, o_ref):
    for t in range(bucket_ref.shape[0]):
        bucket = bucket_ref[t]
        for h in range(DSA_HEADS):
            acc = jnp.zeros(bucket.shape, F32)
            for b in range(N_BUCKETS):
                acc = jnp.where(bucket == b, rb_ref[b, h], acc)
            o_ref[t, h] = acc


def _bias_tiles(rel_bias):
    table = _t5_bucket_table()
    assert (table[MAX_DISTANCE:] == N_BUCKETS - 1).all()
    t = np.arange(DSA_QBLOCK)[None, :]
    s = np.arange(DSA_QBLOCK)[:, None]
    diag = table[np.maximum(t - s, 0)]
    near = table[DSA_QBLOCK + t - s]
    far = np.full_like(diag, N_BUCKETS - 1)
    buckets = jnp.asarray(np.stack([diag, near, far]).astype(np.int32))
    return pl.pallas_call(
        _bias_kernel,
        in_specs=[pl.BlockSpec(memory_space=pltpu.SMEM), pl.BlockSpec(memory_space=pltpu.VMEM)],
        out_specs=pl.BlockSpec(memory_space=pltpu.VMEM),
        out_shape=jax.ShapeDtypeStruct((3, DSA_HEADS, DSA_QBLOCK, DSA_QBLOCK), F32),
        name="t5_bias_tiles",
    )(rel_bias, buckets)


def _proj_kernel(x_ref, mod_ref, gn_ref, w32_ref, w16_ref, *rest):
    aux32, aux16 = rest[:2], rest[2:5]
    o32_ref, o16_ref, vt_ref, iwt_ref = rest[5:9]
    x = x_ref[...]
    y = x * lax.rsqrt(jnp.mean(x * x, axis=-1, keepdims=True) + EPS) * gn_ref[...]
    h = (y * (1.0 + mod_ref[0, 1:2, :]) + mod_ref[0, 0:1, :]).astype(BF16)
    for w_ref, o_ref, epilogues, aux in ((w32_ref, o32_ref, EPILOGUES_32, aux32),
                                         (w16_ref, o16_ref, EPILOGUES_16, aux16)):
        wcol = ocol = 0
        for width, epi in epilogues:
            y = epi(_dot(h, w_ref[:, wcol:wcol + width]), aux)
            wcol += width
            if epi is _epi_value_t:
                vt_ref[0] = y.T.astype(vt_ref.dtype)
                continue
            if epi is _epi_small:
                iwt_ref[...] = y.T[GLA_GATE_RANK:GLA_GATE_RANK + 8, :]
            o_ref[:, ocol:ocol + width] = y.astype(o_ref.dtype)
            ocol += width
        assert wcol == w_ref.shape[1] and ocol == o_ref.shape[1]


def _norm_proj(x2d, mod_l, gn, w32, w16, layer, aux32, aux16, seq, tm):
    tokens, d = x2d.shape
    tiles_per_seq = seq // tm
    assert tm == DSA_KTILE
    const = lambda a: pl.BlockSpec(a.shape, lambda i: (0, 0))
    resident = lambda a: pl.BlockSpec((None,) + a.shape[1:], lambda i: (layer, 0, 0),
                                      pipeline_mode=pl.Buffered(1))
    return pl.pallas_call(
        _proj_kernel,
        grid=(tokens // tm,),
        in_specs=[
            pl.BlockSpec((tm, d), lambda i: (i, 0)),
            pl.BlockSpec((1, 6, d), lambda i: (i // tiles_per_seq, 0, 0)),
            const(gn), resident(w32), resident(w16),
        ] + [const(a) for a in aux32 + aux16],
        out_specs=[pl.BlockSpec((tm, N32), lambda i: (i, 0)), pl.BlockSpec((tm, N16), lambda i: (i, 0)),
                   pl.BlockSpec((1, D_BRANCH, tm), lambda i: (i, 0, 0)),
                   pl.BlockSpec((8, tm), lambda i: (0, i))],
        out_shape=[jax.ShapeDtypeStruct((tokens, N32), F32), jax.ShapeDtypeStruct((tokens, N16), BF16),
                   jax.ShapeDtypeStruct((tokens // tm, D_BRANCH, tm), BF16),
                   jax.ShapeDtypeStruct((8, tokens), F32)],
        compiler_params=pltpu.CompilerParams(
            dimension_semantics=("parallel",), vmem_limit_bytes=VMEM_LIMIT),
        name="norm_proj",
    )(x2d, mod_l, gn, w32, w16, *aux32, *aux16)


def _epi_raw(y, aux):
    return y


def _epi_gelu(y, aux):
    return jax.nn.gelu(y)


def _epi_gelu_ln(y, aux):
    v = jax.nn.gelu(y)
    mu = jnp.mean(v, axis=-1, keepdims=True)
    var = jnp.mean(jnp.square(v - mu), axis=-1, keepdims=True)
    return (v - mu) * lax.rsqrt(var + EPS) * aux[0][...] + aux[1][...]


def _epi_silu(y, aux):
    return jax.nn.silu(y)


def _epi_gate(k):
    def epi(y, aux):
        return jax.nn.sigmoid(y + aux[2][:, k * COL_TILE:(k + 1) * COL_TILE])
    return epi


def _epi_qnorm(y, aux):
    return _head_rms(y, aux[0][...], DSA_HDIM ** -0.5)


def _epi_knorm(y, aux):
    return _head_rms(y, aux[1][...], 1.0)


def _epi_value_t(y, aux):
    return y


def _epi_small(y, aux):
    return y


EPILOGUES_32 = ([(COL_TILE, e) for e in (_epi_gelu, _epi_gelu_ln, _epi_raw, _epi_silu)]
                + [(LANES, _epi_small)])
EPILOGUES_16 = ([(COL_TILE, _epi_gate(k)) for k in range(6)]
                + [(COL_TILE, e) for e in (_epi_qnorm, _epi_knorm, _epi_value_t, _epi_raw)]
                + [(2 * LANES, _epi_raw), (LANES, _epi_raw)])


def _prep_w_in(w_in):
    sizes = (512, 512, 256, 256, 512, 512, 16, 512, 512, 512, 256, 64, 4, 3072)
    offs = np.concatenate([[0], np.cumsum(sizes)])
    seg = lambda k: w_in[:, :, offs[k]:offs[k + 1]]
    (a_u, a_v, g_q, g_k, g_v, g_r, g_a, d_q, d_k, d_v, d_iq, d_ik, d_iw, gates) = [seg(k) for k in range(14)]
    L, d, _ = w_in.shape
    zeros = lambda n: jnp.zeros((L, d, n), w_in.dtype)
    w32 = jnp.concatenate([a_u, a_v, g_q, g_k, g_r, g_a, d_iw,
                           zeros(N32 - C32_SMALL - 20)], axis=-1).astype(BF16)
    w16 = jnp.concatenate([gates, d_q, d_k, d_v, g_v, d_iq, d_ik, d_ik], axis=-1).astype(BF16)
    return w32, w16


def _gla_kernel(qk_ref, v_ref, r_ref, ga_ref, w2_ref, b2_ref, ng_ref, tril_ref, exp_ref,
                o_ref, st_ref, d_scr, g_scr):
    C, SUB, H, DK, DV = GLA_CHUNK, GLA_SUB, GLA_HEADS, GLA_DK, GLA_DV
    HK = H * DK
    n_batch = qk_ref.shape[0]
    n_chunks = qk_ref.shape[1] // C

    @pl.when(pl.program_id(0) == 0)
    def _():
        st_ref[...] = jnp.zeros_like(st_ref)

    lane = lax.broadcasted_iota(jnp.int32, (1, HK), 1)
    head_mask = [(lane >= h * DK) & (lane < (h + 1) * DK) for h in range(H)]
    row = lax.broadcasted_iota(jnp.int32, (C, C), 0)
    col = lax.broadcasted_iota(jnp.int32, (C, C), 1)
    sub_shift = SUB.bit_length() - 1
    blk_lower = (row >> sub_shift) > (col >> sub_shift)
    sub_t = lax.broadcasted_iota(jnp.int32, (SUB, 1), 0)

    mild = None
    for bi in range(n_batch):
        graw = _dot(ga_ref[bi].astype(BF16), w2_ref[...]) + b2_ref[...]
        g_all = jax.nn.log_sigmoid(graw) / GLA_GATE_TAU
        g_scr[bi] = g_all
        chunk_decay = jnp.sum(g_all.reshape(n_chunks, C, HK), axis=1)
        ok = jnp.min(chunk_decay) >= GLA_MILD_DECAY
        mild = ok if mild is None else jnp.logical_and(mild, ok)

    def cumulative_decay(bi, rows):
        g = g_scr[bi, rows, :]
        g_hi = g.astype(BF16)
        g_r1 = g - g_hi.astype(F32)
        g_mid = g_r1.astype(BF16)
        g_lo = (g_r1 - g_mid.astype(F32)).astype(BF16)
        tril = tril_ref[...]
        return _dot(tril, g_hi) + (_dot(tril, g_mid) + _dot(tril, g_lo))

    def finish(bi, rows, o, v, k_dec, b_last, st):
        upd = _dot_tn(v, k_dec)
        new_st = st * jnp.exp(b_last)
        for h in range(H):
            new_st = new_st + jnp.where(head_mask[h], upd[h * DV:(h + 1) * DV, :], 0.0)
        st_ref[bi] = new_st
        o_ref[bi, rows, :] = _head_rms(o, ng_ref[...], 1.0) * r_ref[bi, rows, :]

    def chunk_mild(bi, ci):
        rows = slice(ci * C, (ci + 1) * C)
        q = qk_ref[bi, rows, 0:HK] * (DK ** -0.5)
        k = qk_ref[bi, rows, HK:2 * HK]
        v = v_ref[bi, rows, :]
        b = cumulative_decay(bi, rows)
        b_last = b[C - 1:C, :]
        st = st_ref[bi]
        q_in = q * jnp.exp(b)
        k_out = (k * jnp.exp(-b)).astype(BF16)
        k_dec = (k * jnp.exp(b_last - b)).astype(BF16)
        outs = []
        for h in range(H):
            qm = jnp.where(head_mask[h], q_in, 0.0).astype(BF16)
            a_h = jnp.where(row >= col, _dot_nt(qm, k_out), 0.0).astype(BF16)
            outs.append(_dot_nt(qm, st.astype(BF16)) + _dot(a_h, v[:, h * DV:(h + 1) * DV]))
        finish(bi, rows, jnp.concatenate(outs, axis=1), v, k_dec, b_last, st)

    def chunk(bi, ci):
        r0 = pl.multiple_of(ci * C, C)
        rows = pl.ds(r0, C)
        q = qk_ref[bi, rows, 0:HK] * (DK ** -0.5)
        k = qk_ref[bi, rows, HK:2 * HK]
        v = v_ref[bi, rows, :]
        b = cumulative_decay(bi, rows)
        b_last = b[C - 1:C, :]
        st = st_ref[bi]

        q_in = q * jnp.exp(b)
        k_dec = (k * jnp.exp(b_last - b)).astype(BF16)

        a_off = [jnp.zeros((C, C), F32) for _ in range(H)]
        for j in range(C // SUB - 1):
            bj = b[(j + 1) * SUB - 1:(j + 1) * SUB, :]
            qj = q * jnp.exp(jnp.minimum(b - bj, 0.0))
            in_blk = (lax.broadcasted_iota(jnp.int32, (C, 1), 0) >> sub_shift) == j
            kj = jnp.where(in_blk, k * jnp.exp(jnp.minimum(bj - b, 0.0)), 0.0).astype(BF16)
            for h in range(H):
                a_off[h] = a_off[h] + _dot_nt(jnp.where(head_mask[h], qj, 0.0).astype(BF16), kj)

        for i in range(C // SUB):
            rs = slice(i * SUB, (i + 1) * SUB)
            q_sub, b_sub = q[rs, :], b[rs, :]
            for s in range(SUB):
                ks = k[i * SUB + s:i * SUB + s + 1, :]
                bs = b[i * SUB + s:i * SUB + s + 1, :]
                dterm = q_sub * ks * jnp.exp(jnp.minimum(b_sub - bs, 0.0))
                dterm = jnp.where(sub_t >= s, dterm, 0.0)
                d_scr[s * SUB:(s + 1) * SUB, :] = dterm.astype(BF16)
            gsum = _dot(d_scr[...], exp_ref[...])
            od = jnp.zeros((SUB, H * DV), F32)
            for s in range(SUB):
                vs = v[i * SUB + s:i * SUB + s + 1, :].astype(F32)
                od = od + gsum[s * SUB:(s + 1) * SUB, :] * vs
            o_ref[bi, pl.ds(r0 + i * SUB, SUB), :] = od

        outs = []
        for h in range(H):
            vh = v[:, h * DV:(h + 1) * DV]
            o_inter = _dot_nt(jnp.where(head_mask[h], q_in, 0.0).astype(BF16), st.astype(BF16))
            a_h = jnp.where(blk_lower, a_off[h], 0.0).astype(BF16)
            outs.append(o_inter + _dot(a_h, vh))
        o = o_ref[bi, rows, :] + jnp.concatenate(outs, axis=1)
        finish(bi, rows, o, v, k_dec, b_last, st)

    @pl.when(mild)
    def _():
        for ci in range(n_chunks):
            for bi in range(n_batch):
                chunk_mild(bi, ci)

    @pl.when(jnp.logical_not(mild))
    def _():
        for bi in range(n_batch):
            lax.fori_loop(0, n_chunks, lambda ci, c, bi=bi: (chunk(bi, ci), c)[1], 0)


def _gla(p32, p16, w2p, b2, norm_g, bsz, seq, ts):
    C, SUB, H, DK, DV = GLA_CHUNK, GLA_SUB, GLA_HEADS, GLA_DK, GLA_DV
    tokens = p32.shape[0]
    blocks_per_seq = seq // ts
    tril = jnp.asarray(np.tril(np.ones((C, C), np.float32))).astype(BF16)
    expand = np.zeros((H * DK, H * DV), np.float32)
    for h in range(H):
        expand[h * DK:(h + 1) * DK, h * DV:(h + 1) * DV] = 1.0
    expand = jnp.asarray(expand).astype(BF16)
    p32 = p32.reshape(bsz, seq, -1)
    p16 = p16.reshape(bsz, seq, -1)
    row_map = lambda cb: (lambda i: (0, i, cb))
    const2 = lambda i: (0, 0)
    out = pl.pallas_call(
        _gla_kernel,
        grid=(blocks_per_seq,),
        in_specs=[
            pl.BlockSpec((bsz, ts, 2 * H * DK), row_map(C32_GQ // (2 * H * DK))),
            pl.BlockSpec((bsz, ts, H * DV), row_map(C16_GV // (H * DV))),
            pl.BlockSpec((bsz, ts, H * DV), row_map(C32_R // (H * DV))),
            pl.BlockSpec((bsz, ts, LANES), row_map(C32_SMALL // LANES)),
            pl.BlockSpec(w2p.shape, const2),
            pl.BlockSpec(b2.shape, const2),
            pl.BlockSpec(norm_g.shape, const2),
            pl.BlockSpec(tril.shape, const2),
            pl.BlockSpec(expand.shape, const2),
        ],
        out_specs=pl.BlockSpec((bsz, ts, H * DV), lambda i: (0, i, 0)),
        out_shape=jax.ShapeDtypeStruct((bsz, seq, H * DV), F32),
        scratch_shapes=[pltpu.VMEM((bsz, DV, H * DK), F32), pltpu.VMEM((SUB * SUB, H * DK), BF16),
                        pltpu.VMEM((bsz, ts, H * DK), F32)],
        compiler_params=pltpu.CompilerParams(
            dimension_semantics=("arbitrary",), vmem_limit_bytes=VMEM_LIMIT),
        name="gla_branch",
    )(p32, p16, p32, p32, w2p, b2, norm_g, tril, expand)
    return out.reshape(tokens, H * DV)


def _bit_transpose32(words):
    a = list(words)
    shift, mask = 16, 0x0000FFFF
    while shift:
        m = jnp.int32(np.uint32(mask).astype(np.int32))
        for k in range(32):
            if k & shift == 0:
                t = (a[k] ^ lax.shift_right_logical(a[k + shift], jnp.int32(shift))) & m
                a[k] = a[k] ^ t
                a[k + shift] = a[k + shift] ^ lax.shift_left(t, jnp.int32(shift))
        shift >>= 1
        mask = (mask ^ (mask << shift)) & 0xFFFFFFFF
    return a


def _dsa_kernel(q_ref, iq_ref, iwt_ref, k_ref, vt_ref, ik_ref, bias_ref, ltri_ref,
                o_ref, key_scr, plane_scr, s_scr, acc_scr, *, topk):
    T, TK = DSA_QBLOCK, DSA_KTILE
    SUBS = TK // T
    H, HD = DSA_HEADS, DSA_HDIM
    qb = pl.program_id(1)
    kt_last = qb // SUBS
    n_kt = kt_last + 1
    key_minus_query = (lax.broadcasted_iota(jnp.int32, (TK, T), 0)
                       - lax.broadcasted_iota(jnp.int32, (TK, T), 1))

    def fold(x, op):
        return op(x.reshape(x.shape[0] // 8, 8, T), axis=0)

    def rows_to_one(x, op):
        return op(x, axis=0, keepdims=True)

    def key_rows(kt):
        return pl.ds(pl.multiple_of(kt * TK, TK), TK)

    def tile_loop(n, body, init):
        carry = lax.fori_loop(0, n // 2, lambda i, c: body(2 * i + 1, body(2 * i, c)), init)
        return lax.cond(n % 2 == 1, lambda c: body(n - 1, c), lambda c: c, carry)

    lo_mask = lax.broadcasted_iota(jnp.int32, (T, LANES), 1) < DSA_IDX_DIM
    iq = [iq_ref[:, 0:LANES], iq_ref[:, LANES:2 * LANES]]
    iq_h = [jnp.where(lo_mask, iq[0], 0), jnp.where(lo_mask, 0, iq[0]),
            jnp.where(lo_mask, iq[1], 0), jnp.where(lo_mask, 0, iq[1])]
    iq_all = jnp.concatenate(iq_h, axis=0)
    iw_h = [iwt_ref[h:h + 1, :] for h in range(DSA_IDX_HEADS)]

    GRP_ROWS = 32 * 8
    GRPS = TK // GRP_ROWS
    n_grp = plane_scr.shape[0]

    @pl.when(qb == 0)
    def _():
        plane_scr[...] = jnp.zeros_like(plane_scr)

    n_keys = key_scr.shape[0] * TK
    minus_one_minus_row = -1 - lax.broadcasted_iota(jnp.int32, (TK, T), 0)

    def score_tile(kt):
        ik2 = ik_ref[key_rows(kt), :]
        raw = _dot_nt(ik2, iq_all)
        score = jnp.zeros((TK, T), F32)
        for h in range(DSA_IDX_HEADS):
            score = score + iw_h[h] * jnp.maximum(raw[:, h * T:(h + 1) * T], 0.0)
        bits = lax.bitcast_convert_type(score, jnp.int32)
        zero_key = minus_one_minus_row - kt * TK
        negative_key = (bits ^ jnp.int32(0x7FFFFFFF)) - n_keys
        return jnp.where(score == 0.0, zero_key, jnp.where(bits >= 0, bits, negative_key))

    def store_planes(kt, key):
        unsigned_order = key ^ jnp.int32(INT_MIN)
        for g in range(GRPS):
            words = [unsigned_order[g * GRP_ROWS + 8 * j:g * GRP_ROWS + 8 * j + 8, :] for j in range(32)]
            planes = _bit_transpose32(words)
            for b in range(32):
                plane_scr[kt * GRPS + g, b] = planes[b]

    def score_body(kt, c):
        key = score_tile(kt)
        key_scr[kt] = key
        store_planes(kt, key)
        return c

    tile_loop(kt_last, score_body, 0)
    admissible = key_minus_query <= qb * T - kt_last * TK
    key = score_tile(kt_last)
    key_scr[kt_last] = jnp.where(admissible, key, jnp.int32(INT_MIN))
    store_planes(kt_last, key)
    alive_last = []
    for g in range(GRPS):
        word = jnp.zeros((8, T), jnp.int32)
        for j in range(32):
            adm = admissible[g * GRP_ROWS + 8 * j:g * GRP_ROWS + 8 * j + 8, :]
            word = word | jnp.where(adm, jnp.int32(np.int32(np.uint32(1 << (31 - j)))), 0)
        alive_last.append(word)

    alive = []
    for g in range(n_grp):
        kt = g // GRPS
        full = jnp.broadcast_to(jnp.where(kt < kt_last, jnp.int32(-1), jnp.int32(0)), (8, T))
        alive.append(jnp.where(kt == kt_last, alive_last[g % GRPS], full))

    def bits_body(i, carry):
        above, tau_u, alive = carry
        n_used = len(alive)
        hi_planes = [plane_scr[g, 2 * i] for g in range(n_used)]
        lo_planes = [plane_scr[g, 2 * i + 1] for g in range(n_used)]
        n11 = n10 = n01 = jnp.zeros((8, T), jnp.int32)
        for g in range(n_used):
            with_hi = alive[g] & hi_planes[g]
            both = with_hi & lo_planes[g]
            n11 = n11 + lax.population_count(both)
            n10 = n10 + lax.population_count(with_hi ^ both)
            n01 = n01 + lax.population_count((alive[g] & lo_planes[g]) ^ both)
        c11 = above + rows_to_one(n11, jnp.sum)
        c10 = c11 + rows_to_one(n10, jnp.sum)
        c01 = c10 + rows_to_one(n01, jnp.sum)
        is11, is10, is01 = c11 >= topk, c10 >= topk, c01 >= topk
        hi_bit = is10
        lo_bit = is11 | (is01 & ~is10)
        above = jnp.where(is11, above, jnp.where(is10, c11, jnp.where(is01, c10, c01)))
        flip_hi = jnp.where(hi_bit, jnp.int32(0), jnp.int32(-1))
        flip_lo = jnp.where(lo_bit, jnp.int32(0), jnp.int32(-1))
        alive = tuple(alive[g] & (hi_planes[g] ^ flip_hi) & (lo_planes[g] ^ flip_lo)
                      for g in range(n_used))
        tau_u = (tau_u | jnp.where(hi_bit, lax.shift_left(jnp.int32(1), 31 - 2 * i), 0)
                 | jnp.where(lo_bit, lax.shift_left(jnp.int32(1), 30 - 2 * i), 0))
        return above, tau_u, alive

    def select(n_used):
        def run():
            zero_row = jnp.zeros((1, T), jnp.int32)
            above, tau_u, left = lax.fori_loop(0, 16, bits_body,
                                               (zero_row, zero_row, tuple(alive[:n_used])))
            n_tied = jnp.zeros((8, T), jnp.int32)
            for g in range(n_used):
                n_tied = n_tied + lax.population_count(left[g])
            return above, tau_u, rows_to_one(n_tied, jnp.sum)
        return run

    half = n_grp // 2
    if half >= GRPS and half % GRPS == 0:
        above, tau_u, n_tied = lax.cond((kt_last + 1) * GRPS <= half, select(half), select(n_grp))
    else:
        above, tau_u, n_tied = select(n_grp)()
    n_ge = above + n_tied
    tau = jnp.maximum(tau_u ^ jnp.int32(INT_MIN), jnp.int32(INT_MIN + 1))

    @pl.when(jnp.max(n_ge) > topk)
    def _():
        need = (topk - above).astype(F32)

        def tie_body(kt, seen):
            for j in range(SUBS):
                rows = slice(j * T, (j + 1) * T)
                key = key_scr[kt, rows, :]
                eq = key == tau
                eq_f = jnp.where(eq, 1.0, 0.0)
                pref = _dot(ltri_ref[...], eq_f.astype(BF16)) + seen
                key_scr[kt, rows, :] = jnp.where(eq & (pref > need), tau - 1, key)
                seen = seen + rows_to_one(eq_f, jnp.sum)
            return seen

        tile_loop(n_kt, tie_body, jnp.zeros((1, T), F32))

    q_h = [q_ref[:, h * HD:(h + 1) * HD] for h in range(H)]

    def logits_body(kt, m_run):
        sel = key_scr[kt] >= tau
        kinds = [jnp.clip(qb - (kt * SUBS + j), 0, 2) for j in range(SUBS)]
        new_m = []
        for h in range(H):
            bias = jnp.concatenate([bias_ref[kinds[j], h] for j in range(SUBS)], axis=0)
            s = _dot_nt(k_ref[key_rows(kt), h * HD:(h + 1) * HD], q_h[h]) + bias
            s = jnp.where(sel, s, NEG_BIG)
            s_scr[h, kt] = s
            new_m.append(jnp.maximum(m_run[h], fold(s, jnp.max)))
        return tuple(new_m)

    m_run = tile_loop(n_kt, logits_body, tuple(jnp.full((8, T), NEG_BIG, F32) for _ in range(H)))
    m_h = [rows_to_one(m, jnp.max) for m in m_run]

    for h in range(H):
        acc_scr[h] = jnp.zeros((HD, T), F32)

    def pv_body(kt, l_run):
        new_l = []
        for h in range(H):
            p = jnp.exp(s_scr[h, kt] - m_h[h])
            new_l.append(l_run[h] + fold(p, jnp.sum))
            acc_scr[h] = acc_scr[h] + _dot(vt_ref[kt, h * HD:(h + 1) * HD, :], p.astype(BF16))
        return tuple(new_l)

    l_run = tile_loop(n_kt, pv_body, tuple(jnp.zeros((8, T), F32) for _ in range(H)))

    for h in range(H):
        out_t = acc_scr[h] / rows_to_one(l_run[h], jnp.sum)
        o_ref[:, h * HD:(h + 1) * HD] = out_t.T.astype(o_ref.dtype)


def _dsa(p16, v_t, iw_t, bias_tiles, bsz, seq):
    T = DSA_QBLOCK
    tokens = p16.shape[0]
    nqb = seq // T
    topk = min(DSA_TOPK_MAX, seq // 4)
    TK = min(DSA_KTILE, seq)
    assert TK == DSA_KTILE and seq % TK == 0
    nkt = seq // TK
    ltri = jnp.asarray(np.tril(np.ones((T, T), np.float32))).astype(BF16)
    w = D_BRANCH
    v_t = v_t.reshape(bsz, nkt, w, TK)
    qmap = lambda cb: (lambda b, i: (b * nqb + i, cb))
    return pl.pallas_call(
        functools.partial(_dsa_kernel, topk=topk),
        grid=(bsz, nqb),
        in_specs=[
            pl.BlockSpec((T, w), qmap(C16_Q // w)),
            pl.BlockSpec((T, 2 * LANES), qmap(C16_IQ // (2 * LANES))),
            pl.BlockSpec((8, T), lambda b, i: (0, b * nqb + i)),
            pl.BlockSpec((seq, w), lambda b, i: (b, C16_K // w)),
            pl.BlockSpec((None, nkt, w, TK), lambda b, i: (b, 0, 0, 0)),
            pl.BlockSpec((seq, LANES), lambda b, i: (b, C16_IK // LANES)),
            pl.BlockSpec(bias_tiles.shape, lambda b, i: (0, 0, 0, 0)),
            pl.BlockSpec(ltri.shape, lambda b, i: (0, 0)),
        ],
        out_specs=pl.BlockSpec((T, w), lambda b, i: (b * nqb + i, 0)),
        out_shape=jax.ShapeDtypeStruct((tokens, w), BF16),
        scratch_shapes=[
            pltpu.VMEM((nkt, TK, T), jnp.int32),
            pltpu.VMEM((nkt * TK // 256, 32, 8, T), jnp.int32),
            pltpu.VMEM((DSA_HEADS, nkt, TK, T), F32),
            pltpu.VMEM((DSA_HEADS, DSA_HDIM, T), F32),
        ],
        compiler_params=pltpu.CompilerParams(
            dimension_semantics=("parallel", "arbitrary"), vmem_limit_bytes=VMEM_LIMIT),
        name="dsa_branch",
    )(p16, p16, iw_t, p16, v_t, p16, bias_tiles, ltri)


def _merge_kernel(u_ref, v_ref, ws_ref, bs_ref, yb_ref, yc_ref, ga_ref, gb_ref, gc_ref, x_ref, mod_ref,
                  wbr_ref, wout_ref, gn2_ref, wr_hi_ref, wr_lo_ref, br_ref, x1_ref, h2_ref, comb_ref):
    tm = u_ref.shape[0]
    ya_rows = []
    for c in range(tm // GMLP_CHUNK):
        rows = slice(c * GMLP_CHUNK, (c + 1) * GMLP_CHUNK)
        groups = []
        for g in range(GMLP_GROUPS):
            cols = slice(g * LANES, (g + 1) * LANES)
            mixed = _dot(ws_ref[g], v_ref[rows, cols].astype(BF16)) + bs_ref[:, cols]
            groups.append((u_ref[rows, cols] * mixed).astype(BF16))
        ya_rows.append(jnp.concatenate(groups, axis=1))
    ya = jnp.concatenate(ya_rows, axis=0)

    merged = ga_ref[...] * _dot(ya, wbr_ref[0])
    merged = merged + gb_ref[...] * _dot(yb_ref[...].astype(BF16), wbr_ref[1])
    merged = merged + gc_ref[...] * _dot(yc_ref[...], wbr_ref[2])
    mix = _dot(merged.astype(BF16), wout_ref[...])
    x1 = x_ref[...] + mod_ref[0, 2:3, :] * mix
    x1_ref[...] = x1
    y = x1 * lax.rsqrt(jnp.mean(x1 * x1, axis=-1, keepdims=True) + EPS) * gn2_ref[...]
    h2 = y * (1.0 + mod_ref[0, 4:5, :]) + mod_ref[0, 3:4, :]
    h2_ref[...] = h2.astype(BF16)
    comb, group_onehot = _route(_dot3(h2, wr_hi_ref[...], wr_lo_ref[...]) + br_ref[...])
    hi = comb.astype(BF16).astype(F32)
    mid = (comb - hi).astype(BF16).astype(F32)
    lo = ((comb - hi) - mid).astype(BF16).astype(F32)
    packed = (group_onehot + hi + pltpu.roll(mid, MOE_PART_LANES, 1)
              + pltpu.roll(lo, 2 * MOE_PART_LANES, 1))
    comb_ref[...] = packed.astype(BF16)


def _merge(p32, w_tril, gmlp_bias, yb, yc, p16, x2d, mod_l, wbr, wout, layer, gn2, wr_hi, wr_lo, br,
           seq, tm):
    tokens, d = x2d.shape
    tiles_per_seq = seq // tm
    row = lambda i: (i, 0)
    gate = lambda k: (lambda i: (i, C16_GATES // d + k))
    c2 = lambda i: (0, 0)
    of_layer = lambda a: pl.BlockSpec((None,) + a.shape[1:], lambda i: (layer,) + (0,) * (a.ndim - 1))
    return pl.pallas_call(
        _merge_kernel,
        grid=(tokens // tm,),
        in_specs=[
            pl.BlockSpec((tm, D_BRANCH), lambda i: (i, C32_U // D_BRANCH)),
            pl.BlockSpec((tm, D_BRANCH), lambda i: (i, C32_V // D_BRANCH)),
            of_layer(w_tril), of_layer(gmlp_bias),
            pl.BlockSpec((tm, D_BRANCH), row), pl.BlockSpec((tm, D_BRANCH), row),
            pl.BlockSpec((tm, d), gate(0)), pl.BlockSpec((tm, d), gate(1)), pl.BlockSpec((tm, d), gate(2)),
            pl.BlockSpec((tm, d), row),
            pl.BlockSpec((1, 6, d), lambda i: (i // tiles_per_seq, 0, 0)),
            of_layer(wbr), of_layer(wout),
            pl.BlockSpec(gn2.shape, c2),
            of_layer(wr_hi), of_layer(wr_lo), of_layer(br),
        ],
        out_specs=[pl.BlockSpec((tm, d), row), pl.BlockSpec((tm, d), row),
                   pl.BlockSpec((tm, LANES), row)],
        out_shape=[jax.ShapeDtypeStruct((tokens, d), F32), jax.ShapeDtypeStruct((tokens, d), BF16),
                   jax.ShapeDtypeStruct((tokens, LANES), BF16)],
        compiler_params=pltpu.CompilerParams(
            dimension_semantics=("parallel",), vmem_limit_bytes=VMEM_LIMIT),
        name="gmlp_merge_norm_router",
    )(p32, p32, w_tril, gmlp_bias, yb, yc, p16, p16, p16, x2d, mod_l, wbr, wout, gn2, wr_hi, wr_lo, br)


def _route(lg):
    lane = lax.broadcasted_iota(jnp.int32, lg.shape, 1)
    big = jnp.int32(10 ** 6)
    is_grp = lane < N_GROUPS
    gl = jnp.where(is_grp, lg, -jnp.inf)
    gmax = jnp.max(gl, axis=1, keepdims=True)
    gsum = jnp.sum(jnp.where(is_grp, jnp.exp(lg - gmax), 0.0), axis=1, keepdims=True)
    p_g = 1.0 / gsum
    g_idx = jnp.min(jnp.where(gl == gmax, lane, big), axis=1, keepdims=True)
    first = N_GROUPS + g_idx * EXPERTS_PER_GROUP
    in_grp = (lane >= first) & (lane < first + EXPERTS_PER_GROUP)
    e1 = jnp.where(in_grp, lg, -jnp.inf)
    v1 = jnp.max(e1, axis=1, keepdims=True)
    i1 = jnp.min(jnp.where(e1 == v1, lane, big), axis=1, keepdims=True)
    e2 = jnp.where(in_grp & (lane != i1), lg, -jnp.inf)
    v2 = jnp.max(e2, axis=1, keepdims=True)
    i2 = jnp.min(jnp.where(e2 == v2, lane, big), axis=1, keepdims=True)
    t = jnp.exp(v2 - v1)
    w1 = p_g * (1.0 / (1.0 + t))
    w2 = p_g * (t / (1.0 + t))
    comb = jnp.where(lane == i1, w1, jnp.where(lane == i2, w2, 0.0))
    return comb, jnp.where(lane == g_idx, 1.0, 0.0)


def _moe_kernel(h_ref, comb_ref, x1_ref, mod_ref, wg_ref, wu_ref, wd_ref, tril_ref, o_ref,
                hp_scr, cwp_scr, yp_scr, pt_scr, start_ref, nchunk_ref):
    grp = pl.program_id(1)
    tm = h_ref.shape[0]
    n_sorted = hp_scr.shape[0]
    lane = lax.broadcasted_iota(jnp.int32, (1, LANES), 1)

    @pl.when(grp == 0)
    def _():
        onehot = jnp.where(lane < N_GROUPS, comb_ref[...], 0)
        incl = _dot(tril_ref[...], onehot)
        counts = incl[tm - 1:tm, :]
        start = jnp.int32(0)
        start_row = jnp.zeros((1, LANES), F32)
        for g in range(N_GROUPS):
            cnt = jnp.sum(jnp.where(lane == g, counts, 0.0)).astype(jnp.int32)
            start_ref[g] = start
            nchunk_ref[g] = lax.shift_right_logical(cnt + (MOE_CHUNK - 1), MOE_CHUNK.bit_length() - 1)
            start_row = jnp.where(lane == g, start.astype(F32), start_row)
            align = MOE_ALIGN.bit_length() - 1
            start = start + lax.shift_left(lax.shift_right_logical(cnt + (MOE_ALIGN - 1), align), align)
        dest = jnp.sum(onehot.astype(F32) * (incl - 1.0 + start_row), axis=1, keepdims=True)
        dest_col = dest.astype(jnp.int32)
        dest_row = jnp.broadcast_to(dest, (tm, LANES)).T[0:1, :].astype(jnp.int32)
        back = lax.broadcasted_iota(jnp.int32, (tm, n_sorted), 1) == dest_col
        pt_scr[...] = jnp.where(back, 1.0, 0.0).astype(BF16)
        fwd = lax.broadcasted_iota(jnp.int32, (n_sorted, tm), 0) == dest_row
        fwd = jnp.where(fwd, 1.0, 0.0).astype(BF16)
        hp_scr[...] = _dot(fwd, h_ref[...]).astype(BF16)
        cwp_scr[...] = _dot(fwd, comb_ref[...]).astype(BF16)
        yp_scr[...] = jnp.zeros_like(yp_scr)

    row_id = lax.broadcasted_iota(jnp.int32, (LANES, LANES), 0)
    start = start_ref[grp]

    def chunk(c, carry):
        rows = pl.ds(pl.multiple_of(start + c * MOE_CHUNK, MOE_ALIGN), MOE_CHUNK)
        h = hp_scr[rows, :]
        acts = []
        for j in range(EXPERTS_PER_GROUP):
            lane_of_expert = N_GROUPS + grp * EXPERTS_PER_GROUP + j
            pick = ((row_id == lane_of_expert) | (row_id == lane_of_expert + MOE_PART_LANES)
                    | (row_id == lane_of_expert + 2 * MOE_PART_LANES))
            cw = _dot(cwp_scr[rows, :], jnp.where(pick, 1.0, 0.0).astype(BF16))
            act = (jax.nn.silu(_dot(h, wg_ref[j])) * _dot(h, wu_ref[j])
                   * jnp.concatenate([cw, cw], axis=1))
            acts.append(act.astype(BF16))
        yp_scr[rows, :] = _dot(jnp.concatenate(acts, axis=1), wd_ref[...]).astype(BF16)
        return carry

    lax.fori_loop(0, nchunk_ref[grp], chunk, 0)

    @pl.when(grp == pl.num_programs(1) - 1)
    def _():
        o_ref[...] = x1_ref[...] + mod_ref[0, 5:6, :] * _dot(pt_scr[...], yp_scr[...])


def _moe(h2, comb, x1, mod_l, wg, wu, wd, layer, seq, tm):
    tokens, d = x1.shape
    tiles_per_seq = seq // tm
    n_sorted = tm + N_GROUPS * MOE_ALIGN + MOE_CHUNK
    n_sorted = LANES * pl.cdiv(n_sorted, LANES)
    tril = jnp.asarray(np.tril(np.ones((tm, tm), np.float32))).astype(BF16)
    row = lambda i, g: (i, 0)
    return pl.pallas_call(
        _moe_kernel,
        grid=(tokens // tm, N_GROUPS),
        in_specs=[
            pl.BlockSpec((tm, d), row), pl.BlockSpec((tm, LANES), row),
            pl.BlockSpec((tm, d), row),
            pl.BlockSpec((1, 6, d), lambda i, g: (i // tiles_per_seq, 0, 0)),
            pl.BlockSpec((None, None) + wg.shape[2:], lambda i, g: (layer, g, 0, 0, 0)),
            pl.BlockSpec((None, None) + wu.shape[2:], lambda i, g: (layer, g, 0, 0, 0)),
            pl.BlockSpec((None, None) + wd.shape[2:], lambda i, g: (layer, g, 0, 0)),
            pl.BlockSpec(tril.shape, lambda i, g: (0, 0), pipeline_mode=pl.Buffered(1)),
        ],
        out_specs=pl.BlockSpec((tm, d), row),
        out_shape=jax.ShapeDtypeStruct((tokens, d), F32),
        scratch_shapes=[pltpu.VMEM((n_sorted, d), BF16), pltpu.VMEM((n_sorted, LANES), BF16),
                        pltpu.VMEM((n_sorted, d), BF16), pltpu.VMEM((tm, n_sorted), BF16),
                        pltpu.SMEM((N_GROUPS,), jnp.int32), pltpu.SMEM((N_GROUPS,), jnp.int32)],
        compiler_params=pltpu.CompilerParams(
            dimension_semantics=("parallel", "arbitrary"), vmem_limit_bytes=VMEM_LIMIT),
        name="hier_moe",
    )(h2, comb, x1, mod_l, wg, wu, wd, tril)


def _row_tile(seq, want):
    t = min(want, seq)
    assert seq % t == 0
    return t


def kernel(x, c, w_mod, b_mod, g_norm1, g_norm2, w_in, gmlp_ln_g, gmlp_ln_b, gmlp_w_s, gmlp_b_s,
           gla_w_gate2, gla_b_gate, gla_norm_g, dsa_qnorm_g, dsa_knorm_g, rel_bias, w_branch,
           b_branch_gate, w_out, w_group, b_group, w_router, b_router, w_exp_gate, w_exp_up,
           w_exp_down):
    bsz, seq, d = x.shape
    depth = w_mod.shape[0]
    assert d == D_MODEL and seq % DSA_QBLOCK == 0 and seq % GLA_CHUNK == 0
    tokens = bsz * seq
    tm = _row_tile(seq, 512)

    mods = _modulation(c, w_mod, b_mod)
    bias_tiles = _bias_tiles(rel_bias)
    w32, w16 = _prep_w_in(w_in)

    causal = np.tril(np.ones((GMLP_CHUNK, GMLP_CHUNK), bool))
    w_tril = jnp.where(causal[None, None], gmlp_w_s, 0.0).astype(BF16)
    gmlp_bias = jnp.repeat(jnp.swapaxes(gmlp_b_s, 1, 2), LANES, axis=-1)
    w2p = jnp.zeros((depth, LANES, GLA_HEADS * GLA_DK), F32).at[:, :GLA_GATE_RANK].set(gla_w_gate2)
    w2p = w2p.astype(BF16)
    wr = jnp.zeros((depth, d, LANES), F32)
    wr = wr.at[:, :, :N_GROUPS].set(w_group).at[:, :, N_GROUPS:N_GROUPS + N_EXPERTS].set(w_router)
    wr_hi = wr.astype(BF16)
    wr_lo = (wr - wr_hi.astype(F32)).astype(BF16)
    br = jnp.zeros((depth, 1, LANES), F32)
    br = br.at[:, 0, :N_GROUPS].set(b_group).at[:, 0, N_GROUPS:N_GROUPS + N_EXPERTS].set(b_router)
    wbr = w_branch.astype(BF16)
    wout = w_out.astype(BF16)
    grouped = (depth, N_GROUPS, EXPERTS_PER_GROUP, d, D_EXPERT)
    wg, wu = w_exp_gate.astype(BF16).reshape(grouped), w_exp_up.astype(BF16).reshape(grouped)
    wd = w_exp_down.astype(BF16).reshape(depth, N_GROUPS, EXPERTS_PER_GROUP * D_EXPERT, d)
    tm_moe = _row_tile(seq, 1024)

    x2d = x.reshape(tokens, d)
    for l in range(depth):
        mod_l = mods[l]
        aux32 = [gmlp_ln_g[l][None], gmlp_ln_b[l][None]]
        aux16 = [dsa_qnorm_g[l][None], dsa_knorm_g[l][None], b_branch_gate[l].reshape(1, -1)]
        p32, p16, v_t, iw_t = _norm_proj(x2d, mod_l, g_norm1[l][None], w32, w16, l, aux32, aux16,
                                         seq, tm)
        yb = _gla(p32, p16, w2p[l], gla_b_gate[l][None], gla_norm_g[l][None], bsz, seq, tm)
        yc = _dsa(p16, v_t, iw_t, bias_tiles, bsz, seq)
        x1, h2, comb = _merge(p32, w_tril, gmlp_bias, yb, yc, p16, x2d, mod_l, wbr, wout, l,
                              g_norm2[l][None], wr_hi, wr_lo, br, seq, tm)
        x2d = _moe(h2, comb, x1, mod_l, wg, wu, wd, l, seq, tm_moe)
    return x2d.reshape(bsz, seq, d)
```

```python
import functools
import math

import numpy as np
import jax
import jax.numpy as jnp
from jax import lax
from jax.experimental import pallas as pl
from jax.experimental.pallas import tpu as pltpu

F32 = jnp.float32
BF16 = jnp.bfloat16

D_MODEL = 1024
D_BRANCH = 512
EPS = 1e-6
GMLP_CHUNK = 128
GMLP_GROUPS = 4
GLA_HEADS = 4
GLA_DK = 64
GLA_DV = 128
GLA_GATE_RANK = 16
GLA_GATE_TAU = 16.0
GLA_CHUNK = 128
GLA_SUB = 16
GLA_MILD_DECAY = -60.0
DSA_HEADS = 4
DSA_HDIM = 128
DSA_IDX_HEADS = 4
DSA_IDX_DIM = 64
DSA_QBLOCK = 128
DSA_KTILE = 512
DSA_TOPK_MAX = 256
N_BUCKETS = 32
MAX_DISTANCE = 128
N_GROUPS = 4
EXPERTS_PER_GROUP = 4
N_EXPERTS = 16
D_EXPERT = 256
MOE_CHUNK = 128
MOE_ALIGN = 16
MOE_PART_LANES = 32

LANES = 128
COL_TILE = 512
VMEM_LIMIT = 56 * 1024 * 1024
INT_MIN = -(2 ** 31)
NEG_BIG = -1e30

C32_U, C32_V, C32_GQ, C32_GK, C32_R, C32_SMALL = 0, 512, 1024, 1280, 1536, 2048
N32 = C32_SMALL + LANES
C16_GATES, C16_Q, C16_K, C16_GV, C16_IQ, C16_IK = 0, 3072, 3584, 4096, 4608, 4864
N16 = C16_IK + LANES


def _dot(a, b):
    return jnp.dot(a, b, preferred_element_type=F32)


def _dot_nt(a, b):
    return lax.dot_general(a, b, (((1,), (1,)), ((), ())), preferred_element_type=F32)


def _dot_tn(a, b):
    return lax.dot_general(a, b, (((0,), (0,)), ((), ())), preferred_element_type=F32)


def _split2(a):
    hi = a.astype(BF16)
    lo = (a - hi.astype(F32)).astype(BF16)
    return hi, lo


def _dot3(a, w_hi, w_lo):
    a_hi, a_lo = _split2(a)
    return _dot(a_hi, w_hi) + (_dot(a_lo, w_hi) + _dot(a_hi, w_lo))


def _head_rms(y, g, scale):
    outs = []
    for h in range(y.shape[1] // LANES):
        yh = y[:, h * LANES:(h + 1) * LANES]
        ms = jnp.mean(yh * yh, axis=-1, keepdims=True)
        o = yh * lax.rsqrt(ms + EPS) * g
        if scale != 1.0:
            o = o * scale
        outs.append(o)
    return jnp.concatenate(outs, axis=1)


def _mod_kernel(c_ref, w_ref, b_ref, o_ref):
    a = jax.nn.silu(c_ref[...])
    w_hi, w_lo = _split2(w_ref[...])
    o_ref[...] = _dot3(a, w_hi, w_lo) + b_ref[...]


def _modulation(c, w_mod, b_mod):
    L, d, n = w_mod.shape
    bsz = c.shape[0]
    rows = 8 * pl.cdiv(bsz, 8)
    c_pad = jnp.zeros((rows, d), F32).at[:bsz].set(c)
    tn = 1536
    out = pl.pallas_call(
        _mod_kernel,
        grid=(L, n // tn),
        in_specs=[
            pl.BlockSpec((rows, d), lambda l, j: (0, 0)),
            pl.BlockSpec((None, d, tn), lambda l, j: (l, 0, j)),
            pl.BlockSpec((None, 1, tn), lambda l, j: (l, 0, j)),
        ],
        out_specs=pl.BlockSpec((None, rows, tn), lambda l, j: (l, 0, j)),
        out_shape=jax.ShapeDtypeStruct((L, rows, n), F32),
        compiler_params=pltpu.CompilerParams(
            dimension_semantics=("arbitrary", "arbitrary"), vmem_limit_bytes=VMEM_LIMIT),
        name="adaln_modulation",
    )(c_pad, w_mod, b_mod.reshape(L, 1, n))
    return out[:, :bsz].reshape(L, bsz, 6, d)


def _t5_bucket_table():
    n = np.arange(2 * DSA_QBLOCK)
    max_exact = N_BUCKETS // 2
    large = max_exact + (
        np.log(np.maximum(n, max_exact).astype(np.float32) / max_exact)
        / math.log(MAX_DISTANCE / max_exact) * (N_BUCKETS - max_exact)).astype(np.int32)
    large = np.minimum(large, N_BUCKETS - 1)
    return np.where(n < max_exact, n, large).astype(np.int32)


def _bias_kernel(rb_ref, bucket_ref, o_ref):
    for t in range(bucket_ref.shape[0]):
        bucket = bucket_ref[t]
        for h in range(DSA_HEADS):
            acc = jnp.zeros(bucket.shape, F32)
            for b in range(N_BUCKETS):
                acc = jnp.where(bucket == b, rb_ref[b, h], acc)
            o_ref[t, h] = acc


def _bias_tiles(rel_bias):
    table = _t5_bucket_table()
    assert (table[MAX_DISTANCE:] == N_BUCKETS - 1).all()
    t = np.arange(DSA_QBLOCK)[None, :]
    s = np.arange(DSA_QBLOCK)[:, None]
    diag = table[np.maximum(t - s, 0)]
    near = table[DSA_QBLOCK + t - s]
    far = np.full_like(diag, N_BUCKETS - 1)
    buckets = jnp.asarray(np.stack([diag, near, far]).astype(np.int32))
    return pl.pallas_call(
        _bias_kernel,
        in_specs=[pl.BlockSpec(memory_space=pltpu.SMEM), pl.BlockSpec(memory_space=pltpu.VMEM)],
        out_specs=pl.BlockSpec(memory_space=pltpu.VMEM),
        out_shape=jax.ShapeDtypeStruct((3, DSA_HEADS, DSA_QBLOCK, DSA_QBLOCK), F32),
        name="t5_bias_tiles",
    )(rel_bias, buckets)


def _proj_kernel(x_ref, mod_ref, gn_ref, w32_ref, w16_ref, *rest):
    aux32, aux16 = rest[:2], rest[2:5]
    o32_ref, o16_ref, vt_ref, iwt_ref = rest[5:9]
    x = x_ref[...]
    y = x * lax.rsqrt(jnp.mean(x * x, axis=-1, keepdims=True) + EPS) * gn_ref[...]
    h = (y * (1.0 + mod_ref[0, 1:2, :]) + mod_ref[0, 0:1, :]).astype(BF16)
    for w_ref, o_ref, epilogues, aux in ((w32_ref, o32_ref, EPILOGUES_32, aux32),
                                         (w16_ref, o16_ref, EPILOGUES_16, aux16)):
        wcol = ocol = 0
        for width, epi in epilogues:
            y = epi(_dot(h, w_ref[:, wcol:wcol + width]), aux)
            wcol += width
            if epi is _epi_value_t:
                vt_ref[0] = y.T.astype(vt_ref.dtype)
                continue
            if epi is _epi_small:
                iwt_ref[...] = y.T[GLA_GATE_RANK:GLA_GATE_RANK + 8, :]
            o_ref[:, ocol:ocol + width] = y.astype(o_ref.dtype)
            ocol += width
        assert wcol == w_ref.shape[1] and ocol == o_ref.shape[1]


def _norm_proj(x2d, mod_l, gn, w32, w16, layer, aux32, aux16, seq, tm):
    tokens, d = x2d.shape
    tiles_per_seq = seq // tm
    assert tm == DSA_KTILE
    const = lambda a: pl.BlockSpec(a.shape, lambda i: (0, 0))
    resident = lambda a: pl.BlockSpec((None,) + a.shape[1:], lambda i: (layer, 0, 0),
                                      pipeline_mode=pl.Buffered(1))
    return pl.pallas_call(
        _proj_kernel,
        grid=(tokens // tm,),
        in_specs=[
            pl.BlockSpec((tm, d), lambda i: (i, 0)),
            pl.BlockSpec((1, 6, d), lambda i: (i // tiles_per_seq, 0, 0)),
            const(gn), resident(w32), resident(w16),
        ] + [const(a) for a in aux32 + aux16],
        out_specs=[pl.BlockSpec((tm, N32), lambda i: (i, 0)), pl.BlockSpec((tm, N16), lambda i: (i, 0)),
                   pl.BlockSpec((1, D_BRANCH, tm), lambda i: (i, 0, 0)),
                   pl.BlockSpec((8, tm), lambda i: (0, i))],
        out_shape=[jax.ShapeDtypeStruct((tokens, N32), F32), jax.ShapeDtypeStruct((tokens, N16), BF16),
                   jax.ShapeDtypeStruct((tokens // tm, D_BRANCH, tm), BF16),
                   jax.ShapeDtypeStruct((8, tokens), F32)],
        compiler_params=pltpu.CompilerParams(
            dimension_semantics=("parallel",), vmem_limit_bytes=VMEM_LIMIT),
        name="norm_proj",
    )(x2d, mod_l, gn, w32, w16, *aux32, *aux16)


def _epi_raw(y, aux):
    return y


def _epi_gelu(y, aux):
    return jax.nn.gelu(y)


def _epi_gelu_ln(y, aux):
    v = jax.nn.gelu(y)
    mu = jnp.mean(v, axis=-1, keepdims=True)
    var = jnp.mean(jnp.square(v - mu), axis=-1, keepdims=True)
    return (v - mu) * lax.rsqrt(var + EPS) * aux[0][...] + aux[1][...]


def _epi_silu(y, aux):
    return jax.nn.silu(y)


def _epi_gate(k):
    def epi(y, aux):
        return jax.nn.sigmoid(y + aux[2][:, k * COL_TILE:(k + 1) * COL_TILE])
    return epi


def _epi_qnorm(y, aux):
    return _head_rms(y, aux[0][...], DSA_HDIM ** -0.5)


def _epi_knorm(y, aux):
    return _head_rms(y, aux[1][...], 1.0)


def _epi_value_t(y, aux):
    return y


def _epi_small(y, aux):
    return y


EPILOGUES_32 = ([(COL_TILE, e) for e in (_epi_gelu, _epi_gelu_ln, _epi_raw, _epi_silu)]
                + [(LANES, _epi_small)])
EPILOGUES_16 = ([(COL_TILE, _epi_gate(k)) for k in range(6)]
                + [(COL_TILE, e) for e in (_epi_qnorm, _epi_knorm, _epi_value_t, _epi_raw)]
                + [(2 * LANES, _epi_raw), (LANES, _epi_raw)])


def _prep_w_in(w_in):
    sizes = (512, 512, 256, 256, 512, 512, 16, 512, 512, 512, 256, 64, 4, 3072)
    offs = np.concatenate([[0], np.cumsum(sizes)])
    seg = lambda k: w_in[:, :, offs[k]:offs[k + 1]]
    (a_u, a_v, g_q, g_k, g_v, g_r, g_a, d_q, d_k, d_v, d_iq, d_ik, d_iw, gates) = [seg(k) for k in range(14)]
    L, d, _ = w_in.shape
    zeros = lambda n: jnp.zeros((L, d, n), w_in.dtype)
    w32 = jnp.concatenate([a_u, a_v, g_q, g_k, g_r, g_a, d_iw,
                           zeros(N32 - C32_SMALL - 20)], axis=-1).astype(BF16)
    w16 = jnp.concatenate([gates, d_q, d_k, d_v, g_v, d_iq, d_ik, d_ik], axis=-1).astype(BF16)
    return w32, w16


def _gla_kernel(qk_ref, v_ref, r_ref, ga_ref, w2_ref, b2_ref, ng_ref, tril_ref, exp_ref,
                o_ref, st_ref, d_scr, g_scr):
    C, SUB, H, DK, DV = GLA_CHUNK, GLA_SUB, GLA_HEADS, GLA_DK, GLA_DV
    HK = H * DK
    n_batch = qk_ref.shape[0]
    n_chunks = qk_ref.shape[1] // C

    @pl.when(pl.program_id(0) == 0)
    def _():
        st_ref[...] = jnp.zeros_like(st_ref)

    lane = lax.broadcasted_iota(jnp.int32, (1, HK), 1)
    head_mask = [(lane >= h * DK) & (lane < (h + 1) * DK) for h in range(H)]
    row = lax.broadcasted_iota(jnp.int32, (C, C), 0)
    col = lax.broadcasted_iota(jnp.int32, (C, C), 1)
    sub_shift = SUB.bit_length() - 1
    blk_lower = (row >> sub_shift) > (col >> sub_shift)
    sub_t = lax.broadcasted_iota(jnp.int32, (SUB, 1), 0)

    mild = None
    for bi in range(n_batch):
        graw = _dot(ga_ref[bi].astype(BF16), w2_ref[...]) + b2_ref[...]
        g_all = jax.nn.log_sigmoid(graw) / GLA_GATE_TAU
        g_scr[bi] = g_all
        chunk_decay = jnp.sum(g_all.reshape(n_chunks, C, HK), axis=1)
        ok = jnp.min(chunk_decay) >= GLA_MILD_DECAY
        mild = ok if mild is None else jnp.logical_and(mild, ok)

    def cumulative_decay(bi, rows):
        g = g_scr[bi, rows, :]
        g_hi = g.astype(BF16)
        g_r1 = g - g_hi.astype(F32)
        g_mid = g_r1.astype(BF16)
        g_lo = (g_r1 - g_mid.astype(F32)).astype(BF16)
        tril = tril_ref[...]
        return _dot(tril, g_hi) + (_dot(tril, g_mid) + _dot(tril, g_lo))

    def finish(bi, rows, o, v, k_dec, b_last, st):
        upd = _dot_tn(v, k_dec)
        new_st = st * jnp.exp(b_last)
        for h in range(H):
            new_st = new_st + jnp.where(head_mask[h], upd[h * DV:(h + 1) * DV, :], 0.0)
        st_ref[bi] = new_st
        o_ref[bi, rows, :] = _head_rms(o, ng_ref[...], 1.0) * r_ref[bi, rows, :]

    def chunk_mild(bi, ci):
        rows = slice(ci * C, (ci + 1) * C)
        q = qk_ref[bi, rows, 0:HK] * (DK ** -0.5)
        k = qk_ref[bi, rows, HK:2 * HK]
        v = v_ref[bi, rows, :]
        b = cumulative_decay(bi, rows)
        b_last = b[C - 1:C, :]
        st = st_ref[bi]
        q_in = q * jnp.exp(b)
        k_out = (k * jnp.exp(-b)).astype(BF16)
        k_dec = (k * jnp.exp(b_last - b)).astype(BF16)
        outs = []
        for h in range(H):
            qm = jnp.where(head_mask[h], q_in, 0.0).astype(BF16)
            a_h = jnp.where(row >= col, _dot_nt(qm, k_out), 0.0).astype(BF16)
            outs.append(_dot_nt(qm, st.astype(BF16)) + _dot(a_h, v[:, h * DV:(h + 1) * DV]))
        finish(bi, rows, jnp.concatenate(outs, axis=1), v, k_dec, b_last, st)

    def chunk(bi, ci):
        r0 = pl.multiple_of(ci * C, C)
        rows = pl.ds(r0, C)
        q = qk_ref[bi, rows, 0:HK] * (DK ** -0.5)
        k = qk_ref[bi, rows, HK:2 * HK]
        v = v_ref[bi, rows, :]
        b = cumulative_decay(bi, rows)
        b_last = b[C - 1:C, :]
        st = st_ref[bi]

        q_in = q * jnp.exp(b)
        k_dec = (k * jnp.exp(b_last - b)).astype(BF16)

        a_off = [jnp.zeros((C, C), F32) for _ in range(H)]
        for j in range(C // SUB - 1):
            bj = b[(j + 1) * SUB - 1:(j + 1) * SUB, :]
            qj = q * jnp.exp(jnp.minimum(b - bj, 0.0))
            in_blk = (lax.broadcasted_iota(jnp.int32, (C, 1), 0) >> sub_shift) == j
            kj = jnp.where(in_blk, k * jnp.exp(jnp.minimum(bj - b, 0.0)), 0.0).astype(BF16)
            for h in range(H):
                a_off[h] = a_off[h] + _dot_nt(jnp.where(head_mask[h], qj, 0.0).astype(BF16), kj)

        for i in range(C // SUB):
            rs = slice(i * SUB, (i + 1) * SUB)
            q_sub, b_sub = q[rs, :], b[rs, :]
            for s in range(SUB):
                ks = k[i * SUB + s:i * SUB + s + 1, :]
                bs = b[i * SUB + s:i * SUB + s + 1, :]
                dterm = q_sub * ks * jnp.exp(jnp.minimum(b_sub - bs, 0.0))
                dterm = jnp.where(sub_t >= s, dterm, 0.0)
                d_scr[s * SUB:(s + 1) * SUB, :] = dterm.astype(BF16)
            gsum = _dot(d_scr[...], exp_ref[...])
            od = jnp.zeros((SUB, H * DV), F32)
            for s in range(SUB):
                vs = v[i * SUB + s:i * SUB + s + 1, :].astype(F32)
                od = od + gsum[s * SUB:(s + 1) * SUB, :] * vs
            o_ref[bi, pl.ds(r0 + i * SUB, SUB), :] = od

        outs = []
        for h in range(H):
            vh = v[:, h * DV:(h + 1) * DV]
            o_inter = _dot_nt(jnp.where(head_mask[h], q_in, 0.0).astype(BF16), st.astype(BF16))
            a_h = jnp.where(blk_lower, a_off[h], 0.0).astype(BF16)
            outs.append(o_inter + _dot(a_h, vh))
        o = o_ref[bi, rows, :] + jnp.concatenate(outs, axis=1)
        finish(bi, rows, o, v, k_dec, b_last, st)

    @pl.when(mild)
    def _():
        for ci in range(n_chunks):
            for bi in range(n_batch):
                chunk_mild(bi, ci)

    @pl.when(jnp.logical_not(mild))
    def _():
        for bi in range(n_batch):
            lax.fori_loop(0, n_chunks, lambda ci, c, bi=bi: (chunk(bi, ci), c)[1], 0)


def _gla(p32, p16, w2p, b2, norm_g, bsz, seq, ts):
    C, SUB, H, DK, DV = GLA_CHUNK, GLA_SUB, GLA_HEADS, GLA_DK, GLA_DV
    tokens = p32.shape[0]
    blocks_per_seq = seq // ts
    tril = jnp.asarray(np.tril(np.ones((C, C), np.float32))).astype(BF16)
    expand = np.zeros((H * DK, H * DV), np.float32)
    for h in range(H):
        expand[h * DK:(h + 1) * DK, h * DV:(h + 1) * DV] = 1.0
    expand = jnp.asarray(expand).astype(BF16)
    p32 = p32.reshape(bsz, seq, -1)
    p16 = p16.reshape(bsz, seq, -1)
    row_map = lambda cb: (lambda i: (0, i, cb))
    const2 = lambda i: (0, 0)
    out = pl.pallas_call(
        _gla_kernel,
        grid=(blocks_per_seq,),
        in_specs=[
            pl.BlockSpec((bsz, ts, 2 * H * DK), row_map(C32_GQ // (2 * H * DK))),
            pl.BlockSpec((bsz, ts, H * DV), row_map(C16_GV // (H * DV))),
            pl.BlockSpec((bsz, ts, H * DV), row_map(C32_R // (H * DV))),
            pl.BlockSpec((bsz, ts, LANES), row_map(C32_SMALL // LANES)),
            pl.BlockSpec(w2p.shape, const2),
            pl.BlockSpec(b2.shape, const2),
            pl.BlockSpec(norm_g.shape, const2),
            pl.BlockSpec(tril.shape, const2),
            pl.BlockSpec(expand.shape, const2),
        ],
        out_specs=pl.BlockSpec((bsz, ts, H * DV), lambda i: (0, i, 0)),
        out_shape=jax.ShapeDtypeStruct((bsz, seq, H * DV), F32),
        scratch_shapes=[pltpu.VMEM((bsz, DV, H * DK), F32), pltpu.VMEM((SUB * SUB, H * DK), BF16),
                        pltpu.VMEM((bsz, ts, H * DK), F32)],
        compiler_params=pltpu.CompilerParams(
            dimension_semantics=("arbitrary",), vmem_limit_bytes=VMEM_LIMIT),
        name="gla_branch",
    )(p32, p16, p32, p32, w2p, b2, norm_g, tril, expand)
    return out.reshape(tokens, H * DV)


def _bit_transpose32(words):
    a = list(words)
    shift, mask = 16, 0x0000FFFF
    while shift:
        m = jnp.int32(np.uint32(mask).astype(np.int32))
        for k in range(32):
            if k & shift == 0:
                t = (a[k] ^ lax.shift_right_logical(a[k + shift], jnp.int32(shift))) & m
                a[k] = a[k] ^ t
                a[k + shift] = a[k + shift] ^ lax.shift_left(t, jnp.int32(shift))
        shift >>= 1
        mask = (mask ^ (mask << shift)) & 0xFFFFFFFF
    return a


def _dsa_kernel(q_ref, iq_ref, iwt_ref, k_ref, vt_ref, ik_ref, bias_ref, ltri_ref,
                o_ref, key_scr, plane_scr, s_scr, acc_scr, *, topk):
    T, TK = DSA_QBLOCK, DSA_KTILE
    SUBS = TK // T
    H, HD = DSA_HEADS, DSA_HDIM
    qb = pl.program_id(1)
    kt_last = qb // SUBS
    n_kt = kt_last + 1
    key_minus_query = (lax.broadcasted_iota(jnp.int32, (TK, T), 0)
                       - lax.broadcasted_iota(jnp.int32, (TK, T), 1))

    def fold(x, op):
        return op(x.reshape(x.shape[0] // 8, 8, T), axis=0)

    def rows_to_one(x, op):
        return op(x, axis=0, keepdims=True)

    def key_rows(kt):
        return pl.ds(pl.multiple_of(kt * TK, TK), TK)

    def tile_loop(n, body, init):
        carry = lax.fori_loop(0, n // 2, lambda i, c: body(2 * i + 1, body(2 * i, c)), init)
        return lax.cond(n % 2 == 1, lambda c: body(n - 1, c), lambda c: c, carry)

    lo_mask = lax.broadcasted_iota(jnp.int32, (T, LANES), 1) < DSA_IDX_DIM
    iq = [iq_ref[:, 0:LANES], iq_ref[:, LANES:2 * LANES]]
    iq_h = [jnp.where(lo_mask, iq[0], 0), jnp.where(lo_mask, 0, iq[0]),
            jnp.where(lo_mask, iq[1], 0), jnp.where(lo_mask, 0, iq[1])]
    iq_all = jnp.concatenate(iq_h, axis=0)
    iw_h = [iwt_ref[h:h + 1, :] for h in range(DSA_IDX_HEADS)]

    GRP_ROWS = 32 * 8
    GRPS = TK // GRP_ROWS
    n_grp = plane_scr.shape[0]

    @pl.when(qb == 0)
    def _():
        plane_scr[...] = jnp.zeros_like(plane_scr)

    n_keys = key_scr.shape[0] * TK
    minus_one_minus_row = -1 - lax.broadcasted_iota(jnp.int32, (TK, T), 0)

    def score_tile(kt):
        ik2 = ik_ref[key_rows(kt), :]
        raw = _dot_nt(ik2, iq_all)
        score = jnp.zeros((TK, T), F32)
        for h in range(DSA_IDX_HEADS):
            score = score + iw_h[h] * jnp.maximum(raw[:, h * T:(h + 1) * T], 0.0)
        bits = lax.bitcast_convert_type(score, jnp.int32)
        zero_key = minus_one_minus_row - kt * TK
        negative_key = (bits ^ jnp.int32(0x7FFFFFFF)) - n_keys
        return jnp.where(score == 0.0, zero_key, jnp.where(bits >= 0, bits, negative_key))

    def store_planes(kt, key):
        unsigned_order = key ^ jnp.int32(INT_MIN)
        for g in range(GRPS):
            words = [unsigned_order[g * GRP_ROWS + 8 * j:g * GRP_ROWS + 8 * j + 8, :] for j in range(32)]
            planes = _bit_transpose32(words)
            for b in range(32):
                plane_scr[kt * GRPS + g, b] = planes[b]

    def score_body(kt, c):
        key = score_tile(kt)
        key_scr[kt] = key
        store_planes(kt, key)
        return c

    tile_loop(kt_last, score_body, 0)
    admissible = key_minus_query <= qb * T - kt_last * TK
    key = score_tile(kt_last)
    key_scr[kt_last] = jnp.where(admissible, key, jnp.int32(INT_MIN))
    store_planes(kt_last, key)
    alive_last = []
    for g in range(GRPS):
        word = jnp.zeros((8, T), jnp.int32)
        for j in range(32):
            adm = admissible[g * GRP_ROWS + 8 * j:g * GRP_ROWS + 8 * j + 8, :]
            word = word | jnp.where(adm, jnp.int32(np.int32(np.uint32(1 << (31 - j)))), 0)
        alive_last.append(word)

    alive = []
    for g in range(n_grp):
        kt = g // GRPS
        full = jnp.broadcast_to(jnp.where(kt < kt_last, jnp.int32(-1), jnp.int32(0)), (8, T))
        alive.append(jnp.where(kt == kt_last, alive_last[g % GRPS], full))

    def bits_body(i, carry):
        above, tau_u, alive = carry
        n_used = len(alive)
        hi_planes = [plane_scr[g, 2 * i] for g in range(n_used)]
        lo_planes = [plane_scr[g, 2 * i + 1] for g in range(n_used)]
        n11 = n10 = n01 = jnp.zeros((8, T), jnp.int32)
        for g in range(n_used):
            with_hi = alive[g] & hi_planes[g]
            both = with_hi & lo_planes[g]
            n11 = n11 + lax.population_count(both)
            n10 = n10 + lax.population_count(with_hi ^ both)
            n01 = n01 + lax.population_count((alive[g] & lo_planes[g]) ^ both)
        c11 = above + rows_to_one(n11, jnp.sum)
        c10 = c11 + rows_to_one(n10, jnp.sum)
        c01 = c10 + rows_to_one(n01, jnp.sum)
        is11, is10, is01 = c11 >= topk, c10 >= topk, c01 >= topk
        hi_bit = is10
        lo_bit = is11 | (is01 & ~is10)
        above = jnp.where(is11, above, jnp.where(is10, c11, jnp.where(is01, c10, c01)))
        flip_hi = jnp.where(hi_bit, jnp.int32(0), jnp.int32(-1))
        flip_lo = jnp.where(lo_bit, jnp.int32(0), jnp.int32(-1))
        alive = tuple(alive[g] & (hi_planes[g] ^ flip_hi) & (lo_planes[g] ^ flip_lo)
                      for g in range(n_used))
        tau_u = (tau_u | jnp.where(hi_bit, lax.shift_left(jnp.int32(1), 31 - 2 * i), 0)
                 | jnp.where(lo_bit, lax.shift_left(jnp.int32(1), 30 - 2 * i), 0))
        return above, tau_u, alive

    def select(n_used):
        def run():
            zero_row = jnp.zeros((1, T), jnp.int32)
            above, tau_u, left = lax.fori_loop(0, 16, bits_body,
                                               (zero_row, zero_row, tuple(alive[:n_used])))
            n_tied = jnp.zeros((8, T), jnp.int32)
            for g in range(n_used):
                n_tied = n_tied + lax.population_count(left[g])
            return above, tau_u, rows_to_one(n_tied, jnp.sum)
        return run

    half = n_grp // 2
    if half >= GRPS and half % GRPS == 0:
        above, tau_u, n_tied = lax.cond((kt_last + 1) * GRPS <= half, select(half), select(n_grp))
    else:
        above, tau_u, n_tied = select(n_grp)()
    n_ge = above + n_tied
    tau = jnp.maximum(tau_u ^ jnp.int32(INT_MIN), jnp.int32(INT_MIN + 1))

    @pl.when(jnp.max(n_ge) > topk)
    def _():
        need = (topk - above).astype(F32)

        def tie_body(kt, seen):
            for j in range(SUBS):
                rows = slice(j * T, (j + 1) * T)
                key = key_scr[kt, rows, :]
                eq = key == tau
                eq_f = jnp.where(eq, 1.0, 0.0)
                pref = _dot(ltri_ref[...], eq_f.astype(BF16)) + seen
                key_scr[kt, rows, :] = jnp.where(eq & (pref > need), tau - 1, key)
                seen = seen + rows_to_one(eq_f, jnp.sum)
            return seen

        tile_loop(n_kt, tie_body, jnp.zeros((1, T), F32))

    q_h = [q_ref[:, h * HD:(h + 1) * HD] for h in range(H)]

    def logits_body(kt, m_run):
        sel = key_scr[kt] >= tau
        kinds = [jnp.clip(qb - (kt * SUBS + j), 0, 2) for j in range(SUBS)]
        new_m = []
        for h in range(H):
            bias = jnp.concatenate([bias_ref[kinds[j], h] for j in range(SUBS)], axis=0)
            s = _dot_nt(k_ref[key_rows(kt), h * HD:(h + 1) * HD], q_h[h]) + bias
            s = jnp.where(sel, s, NEG_BIG)
            s_scr[h, kt] = s
            new_m.append(jnp.maximum(m_run[h], fold(s, jnp.max)))
        return tuple(new_m)

    m_run = tile_loop(n_kt, logits_body, tuple(jnp.full((8, T), NEG_BIG, F32) for _ in range(H)))
    m_h = [rows_to_one(m, jnp.max) for m in m_run]

    for h in range(H):
        acc_scr[h] = jnp.zeros((HD, T), F32)

    def pv_body(kt, l_run):
        new_l = []
        for h in range(H):
            p = jnp.exp(s_scr[h, kt] - m_h[h])
            new_l.append(l_run[h] + fold(p, jnp.sum))
            acc_scr[h] = acc_scr[h] + _dot(vt_ref[kt, h * HD:(h + 1) * HD, :], p.astype(BF16))
        return tuple(new_l)

    l_run = tile_loop(n_kt, pv_body, tuple(jnp.zeros((8, T), F32) for _ in range(H)))

    for h in range(H):
        out_t = acc_scr[h] / rows_to_one(l_run[h], jnp.sum)
        o_ref[:, h * HD:(h + 1) * HD] = out_t.T.astype(o_ref.dtype)


def _dsa(p16, v_t, iw_t, bias_tiles, bsz, seq):
    T = DSA_QBLOCK
    tokens = p16.shape[0]
    nqb = seq // T
    topk = min(DSA_TOPK_MAX, seq // 4)
    TK = min(DSA_KTILE, seq)
    assert TK == DSA_KTILE and seq % TK == 0
    nkt = seq // TK
    ltri = jnp.asarray(np.tril(np.ones((T, T), np.float32))).astype(BF16)
    w = D_BRANCH
    v_t = v_t.reshape(bsz, nkt, w, TK)
    qmap = lambda cb: (lambda b, i: (b * nqb + i, cb))
    return pl.pallas_call(
        functools.partial(_dsa_kernel, topk=topk),
        grid=(bsz, nqb),
        in_specs=[
            pl.BlockSpec((T, w), qmap(C16_Q // w)),
            pl.BlockSpec((T, 2 * LANES), qmap(C16_IQ // (2 * LANES))),
            pl.BlockSpec((8, T), lambda b, i: (0, b * nqb + i)),
            pl.BlockSpec((seq, w), lambda b, i: (b, C16_K // w)),
            pl.BlockSpec((None, nkt, w, TK), lambda b, i: (b, 0, 0, 0)),
            pl.BlockSpec((seq, LANES), lambda b, i: (b, C16_IK // LANES)),
            pl.BlockSpec(bias_tiles.shape, lambda b, i: (0, 0, 0, 0)),
            pl.BlockSpec(ltri.shape, lambda b, i: (0, 0)),
        ],
        out_specs=pl.BlockSpec((T, w), lambda b, i: (b * nqb + i, 0)),
        out_shape=jax.ShapeDtypeStruct((tokens, w), BF16),
        scratch_shapes=[
            pltpu.VMEM((nkt, TK, T), jnp.int32),
            pltpu.VMEM((nkt * TK // 256, 32, 8, T), jnp.int32),
            pltpu.VMEM((DSA_HEADS, nkt, TK, T), F32),
            pltpu.VMEM((DSA_HEADS, DSA_HDIM, T), F32),
        ],
        compiler_params=pltpu.CompilerParams(
            dimension_semantics=("parallel", "arbitrary"), vmem_limit_bytes=VMEM_LIMIT),
        name="dsa_branch",
    )(p16, p16, iw_t, p16, v_t, p16, bias_tiles, ltri)


def _merge_kernel(u_ref, v_ref, ws_ref, bs_ref, yb_ref, yc_ref, ga_ref, gb_ref, gc_ref, x_ref, mod_ref,
                  wbr_ref, wout_ref, gn2_ref, wr_hi_ref, wr_lo_ref, br_ref, x1_ref, h2_ref, comb_ref):
    tm = u_ref.shape[0]
    ya_rows = []
    for c in range(tm // GMLP_CHUNK):
        rows = slice(c * GMLP_CHUNK, (c + 1) * GMLP_CHUNK)
        groups = []
        for g in range(GMLP_GROUPS):
            cols = slice(g * LANES, (g + 1) * LANES)
            mixed = _dot(ws_ref[g], v_ref[rows, cols].astype(BF16)) + bs_ref[:, cols]
            groups.append((u_ref[rows, cols] * mixed).astype(BF16))
        ya_rows.append(jnp.concatenate(groups, axis=1))
    ya = jnp.concatenate(ya_rows, axis=0)

    merged = ga_ref[...] * _dot(ya, wbr_ref[0])
    merged = merged + gb_ref[...] * _dot(yb_ref[...].astype(BF16), wbr_ref[1])
    merged = merged + gc_ref[...] * _dot(yc_ref[...], wbr_ref[2])
    mix = _dot(merged.astype(BF16), wout_ref[...])
    x1 = x_ref[...] + mod_ref[0, 2:3, :] * mix
    x1_ref[...] = x1
    y = x1 * lax.rsqrt(jnp.mean(x1 * x1, axis=-1, keepdims=True) + EPS) * gn2_ref[...]
    h2 = y * (1.0 + mod_ref[0, 4:5, :]) + mod_ref[0, 3:4, :]
    h2_ref[...] = h2.astype(BF16)
    logits = _dot3(h2, wr_hi_ref[...], wr_lo_ref[...]) + br_ref[...]
    comb_ref[...] = _route_packed(logits).astype(BF16)


def _merge(p32, w_tril, gmlp_bias, yb, yc, p16, x2d, mod_l, wbr, wout, layer, gn2, wr_hi, wr_lo, br,
           seq, tm):
    tokens, d = x2d.shape
    tiles_per_seq = seq // tm
    row = lambda i: (i, 0)
    gate = lambda k: (lambda i: (i, C16_GATES // d + k))
    c2 = lambda i: (0, 0)
    of_layer = lambda a: pl.BlockSpec((None,) + a.shape[1:], lambda i: (layer,) + (0,) * (a.ndim - 1))
    return pl.pallas_call(
        _merge_kernel,
        grid=(tokens // tm,),
        in_specs=[
            pl.BlockSpec((tm, D_BRANCH), lambda i: (i, C32_U // D_BRANCH)),
            pl.BlockSpec((tm, D_BRANCH), lambda i: (i, C32_V // D_BRANCH)),
            of_layer(w_tril), of_layer(gmlp_bias),
            pl.BlockSpec((tm, D_BRANCH), row), pl.BlockSpec((tm, D_BRANCH), row),
            pl.BlockSpec((tm, d), gate(0)), pl.BlockSpec((tm, d), gate(1)), pl.BlockSpec((tm, d), gate(2)),
            pl.BlockSpec((tm, d), row),
            pl.BlockSpec((1, 6, d), lambda i: (i // tiles_per_seq, 0, 0)),
            of_layer(wbr), of_layer(wout),
            pl.BlockSpec(gn2.shape, c2),
            of_layer(wr_hi), of_layer(wr_lo), of_layer(br),
        ],
        out_specs=[pl.BlockSpec((tm, d), row), pl.BlockSpec((tm, d), row),
                   pl.BlockSpec((tm, LANES), row)],
        out_shape=[jax.ShapeDtypeStruct((tokens, d), F32), jax.ShapeDtypeStruct((tokens, d), BF16),
                   jax.ShapeDtypeStruct((tokens, LANES), BF16)],
        compiler_params=pltpu.CompilerParams(
            dimension_semantics=("parallel",), vmem_limit_bytes=VMEM_LIMIT),
        name="gmlp_merge_norm_router",
    )(p32, p32, w_tril, gmlp_bias, yb, yc, p16, p16, p16, x2d, mod_l, wbr, wout, gn2, wr_hi, wr_lo, br)


def _route_packed(lg):
    n_rows = lg.shape[0]
    t = lg.T[0:MOE_PART_LANES, :]
    idx = lax.broadcasted_iota(jnp.int32, t.shape, 0)
    big = jnp.int32(10 ** 6)
    top = lambda x, op: op(x, axis=0, keepdims=True)
    is_grp = idx < N_GROUPS
    gl = jnp.where(is_grp, t, -jnp.inf)
    gmax = top(gl, jnp.max)
    p_g = 1.0 / top(jnp.where(is_grp, jnp.exp(t - gmax), 0.0), jnp.sum)
    g_idx = top(jnp.where(gl == gmax, idx, big), jnp.min)
    first = N_GROUPS + g_idx * EXPERTS_PER_GROUP
    in_grp = (idx >= first) & (idx < first + EXPERTS_PER_GROUP)
    e1 = jnp.where(in_grp, t, -jnp.inf)
    v1 = top(e1, jnp.max)
    i1 = top(jnp.where(e1 == v1, idx, big), jnp.min)
    e2 = jnp.where(in_grp & (idx != i1), t, -jnp.inf)
    v2 = top(e2, jnp.max)
    i2 = top(jnp.where(e2 == v2, idx, big), jnp.min)
    r = jnp.exp(v2 - v1)
    w1 = p_g * (1.0 / (1.0 + r))
    w2 = p_g * (r / (1.0 + r))
    comb = jnp.where(idx == i1, w1, jnp.where(idx == i2, w2, 0.0))
    hi = comb.astype(BF16).astype(F32)
    mid = (comb - hi).astype(BF16).astype(F32)
    lo = ((comb - hi) - mid).astype(BF16).astype(F32)
    first_part = jnp.where(idx == g_idx, 1.0, 0.0) + hi
    pad = jnp.zeros((LANES - 3 * MOE_PART_LANES, n_rows), F32)
    return jnp.concatenate([first_part, mid, lo, pad], axis=0).T


def _moe_kernel(h_ref, comb_ref, x1_ref, mod_ref, wg_ref, wu_ref, wd_ref, tril_ref, o_ref,
                hp_scr, cwp_scr, yp_scr, pt_scr, start_ref, nchunk_ref):
    grp = pl.program_id(1)
    tm = h_ref.shape[0]
    n_sorted = hp_scr.shape[0]
    lane = lax.broadcasted_iota(jnp.int32, (1, LANES), 1)

    @pl.when(grp == 0)
    def _():
        onehot = jnp.where(lane < N_GROUPS, comb_ref[...], 0)
        incl = _dot(tril_ref[...], onehot)
        counts = incl[tm - 1:tm, :]
        start = jnp.int32(0)
        start_row = jnp.zeros((1, LANES), F32)
        for g in range(N_GROUPS):
            cnt = jnp.sum(jnp.where(lane == g, counts, 0.0)).astype(jnp.int32)
            start_ref[g] = start
            nchunk_ref[g] = lax.shift_right_logical(cnt + (MOE_CHUNK - 1), MOE_CHUNK.bit_length() - 1)
            start_row = jnp.where(lane == g, start.astype(F32), start_row)
            align = MOE_ALIGN.bit_length() - 1
            start = start + lax.shift_left(lax.shift_right_logical(cnt + (MOE_ALIGN - 1), align), align)
        dest = jnp.sum(onehot.astype(F32) * (incl - 1.0 + start_row), axis=1, keepdims=True)
        dest_col = dest.astype(jnp.int32)
        dest_row = jnp.broadcast_to(dest, (tm, LANES)).T[0:1, :].astype(jnp.int32)
        back = lax.broadcasted_iota(jnp.int32, (tm, n_sorted), 1) == dest_col
        pt_scr[...] = jnp.where(back, 1.0, 0.0).astype(BF16)
        fwd = lax.broadcasted_iota(jnp.int32, (n_sorted, tm), 0) == dest_row
        fwd = jnp.where(fwd, 1.0, 0.0).astype(BF16)
        hp_scr[...] = _dot(fwd, h_ref[...]).astype(BF16)
        cwp_scr[...] = _dot(fwd, comb_ref[...]).astype(BF16)
        yp_scr[...] = jnp.zeros_like(yp_scr)

    row_id = lax.broadcasted_iota(jnp.int32, (LANES, LANES), 0)
    start = start_ref[grp]

    def chunk(c, carry):
        rows = pl.ds(pl.multiple_of(start + c * MOE_CHUNK, MOE_ALIGN), MOE_CHUNK)
        h = hp_scr[rows, :]
        acts = []
        for j in range(EXPERTS_PER_GROUP):
            lane_of_expert = N_GROUPS + grp * EXPERTS_PER_GROUP + j
            pick = ((row_id == lane_of_expert) | (row_id == lane_of_expert + MOE_PART_LANES)
                    | (row_id == lane_of_expert + 2 * MOE_PART_LANES))
            cw = _dot(cwp_scr[rows, :], jnp.where(pick, 1.0, 0.0).astype(BF16))
            act = (jax.nn.silu(_dot(h, wg_ref[j])) * _dot(h, wu_ref[j])
                   * jnp.concatenate([cw, cw], axis=1))
            acts.append(act.astype(BF16))
        yp_scr[rows, :] = _dot(jnp.concatenate(acts, axis=1), wd_ref[...]).astype(BF16)
        return carry

    lax.fori_loop(0, nchunk_ref[grp], chunk, 0)

    @pl.when(grp == pl.num_programs(1) - 1)
    def _():
        o_ref[...] = x1_ref[...] + mod_ref[0, 5:6, :] * _dot(pt_scr[...], yp_scr[...])


def _moe(h2, comb, x1, mod_l, wg, wu, wd, layer, seq, tm):
    tokens, d = x1.shape
    tiles_per_seq = seq // tm
    n_sorted = tm + N_GROUPS * MOE_ALIGN + MOE_CHUNK
    n_sorted = LANES * pl.cdiv(n_sorted, LANES)
    tril = jnp.asarray(np.tril(np.ones((tm, tm), np.float32))).astype(BF16)
    row = lambda i, g: (i, 0)
    return pl.pallas_call(
        _moe_kernel,
        grid=(tokens // tm, N_GROUPS),
        in_specs=[
            pl.BlockSpec((tm, d), row), pl.BlockSpec((tm, LANES), row),
            pl.BlockSpec((tm, d), row),
            pl.BlockSpec((1, 6, d), lambda i, g: (i // tiles_per_seq, 0, 0)),
            pl.BlockSpec((None, None) + wg.shape[2:], lambda i, g: (layer, g, 0, 0, 0)),
            pl.BlockSpec((None, None) + wu.shape[2:], lambda i, g: (layer, g, 0, 0, 0)),
            pl.BlockSpec((None, None) + wd.shape[2:], lambda i, g: (layer, g, 0, 0)),
            pl.BlockSpec(tril.shape, lambda i, g: (0, 0), pipeline_mode=pl.Buffered(1)),
        ],
        out_specs=pl.BlockSpec((tm, d), row),
        out_shape=jax.ShapeDtypeStruct((tokens, d), F32),
        scratch_shapes=[pltpu.VMEM((n_sorted, d), BF16), pltpu.VMEM((n_sorted, LANES), BF16),
                        pltpu.VMEM((n_sorted, d), BF16), pltpu.VMEM((tm, n_sorted), BF16),
                        pltpu.SMEM((N_GROUPS,), jnp.int32), pltpu.SMEM((N_GROUPS,), jnp.int32)],
        compiler_params=pltpu.CompilerParams(
            dimension_semantics=("parallel", "arbitrary"), vmem_limit_bytes=VMEM_LIMIT),
        name="hier_moe",
    )(h2, comb, x1, mod_l, wg, wu, wd, tril)


def _row_tile(seq, want):
    t = min(want, seq)
    assert seq % t == 0
    return t


def kernel(x, c, w_mod, b_mod, g_norm1, g_norm2, w_in, gmlp_ln_g, gmlp_ln_b, gmlp_w_s, gmlp_b_s,
           gla_w_gate2, gla_b_gate, gla_norm_g, dsa_qnorm_g, dsa_knorm_g, rel_bias, w_branch,
           b_branch_gate, w_out, w_group, b_group, w_router, b_router, w_exp_gate, w_exp_up,
           w_exp_down):
    bsz, seq, d = x.shape
    depth = w_mod.shape[0]
    assert d == D_MODEL and seq % DSA_QBLOCK == 0 and seq % GLA_CHUNK == 0
    tokens = bsz * seq
    tm = _row_tile(seq, 512)

    mods = _modulation(c, w_mod, b_mod)
    bias_tiles = _bias_tiles(rel_bias)
    w32, w16 = _prep_w_in(w_in)

    causal = np.tril(np.ones((GMLP_CHUNK, GMLP_CHUNK), bool))
    w_tril = jnp.where(causal[None, None], gmlp_w_s, 0.0).astype(BF16)
    gmlp_bias = jnp.repeat(jnp.swapaxes(gmlp_b_s, 1, 2), LANES, axis=-1)
    w2p = jnp.zeros((depth, LANES, GLA_HEADS * GLA_DK), F32).at[:, :GLA_GATE_RANK].set(gla_w_gate2)
    w2p = w2p.astype(BF16)
    wr = jnp.zeros((depth, d, LANES), F32)
    wr = wr.at[:, :, :N_GROUPS].set(w_group).at[:, :, N_GROUPS:N_GROUPS + N_EXPERTS].set(w_router)
    wr_hi = wr.astype(BF16)
    wr_lo = (wr - wr_hi.astype(F32)).astype(BF16)
    br = jnp.zeros((depth, 1, LANES), F32)
    br = br.at[:, 0, :N_GROUPS].set(b_group).at[:, 0, N_GROUPS:N_GROUPS + N_EXPERTS].set(b_router)
    wbr = w_branch.astype(BF16)
    wout = w_out.astype(BF16)
    grouped = (depth, N_GROUPS, EXPERTS_PER_GROUP, d, D_EXPERT)
    wg, wu = w_exp_gate.astype(BF16).reshape(grouped), w_exp_up.astype(BF16).reshape(grouped)
    wd = w_exp_down.astype(BF16).reshape(depth, N_GROUPS, EXPERTS_PER_GROUP * D_EXPERT, d)
    tm_moe = _row_tile(seq, 1024)

    x2d = x.reshape(tokens, d)
    for l in range(depth):
        mod_l = mods[l]
        aux32 = [gmlp_ln_g[l][None], gmlp_ln_b[l][None]]
        aux16 = [dsa_qnorm_g[l][None], dsa_knorm_g[l][None], b_branch_gate[l].reshape(1, -1)]
        p32, p16, v_t, iw_t = _norm_proj(x2d, mod_l, g_norm1[l][None], w32, w16, l, aux32, aux16,
                                         seq, tm)
        yb = _gla(p32, p16, w2p[l], gla_b_gate[l][None], gla_norm_g[l][None], bsz, seq, tm)
        yc = _dsa(p16, v_t, iw_t, bias_tiles, bsz, seq)
        x1, h2, comb = _merge(p32, w_tril, gmlp_bias, yb, yc, p16, x2d, mod_l, wbr, wout, l,
                              g_norm2[l][None], wr_hi, wr_lo, br, seq, tm)
        x2d = _moe(h2, comb, x1, mod_l, wg, wu, wd, l, seq, tm_moe)
    return x2d.reshape(bsz, seq, d)
```

```python
import functools
import math

import numpy as np
import jax
import jax.numpy as jnp
from jax import lax
from jax.experimental import pallas as pl
from jax.experimental.pallas import tpu as pltpu

F32 = jnp.float32
BF16 = jnp.bfloat16

D_MODEL = 1024
D_BRANCH = 512
EPS = 1e-6
GMLP_CHUNK = 128
GMLP_GROUPS = 4
GLA_HEADS = 4
GLA_DK = 64
GLA_DV = 128
GLA_GATE_RANK = 16
GLA_GATE_TAU = 16.0
GLA_CHUNK = 128
GLA_SUB = 16
GLA_MILD_DECAY = -60.0
DSA_HEADS = 4
DSA_HDIM = 128
DSA_IDX_HEADS = 4
DSA_IDX_DIM = 64
DSA_QBLOCK = 128
DSA_KTILE = 512
DSA_TOPK_MAX = 256
N_BUCKETS = 32
MAX_DISTANCE = 128
N_GROUPS = 4
EXPERTS_PER_GROUP = 4
N_EXPERTS = 16
D_EXPERT = 256
MOE_CHUNK = 128
MOE_ALIGN = 16
MOE_PART_LANES = 32

LANES = 128
COL_TILE = 512
VMEM_LIMIT = 56 * 1024 * 1024
INT_MIN = -(2 ** 31)
NEG_BIG = -1e30

C32_U, C32_V, C32_GQ, C32_GK, C32_R, C32_SMALL = 0, 512, 1024, 1280, 1536, 2048
N32 = C32_SMALL + LANES
C16_GATES, C16_Q, C16_K, C16_GV, C16_IQ, C16_IK = 0, 3072, 3584, 4096, 4608, 4864
N16 = C16_IK + LANES


def _dot(a, b):
    return jnp.dot(a, b, preferred_element_type=F32)


def _dot_nt(a, b):
    return lax.dot_general(a, b, (((1,), (1,)), ((), ())), preferred_element_type=F32)


def _dot_tn(a, b):
    return lax.dot_general(a, b, (((0,), (0,)), ((), ())), preferred_element_type=F32)


def _split2(a):
    hi = a.astype(BF16)
    lo = (a - hi.astype(F32)).astype(BF16)
    return hi, lo


def _dot3(a, w_hi, w_lo):
    a_hi, a_lo = _split2(a)
    return _dot(a_hi, w_hi) + (_dot(a_lo, w_hi) + _dot(a_hi, w_lo))


def _head_rms(y, g, scale):
    outs = []
    for h in range(y.shape[1] // LANES):
        yh = y[:, h * LANES:(h + 1) * LANES]
        ms = jnp.mean(yh * yh, axis=-1, keepdims=True)
        o = yh * lax.rsqrt(ms + EPS) * g
        if scale != 1.0:
            o = o * scale
        outs.append(o)
    return jnp.concatenate(outs, axis=1)


def _mod_kernel(c_ref, w_ref, b_ref, o_ref):
    a = jax.nn.silu(c_ref[...])
    w_hi, w_lo = _split2(w_ref[...])
    o_ref[...] = _dot3(a, w_hi, w_lo) + b_ref[...]


def _modulation(c, w_mod, b_mod):
    L, d, n = w_mod.shape
    bsz = c.shape[0]
    rows = 8 * pl.cdiv(bsz, 8)
    c_pad = jnp.zeros((rows, d), F32).at[:bsz].set(c)
    tn = 1536
    out = pl.pallas_call(
        _mod_kernel,
        grid=(L, n // tn),
        in_specs=[
            pl.BlockSpec((rows, d), lambda l, j: (0, 0)),
            pl.BlockSpec((None, d, tn), lambda l, j: (l, 0, j)),
            pl.BlockSpec((None, 1, tn), lambda l, j: (l, 0, j)),
        ],
        out_specs=pl.BlockSpec((None, rows, tn), lambda l, j: (l, 0, j)),
        out_shape=jax.ShapeDtypeStruct((L, rows, n), F32),
        compiler_params=pltpu.CompilerParams(
            dimension_semantics=("arbitrary", "arbitrary"), vmem_limit_bytes=VMEM_LIMIT),
        name="adaln_modulation",
    )(c_pad, w_mod, b_mod.reshape(L, 1, n))
    return out[:, :bsz].reshape(L, bsz, 6, d)


def _t5_bucket_table():
    n = np.arange(2 * DSA_QBLOCK)
    max_exact = N_BUCKETS // 2
    large = max_exact + (
        np.log(np.maximum(n, max_exact).astype(np.float32) / max_exact)
        / math.log(MAX_DISTANCE / max_exact) * (N_BUCKETS - max_exact)).astype(np.int32)
    large = np.minimum(large, N_BUCKETS - 1)
    return np.where(n < max_exact, n, large).astype(np.int32)


def _bias_kernel(rb_ref, bucket_ref, o_ref):
    for t in range(bucket_ref.shape[0]):
        bucket = bucket_ref[t]
        for h in range(DSA_HEADS):
            acc = jnp.zeros(bucket.shape, F32)
            for b in range(N_BUCKETS):
                acc = jnp.where(bucket == b, rb_ref[b, h], acc)
            o_ref[t, h] = acc


def _bias_tiles(rel_bias):
    table = _t5_bucket_table()
    assert (table[MAX_DISTANCE:] == N_BUCKETS - 1).all()
    t = np.arange(DSA_QBLOCK)[None, :]
    s = np.arange(DSA_QBLOCK)[:, None]
    diag = table[np.maximum(t - s, 0)]
    near = table[DSA_QBLOCK + t - s]
    far = np.full_like(diag, N_BUCKETS - 1)
    buckets = jnp.asarray(np.stack([diag, near, far]).astype(np.int32))
    return pl.pallas_call(
        _bias_kernel,
        in_specs=[pl.BlockSpec(memory_space=pltpu.SMEM), pl.BlockSpec(memory_space=pltpu.VMEM)],
        out_specs=pl.BlockSpec(memory_space=pltpu.VMEM),
        out_shape=jax.ShapeDtypeStruct((3, DSA_HEADS, DSA_QBLOCK, DSA_QBLOCK), F32),
        name="t5_bias_tiles",
    )(rel_bias, buckets)


def _proj_kernel(x_ref, mod_ref, gn_ref, w32_ref, w16_ref, *rest):
    aux32, aux16 = rest[:2], rest[2:5]
    o32_ref, o16_ref, vt_ref, iwt_ref = rest[5:9]
    x = x_ref[...]
    y = x * lax.rsqrt(jnp.mean(x * x, axis=-1, keepdims=True) + EPS) * gn_ref[...]
    h = (y * (1.0 + mod_ref[0, 1:2, :]) + mod_ref[0, 0:1, :]).astype(BF16)
    for w_ref, o_ref, epilogues, aux in ((w32_ref, o32_ref, EPILOGUES_32, aux32),
                                         (w16_ref, o16_ref, EPILOGUES_16, aux16)):
        wcol = ocol = 0
        for width, epi in epilogues:
            y = epi(_dot(h, w_ref[:, wcol:wcol + width]), aux)
            wcol += width
            if epi is _epi_value_t:
                vt_ref[0] = y.T.astype(vt_ref.dtype)
                continue
            if epi is _epi_small:
                iwt_ref[...] = y.T[GLA_GATE_RANK:GLA_GATE_RANK + 8, :]
            o_ref[:, ocol:ocol + width] = y.astype(o_ref.dtype)
            ocol += width
        assert wcol == w_ref.shape[1] and ocol == o_ref.shape[1]


def _norm_proj(x2d, mod_l, gn, w32, w16, layer, aux32, aux16, seq, tm):
    tokens, d = x2d.shape
    tiles_per_seq = seq // tm
    assert tm == DSA_KTILE
    const = lambda a: pl.BlockSpec(a.shape, lambda i: (0, 0))
    resident = lambda a: pl.BlockSpec((None,) + a.shape[1:], lambda i: (layer, 0, 0),
                                      pipeline_mode=pl.Buffered(1))
    return pl.pallas_call(
        _proj_kernel,
        grid=(tokens // tm,),
        in_specs=[
            pl.BlockSpec((tm, d), lambda i: (i, 0)),
            pl.BlockSpec((1, 6, d), lambda i: (i // tiles_per_seq, 0, 0)),
            const(gn), resident(w32), resident(w16),
        ] + [const(a) for a in aux32 + aux16],
        out_specs=[pl.BlockSpec((tm, N32), lambda i: (i, 0)), pl.BlockSpec((tm, N16), lambda i: (i, 0)),
                   pl.BlockSpec((1, D_BRANCH, tm), lambda i: (i, 0, 0)),
                   pl.BlockSpec((8, tm), lambda i: (0, i))],
        out_shape=[jax.ShapeDtypeStruct((tokens, N32), F32), jax.ShapeDtypeStruct((tokens, N16), BF16),
                   jax.ShapeDtypeStruct((tokens // tm, D_BRANCH, tm), BF16),
                   jax.ShapeDtypeStruct((8, tokens), F32)],
        compiler_params=pltpu.CompilerParams(
            dimension_semantics=("parallel",), vmem_limit_bytes=VMEM_LIMIT),
        name="norm_proj",
    )(x2d, mod_l, gn, w32, w16, *aux32, *aux16)


def _epi_raw(y, aux):
    return y


def _epi_gelu(y, aux):
    return jax.nn.gelu(y)


def _epi_gelu_ln(y, aux):
    v = jax.nn.gelu(y)
    mu = jnp.mean(v, axis=-1, keepdims=True)
    var = jnp.mean(jnp.square(v - mu), axis=-1, keepdims=True)
    return (v - mu) * lax.rsqrt(var + EPS) * aux[0][...] + aux[1][...]


def _epi_silu(y, aux):
    return jax.nn.silu(y)


def _epi_gate(k):
    def epi(y, aux):
        return jax.nn.sigmoid(y + aux[2][:, k * COL_TILE:(k + 1) * COL_TILE])
    return epi


def _epi_qnorm(y, aux):
    return _head_rms(y, aux[0][...], DSA_HDIM ** -0.5)


def _epi_knorm(y, aux):
    return _head_rms(y, aux[1][...], 1.0)


def _epi_value_t(y, aux):
    return y


def _epi_small(y, aux):
    return y


EPILOGUES_32 = ([(COL_TILE, e) for e in (_epi_gelu, _epi_gelu_ln, _epi_raw, _epi_silu)]
                + [(LANES, _epi_small)])
EPILOGUES_16 = ([(COL_TILE, _epi_gate(k)) for k in range(6)]
                + [(COL_TILE, e) for e in (_epi_qnorm, _epi_knorm, _epi_value_t, _epi_raw)]
                + [(2 * LANES, _epi_raw), (LANES, _epi_raw)])


def _prep_w_in(w_in):
    sizes = (512, 512, 256, 256, 512, 512, 16, 512, 512, 512, 256, 64, 4, 3072)
    offs = np.concatenate([[0], np.cumsum(sizes)])
    seg = lambda k: w_in[:, :, offs[k]:offs[k + 1]]
    (a_u, a_v, g_q, g_k, g_v, g_r, g_a, d_q, d_k, d_v, d_iq, d_ik, d_iw, gates) = [seg(k) for k in range(14)]
    L, d, _ = w_in.shape
    zeros = lambda n: jnp.zeros((L, d, n), w_in.dtype)
    w32 = jnp.concatenate([a_u, a_v, g_q, g_k, g_r, g_a, d_iw,
                           zeros(N32 - C32_SMALL - 20)], axis=-1).astype(BF16)
    w16 = jnp.concatenate([gates, d_q, d_k, d_v, g_v, d_iq, d_ik, d_ik], axis=-1).astype(BF16)
    return w32, w16


def _gla_kernel(qk_ref, v_ref, r_ref, ga_ref, w2_ref, b2_ref, ng_ref, tril_ref, exp_ref,
                o_ref, st_ref, d_scr, g_scr):
    C, SUB, H, DK, DV = GLA_CHUNK, GLA_SUB, GLA_HEADS, GLA_DK, GLA_DV
    HK = H * DK
    n_batch = qk_ref.shape[0]
    n_chunks = qk_ref.shape[1] // C

    @pl.when(pl.program_id(0) == 0)
    def _():
        st_ref[...] = jnp.zeros_like(st_ref)

    lane = lax.broadcasted_iota(jnp.int32, (1, HK), 1)
    head_mask = [(lane >= h * DK) & (lane < (h + 1) * DK) for h in range(H)]
    row = lax.broadcasted_iota(jnp.int32, (C, C), 0)
    col = lax.broadcasted_iota(jnp.int32, (C, C), 1)
    sub_shift = SUB.bit_length() - 1
    blk_lower = (row >> sub_shift) > (col >> sub_shift)
    sub_t = lax.broadcasted_iota(jnp.int32, (SUB, 1), 0)

    mild = None
    for bi in range(n_batch):
        graw = _dot(ga_ref[bi].astype(BF16), w2_ref[...]) + b2_ref[...]
        g_all = jax.nn.log_sigmoid(graw) / GLA_GATE_TAU
        g_scr[bi] = g_all
        chunk_decay = jnp.sum(g_all.reshape(n_chunks, C, HK), axis=1)
        ok = jnp.min(chunk_decay) >= GLA_MILD_DECAY
        mild = ok if mild is None else jnp.logical_and(mild, ok)

    def cumulative_decay(bi, rows):
        g = g_scr[bi, rows, :]
        g_hi = g.astype(BF16)
        g_r1 = g - g_hi.astype(F32)
        g_mid = g_r1.astype(BF16)
        g_lo = (g_r1 - g_mid.astype(F32)).astype(BF16)
        tril = tril_ref[...]
        return _dot(tril, g_hi) + (_dot(tril, g_mid) + _dot(tril, g_lo))

    def finish(bi, rows, o, v, k_dec, b_last, st):
        upd = _dot_tn(v, k_dec)
        new_st = st * jnp.exp(b_last)
        for h in range(H):
            new_st = new_st + jnp.where(head_mask[h], upd[h * DV:(h + 1) * DV, :], 0.0)
        st_ref[bi] = new_st
        o_ref[bi, rows, :] = _head_rms(o, ng_ref[...], 1.0) * r_ref[bi, rows, :]

    def chunk_mild(bi, ci):
        rows = slice(ci * C, (ci + 1) * C)
        q = qk_ref[bi, rows, 0:HK] * (DK ** -0.5)
        k = qk_ref[bi, rows, HK:2 * HK]
        v = v_ref[bi, rows, :]
        b = cumulative_decay(bi, rows)
        b_last = b[C - 1:C, :]
        st = st_ref[bi]
        q_in = q * jnp.exp(b)
        k_out = (k * jnp.exp(-b)).astype(BF16)
        k_dec = (k * jnp.exp(b_last - b)).astype(BF16)
        outs = []
        for h in range(H):
            qm = jnp.where(head_mask[h], q_in, 0.0).astype(BF16)
            a_h = jnp.where(row >= col, _dot_nt(qm, k_out), 0.0).astype(BF16)
            outs.append(_dot_nt(qm, st.astype(BF16)) + _dot(a_h, v[:, h * DV:(h + 1) * DV]))
        finish(bi, rows, jnp.concatenate(outs, axis=1), v, k_dec, b_last, st)

    def chunk(bi, ci):
        r0 = pl.multiple_of(ci * C, C)
        rows = pl.ds(r0, C)
        q = qk_ref[bi, rows, 0:HK] * (DK ** -0.5)
        k = qk_ref[bi, rows, HK:2 * HK]
        v = v_ref[bi, rows, :]
        b = cumulative_decay(bi, rows)
        b_last = b[C - 1:C, :]
        st = st_ref[bi]

        q_in = q * jnp.exp(b)
        k_dec = (k * jnp.exp(b_last - b)).astype(BF16)

        a_off = [jnp.zeros((C, C), F32) for _ in range(H)]
        for j in range(C // SUB - 1):
            bj = b[(j + 1) * SUB - 1:(j + 1) * SUB, :]
            qj = q * jnp.exp(jnp.minimum(b - bj, 0.0))
            in_blk = (lax.broadcasted_iota(jnp.int32, (C, 1), 0) >> sub_shift) == j
            kj = jnp.where(in_blk, k * jnp.exp(jnp.minimum(bj - b, 0.0)), 0.0).astype(BF16)
            for h in range(H):
                a_off[h] = a_off[h] + _dot_nt(jnp.where(head_mask[h], qj, 0.0).astype(BF16), kj)

        for i in range(C // SUB):
            rs = slice(i * SUB, (i + 1) * SUB)
            q_sub, b_sub = q[rs, :], b[rs, :]
            for s in range(SUB):
                ks = k[i * SUB + s:i * SUB + s + 1, :]
                bs = b[i * SUB + s:i * SUB + s + 1, :]
                dterm = q_sub * ks * jnp.exp(jnp.minimum(b_sub - bs, 0.0))
                dterm = jnp.where(sub_t >= s, dterm, 0.0)
                d_scr[s * SUB:(s + 1) * SUB, :] = dterm.astype(BF16)
            gsum = _dot(d_scr[...], exp_ref[...])
            od = jnp.zeros((SUB, H * DV), F32)
            for s in range(SUB):
                vs = v[i * SUB + s:i * SUB + s + 1, :].astype(F32)
                od = od + gsum[s * SUB:(s + 1) * SUB, :] * vs
            o_ref[bi, pl.ds(r0 + i * SUB, SUB), :] = od

        outs = []
        for h in range(H):
            vh = v[:, h * DV:(h + 1) * DV]
            o_inter = _dot_nt(jnp.where(head_mask[h], q_in, 0.0).astype(BF16), st.astype(BF16))
            a_h = jnp.where(blk_lower, a_off[h], 0.0).astype(BF16)
            outs.append(o_inter + _dot(a_h, vh))
        o = o_ref[bi, rows, :] + jnp.concatenate(outs, axis=1)
        finish(bi, rows, o, v, k_dec, b_last, st)

    @pl.when(mild)
    def _():
        for ci in range(n_chunks):
            for bi in range(n_batch):
                chunk_mild(bi, ci)

    @pl.when(jnp.logical_not(mild))
    def _():
        for bi in range(n_batch):
            lax.fori_loop(0, n_chunks, lambda ci, c, bi=bi: (chunk(bi, ci), c)[1], 0)


def _gla(p32, p16, w2p, b2, norm_g, bsz, seq, ts):
    C, SUB, H, DK, DV = GLA_CHUNK, GLA_SUB, GLA_HEADS, GLA_DK, GLA_DV
    tokens = p32.shape[0]
    blocks_per_seq = seq // ts
    tril = jnp.asarray(np.tril(np.ones((C, C), np.float32))).astype(BF16)
    expand = np.zeros((H * DK, H * DV), np.float32)
    for h in range(H):
        expand[h * DK:(h + 1) * DK, h * DV:(h + 1) * DV] = 1.0
    expand = jnp.asarray(expand).astype(BF16)
    p32 = p32.reshape(bsz, seq, -1)
    p16 = p16.reshape(bsz, seq, -1)
    row_map = lambda cb: (lambda i: (0, i, cb))
    const2 = lambda i: (0, 0)
    out = pl.pallas_call(
        _gla_kernel,
        grid=(blocks_per_seq,),
        in_specs=[
            pl.BlockSpec((bsz, ts, 2 * H * DK), row_map(C32_GQ // (2 * H * DK))),
            pl.BlockSpec((bsz, ts, H * DV), row_map(C16_GV // (H * DV))),
            pl.BlockSpec((bsz, ts, H * DV), row_map(C32_R // (H * DV))),
            pl.BlockSpec((bsz, ts, LANES), row_map(C32_SMALL // LANES)),
            pl.BlockSpec(w2p.shape, const2),
            pl.BlockSpec(b2.shape, const2),
            pl.BlockSpec(norm_g.shape, const2),
            pl.BlockSpec(tril.shape, const2),
            pl.BlockSpec(expand.shape, const2),
        ],
        out_specs=pl.BlockSpec((bsz, ts, H * DV), lambda i: (0, i, 0)),
        out_shape=jax.ShapeDtypeStruct((bsz, seq, H * DV), F32),
        scratch_shapes=[pltpu.VMEM((bsz, DV, H * DK), F32), pltpu.VMEM((SUB * SUB, H * DK), BF16),
                        pltpu.VMEM((bsz, ts, H * DK), F32)],
        compiler_params=pltpu.CompilerParams(
            dimension_semantics=("arbitrary",), vmem_limit_bytes=VMEM_LIMIT),
        name="gla_branch",
    )(p32, p16, p32, p32, w2p, b2, norm_g, tril, expand)
    return out.reshape(tokens, H * DV)


def _bit_transpose32(words):
    a = list(words)
    shift, mask = 16, 0x0000FFFF
    while shift:
        m = jnp.int32(np.uint32(mask).astype(np.int32))
        for k in range(32):
            if k & shift == 0:
                t = (a[k] ^ lax.shift_right_logical(a[k + shift], jnp.int32(shift))) & m
                a[k] = a[k] ^ t
                a[k + shift] = a[k + shift] ^ lax.shift_left(t, jnp.int32(shift))
        shift >>= 1
        mask = (mask ^ (mask << shift)) & 0xFFFFFFFF
    return a


def _dsa_kernel(q_ref, iq_ref, iwt_ref, k_ref, vt_ref, ik_ref, bias_ref, ltri_ref,
                o_ref, key_scr, plane_scr, s_scr, acc_scr, *, topk):
    T, TK = DSA_QBLOCK, DSA_KTILE
    SUBS = TK // T
    H, HD = DSA_HEADS, DSA_HDIM
    qb = pl.program_id(1)
    kt_last = qb // SUBS
    n_kt = kt_last + 1
    key_minus_query = (lax.broadcasted_iota(jnp.int32, (TK, T), 0)
                       - lax.broadcasted_iota(jnp.int32, (TK, T), 1))

    def fold(x, op):
        return op(x.reshape(x.shape[0] // 8, 8, T), axis=0)

    def rows_to_one(x, op):
        return op(x, axis=0, keepdims=True)

    def key_rows(kt):
        return pl.ds(pl.multiple_of(kt * TK, TK), TK)

    def tile_loop(n, body, init):
        carry = lax.fori_loop(0, n // 2, lambda i, c: body(2 * i + 1, body(2 * i, c)), init)
        return lax.cond(n % 2 == 1, lambda c: body(n - 1, c), lambda c: c, carry)

    lo_mask = lax.broadcasted_iota(jnp.int32, (T, LANES), 1) < DSA_IDX_DIM
    iq = [iq_ref[:, 0:LANES], iq_ref[:, LANES:2 * LANES]]
    iq_h = [jnp.where(lo_mask, iq[0], 0), jnp.where(lo_mask, 0, iq[0]),
            jnp.where(lo_mask, iq[1], 0), jnp.where(lo_mask, 0, iq[1])]
    iq_all = jnp.concatenate(iq_h, axis=0)
    iw_h = [iwt_ref[h:h + 1, :] for h in range(DSA_IDX_HEADS)]

    GRP_ROWS = 32 * 8
    GRPS = TK // GRP_ROWS
    n_grp = plane_scr.shape[0]

    @pl.when(qb == 0)
    def _():
        plane_scr[...] = jnp.zeros_like(plane_scr)

    n_keys = key_scr.shape[0] * TK
    minus_one_minus_row = -1 - lax.broadcasted_iota(jnp.int32, (TK, T), 0)

    def score_tile(kt):
        ik2 = ik_ref[key_rows(kt), :]
        raw = _dot_nt(ik2, iq_all)
        score = jnp.zeros((TK, T), F32)
        for h in range(DSA_IDX_HEADS):
            score = score + iw_h[h] * jnp.maximum(raw[:, h * T:(h + 1) * T], 0.0)
        bits = lax.bitcast_convert_type(score, jnp.int32)
        zero_key = minus_one_minus_row - kt * TK
        negative_key = (bits ^ jnp.int32(0x7FFFFFFF)) - n_keys
        return jnp.where(score == 0.0, zero_key, jnp.where(bits >= 0, bits, negative_key))

    def store_planes(kt, key):
        unsigned_order = key ^ jnp.int32(INT_MIN)
        for g in range(GRPS):
            words = [unsigned_order[g * GRP_ROWS + 8 * j:g * GRP_ROWS + 8 * j + 8, :] for j in range(32)]
            planes = _bit_transpose32(words)
            for b in range(32):
                plane_scr[kt * GRPS + g, b] = planes[b]

    def score_body(kt, c):
        key = score_tile(kt)
        key_scr[kt] = key
        store_planes(kt, key)
        return c

    tile_loop(kt_last, score_body, 0)
    admissible = key_minus_query <= qb * T - kt_last * TK
    key = score_tile(kt_last)
    key_scr[kt_last] = jnp.where(admissible, key, jnp.int32(INT_MIN))
    store_planes(kt_last, key)
    alive_last = []
    for g in range(GRPS):
        word = jnp.zeros((8, T), jnp.int32)
        for j in range(32):
            adm = admissible[g * GRP_ROWS + 8 * j:g * GRP_ROWS + 8 * j + 8, :]
            word = word | jnp.where(adm, jnp.int32(np.int32(np.uint32(1 << (31 - j)))), 0)
        alive_last.append(word)

    alive = []
    for g in range(n_grp):
        kt = g // GRPS
        full = jnp.broadcast_to(jnp.where(kt < kt_last, jnp.int32(-1), jnp.int32(0)), (8, T))
        alive.append(jnp.where(kt == kt_last, alive_last[g % GRPS], full))

    def bits_body(i, carry):
        above, tau_u, alive = carry
        n_used = len(alive)
        hi_planes = [plane_scr[g, 2 * i] for g in range(n_used)]
        lo_planes = [plane_scr[g, 2 * i + 1] for g in range(n_used)]
        n11 = n10 = n01 = jnp.zeros((8, T), jnp.int32)
        for g in range(n_used):
            with_hi = alive[g] & hi_planes[g]
            both = with_hi & lo_planes[g]
            n11 = n11 + lax.population_count(both)
            n10 = n10 + lax.population_count(with_hi ^ both)
            n01 = n01 + lax.population_count((alive[g] & lo_planes[g]) ^ both)
        c11 = above + rows_to_one(n11, jnp.sum)
        c10 = c11 + rows_to_one(n10, jnp.sum)
        c01 = c10 + rows_to_one(n01, jnp.sum)
        is11, is10, is01 = c11 >= topk, c10 >= topk, c01 >= topk
        hi_bit = is10
        lo_bit = is11 | (is01 & ~is10)
        above = jnp.where(is11, above, jnp.where(is10, c11, jnp.where(is01, c10, c01)))
        flip_hi = jnp.where(hi_bit, jnp.int32(0), jnp.int32(-1))
        flip_lo = jnp.where(lo_bit, jnp.int32(0), jnp.int32(-1))
        alive = tuple(alive[g] & (hi_planes[g] ^ flip_hi) & (lo_planes[g] ^ flip_lo)
                      for g in range(n_used))
        tau_u = (tau_u | jnp.where(hi_bit, lax.shift_left(jnp.int32(1), 31 - 2 * i), 0)
                 | jnp.where(lo_bit, lax.shift_left(jnp.int32(1), 30 - 2 * i), 0))
        return above, tau_u, alive

    def select(n_used):
        def run():
            zero_row = jnp.zeros((1, T), jnp.int32)
            above, tau_u, left = lax.fori_loop(0, 16, bits_body,
                                               (zero_row, zero_row, tuple(alive[:n_used])))
            n_tied = jnp.zeros((8, T), jnp.int32)
            for g in range(n_used):
                n_tied = n_tied + lax.population_count(left[g])
            return above, tau_u, rows_to_one(n_tied, jnp.sum)
        return run

    half = n_grp // 2
    if half >= GRPS and half % GRPS == 0:
        above, tau_u, n_tied = lax.cond((kt_last + 1) * GRPS <= half, select(half), select(n_grp))
    else:
        above, tau_u, n_tied = select(n_grp)()
    n_ge = above + n_tied
    tau = jnp.maximum(tau_u ^ jnp.int32(INT_MIN), jnp.int32(INT_MIN + 1))

    @pl.when(jnp.max(n_ge) > topk)
    def _():
        need = (topk - above).astype(F32)

        def tie_body(kt, seen):
            for j in range(SUBS):
                rows = slice(j * T, (j + 1) * T)
                key = key_scr[kt, rows, :]
                eq = key == tau
                eq_f = jnp.where(eq, 1.0, 0.0)
                pref = _dot(ltri_ref[...], eq_f.astype(BF16)) + seen
                key_scr[kt, rows, :] = jnp.where(eq & (pref > need), tau - 1, key)
                seen = seen + rows_to_one(eq_f, jnp.sum)
            return seen

        tile_loop(n_kt, tie_body, jnp.zeros((1, T), F32))

    q_h = [q_ref[:, h * HD:(h + 1) * HD] for h in range(H)]

    def logits_body(kt, m_run):
        sel = key_scr[kt] >= tau
        kinds = [jnp.clip(qb - (kt * SUBS + j), 0, 2) for j in range(SUBS)]
        new_m = []
        for h in range(H):
            bias = jnp.concatenate([bias_ref[kinds[j], h] for j in range(SUBS)], axis=0)
            s = _dot_nt(k_ref[key_rows(kt), h * HD:(h + 1) * HD], q_h[h]) + bias
            s = jnp.where(sel, s, NEG_BIG)
            s_scr[h, kt] = s
            new_m.append(jnp.maximum(m_run[h], fold(s, jnp.max)))
        return tuple(new_m)

    m_run = tile_loop(n_kt, logits_body, tuple(jnp.full((8, T), NEG_BIG, F32) for _ in range(H)))
    m_h = [rows_to_one(m, jnp.max) for m in m_run]

    for h in range(H):
        acc_scr[h] = jnp.zeros((HD, T), F32)

    def pv_body(kt, l_run):
        new_l = []
        for h in range(H):
            p = jnp.exp(s_scr[h, kt] - m_h[h])
            new_l.append(l_run[h] + fold(p, jnp.sum))
            acc_scr[h] = acc_scr[h] + _dot(vt_ref[kt, h * HD:(h + 1) * HD, :], p.astype(BF16))
        return tuple(new_l)

    l_run = tile_loop(n_kt, pv_body, tuple(jnp.zeros((8, T), F32) for _ in range(H)))

    for h in range(H):
        out_t = acc_scr[h] / rows_to_one(l_run[h], jnp.sum)
        o_ref[:, h * HD:(h + 1) * HD] = out_t.T.astype(o_ref.dtype)


def _dsa(p16, v_t, iw_t, bias_tiles, bsz, seq):
    T = DSA_QBLOCK
    tokens = p16.shape[0]
    nqb = seq // T
    topk = min(DSA_TOPK_MAX, seq // 4)
    TK = min(DSA_KTILE, seq)
    assert TK == DSA_KTILE and seq % TK == 0
    nkt = seq // TK
    ltri = jnp.asarray(np.tril(np.ones((T, T), np.float32))).astype(BF16)
    w = D_BRANCH
    v_t = v_t.reshape(bsz, nkt, w, TK)
    qmap = lambda cb: (lambda b, i: (b * nqb + i, cb))
    return pl.pallas_call(
        functools.partial(_dsa_kernel, topk=topk),
        grid=(bsz, nqb),
        in_specs=[
            pl.BlockSpec((T, w), qmap(C16_Q // w)),
            pl.BlockSpec((T, 2 * LANES), qmap(C16_IQ // (2 * LANES))),
            pl.BlockSpec((8, T), lambda b, i: (0, b * nqb + i)),
            pl.BlockSpec((seq, w), lambda b, i: (b, C16_K // w)),
            pl.BlockSpec((None, nkt, w, TK), lambda b, i: (b, 0, 0, 0)),
            pl.BlockSpec((seq, LANES), lambda b, i: (b, C16_IK // LANES)),
            pl.BlockSpec(bias_tiles.shape, lambda b, i: (0, 0, 0, 0)),
            pl.BlockSpec(ltri.shape, lambda b, i: (0, 0)),
        ],
        out_specs=pl.BlockSpec((T, w), lambda b, i: (b * nqb + i, 0)),
        out_shape=jax.ShapeDtypeStruct((tokens, w), BF16),
        scratch_shapes=[
            pltpu.VMEM((nkt, TK, T), jnp.int32),
            pltpu.VMEM((nkt * TK // 256, 32, 8, T), jnp.int32),
            pltpu.VMEM((DSA_HEADS, nkt, TK, T), F32),
            pltpu.VMEM((DSA_HEADS, DSA_HDIM, T), F32),
        ],
        compiler_params=pltpu.CompilerParams(
            dimension_semantics=("parallel", "arbitrary"), vmem_limit_bytes=VMEM_LIMIT),
        name="dsa_branch",
    )(p16, p16, iw_t, p16, v_t, p16, bias_tiles, ltri)


def _merge_kernel(u_ref, v_ref, ws_ref, bs_ref, yb_ref, yc_ref, ga_ref, gb_ref, gc_ref, x_ref, mod_ref,
                  wbr_ref, wout_ref, gn2_ref, wr_hi_ref, wr_lo_ref, br_ref, x1_ref, h2_ref, comb_ref):
    tm = u_ref.shape[0]
    ya_rows = []
    for c in range(tm // GMLP_CHUNK):
        rows = slice(c * GMLP_CHUNK, (c + 1) * GMLP_CHUNK)
        groups = []
        for g in range(GMLP_GROUPS):
            cols = slice(g * LANES, (g + 1) * LANES)
            mixed = _dot(ws_ref[g], v_ref[rows, cols].astype(BF16)) + bs_ref[:, cols]
            groups.append((u_ref[rows, cols] * mixed).astype(BF16))
        ya_rows.append(jnp.concatenate(groups, axis=1))
    ya = jnp.concatenate(ya_rows, axis=0)

    merged = ga_ref[...] * _dot(ya, wbr_ref[0])
    merged = merged + gb_ref[...] * _dot(yb_ref[...].astype(BF16), wbr_ref[1])
    merged = merged + gc_ref[...] * _dot(yc_ref[...], wbr_ref[2])
    mix = _dot(merged.astype(BF16), wout_ref[...])
    x1 = x_ref[...] + mod_ref[0, 2:3, :] * mix
    x1_ref[...] = x1
    y = x1 * lax.rsqrt(jnp.mean(x1 * x1, axis=-1, keepdims=True) + EPS) * gn2_ref[...]
    h2 = y * (1.0 + mod_ref[0, 4:5, :]) + mod_ref[0, 3:4, :]
    h2_ref[...] = h2.astype(BF16)
    logits = _dot3(h2, wr_hi_ref[...], wr_lo_ref[...]) + br_ref[...]
    comb_ref[...] = _route_packed(logits).astype(BF16)


def _merge(p32, w_tril, gmlp_bias, yb, yc, p16, x2d, mod_l, wbr, wout, layer, gn2, wr_hi, wr_lo, br,
           seq, tm):
    tokens, d = x2d.shape
    tiles_per_seq = seq // tm
    row = lambda i: (i, 0)
    gate = lambda k: (lambda i: (i, C16_GATES // d + k))
    c2 = lambda i: (0, 0)
    of_layer = lambda a: pl.BlockSpec((None,) + a.shape[1:], lambda i: (layer,) + (0,) * (a.ndim - 1))
    return pl.pallas_call(
        _merge_kernel,
        grid=(tokens // tm,),
        in_specs=[
            pl.BlockSpec((tm, D_BRANCH), lambda i: (i, C32_U // D_BRANCH)),
            pl.BlockSpec((tm, D_BRANCH), lambda i: (i, C32_V // D_BRANCH)),
            of_layer(w_tril), of_layer(gmlp_bias),
            pl.BlockSpec((tm, D_BRANCH), row), pl.BlockSpec((tm, D_BRANCH), row),
            pl.BlockSpec((tm, d), gate(0)), pl.BlockSpec((tm, d), gate(1)), pl.BlockSpec((tm, d), gate(2)),
            pl.BlockSpec((tm, d), row),
            pl.BlockSpec((1, 6, d), lambda i: (i // tiles_per_seq, 0, 0)),
            of_layer(wbr), of_layer(wout),
            pl.BlockSpec(gn2.shape, c2),
            of_layer(wr_hi), of_layer(wr_lo), of_layer(br),
        ],
        out_specs=[pl.BlockSpec((tm, d), row), pl.BlockSpec((tm, d), row),
                   pl.BlockSpec((tm, LANES), row)],
        out_shape=[jax.ShapeDtypeStruct((tokens, d), F32), jax.ShapeDtypeStruct((tokens, d), BF16),
                   jax.ShapeDtypeStruct((tokens, LANES), BF16)],
        compiler_params=pltpu.CompilerParams(
            dimension_semantics=("parallel",), vmem_limit_bytes=VMEM_LIMIT),
        name="gmlp_merge_norm_router",
    )(p32, p32, w_tril, gmlp_bias, yb, yc, p16, p16, p16, x2d, mod_l, wbr, wout, gn2, wr_hi, wr_lo, br)


def _route_packed(lg):
    n_rows = lg.shape[0]
    t = lg.T[0:MOE_PART_LANES, :]
    idx = lax.broadcasted_iota(jnp.int32, t.shape, 0)
    big = jnp.int32(10 ** 6)
    top = lambda x, op: op(x, axis=0, keepdims=True)
    is_grp = idx < N_GROUPS
    gl = jnp.where(is_grp, t, -jnp.inf)
    gmax = top(gl, jnp.max)
    p_g = 1.0 / top(jnp.where(is_grp, jnp.exp(t - gmax), 0.0), jnp.sum)
    g_idx = top(jnp.where(gl == gmax, idx, big), jnp.min)
    first = N_GROUPS + g_idx * EXPERTS_PER_GROUP
    in_grp = (idx >= first) & (idx < first + EXPERTS_PER_GROUP)
    e1 = jnp.where(in_grp, t, -jnp.inf)
    v1 = top(e1, jnp.max)
    i1 = top(jnp.where(e1 == v1, idx, big), jnp.min)
    e2 = jnp.where(in_grp & (idx != i1), t, -jnp.inf)
    v2 = top(e2, jnp.max)
    i2 = top(jnp.where(e2 == v2, idx, big), jnp.min)
    r = jnp.exp(v2 - v1)
    w1 = p_g * (1.0 / (1.0 + r))
    w2 = p_g * (r / (1.0 + r))
    comb = jnp.where(idx == i1, w1, jnp.where(idx == i2, w2, 0.0))
    hi = comb.astype(BF16).astype(F32)
    mid = (comb - hi).astype(BF16).astype(F32)
    lo = ((comb - hi) - mid).astype(BF16).astype(F32)
    first_part = jnp.where(idx == g_idx, 1.0, 0.0) + hi
    pad = jnp.zeros((LANES - 3 * MOE_PART_LANES, n_rows), F32)
    return jnp.concatenate([first_part, mid, lo, pad], axis=0).T


def _moe_kernel(h_ref, comb_ref, x1_ref, mod_ref, wg_ref, wu_ref, wd_ref, tril_ref, o_ref,
                hp_scr, cwp_scr, yp_scr, pt_scr, start_ref, nchunk_ref):
    grp = pl.program_id(1)
    tm = h_ref.shape[0]
    n_sorted = hp_scr.shape[0]
    lane = lax.broadcasted_iota(jnp.int32, (1, LANES), 1)

    @pl.when(grp == 0)
    def _():
        onehot = jnp.where(lane < N_GROUPS, comb_ref[...], 0)
        incl = _dot(tril_ref[...], onehot)
        counts = incl[tm - 1:tm, :]
        start = jnp.int32(0)
        start_row = jnp.zeros((1, LANES), F32)
        for g in range(N_GROUPS):
            cnt = jnp.sum(jnp.where(lane == g, counts, 0.0)).astype(jnp.int32)
            start_ref[g] = start
            nchunk_ref[g] = lax.shift_right_logical(cnt + (MOE_CHUNK - 1), MOE_CHUNK.bit_length() - 1)
            start_row = jnp.where(lane == g, start.astype(F32), start_row)
            align = MOE_ALIGN.bit_length() - 1
            start = start + lax.shift_left(lax.shift_right_logical(cnt + (MOE_ALIGN - 1), align), align)
        dest = jnp.sum(onehot.astype(F32) * (incl - 1.0 + start_row), axis=1, keepdims=True)
        dest_col = dest.astype(jnp.int32)
        dest_row = jnp.broadcast_to(dest, (tm, LANES)).T[0:1, :].astype(jnp.int32)
        back = lax.broadcasted_iota(jnp.int32, (tm, n_sorted), 1) == dest_col
        pt_scr[...] = jnp.where(back, 1.0, 0.0).astype(BF16)
        fwd = lax.broadcasted_iota(jnp.int32, (n_sorted, tm), 0) == dest_row
        fwd = jnp.where(fwd, 1.0, 0.0).astype(BF16)
        hp_scr[...] = _dot(fwd, h_ref[...]).astype(BF16)
        cwp_scr[...] = _dot(fwd, comb_ref[...]).astype(BF16)
        yp_scr[...] = jnp.zeros_like(yp_scr)

    row_id = lax.broadcasted_iota(jnp.int32, (LANES, LANES), 0)
    start = start_ref[grp]

    def chunk(c, carry):
        rows = pl.ds(pl.multiple_of(start + c * MOE_CHUNK, MOE_ALIGN), MOE_CHUNK)
        h = hp_scr[rows, :]
        acts = []
        for j in range(EXPERTS_PER_GROUP):
            lane_of_expert = N_GROUPS + grp * EXPERTS_PER_GROUP + j
            pick = ((row_id == lane_of_expert) | (row_id == lane_of_expert + MOE_PART_LANES)
                    | (row_id == lane_of_expert + 2 * MOE_PART_LANES))
            cw = _dot(cwp_scr[rows, :], jnp.where(pick, 1.0, 0.0).astype(BF16))
            act = (jax.nn.silu(_dot(h, wg_ref[j])) * _dot(h, wu_ref[j])
                   * jnp.concatenate([cw, cw], axis=1))
            acts.append(act.astype(BF16))
        yp_scr[rows, :] = _dot(jnp.concatenate(acts, axis=1), wd_ref[...]).astype(BF16)
        return carry

    n_chunks = nchunk_ref[grp]
    lax.fori_loop(0, n_chunks // 2, lambda i, c: chunk(2 * i + 1, chunk(2 * i, c)), 0)

    @pl.when(n_chunks % 2 == 1)
    def _():
        chunk(n_chunks - 1, 0)

    @pl.when(grp == pl.num_programs(1) - 1)
    def _():
        o_ref[...] = x1_ref[...] + mod_ref[0, 5:6, :] * _dot(pt_scr[...], yp_scr[...])


def _moe(h2, comb, x1, mod_l, wg, wu, wd, layer, seq, tm):
    tokens, d = x1.shape
    tiles_per_seq = seq // tm
    n_sorted = tm + N_GROUPS * MOE_ALIGN + MOE_CHUNK
    n_sorted = LANES * pl.cdiv(n_sorted, LANES)
    tril = jnp.asarray(np.tril(np.ones((tm, tm), np.float32))).astype(BF16)
    row = lambda i, g: (i, 0)
    return pl.pallas_call(
        _moe_kernel,
        grid=(tokens // tm, N_GROUPS),
        in_specs=[
            pl.BlockSpec((tm, d), row), pl.BlockSpec((tm, LANES), row),
            pl.BlockSpec((tm, d), row),
            pl.BlockSpec((1, 6, d), lambda i, g: (i // tiles_per_seq, 0, 0)),
            pl.BlockSpec((None, None) + wg.shape[2:], lambda i, g: (layer, g, 0, 0, 0)),
            pl.BlockSpec((None, None) + wu.shape[2:], lambda i, g: (layer, g, 0, 0, 0)),
            pl.BlockSpec((None, None) + wd.shape[2:], lambda i, g: (layer, g, 0, 0)),
            pl.BlockSpec(tril.shape, lambda i, g: (0, 0), pipeline_mode=pl.Buffered(1)),
        ],
        out_specs=pl.BlockSpec((tm, d), row),
        out_shape=jax.ShapeDtypeStruct((tokens, d), F32),
        scratch_shapes=[pltpu.VMEM((n_sorted, d), BF16), pltpu.VMEM((n_sorted, LANES), BF16),
                        pltpu.VMEM((n_sorted, d), BF16), pltpu.VMEM((tm, n_sorted), BF16),
                        pltpu.SMEM((N_GROUPS,), jnp.int32), pltpu.SMEM((N_GROUPS,), jnp.int32)],
        compiler_params=pltpu.CompilerParams(
            dimension_semantics=("parallel", "arbitrary"), vmem_limit_bytes=VMEM_LIMIT),
        name="hier_moe",
    )(h2, comb, x1, mod_l, wg, wu, wd, tril)


def _row_tile(seq, want):
    t = min(want, seq)
    assert seq % t == 0
    return t


def kernel(x, c, w_mod, b_mod, g_norm1, g_norm2, w_in, gmlp_ln_g, gmlp_ln_b, gmlp_w_s, gmlp_b_s,
           gla_w_gate2, gla_b_gate, gla_norm_g, dsa_qnorm_g, dsa_knorm_g, rel_bias, w_branch,
           b_branch_gate, w_out, w_group, b_group, w_router, b_router, w_exp_gate, w_exp_up,
           w_exp_down):
    bsz, seq, d = x.shape
    depth = w_mod.shape[0]
    assert d == D_MODEL and seq % DSA_QBLOCK == 0 and seq % GLA_CHUNK == 0
    tokens = bsz * seq
    tm = _row_tile(seq, 512)

    mods = _modulation(c, w_mod, b_mod)
    bias_tiles = _bias_tiles(rel_bias)
    w32, w16 = _prep_w_in(w_in)

    causal = np.tril(np.ones((GMLP_CHUNK, GMLP_CHUNK), bool))
    w_tril = jnp.where(causal[None, None], gmlp_w_s, 0.0).astype(BF16)
    gmlp_bias = jnp.repeat(jnp.swapaxes(gmlp_b_s, 1, 2), LANES, axis=-1)
    w2p = jnp.zeros((depth, LANES, GLA_HEADS * GLA_DK), F32).at[:, :GLA_GATE_RANK].set(gla_w_gate2)
    w2p = w2p.astype(BF16)
    wr = jnp.zeros((depth, d, LANES), F32)
    wr = wr.at[:, :, :N_GROUPS].set(w_group).at[:, :, N_GROUPS:N_GROUPS + N_EXPERTS].set(w_router)
    wr_hi = wr.astype(BF16)
    wr_lo = (wr - wr_hi.astype(F32)).astype(BF16)
    br = jnp.zeros((depth, 1, LANES), F32)
    br = br.at[:, 0, :N_GROUPS].set(b_group).at[:, 0, N_GROUPS:N_GROUPS + N_EXPERTS].set(b_router)
    wbr = w_branch.astype(BF16)
    wout = w_out.astype(BF16)
    grouped = (depth, N_GROUPS, EXPERTS_PER_GROUP, d, D_EXPERT)
    wg, wu = w_exp_gate.astype(BF16).reshape(grouped), w_exp_up.astype(BF16).reshape(grouped)
    wd = w_exp_down.astype(BF16).reshape(depth, N_GROUPS, EXPERTS_PER_GROUP * D_EXPERT, d)
    tm_moe = _row_tile(seq, 1024)

    x2d = x.reshape(tokens, d)
    for l in range(depth):
        mod_l = mods[l]
        aux32 = [gmlp_ln_g[l][None], gmlp_ln_b[l][None]]
        aux16 = [dsa_qnorm_g[l][None], dsa_knorm_g[l][None], b_branch_gate[l].reshape(1, -1)]
        p32, p16, v_t, iw_t = _norm_proj(x2d, mod_l, g_norm1[l][None], w32, w16, l, aux32, aux16,
                                         seq, tm)
        yb = _gla(p32, p16, w2p[l], gla_b_gate[l][None], gla_norm_g[l][None], bsz, seq, tm)
        yc = _dsa(p16, v_t, iw_t, bias_tiles, bsz, seq)
        x1, h2, comb = _merge(p32, w_tril, gmlp_bias, yb, yc, p16, x2d, mod_l, wbr, wout, l,
                              g_norm2[l][None], wr_hi, wr_lo, br, seq, tm)
        x2d = _moe(h2, comb, x1, mod_l, wg, wu, wd, l, seq, tm_moe)
    return x2d.reshape(bsz, seq, d)
```

```python
import functools
import math

import numpy as np
import jax
import jax.numpy as jnp
from jax import lax
from jax.experimental import pallas as pl
from jax.experimental.pallas import tpu as pltpu

F32 = jnp.float32
BF16 = jnp.bfloat16

D_MODEL = 1024
D_BRANCH = 512
EPS = 1e-6
GMLP_CHUNK = 128
GMLP_GROUPS = 4
GLA_HEADS = 4
GLA_DK = 64
GLA_DV = 128
GLA_GATE_RANK = 16
GLA_GATE_TAU = 16.0
GLA_CHUNK = 128
GLA_SUB = 16
GLA_MILD_DECAY = -60.0
DSA_HEADS = 4
DSA_HDIM = 128
DSA_IDX_HEADS = 4
DSA_IDX_DIM = 64
DSA_QBLOCK = 128
DSA_KTILE = 512
DSA_TOPK_MAX = 256
N_BUCKETS = 32
MAX_DISTANCE = 128
N_GROUPS = 4
EXPERTS_PER_GROUP = 4
N_EXPERTS = 16
D_EXPERT = 256
MOE_CHUNK = 128
MOE_ALIGN = 16
MOE_PART_LANES = 32

LANES = 128
COL_TILE = 512
VMEM_LIMIT = 56 * 1024 * 1024
INT_MIN = -(2 ** 31)
LOG2_E = math.log2(math.e)
NEG_BIG = -1e30

C32_U, C32_V, C32_GQ, C32_GK, C32_R, C32_SMALL = 0, 512, 1024, 1280, 1536, 2048
N32 = C32_SMALL + LANES
C16_GATES, C16_Q, C16_K, C16_GV, C16_IQ, C16_IK = 0, 3072, 3584, 4096, 4608, 4864
N16 = C16_IK + LANES


def _dot(a, b):
    return jnp.dot(a, b, preferred_element_type=F32)


def _dot_nt(a, b):
    return lax.dot_general(a, b, (((1,), (1,)), ((), ())), preferred_element_type=F32)


def _dot_tn(a, b):
    return lax.dot_general(a, b, (((0,), (0,)), ((), ())), preferred_element_type=F32)


def _split2(a):
    hi = a.astype(BF16)
    lo = (a - hi.astype(F32)).astype(BF16)
    return hi, lo


def _dot3(a, w_hi, w_lo):
    a_hi, a_lo = _split2(a)
    return _dot(a_hi, w_hi) + (_dot(a_lo, w_hi) + _dot(a_hi, w_lo))


def _head_rms(y, g, scale):
    outs = []
    for h in range(y.shape[1] // LANES):
        yh = y[:, h * LANES:(h + 1) * LANES]
        ms = jnp.mean(yh * yh, axis=-1, keepdims=True)
        o = yh * lax.rsqrt(ms + EPS) * g
        if scale != 1.0:
            o = o * scale
        outs.append(o)
    return jnp.concatenate(outs, axis=1)


def _mod_kernel(c_ref, w_ref, b_ref, o_ref):
    a = jax.nn.silu(c_ref[...])
    w_hi, w_lo = _split2(w_ref[...])
    o_ref[...] = _dot3(a, w_hi, w_lo) + b_ref[...]


def _modulation(c, w_mod, b_mod):
    L, d, n = w_mod.shape
    bsz = c.shape[0]
    rows = 8 * pl.cdiv(bsz, 8)
    c_pad = jnp.zeros((rows, d), F32).at[:bsz].set(c)
    tn = 1536
    out = pl.pallas_call(
        _mod_kernel,
        grid=(L, n // tn),
        in_specs=[
            pl.BlockSpec((rows, d), lambda l, j: (0, 0)),
            pl.BlockSpec((None, d, tn), lambda l, j: (l, 0, j)),
            pl.BlockSpec((None, 1, tn), lambda l, j: (l, 0, j)),
        ],
        out_specs=pl.BlockSpec((None, rows, tn), lambda l, j: (l, 0, j)),
        out_shape=jax.ShapeDtypeStruct((L, rows, n), F32),
        compiler_params=pltpu.CompilerParams(
            dimension_semantics=("arbitrary", "arbitrary"), vmem_limit_bytes=VMEM_LIMIT),
        name="adaln_modulation",
    )(c_pad, w_mod, b_mod.reshape(L, 1, n))
    return out[:, :bsz].reshape(L, bsz, 6, d)


def _t5_bucket_table():
    n = np.arange(2 * DSA_QBLOCK)
    max_exact = N_BUCKETS // 2
    large = max_exact + (
        np.log(np.maximum(n, max_exact).astype(np.float32) / max_exact)
        / math.log(MAX_DISTANCE / max_exact) * (N_BUCKETS - max_exact)).astype(np.int32)
    large = np.minimum(large, N_BUCKETS - 1)
    return np.where(n < max_exact, n, large).astype(np.int32)


def _bias_kernel(rb_ref, bucket_ref, o_ref):
    for t in range(bucket_ref.shape[0]):
        bucket = bucket_ref[t]
        for h in range(DSA_HEADS):
            acc = jnp.zeros(bucket.shape, F32)
            for b in range(N_BUCKETS):
                acc = jnp.where(bucket == b, rb_ref[b, h], acc)
            o_ref[t, h] = acc * LOG2_E


def _bias_tiles(rel_bias):
    table = _t5_bucket_table()
    assert (table[MAX_DISTANCE:] == N_BUCKETS - 1).all()
    t = np.arange(DSA_QBLOCK)[None, :]
    s = np.arange(DSA_QBLOCK)[:, None]
    diag = table[np.maximum(t - s, 0)]
    near = table[DSA_QBLOCK + t - s]
    far = np.full_like(diag, N_BUCKETS - 1)
    buckets = jnp.asarray(np.stack([diag, near, far]).astype(np.int32))
    return pl.pallas_call(
        _bias_kernel,
        in_specs=[pl.BlockSpec(memory_space=pltpu.SMEM), pl.BlockSpec(memory_space=pltpu.VMEM)],
        out_specs=pl.BlockSpec(memory_space=pltpu.VMEM),
        out_shape=jax.ShapeDtypeStruct((3, DSA_HEADS, DSA_QBLOCK, DSA_QBLOCK), F32),
        name="t5_bias_tiles",
    )(rel_bias, buckets)


def _proj_kernel(x_ref, mod_ref, gn_ref, w32_ref, w16_ref, *rest):
    aux32, aux16 = rest[:2], rest[2:5]
    o32_ref, o16_ref, vt_ref, iwt_ref = rest[5:9]
    x = x_ref[...]
    y = x * lax.rsqrt(jnp.mean(x * x, axis=-1, keepdims=True) + EPS) * gn_ref[...]
    h = (y * (1.0 + mod_ref[0, 1:2, :]) + mod_ref[0, 0:1, :]).astype(BF16)
    for w_ref, o_ref, epilogues, aux in ((w32_ref, o32_ref, EPILOGUES_32, aux32),
                                         (w16_ref, o16_ref, EPILOGUES_16, aux16)):
        wcol = ocol = 0
        for width, epi in epilogues:
            y = epi(_dot(h, w_ref[:, wcol:wcol + width]), aux)
            wcol += width
            if epi is _epi_value_t:
                vt_ref[0] = y.T.astype(vt_ref.dtype)
                continue
            if epi is _epi_small:
                iwt_ref[...] = y.T[GLA_GATE_RANK:GLA_GATE_RANK + 8, :]
            o_ref[:, ocol:ocol + width] = y.astype(o_ref.dtype)
            ocol += width
        assert wcol == w_ref.shape[1] and ocol == o_ref.shape[1]


def _norm_proj(x2d, mod_l, gn, w32, w16, layer, aux32, aux16, seq, tm):
    tokens, d = x2d.shape
    tiles_per_seq = seq // tm
    assert tm == DSA_KTILE
    const = lambda a: pl.BlockSpec(a.shape, lambda i: (0, 0))
    resident = lambda a: pl.BlockSpec((None,) + a.shape[1:], lambda i: (layer, 0, 0),
                                      pipeline_mode=pl.Buffered(1))
    return pl.pallas_call(
        _proj_kernel,
        grid=(tokens // tm,),
        in_specs=[
            pl.BlockSpec((tm, d), lambda i: (i, 0)),
            pl.BlockSpec((1, 6, d), lambda i: (i // tiles_per_seq, 0, 0)),
            const(gn), resident(w32), resident(w16),
        ] + [const(a) for a in aux32 + aux16],
        out_specs=[pl.BlockSpec((tm, N32), lambda i: (i, 0)), pl.BlockSpec((tm, N16), lambda i: (i, 0)),
                   pl.BlockSpec((1, D_BRANCH, tm), lambda i: (i, 0, 0)),
                   pl.BlockSpec((8, tm), lambda i: (0, i))],
        out_shape=[jax.ShapeDtypeStruct((tokens, N32), F32), jax.ShapeDtypeStruct((tokens, N16), BF16),
                   jax.ShapeDtypeStruct((tokens // tm, D_BRANCH, tm), BF16),
                   jax.ShapeDtypeStruct((8, tokens), F32)],
        compiler_params=pltpu.CompilerParams(
            dimension_semantics=("parallel",), vmem_limit_bytes=VMEM_LIMIT),
        name="norm_proj",
    )(x2d, mod_l, gn, w32, w16, *aux32, *aux16)


def _epi_raw(y, aux):
    return y


def _epi_gelu(y, aux):
    return jax.nn.gelu(y)


def _epi_gelu_ln(y, aux):
    v = jax.nn.gelu(y)
    mu = jnp.mean(v, axis=-1, keepdims=True)
    var = jnp.mean(jnp.square(v - mu), axis=-1, keepdims=True)
    return (v - mu) * lax.rsqrt(var + EPS) * aux[0][...] + aux[1][...]


def _epi_silu(y, aux):
    return jax.nn.silu(y)


def _epi_gate(k):
    def epi(y, aux):
        return jax.nn.sigmoid(y + aux[2][:, k * COL_TILE:(k + 1) * COL_TILE])
    return epi


def _epi_qnorm(y, aux):
    return _head_rms(y, aux[0][...], DSA_HDIM ** -0.5 * LOG2_E)


def _epi_knorm(y, aux):
    return _head_rms(y, aux[1][...], 1.0)


def _epi_value_t(y, aux):
    return y


def _epi_small(y, aux):
    return y


EPILOGUES_32 = ([(COL_TILE, e) for e in (_epi_gelu, _epi_gelu_ln, _epi_raw, _epi_silu)]
                + [(LANES, _epi_small)])
EPILOGUES_16 = ([(COL_TILE, _epi_gate(k)) for k in range(6)]
                + [(COL_TILE, e) for e in (_epi_qnorm, _epi_knorm, _epi_value_t, _epi_raw)]
                + [(2 * LANES, _epi_raw), (LANES, _epi_raw)])


def _prep_w_in(w_in):
    sizes = (512, 512, 256, 256, 512, 512, 16, 512, 512, 512, 256, 64, 4, 3072)
    offs = np.concatenate([[0], np.cumsum(sizes)])
    seg = lambda k: w_in[:, :, offs[k]:offs[k + 1]]
    (a_u, a_v, g_q, g_k, g_v, g_r, g_a, d_q, d_k, d_v, d_iq, d_ik, d_iw, gates) = [seg(k) for k in range(14)]
    L, d, _ = w_in.shape
    zeros = lambda n: jnp.zeros((L, d, n), w_in.dtype)
    w32 = jnp.concatenate([a_u, a_v, g_q, g_k, g_r, g_a, d_iw,
                           zeros(N32 - C32_SMALL - 20)], axis=-1).astype(BF16)
    w16 = jnp.concatenate([gates, d_q, d_k, d_v, g_v, d_iq, d_ik, d_ik], axis=-1).astype(BF16)
    return w32, w16


def _gla_kernel(qk_ref, v_ref, r_ref, ga_ref, w2_ref, b2_ref, ng_ref, tril_ref, exp_ref,
                o_ref, st_ref, d_scr, g_scr):
    C, SUB, H, DK, DV = GLA_CHUNK, GLA_SUB, GLA_HEADS, GLA_DK, GLA_DV
    HK = H * DK
    n_batch = qk_ref.shape[0]
    n_chunks = qk_ref.shape[1] // C

    @pl.when(pl.program_id(0) == 0)
    def _():
        st_ref[...] = jnp.zeros_like(st_ref)

    lane = lax.broadcasted_iota(jnp.int32, (1, HK), 1)
    head_mask = [(lane >= h * DK) & (lane < (h + 1) * DK) for h in range(H)]
    row = lax.broadcasted_iota(jnp.int32, (C, C), 0)
    col = lax.broadcasted_iota(jnp.int32, (C, C), 1)
    sub_shift = SUB.bit_length() - 1
    blk_lower = (row >> sub_shift) > (col >> sub_shift)
    sub_t = lax.broadcasted_iota(jnp.int32, (SUB, 1), 0)

    mild = None
    for bi in range(n_batch):
        graw = _dot(ga_ref[bi].astype(BF16), w2_ref[...]) + b2_ref[...]
        g_all = jax.nn.log_sigmoid(graw) / GLA_GATE_TAU
        g_scr[bi] = g_all
        chunk_decay = jnp.sum(g_all.reshape(n_chunks, C, HK), axis=1)
        ok = jnp.min(chunk_decay) >= GLA_MILD_DECAY
        mild = ok if mild is None else jnp.logical_and(mild, ok)

    def cumulative_decay(bi, rows):
        g = g_scr[bi, rows, :]
        g_hi = g.astype(BF16)
        g_r1 = g - g_hi.astype(F32)
        g_mid = g_r1.astype(BF16)
        g_lo = (g_r1 - g_mid.astype(F32)).astype(BF16)
        tril = tril_ref[...]
        return _dot(tril, g_hi) + (_dot(tril, g_mid) + _dot(tril, g_lo))

    def finish(bi, rows, o, v, k_dec, b_last, st):
        upd = _dot_tn(v, k_dec)
        new_st = st * jnp.exp(b_last)
        for h in range(H):
            new_st = new_st + jnp.where(head_mask[h], upd[h * DV:(h + 1) * DV, :], 0.0)
        st_ref[bi] = new_st
        o_ref[bi, rows, :] = _head_rms(o, ng_ref[...], 1.0) * r_ref[bi, rows, :]

    def chunk_mild(bi, ci):
        rows = slice(ci * C, (ci + 1) * C)
        q = qk_ref[bi, rows, 0:HK] * (DK ** -0.5)
        k = qk_ref[bi, rows, HK:2 * HK]
        v = v_ref[bi, rows, :]
        b = cumulative_decay(bi, rows)
        b_last = b[C - 1:C, :]
        st = st_ref[bi]
        q_in = q * jnp.exp(b)
        k_out = (k * jnp.exp(-b)).astype(BF16)
        k_dec = (k * jnp.exp(b_last - b)).astype(BF16)
        outs = []
        for h in range(H):
            qm = jnp.where(head_mask[h], q_in, 0.0).astype(BF16)
            a_h = jnp.where(row >= col, _dot_nt(qm, k_out), 0.0).astype(BF16)
            outs.append(_dot_nt(qm, st.astype(BF16)) + _dot(a_h, v[:, h * DV:(h + 1) * DV]))
        finish(bi, rows, jnp.concatenate(outs, axis=1), v, k_dec, b_last, st)

    def chunk(bi, ci):
        r0 = pl.multiple_of(ci * C, C)
        rows = pl.ds(r0, C)
        q = qk_ref[bi, rows, 0:HK] * (DK ** -0.5)
        k = qk_ref[bi, rows, HK:2 * HK]
        v = v_ref[bi, rows, :]
        b = cumulative_decay(bi, rows)
        b_last = b[C - 1:C, :]
        st = st_ref[bi]

        q_in = q * jnp.exp(b)
        k_dec = (k * jnp.exp(b_last - b)).astype(BF16)

        a_off = [jnp.zeros((C, C), F32) for _ in range(H)]
        for j in range(C // SUB - 1):
            bj = b[(j + 1) * SUB - 1:(j + 1) * SUB, :]
            qj = q * jnp.exp(jnp.minimum(b - bj, 0.0))
            in_blk = (lax.broadcasted_iota(jnp.int32, (C, 1), 0) >> sub_shift) == j
            kj = jnp.where(in_blk, k * jnp.exp(jnp.minimum(bj - b, 0.0)), 0.0).astype(BF16)
            for h in range(H):
                a_off[h] = a_off[h] + _dot_nt(jnp.where(head_mask[h], qj, 0.0).astype(BF16), kj)

        for i in range(C // SUB):
            rs = slice(i * SUB, (i + 1) * SUB)
            q_sub, b_sub = q[rs, :], b[rs, :]
            for s in range(SUB):
                ks = k[i * SUB + s:i * SUB + s + 1, :]
                bs = b[i * SUB + s:i * SUB + s + 1, :]
                dterm = q_sub * ks * jnp.exp(jnp.minimum(b_sub - bs, 0.0))
                dterm = jnp.where(sub_t >= s, dterm, 0.0)
                d_scr[s * SUB:(s + 1) * SUB, :] = dterm.astype(BF16)
            gsum = _dot(d_scr[...], exp_ref[...])
            od = jnp.zeros((SUB, H * DV), F32)
            for s in range(SUB):
                vs = v[i * SUB + s:i * SUB + s + 1, :].astype(F32)
                od = od + gsum[s * SUB:(s + 1) * SUB, :] * vs
            o_ref[bi, pl.ds(r0 + i * SUB, SUB), :] = od

        outs = []
        for h in range(H):
            vh = v[:, h * DV:(h + 1) * DV]
            o_inter = _dot_nt(jnp.where(head_mask[h], q_in, 0.0).astype(BF16), st.astype(BF16))
            a_h = jnp.where(blk_lower, a_off[h], 0.0).astype(BF16)
            outs.append(o_inter + _dot(a_h, vh))
        o = o_ref[bi, rows, :] + jnp.concatenate(outs, axis=1)
        finish(bi, rows, o, v, k_dec, b_last, st)

    @pl.when(mild)
    def _():
        for ci in range(n_chunks):
            for bi in range(n_batch):
                chunk_mild(bi, ci)

    @pl.when(jnp.logical_not(mild))
    def _():
        for bi in range(n_batch):
            lax.fori_loop(0, n_chunks, lambda ci, c, bi=bi: (chunk(bi, ci), c)[1], 0)


def _gla(p32, p16, w2p, b2, norm_g, bsz, seq, ts):
    C, SUB, H, DK, DV = GLA_CHUNK, GLA_SUB, GLA_HEADS, GLA_DK, GLA_DV
    tokens = p32.shape[0]
    blocks_per_seq = seq // ts
    tril = jnp.asarray(np.tril(np.ones((C, C), np.float32))).astype(BF16)
    expand = np.zeros((H * DK, H * DV), np.float32)
    for h in range(H):
        expand[h * DK:(h + 1) * DK, h * DV:(h + 1) * DV] = 1.0
    expand = jnp.asarray(expand).astype(BF16)
    p32 = p32.reshape(bsz, seq, -1)
    p16 = p16.reshape(bsz, seq, -1)
    row_map = lambda cb: (lambda i: (0, i, cb))
    const2 = lambda i: (0, 0)
    out = pl.pallas_call(
        _gla_kernel,
        grid=(blocks_per_seq,),
        in_specs=[
            pl.BlockSpec((bsz, ts, 2 * H * DK), row_map(C32_GQ // (2 * H * DK))),
            pl.BlockSpec((bsz, ts, H * DV), row_map(C16_GV // (H * DV))),
            pl.BlockSpec((bsz, ts, H * DV), row_map(C32_R // (H * DV))),
            pl.BlockSpec((bsz, ts, LANES), row_map(C32_SMALL // LANES)),
            pl.BlockSpec(w2p.shape, const2),
            pl.BlockSpec(b2.shape, const2),
            pl.BlockSpec(norm_g.shape, const2),
            pl.BlockSpec(tril.shape, const2),
            pl.BlockSpec(expand.shape, const2),
        ],
        out_specs=pl.BlockSpec((bsz, ts, H * DV), lambda i: (0, i, 0)),
        out_shape=jax.ShapeDtypeStruct((bsz, seq, H * DV), F32),
        scratch_shapes=[pltpu.VMEM((bsz, DV, H * DK), F32), pltpu.VMEM((SUB * SUB, H * DK), BF16),
                        pltpu.VMEM((bsz, ts, H * DK), F32)],
        compiler_params=pltpu.CompilerParams(
            dimension_semantics=("arbitrary",), vmem_limit_bytes=VMEM_LIMIT),
        name="gla_branch",
    )(p32, p16, p32, p32, w2p, b2, norm_g, tril, expand)
    return out.reshape(tokens, H * DV)


def _bit_transpose32(words):
    a = list(words)
    shift, mask = 16, 0x0000FFFF
    while shift:
        m = jnp.int32(np.uint32(mask).astype(np.int32))
        for k in range(32):
            if k & shift == 0:
                t = (a[k] ^ lax.shift_right_logical(a[k + shift], jnp.int32(shift))) & m
                a[k] = a[k] ^ t
                a[k + shift] = a[k + shift] ^ lax.shift_left(t, jnp.int32(shift))
        shift >>= 1
        mask = (mask ^ (mask << shift)) & 0xFFFFFFFF
    return a


def _dsa_kernel(q_ref, iq_ref, iwt_ref, k_ref, vt_ref, ik_ref, bias_ref, ltri_ref,
                o_ref, key_scr, plane_scr, s_scr, acc_scr, *, topk):
    T, TK = DSA_QBLOCK, DSA_KTILE
    SUBS = TK // T
    H, HD = DSA_HEADS, DSA_HDIM
    qb = pl.program_id(1)
    kt_last = qb // SUBS
    n_kt = kt_last + 1
    key_minus_query = (lax.broadcasted_iota(jnp.int32, (TK, T), 0)
                       - lax.broadcasted_iota(jnp.int32, (TK, T), 1))

    def fold(x, op):
        return op(x.reshape(x.shape[0] // 8, 8, T), axis=0)

    def rows_to_one(x, op):
        return op(x, axis=0, keepdims=True)

    def key_rows(kt):
        return pl.ds(pl.multiple_of(kt * TK, TK), TK)

    def tile_loop(n, body, init):
        carry = lax.fori_loop(0, n // 2, lambda i, c: body(2 * i + 1, body(2 * i, c)), init)
        return lax.cond(n % 2 == 1, lambda c: body(n - 1, c), lambda c: c, carry)

    lo_mask = lax.broadcasted_iota(jnp.int32, (T, LANES), 1) < DSA_IDX_DIM
    iq = [iq_ref[:, 0:LANES], iq_ref[:, LANES:2 * LANES]]
    iq_h = [jnp.where(lo_mask, iq[0], 0), jnp.where(lo_mask, 0, iq[0]),
            jnp.where(lo_mask, iq[1], 0), jnp.where(lo_mask, 0, iq[1])]
    iq_all = jnp.concatenate(iq_h, axis=0)
    iw_h = [iwt_ref[h:h + 1, :] for h in range(DSA_IDX_HEADS)]

    GRP_ROWS = 32 * 8
    GRPS = TK // GRP_ROWS
    n_grp = plane_scr.shape[0]

    @pl.when(qb == 0)
    def _():
        plane_scr[...] = jnp.zeros_like(plane_scr)

    n_keys = key_scr.shape[0] * TK
    minus_one_minus_row = -1 - lax.broadcasted_iota(jnp.int32, (TK, T), 0)

    def score_tile(kt):
        ik2 = ik_ref[key_rows(kt), :]
        raw = _dot_nt(ik2, iq_all)
        score = jnp.zeros((TK, T), F32)
        for h in range(DSA_IDX_HEADS):
            score = score + iw_h[h] * jnp.maximum(raw[:, h * T:(h + 1) * T], 0.0)
        bits = lax.bitcast_convert_type(score, jnp.int32)
        zero_key = minus_one_minus_row - kt * TK
        negative_key = (bits ^ jnp.int32(0x7FFFFFFF)) - n_keys
        return jnp.where(score == 0.0, zero_key, jnp.where(bits >= 0, bits, negative_key))

    def store_planes(kt, key):
        unsigned_order = key ^ jnp.int32(INT_MIN)
        for g in range(GRPS):
            words = [unsigned_order[g * GRP_ROWS + 8 * j:g * GRP_ROWS + 8 * j + 8, :] for j in range(32)]
            planes = _bit_transpose32(words)
            for b in range(32):
                plane_scr[kt * GRPS + g, b] = planes[b]

    def score_body(kt, c):
        key = score_tile(kt)
        key_scr[kt] = key
        store_planes(kt, key)
        return c

    tile_loop(kt_last, score_body, 0)
    admissible = key_minus_query <= qb * T - kt_last * TK
    key = score_tile(kt_last)
    key_scr[kt_last] = jnp.where(admissible, key, jnp.int32(INT_MIN))
    store_planes(kt_last, key)
    alive_last = []
    for g in range(GRPS):
        word = jnp.zeros((8, T), jnp.int32)
        for j in range(32):
            adm = admissible[g * GRP_ROWS + 8 * j:g * GRP_ROWS + 8 * j + 8, :]
            word = word | jnp.where(adm, jnp.int32(np.int32(np.uint32(1 << (31 - j)))), 0)
        alive_last.append(word)

    alive = []
    for g in range(n_grp):
        kt = g // GRPS
        full = jnp.broadcast_to(jnp.where(kt < kt_last, jnp.int32(-1), jnp.int32(0)), (8, T))
        alive.append(jnp.where(kt == kt_last, alive_last[g % GRPS], full))

    def bits_body(i, carry):
        above, tau_u, alive = carry
        n_used = len(alive)
        hi_planes = [plane_scr[g, 2 * i] for g in range(n_used)]
        lo_planes = [plane_scr[g, 2 * i + 1] for g in range(n_used)]
        n11 = n10 = n01 = jnp.zeros((8, T), jnp.int32)
        for g in range(n_used):
            with_hi = alive[g] & hi_planes[g]
            both = with_hi & lo_planes[g]
            n11 = n11 + lax.population_count(both)
            n10 = n10 + lax.population_count(with_hi ^ both)
            n01 = n01 + lax.population_count((alive[g] & lo_planes[g]) ^ both)
        c11 = above + rows_to_one(n11, jnp.sum)
        c10 = c11 + rows_to_one(n10, jnp.sum)
        c01 = c10 + rows_to_one(n01, jnp.sum)
        is11, is10, is01 = c11 >= topk, c10 >= topk, c01 >= topk
        hi_bit = is10
        lo_bit = is11 | (is01 & ~is10)
        above = jnp.where(is11, above, jnp.where(is10, c11, jnp.where(is01, c10, c01)))
        flip_hi = jnp.where(hi_bit, jnp.int32(0), jnp.int32(-1))
        flip_lo = jnp.where(lo_bit, jnp.int32(0), jnp.int32(-1))
        alive = tuple(alive[g] & (hi_planes[g] ^ flip_hi) & (lo_planes[g] ^ flip_lo)
                      for g in range(n_used))
        tau_u = (tau_u | jnp.where(hi_bit, lax.shift_left(jnp.int32(1), 31 - 2 * i), 0)
                 | jnp.where(lo_bit, lax.shift_left(jnp.int32(1), 30 - 2 * i), 0))
        return above, tau_u, alive

    def select(n_used):
        def run():
            zero_row = jnp.zeros((1, T), jnp.int32)
            above, tau_u, left = lax.fori_loop(0, 16, bits_body,
                                               (zero_row, zero_row, tuple(alive[:n_used])))
            n_tied = jnp.zeros((8, T), jnp.int32)
            for g in range(n_used):
                n_tied = n_tied + lax.population_count(left[g])
            return above, tau_u, rows_to_one(n_tied, jnp.sum)
        return run

    half = n_grp // 2
    if half >= GRPS and half % GRPS == 0:
        above, tau_u, n_tied = lax.cond((kt_last + 1) * GRPS <= half, select(half), select(n_grp))
    else:
        above, tau_u, n_tied = select(n_grp)()
    n_ge = above + n_tied
    tau = jnp.maximum(tau_u ^ jnp.int32(INT_MIN), jnp.int32(INT_MIN + 1))

    @pl.when(jnp.max(n_ge) > topk)
    def _():
        need = (topk - above).astype(F32)

        def tie_body(kt, seen):
            for j in range(SUBS):
                rows = slice(j * T, (j + 1) * T)
                key = key_scr[kt, rows, :]
                eq = key == tau
                eq_f = jnp.where(eq, 1.0, 0.0)
                pref = _dot(ltri_ref[...], eq_f.astype(BF16)) + seen
                key_scr[kt, rows, :] = jnp.where(eq & (pref > need), tau - 1, key)
                seen = seen + rows_to_one(eq_f, jnp.sum)
            return seen

        tile_loop(n_kt, tie_body, jnp.zeros((1, T), F32))

    q_h = [q_ref[:, h * HD:(h + 1) * HD] for h in range(H)]

    def logits_body(kt, m_run):
        sel = key_scr[kt] >= tau
        kinds = [jnp.clip(qb - (kt * SUBS + j), 0, 2) for j in range(SUBS)]
        new_m = []
        for h in range(H):
            bias = jnp.concatenate([bias_ref[kinds[j], h] for j in range(SUBS)], axis=0)
            s = _dot_nt(k_ref[key_rows(kt), h * HD:(h + 1) * HD], q_h[h]) + bias
            s = jnp.where(sel, s, NEG_BIG)
            s_scr[h, kt] = s
            new_m.append(jnp.maximum(m_run[h], fold(s, jnp.max)))
        return tuple(new_m)

    m_run = tile_loop(n_kt, logits_body, tuple(jnp.full((8, T), NEG_BIG, F32) for _ in range(H)))
    m_h = [rows_to_one(m, jnp.max) for m in m_run]

    for h in range(H):
        acc_scr[h] = jnp.zeros((HD, T), F32)

    def pv_body(kt, l_run):
        new_l = []
        for h in range(H):
            p = jnp.exp2(s_scr[h, kt] - m_h[h])
            new_l.append(l_run[h] + fold(p, jnp.sum))
            acc_scr[h] = acc_scr[h] + _dot(vt_ref[kt, h * HD:(h + 1) * HD, :], p.astype(BF16))
        return tuple(new_l)

    l_run = tile_loop(n_kt, pv_body, tuple(jnp.zeros((8, T), F32) for _ in range(H)))

    for h in range(H):
        out_t = acc_scr[h] / rows_to_one(l_run[h], jnp.sum)
        o_ref[:, h * HD:(h + 1) * HD] = out_t.T.astype(o_ref.dtype)


def _dsa(p16, v_t, iw_t, bias_tiles, bsz, seq):
    T = DSA_QBLOCK
    tokens = p16.shape[0]
    nqb = seq // T
    topk = min(DSA_TOPK_MAX, seq // 4)
    TK = min(DSA_KTILE, seq)
    assert TK == DSA_KTILE and seq % TK == 0
    nkt = seq // TK
    ltri = jnp.asarray(np.tril(np.ones((T, T), np.float32))).astype(BF16)
    w = D_BRANCH
    v_t = v_t.reshape(bsz, nkt, w, TK)
    qmap = lambda cb: (lambda b, i: (b * nqb + i, cb))
    return pl.pallas_call(
        functools.partial(_dsa_kernel, topk=topk),
        grid=(bsz, nqb),
        in_specs=[
            pl.BlockSpec((T, w), qmap(C16_Q // w)),
            pl.BlockSpec((T, 2 * LANES), qmap(C16_IQ // (2 * LANES))),
            pl.BlockSpec((8, T), lambda b, i: (0, b * nqb + i)),
            pl.BlockSpec((seq, w), lambda b, i: (b, C16_K // w)),
            pl.BlockSpec((None, nkt, w, TK), lambda b, i: (b, 0, 0, 0)),
            pl.BlockSpec((seq, LANES), lambda b, i: (b, C16_IK // LANES)),
            pl.BlockSpec(bias_tiles.shape, lambda b, i: (0, 0, 0, 0)),
            pl.BlockSpec(ltri.shape, lambda b, i: (0, 0)),
        ],
        out_specs=pl.BlockSpec((T, w), lambda b, i: (b * nqb + i, 0)),
        out_shape=jax.ShapeDtypeStruct((tokens, w), BF16),
        scratch_shapes=[
            pltpu.VMEM((nkt, TK, T), jnp.int32),
            pltpu.VMEM((nkt * TK // 256, 32, 8, T), jnp.int32),
            pltpu.VMEM((DSA_HEADS, nkt, TK, T), F32),
            pltpu.VMEM((DSA_HEADS, DSA_HDIM, T), F32),
        ],
        compiler_params=pltpu.CompilerParams(
            dimension_semantics=("parallel", "arbitrary"), vmem_limit_bytes=VMEM_LIMIT),
        name="dsa_branch",
    )(p16, p16, iw_t, p16, v_t, p16, bias_tiles, ltri)


def _merge_kernel(u_ref, v_ref, ws_ref, bs_ref, yb_ref, yc_ref, ga_ref, gb_ref, gc_ref, x_ref, mod_ref,
                  wbr_ref, wout_ref, gn2_ref, wr_hi_ref, wr_lo_ref, br_ref, x1_ref, h2_ref, comb_ref):
    tm = u_ref.shape[0]
    ya_rows = []
    for c in range(tm // GMLP_CHUNK):
        rows = slice(c * GMLP_CHUNK, (c + 1) * GMLP_CHUNK)
        groups = []
        for g in range(GMLP_GROUPS):
            cols = slice(g * LANES, (g + 1) * LANES)
            mixed = _dot(ws_ref[g], v_ref[rows, cols].astype(BF16)) + bs_ref[:, cols]
            groups.append((u_ref[rows, cols] * mixed).astype(BF16))
        ya_rows.append(jnp.concatenate(groups, axis=1))
    ya = jnp.concatenate(ya_rows, axis=0)

    merged = ga_ref[...] * _dot(ya, wbr_ref[0])
    merged = merged + gb_ref[...] * _dot(yb_ref[...].astype(BF16), wbr_ref[1])
    merged = merged + gc_ref[...] * _dot(yc_ref[...], wbr_ref[2])
    mix = _dot(merged.astype(BF16), wout_ref[...])
    x1 = x_ref[...] + mod_ref[0, 2:3, :] * mix
    x1_ref[...] = x1
    y = x1 * lax.rsqrt(jnp.mean(x1 * x1, axis=-1, keepdims=True) + EPS) * gn2_ref[...]
    h2 = y * (1.0 + mod_ref[0, 4:5, :]) + mod_ref[0, 3:4, :]
    h2_ref[...] = h2.astype(BF16)
    logits = _dot3(h2, wr_hi_ref[...], wr_lo_ref[...]) + br_ref[...]
    comb_ref[...] = _route_packed(logits).astype(BF16)


def _merge(p32, w_tril, gmlp_bias, yb, yc, p16, x2d, mod_l, wbr, wout, layer, gn2, wr_hi, wr_lo, br,
           seq, tm):
    tokens, d = x2d.shape
    tiles_per_seq = seq // tm
    row = lambda i: (i, 0)
    gate = lambda k: (lambda i: (i, C16_GATES // d + k))
    c2 = lambda i: (0, 0)
    of_layer = lambda a: pl.BlockSpec((None,) + a.shape[1:], lambda i: (layer,) + (0,) * (a.ndim - 1))
    return pl.pallas_call(
        _merge_kernel,
        grid=(tokens // tm,),
        in_specs=[
            pl.BlockSpec((tm, D_BRANCH), lambda i: (i, C32_U // D_BRANCH)),
            pl.BlockSpec((tm, D_BRANCH), lambda i: (i, C32_V // D_BRANCH)),
            of_layer(w_tril), of_layer(gmlp_bias),
            pl.BlockSpec((tm, D_BRANCH), row), pl.BlockSpec((tm, D_BRANCH), row),
            pl.BlockSpec((tm, d), gate(0)), pl.BlockSpec((tm, d), gate(1)), pl.BlockSpec((tm, d), gate(2)),
            pl.BlockSpec((tm, d), row),
            pl.BlockSpec((1, 6, d), lambda i: (i // tiles_per_seq, 0, 0)),
            of_layer(wbr), of_layer(wout),
            pl.BlockSpec(gn2.shape, c2),
            of_layer(wr_hi), of_layer(wr_lo), of_layer(br),
        ],
        out_specs=[pl.BlockSpec((tm, d), row), pl.BlockSpec((tm, d), row),
                   pl.BlockSpec((tm, LANES), row)],
        out_shape=[jax.ShapeDtypeStruct((tokens, d), F32), jax.ShapeDtypeStruct((tokens, d), BF16),
                   jax.ShapeDtypeStruct((tokens, LANES), BF16)],
        compiler_params=pltpu.CompilerParams(
            dimension_semantics=("parallel",), vmem_limit_bytes=VMEM_LIMIT),
        name="gmlp_merge_norm_router",
    )(p32, p32, w_tril, gmlp_bias, yb, yc, p16, p16, p16, x2d, mod_l, wbr, wout, gn2, wr_hi, wr_lo, br)


def _route_packed(lg):
    n_rows = lg.shape[0]
    t = lg.T[0:MOE_PART_LANES, :]
    idx = lax.broadcasted_iota(jnp.int32, t.shape, 0)
    big = jnp.int32(10 ** 6)
    top = lambda x, op: op(x, axis=0, keepdims=True)
    is_grp = idx < N_GROUPS
    gl = jnp.where(is_grp, t, -jnp.inf)
    gmax = top(gl, jnp.max)
    p_g = 1.0 / top(jnp.where(is_grp, jnp.exp(t - gmax), 0.0), jnp.sum)
    g_idx = top(jnp.where(gl == gmax, idx, big), jnp.min)
    first = N_GROUPS + g_idx * EXPERTS_PER_GROUP
    in_grp = (idx >= first) & (idx < first + EXPERTS_PER_GROUP)
    e1 = jnp.where(in_grp, t, -jnp.inf)
    v1 = top(e1, jnp.max)
    i1 = top(jnp.where(e1 == v1, idx, big), jnp.min)
    e2 = jnp.where(in_grp & (idx != i1), t, -jnp.inf)
    v2 = top(e2, jnp.max)
    i2 = top(jnp.where(e2 == v2, idx, big), jnp.min)
    r = jnp.exp(v2 - v1)
    w1 = p_g * (1.0 / (1.0 + r))
    w2 = p_g * (r / (1.0 + r))
    comb = jnp.where(idx == i1, w1, jnp.where(idx == i2, w2, 0.0))
    hi = comb.astype(BF16).astype(F32)
    mid = (comb - hi).astype(BF16).astype(F32)
    lo = ((comb - hi) - mid).astype(BF16).astype(F32)
    first_part = jnp.where(idx == g_idx, 1.0, 0.0) + hi
    pad = jnp.zeros((LANES - 3 * MOE_PART_LANES, n_rows), F32)
    return jnp.concatenate([first_part, mid, lo, pad], axis=0).T


def _moe_kernel(h_ref, comb_ref, x1_ref, mod_ref, wg_ref, wu_ref, wd_ref, tril_ref, o_ref,
                hp_scr, cwp_scr, yp_scr, pt_scr, start_ref, nchunk_ref):
    grp = pl.program_id(1)
    tm = h_ref.shape[0]
    n_sorted = hp_scr.shape[0]
    lane = lax.broadcasted_iota(jnp.int32, (1, LANES), 1)

    @pl.when(grp == 0)
    def _():
        onehot = jnp.where(lane < N_GROUPS, comb_ref[...], 0)
        incl = _dot(tril_ref[...], onehot)
        counts = incl[tm - 1:tm, :]
        start = jnp.int32(0)
        start_row = jnp.zeros((1, LANES), F32)
        for g in range(N_GROUPS):
            cnt = jnp.sum(jnp.where(lane == g, counts, 0.0)).astype(jnp.int32)
            start_ref[g] = start
            nchunk_ref[g] = lax.shift_right_logical(cnt + (MOE_CHUNK - 1), MOE_CHUNK.bit_length() - 1)
            start_row = jnp.where(lane == g, start.astype(F32), start_row)
            align = MOE_ALIGN.bit_length() - 1
            start = start + lax.shift_left(lax.shift_right_logical(cnt + (MOE_ALIGN - 1), align), align)
        dest = jnp.sum(onehot.astype(F32) * (incl - 1.0 + start_row), axis=1, keepdims=True)
        dest_col = dest.astype(jnp.int32)
        dest_row = jnp.broadcast_to(dest, (tm, LANES)).T[0:1, :].astype(jnp.int32)
        back = lax.broadcasted_iota(jnp.int32, (tm, n_sorted), 1) == dest_col
        pt_scr[...] = jnp.where(back, 1.0, 0.0).astype(BF16)
        fwd = lax.broadcasted_iota(jnp.int32, (n_sorted, tm), 0) == dest_row
        fwd = jnp.where(fwd, 1.0, 0.0).astype(BF16)
        hp_scr[...] = _dot(fwd, h_ref[...]).astype(BF16)
        cwp_scr[...] = _dot(fwd, comb_ref[...]).astype(BF16)
        yp_scr[...] = jnp.zeros_like(yp_scr)

    row_id = lax.broadcasted_iota(jnp.int32, (LANES, LANES), 0)
    start = start_ref[grp]

    def chunk(c, carry):
        rows = pl.ds(pl.multiple_of(start + c * MOE_CHUNK, MOE_ALIGN), MOE_CHUNK)
        h = hp_scr[rows, :]
        acts = []
        for j in range(EXPERTS_PER_GROUP):
            lane_of_expert = N_GROUPS + grp * EXPERTS_PER_GROUP + j
            pick = ((row_id == lane_of_expert) | (row_id == lane_of_expert + MOE_PART_LANES)
                    | (row_id == lane_of_expert + 2 * MOE_PART_LANES))
            cw = _dot(cwp_scr[rows, :], jnp.where(pick, 1.0, 0.0).astype(BF16))
            act = (jax.nn.silu(_dot(h, wg_ref[j])) * _dot(h, wu_ref[j])
                   * jnp.concatenate([cw, cw], axis=1))
            acts.append(act.astype(BF16))
        yp_scr[rows, :] = _dot(jnp.concatenate(acts, axis=1), wd_ref[...]).astype(BF16)
        return carry

    n_chunks = nchunk_ref[grp]
    lax.fori_loop(0, n_chunks // 2, lambda i, c: chunk(2 * i + 1, chunk(2 * i, c)), 0)

    @pl.when(n_chunks % 2 == 1)
    def _():
        chunk(n_chunks - 1, 0)

    @pl.when(grp == pl.num_programs(1) - 1)
    def _():
        o_ref[...] = x1_ref[...] + mod_ref[0, 5:6, :] * _dot(pt_scr[...], yp_scr[...])


def _moe(h2, comb, x1, mod_l, wg, wu, wd, layer, seq, tm):
    tokens, d = x1.shape
    tiles_per_seq = seq // tm
    n_sorted = tm + N_GROUPS * MOE_ALIGN + MOE_CHUNK
    n_sorted = LANES * pl.cdiv(n_sorted, LANES)
    tril = jnp.asarray(np.tril(np.ones((tm, tm), np.float32))).astype(BF16)
    row = lambda i, g: (i, 0)
    return pl.pallas_call(
        _moe_kernel,
        grid=(tokens // tm, N_GROUPS),
        in_specs=[
            pl.BlockSpec((tm, d), row), pl.BlockSpec((tm, LANES), row),
            pl.BlockSpec((tm, d), row),
            pl.BlockSpec((1, 6, d), lambda i, g: (i // tiles_per_seq, 0, 0)),
            pl.BlockSpec((None, None) + wg.shape[2:], lambda i, g: (layer, g, 0, 0, 0)),
            pl.BlockSpec((None, None) + wu.shape[2:], lambda i, g: (layer, g, 0, 0, 0)),
            pl.BlockSpec((None, None) + wd.shape[2:], lambda i, g: (layer, g, 0, 0)),
            pl.BlockSpec(tril.shape, lambda i, g: (0, 0), pipeline_mode=pl.Buffered(1)),
        ],
        out_specs=pl.BlockSpec((tm, d), row),
        out_shape=jax.ShapeDtypeStruct((tokens, d), F32),
        scratch_shapes=[pltpu.VMEM((n_sorted, d), BF16), pltpu.VMEM((n_sorted, LANES), BF16),
                        pltpu.VMEM((n_sorted, d), BF16), pltpu.VMEM((tm, n_sorted), BF16),
                        pltpu.SMEM((N_GROUPS,), jnp.int32), pltpu.SMEM((N_GROUPS,), jnp.int32)],
        compiler_params=pltpu.CompilerParams(
            dimension_semantics=("parallel", "arbitrary"), vmem_limit_bytes=VMEM_LIMIT),
        name="hier_moe",
    )(h2, comb, x1, mod_l, wg, wu, wd, tril)


def _row_tile(seq, want):
    t = min(want, seq)
    assert seq % t == 0
    return t


def kernel(x, c, w_mod, b_mod, g_norm1, g_norm2, w_in, gmlp_ln_g, gmlp_ln_b, gmlp_w_s, gmlp_b_s,
           gla_w_gate2, gla_b_gate, gla_norm_g, dsa_qnorm_g, dsa_knorm_g, rel_bias, w_branch,
           b_branch_gate, w_out, w_group, b_group, w_router, b_router, w_exp_gate, w_exp_up,
           w_exp_down):
    bsz, seq, d = x.shape
    depth = w_mod.shape[0]
    assert d == D_MODEL and seq % DSA_QBLOCK == 0 and seq % GLA_CHUNK == 0
    tokens = bsz * seq
    tm = _row_tile(seq, 512)

    mods = _modulation(c, w_mod, b_mod)
    bias_tiles = _bias_tiles(rel_bias)
    w32, w16 = _prep_w_in(w_in)

    causal = np.tril(np.ones((GMLP_CHUNK, GMLP_CHUNK), bool))
    w_tril = jnp.where(causal[None, None], gmlp_w_s, 0.0).astype(BF16)
    gmlp_bias = jnp.repeat(jnp.swapaxes(gmlp_b_s, 1, 2), LANES, axis=-1)
    w2p = jnp.zeros((depth, LANES, GLA_HEADS * GLA_DK), F32).at[:, :GLA_GATE_RANK].set(gla_w_gate2)
    w2p = w2p.astype(BF16)
    wr = jnp.concatenate(
        [w_group, w_router, jnp.zeros((depth, d, LANES - N_GROUPS - N_EXPERTS), F32)], axis=-1)
    wr_hi = wr.astype(BF16)
    wr_lo = (wr - wr_hi.astype(F32)).astype(BF16)
    br = jnp.zeros((depth, 1, LANES), F32)
    br = br.at[:, 0, :N_GROUPS].set(b_group).at[:, 0, N_GROUPS:N_GROUPS + N_EXPERTS].set(b_router)
    wbr = w_branch.astype(BF16)
    wout = w_out.astype(BF16)
    grouped = (depth, N_GROUPS, EXPERTS_PER_GROUP, d, D_EXPERT)
    wg, wu = w_exp_gate.astype(BF16).reshape(grouped), w_exp_up.astype(BF16).reshape(grouped)
    wd = w_exp_down.astype(BF16).reshape(depth, N_GROUPS, EXPERTS_PER_GROUP * D_EXPERT, d)
    tm_moe = _row_tile(seq, 1024)

    x2d = x.reshape(tokens, d)
    for l in range(depth):
        mod_l = mods[l]
        aux32 = [gmlp_ln_g[l][None], gmlp_ln_b[l][None]]
        aux16 = [dsa_qnorm_g[l][None], dsa_knorm_g[l][None], b_branch_gate[l].reshape(1, -1)]
        p32, p16, v_t, iw_t = _norm_proj(x2d, mod_l, g_norm1[l][None], w32, w16, l, aux32, aux16,
                                         seq, tm)
        yb = _gla(p32, p16, w2p[l], gla_b_gate[l][None], gla_norm_g[l][None], bsz, seq, tm)
        yc = _dsa(p16, v_t, iw_t, bias_tiles, bsz, seq)
        x1, h2, comb = _merge(p32, w_tril, gmlp_bias, yb, yc, p16, x2d, mod_l, wbr, wout, l,
                              g_norm2[l][None], wr_hi, wr_lo, br, seq, tm)
        x2d = _moe(h2, comb, x1, mod_l, wg, wu, wd, l, seq, tm_moe)
    return x2d.reshape(bsz, seq, d)
```

```python
import functools
import math

import numpy as np
import jax
import jax.numpy as jnp
from jax import lax
from jax.experimental import pallas as pl
from jax.experimental.pallas import tpu as pltpu

F32 = jnp.float32
BF16 = jnp.bfloat16

D_MODEL = 1024
D_BRANCH = 512
EPS = 1e-6
GMLP_CHUNK = 128
GMLP_GROUPS = 4
GLA_HEADS = 4
GLA_DK = 64
GLA_DV = 128
GLA_GATE_RANK = 16
GLA_GATE_TAU = 16.0
GLA_CHUNK = 128
GLA_SUB = 16
GLA_MILD_DECAY = -60.0
DSA_HEADS = 4
DSA_HDIM = 128
DSA_IDX_HEADS = 4
DSA_IDX_DIM = 64
DSA_QBLOCK = 128
DSA_KTILE = 512
DSA_TOPK_MAX = 256
N_BUCKETS = 32
MAX_DISTANCE = 128
N_GROUPS = 4
EXPERTS_PER_GROUP = 4
N_EXPERTS = 16
D_EXPERT = 256
MOE_CHUNK = 128
MOE_ALIGN = 16
MOE_PART_LANES = 32

LANES = 128
COL_TILE = 512
VMEM_LIMIT = 56 * 1024 * 1024
INT_MIN = -(2 ** 31)
LOG2_E = math.log2(math.e)
NEG_BIG = -1e30

C32_U, C32_V, C32_GQ, C32_GK, C32_R, C32_SMALL = 0, 512, 1024, 1280, 1536, 2048
N32 = C32_SMALL + LANES
C16_GATES, C16_Q, C16_K, C16_GV, C16_IQ, C16_IK = 0, 3072, 3584, 4096, 4608, 4864
N16 = C16_IK + LANES


def _dot(a, b):
    return jnp.dot(a, b, preferred_element_type=F32)


def _dot_nt(a, b):
    return lax.dot_general(a, b, (((1,), (1,)), ((), ())), preferred_element_type=F32)


def _dot_tn(a, b):
    return lax.dot_general(a, b, (((0,), (0,)), ((), ())), preferred_element_type=F32)


def _split2(a):
    hi = a.astype(BF16)
    lo = (a - hi.astype(F32)).astype(BF16)
    return hi, lo


def _dot3(a, w_hi, w_lo):
    a_hi, a_lo = _split2(a)
    return _dot(a_hi, w_hi) + (_dot(a_lo, w_hi) + _dot(a_hi, w_lo))


def _head_rms(y, g, scale):
    outs = []
    for h in range(y.shape[1] // LANES):
        yh = y[:, h * LANES:(h + 1) * LANES]
        ms = jnp.mean(yh * yh, axis=-1, keepdims=True)
        o = yh * lax.rsqrt(ms + EPS) * g
        if scale != 1.0:
            o = o * scale
        outs.append(o)
    return jnp.concatenate(outs, axis=1)


def _mod_kernel(c_ref, w_ref, b_ref, o_ref):
    a = jax.nn.silu(c_ref[...])
    w_hi, w_lo = _split2(w_ref[...])
    o_ref[...] = _dot3(a, w_hi, w_lo) + b_ref[...]


def _modulation(c, w_mod, b_mod):
    L, d, n = w_mod.shape
    bsz = c.shape[0]
    rows = 8 * pl.cdiv(bsz, 8)
    c_pad = jnp.zeros((rows, d), F32).at[:bsz].set(c)
    tn = 1536
    out = pl.pallas_call(
        _mod_kernel,
        grid=(L, n // tn),
        in_specs=[
            pl.BlockSpec((rows, d), lambda l, j: (0, 0)),
            pl.BlockSpec((None, d, tn), lambda l, j: (l, 0, j)),
            pl.BlockSpec((None, 1, tn), lambda l, j: (l, 0, j)),
        ],
        out_specs=pl.BlockSpec((None, rows, tn), lambda l, j: (l, 0, j)),
        out_shape=jax.ShapeDtypeStruct((L, rows, n), F32),
        compiler_params=pltpu.CompilerParams(
            dimension_semantics=("arbitrary", "arbitrary"), vmem_limit_bytes=VMEM_LIMIT),
        name="adaln_modulation",
    )(c_pad, w_mod, b_mod.reshape(L, 1, n))
    return out[:, :bsz].reshape(L, bsz, 6, d)


def _t5_bucket_table():
    n = np.arange(2 * DSA_QBLOCK)
    max_exact = N_BUCKETS // 2
    large = max_exact + (
        np.log(np.maximum(n, max_exact).astype(np.float32) / max_exact)
        / math.log(MAX_DISTANCE / max_exact) * (N_BUCKETS - max_exact)).astype(np.int32)
    large = np.minimum(large, N_BUCKETS - 1)
    return np.where(n < max_exact, n, large).astype(np.int32)


def _bias_kernel(rb_ref, bucket_ref, o_ref):
    for t in range(bucket_ref.shape[0]):
        bucket = bucket_ref[t]
        for h in range(DSA_HEADS):
            acc = jnp.zeros(bucket.shape, F32)
            for b in range(N_BUCKETS):
                acc = jnp.where(bucket == b, rb_ref[b, h], acc)
            o_ref[t, h] = acc * LOG2_E


def _bias_tiles(rel_bias):
    table = _t5_bucket_table()
    assert (table[MAX_DISTANCE:] == N_BUCKETS - 1).all()
    t = np.arange(DSA_QBLOCK)[None, :]
    s = np.arange(DSA_QBLOCK)[:, None]
    diag = table[np.maximum(t - s, 0)]
    near = table[DSA_QBLOCK + t - s]
    far = np.full_like(diag, N_BUCKETS - 1)
    buckets = jnp.asarray(np.stack([diag, near, far]).astype(np.int32))
    return pl.pallas_call(
        _bias_kernel,
        in_specs=[pl.BlockSpec(memory_space=pltpu.SMEM), pl.BlockSpec(memory_space=pltpu.VMEM)],
        out_specs=pl.BlockSpec(memory_space=pltpu.VMEM),
        out_shape=jax.ShapeDtypeStruct((3, DSA_HEADS, DSA_QBLOCK, DSA_QBLOCK), F32),
        name="t5_bias_tiles",
    )(rel_bias, buckets)


def _proj_kernel(x_ref, mod_ref, gn_ref, w32_ref, w16_ref, *rest):
    aux32, aux16 = rest[:2], rest[2:5]
    o32_ref, o16_ref, vt_ref, iwt_ref = rest[5:9]
    x = x_ref[...]
    y = x * lax.rsqrt(jnp.mean(x * x, axis=-1, keepdims=True) + EPS) * gn_ref[...]
    h = (y * (1.0 + mod_ref[0, 1:2, :]) + mod_ref[0, 0:1, :]).astype(BF16)
    for w_ref, o_ref, epilogues, aux in ((w32_ref, o32_ref, EPILOGUES_32, aux32),
                                         (w16_ref, o16_ref, EPILOGUES_16, aux16)):
        wcol = ocol = 0
        for width, epi in epilogues:
            y = epi(_dot(h, w_ref[:, wcol:wcol + width]), aux)
            wcol += width
            if epi is _epi_value_t:
                vt_ref[0] = y.T.astype(vt_ref.dtype)
                continue
            if epi is _epi_small:
                iwt_ref[...] = y.T[GLA_GATE_RANK:GLA_GATE_RANK + 8, :]
            o_ref[:, ocol:ocol + width] = y.astype(o_ref.dtype)
            ocol += width
        assert wcol == w_ref.shape[1] and ocol == o_ref.shape[1]


def _norm_proj(x2d, mod_l, gn, w32, w16, layer, aux32, aux16, seq, tm):
    tokens, d = x2d.shape
    tiles_per_seq = seq // tm
    assert tm == DSA_KTILE
    const = lambda a: pl.BlockSpec(a.shape, lambda i: (0, 0))
    resident = lambda a: pl.BlockSpec((None,) + a.shape[1:], lambda i: (layer, 0, 0),
                                      pipeline_mode=pl.Buffered(1))
    return pl.pallas_call(
        _proj_kernel,
        grid=(tokens // tm,),
        in_specs=[
            pl.BlockSpec((tm, d), lambda i: (i, 0)),
            pl.BlockSpec((1, 6, d), lambda i: (i // tiles_per_seq, 0, 0)),
            const(gn), resident(w32), resident(w16),
        ] + [const(a) for a in aux32 + aux16],
        out_specs=[pl.BlockSpec((tm, N32), lambda i: (i, 0)), pl.BlockSpec((tm, N16), lambda i: (i, 0)),
                   pl.BlockSpec((1, D_BRANCH, tm), lambda i: (i, 0, 0)),
                   pl.BlockSpec((8, tm), lambda i: (0, i))],
        out_shape=[jax.ShapeDtypeStruct((tokens, N32), F32), jax.ShapeDtypeStruct((tokens, N16), BF16),
                   jax.ShapeDtypeStruct((tokens // tm, D_BRANCH, tm), BF16),
                   jax.ShapeDtypeStruct((8, tokens), F32)],
        compiler_params=pltpu.CompilerParams(
            dimension_semantics=("parallel",), vmem_limit_bytes=VMEM_LIMIT),
        name="norm_proj",
    )(x2d, mod_l, gn, w32, w16, *aux32, *aux16)


def _epi_raw(y, aux):
    return y


def _epi_gelu(y, aux):
    return jax.nn.gelu(y)


def _epi_gelu_ln(y, aux):
    v = jax.nn.gelu(y)
    mu = jnp.mean(v, axis=-1, keepdims=True)
    var = jnp.mean(jnp.square(v - mu), axis=-1, keepdims=True)
    return (v - mu) * lax.rsqrt(var + EPS) * aux[0][...] + aux[1][...]


def _epi_silu(y, aux):
    return jax.nn.silu(y)


def _epi_gate(k):
    def epi(y, aux):
        return jax.nn.sigmoid(y + aux[2][:, k * COL_TILE:(k + 1) * COL_TILE])
    return epi


def _epi_qnorm(y, aux):
    return _head_rms(y, aux[0][...], DSA_HDIM ** -0.5 * LOG2_E)


def _epi_knorm(y, aux):
    return _head_rms(y, aux[1][...], 1.0)


def _epi_value_t(y, aux):
    return y


def _epi_small(y, aux):
    return y


EPILOGUES_32 = ([(COL_TILE, e) for e in (_epi_gelu, _epi_gelu_ln, _epi_raw, _epi_silu)]
                + [(LANES, _epi_small)])
EPILOGUES_16 = ([(COL_TILE, _epi_gate(k)) for k in range(6)]
                + [(COL_TILE, e) for e in (_epi_qnorm, _epi_knorm, _epi_value_t, _epi_raw)]
                + [(2 * LANES, _epi_raw), (LANES, _epi_raw)])


def _prep_w_in(w_in):
    sizes = (512, 512, 256, 256, 512, 512, 16, 512, 512, 512, 256, 64, 4, 3072)
    offs = np.concatenate([[0], np.cumsum(sizes)])
    seg = lambda k: w_in[:, :, offs[k]:offs[k + 1]]
    (a_u, a_v, g_q, g_k, g_v, g_r, g_a, d_q, d_k, d_v, d_iq, d_ik, d_iw, gates) = [seg(k) for k in range(14)]
    L, d, _ = w_in.shape
    zeros = lambda n: jnp.zeros((L, d, n), w_in.dtype)
    w32 = jnp.concatenate([a_u, a_v, g_q, g_k, g_r, g_a, d_iw,
                           zeros(N32 - C32_SMALL - 20)], axis=-1).astype(BF16)
    w16 = jnp.concatenate([gates, d_q, d_k, d_v, g_v, d_iq, d_ik, d_ik], axis=-1).astype(BF16)
    return w32, w16


def _gla_kernel(qk_ref, v_ref, r_ref, ga_ref, w2_ref, b2_ref, ng_ref, tril_ref, exp_ref,
                o_ref, st_ref, d_scr, g_scr):
    C, SUB, H, DK, DV = GLA_CHUNK, GLA_SUB, GLA_HEADS, GLA_DK, GLA_DV
    HK = H * DK
    n_batch = qk_ref.shape[0]
    n_chunks = qk_ref.shape[1] // C

    @pl.when(pl.program_id(0) == 0)
    def _():
        st_ref[...] = jnp.zeros_like(st_ref)

    lane = lax.broadcasted_iota(jnp.int32, (1, HK), 1)
    head_mask = [(lane >= h * DK) & (lane < (h + 1) * DK) for h in range(H)]
    row = lax.broadcasted_iota(jnp.int32, (C, C), 0)
    col = lax.broadcasted_iota(jnp.int32, (C, C), 1)
    sub_shift = SUB.bit_length() - 1
    blk_lower = (row >> sub_shift) > (col >> sub_shift)
    sub_t = lax.broadcasted_iota(jnp.int32, (SUB, 1), 0)

    mild = None
    for bi in range(n_batch):
        graw = _dot(ga_ref[bi].astype(BF16), w2_ref[...]) + b2_ref[...]
        g_all = jax.nn.log_sigmoid(graw) / GLA_GATE_TAU
        g_scr[bi] = g_all
        chunk_decay = jnp.sum(g_all.reshape(n_chunks, C, HK), axis=1)
        ok = jnp.min(chunk_decay) >= GLA_MILD_DECAY
        mild = ok if mild is None else jnp.logical_and(mild, ok)

    def cumulative_decay(bi, rows):
        g = g_scr[bi, rows, :]
        g_hi = g.astype(BF16)
        g_r1 = g - g_hi.astype(F32)
        g_mid = g_r1.astype(BF16)
        g_lo = (g_r1 - g_mid.astype(F32)).astype(BF16)
        tril = tril_ref[...]
        return _dot(tril, g_hi) + (_dot(tril, g_mid) + _dot(tril, g_lo))

    def finish(bi, rows, o, v, k_dec, b_last, st):
        upd = _dot_tn(v, k_dec)
        new_st = st * jnp.exp(b_last)
        for h in range(H):
            new_st = new_st + jnp.where(head_mask[h], upd[h * DV:(h + 1) * DV, :], 0.0)
        st_ref[bi] = new_st
        o_ref[bi, rows, :] = _head_rms(o, ng_ref[...], 1.0) * r_ref[bi, rows, :]

    def chunk_mild(bi, ci):
        rows = slice(ci * C, (ci + 1) * C)
        q = qk_ref[bi, rows, 0:HK] * (DK ** -0.5)
        k = qk_ref[bi, rows, HK:2 * HK]
        v = v_ref[bi, rows, :]
        b = cumulative_decay(bi, rows)
        b_last = b[C - 1:C, :]
        st = st_ref[bi]
        q_in = q * jnp.exp(b)
        k_out = (k * jnp.exp(-b)).astype(BF16)
        k_dec = (k * jnp.exp(b_last - b)).astype(BF16)
        outs = []
        for h in range(H):
            qm = jnp.where(head_mask[h], q_in, 0.0).astype(BF16)
            a_h = jnp.where(row >= col, _dot_nt(qm, k_out), 0.0).astype(BF16)
            outs.append(_dot_nt(qm, st.astype(BF16)) + _dot(a_h, v[:, h * DV:(h + 1) * DV]))
        finish(bi, rows, jnp.concatenate(outs, axis=1), v, k_dec, b_last, st)

    def chunk(bi, ci):
        r0 = pl.multiple_of(ci * C, C)
        rows = pl.ds(r0, C)
        q = qk_ref[bi, rows, 0:HK] * (DK ** -0.5)
        k = qk_ref[bi, rows, HK:2 * HK]
        v = v_ref[bi, rows, :]
        b = cumulative_decay(bi, rows)
        b_last = b[C - 1:C, :]
        st = st_ref[bi]

        q_in = q * jnp.exp(b)
        k_dec = (k * jnp.exp(b_last - b)).astype(BF16)

        a_off = [jnp.zeros((C, C), F32) for _ in range(H)]
        for j in range(C // SUB - 1):
            bj = b[(j + 1) * SUB - 1:(j + 1) * SUB, :]
            qj = q * jnp.exp(jnp.minimum(b - bj, 0.0))
            in_blk = (lax.broadcasted_iota(jnp.int32, (C, 1), 0) >> sub_shift) == j
            kj = jnp.where(in_blk, k * jnp.exp(jnp.minimum(bj - b, 0.0)), 0.0).astype(BF16)
            for h in range(H):
                a_off[h] = a_off[h] + _dot_nt(jnp.where(head_mask[h], qj, 0.0).astype(BF16), kj)

        for i in range(C // SUB):
            rs = slice(i * SUB, (i + 1) * SUB)
            q_sub, b_sub = q[rs, :], b[rs, :]
            for s in range(SUB):
                ks = k[i * SUB + s:i * SUB + s + 1, :]
                bs = b[i * SUB + s:i * SUB + s + 1, :]
                dterm = q_sub * ks * jnp.exp(jnp.minimum(b_sub - bs, 0.0))
                dterm = jnp.where(sub_t >= s, dterm, 0.0)
                d_scr[s * SUB:(s + 1) * SUB, :] = dterm.astype(BF16)
            gsum = _dot(d_scr[...], exp_ref[...])
            od = jnp.zeros((SUB, H * DV), F32)
            for s in range(SUB):
                vs = v[i * SUB + s:i * SUB + s + 1, :].astype(F32)
                od = od + gsum[s * SUB:(s + 1) * SUB, :] * vs
            o_ref[bi, pl.ds(r0 + i * SUB, SUB), :] = od

        outs = []
        for h in range(H):
            vh = v[:, h * DV:(h + 1) * DV]
            o_inter = _dot_nt(jnp.where(head_mask[h], q_in, 0.0).astype(BF16), st.astype(BF16))
            a_h = jnp.where(blk_lower, a_off[h], 0.0).astype(BF16)
            outs.append(o_inter + _dot(a_h, vh))
        o = o_ref[bi, rows, :] + jnp.concatenate(outs, axis=1)
        finish(bi, rows, o, v, k_dec, b_last, st)

    @pl.when(mild)
    def _():
        for ci in range(n_chunks):
            for bi in range(n_batch):
                chunk_mild(bi, ci)

    @pl.when(jnp.logical_not(mild))
    def _():
        for bi in range(n_batch):
            lax.fori_loop(0, n_chunks, lambda ci, c, bi=bi: (chunk(bi, ci), c)[1], 0)


def _gla(p32, p16, w2p, b2, norm_g, bsz, seq, ts):
    C, SUB, H, DK, DV = GLA_CHUNK, GLA_SUB, GLA_HEADS, GLA_DK, GLA_DV
    tokens = p32.shape[0]
    blocks_per_seq = seq // ts
    tril = jnp.asarray(np.tril(np.ones((C, C), np.float32))).astype(BF16)
    expand = np.zeros((H * DK, H * DV), np.float32)
    for h in range(H):
        expand[h * DK:(h + 1) * DK, h * DV:(h + 1) * DV] = 1.0
    expand = jnp.asarray(expand).astype(BF16)
    p32 = p32.reshape(bsz, seq, -1)
    p16 = p16.reshape(bsz, seq, -1)
    row_map = lambda cb: (lambda i: (0, i, cb))
    const2 = lambda i: (0, 0)
    out = pl.pallas_call(
        _gla_kernel,
        grid=(blocks_per_seq,),
        in_specs=[
            pl.BlockSpec((bsz, ts, 2 * H * DK), row_map(C32_GQ // (2 * H * DK))),
            pl.BlockSpec((bsz, ts, H * DV), row_map(C16_GV // (H * DV))),
            pl.BlockSpec((bsz, ts, H * DV), row_map(C32_R // (H * DV))),
            pl.BlockSpec((bsz, ts, LANES), row_map(C32_SMALL // LANES)),
            pl.BlockSpec(w2p.shape, const2),
            pl.BlockSpec(b2.shape, const2),
            pl.BlockSpec(norm_g.shape, const2),
            pl.BlockSpec(tril.shape, const2),
            pl.BlockSpec(expand.shape, const2),
        ],
        out_specs=pl.BlockSpec((bsz, ts, H * DV), lambda i: (0, i, 0)),
        out_shape=jax.ShapeDtypeStruct((bsz, seq, H * DV), F32),
        scratch_shapes=[pltpu.VMEM((bsz, DV, H * DK), F32), pltpu.VMEM((SUB * SUB, H * DK), BF16),
                        pltpu.VMEM((bsz, ts, H * DK), F32)],
        compiler_params=pltpu.CompilerParams(
            dimension_semantics=("arbitrary",), vmem_limit_bytes=VMEM_LIMIT),
        name="gla_branch",
    )(p32, p16, p32, p32, w2p, b2, norm_g, tril, expand)
    return out.reshape(tokens, H * DV)


def _bit_transpose32(words):
    a = list(words)
    shift, mask = 16, 0x0000FFFF
    while shift:
        m = jnp.int32(np.uint32(mask).astype(np.int32))
        for k in range(32):
            if k & shift == 0:
                t = (a[k] ^ lax.shift_right_logical(a[k + shift], jnp.int32(shift))) & m
                a[k] = a[k] ^ t
                a[k + shift] = a[k + shift] ^ lax.shift_left(t, jnp.int32(shift))
        shift >>= 1
        mask = (mask ^ (mask << shift)) & 0xFFFFFFFF
    return a


def _dsa_kernel(q_ref, iq_ref, iwt_ref, k_ref, vt_ref, ik_ref, bias_ref, ltri_ref,
                o_ref, key_scr, plane_scr, s_scr, acc_scr, *, topk):
    T, TK = DSA_QBLOCK, DSA_KTILE
    SUBS = TK // T
    H, HD = DSA_HEADS, DSA_HDIM
    qb = pl.program_id(1)
    kt_last = qb // SUBS
    n_kt = kt_last + 1
    key_minus_query = (lax.broadcasted_iota(jnp.int32, (TK, T), 0)
                       - lax.broadcasted_iota(jnp.int32, (TK, T), 1))

    def fold(x, op):
        return op(x.reshape(x.shape[0] // 8, 8, T), axis=0)

    def rows_to_one(x, op):
        return op(x, axis=0, keepdims=True)

    def key_rows(kt):
        return pl.ds(pl.multiple_of(kt * TK, TK), TK)

    def tile_loop(n, body, init):
        carry = lax.fori_loop(0, n // 2, lambda i, c: body(2 * i + 1, body(2 * i, c)), init)
        return lax.cond(n % 2 == 1, lambda c: body(n - 1, c), lambda c: c, carry)

    lo_mask = lax.broadcasted_iota(jnp.int32, (T, LANES), 1) < DSA_IDX_DIM
    iq = [iq_ref[:, 0:LANES], iq_ref[:, LANES:2 * LANES]]
    iq_h = [jnp.where(lo_mask, iq[0], 0), jnp.where(lo_mask, 0, iq[0]),
            jnp.where(lo_mask, iq[1], 0), jnp.where(lo_mask, 0, iq[1])]
    iq_all = jnp.concatenate(iq_h, axis=0)
    iw_h = [iwt_ref[h:h + 1, :] for h in range(DSA_IDX_HEADS)]

    GRP_ROWS = 32 * 8
    GRPS = TK // GRP_ROWS
    n_grp = plane_scr.shape[0]

    @pl.when(qb == 0)
    def _():
        plane_scr[...] = jnp.zeros_like(plane_scr)

    n_keys = key_scr.shape[0] * TK
    minus_one_minus_row = -1 - lax.broadcasted_iota(jnp.int32, (TK, T), 0)

    def score_tile(kt):
        ik2 = ik_ref[key_rows(kt), :]
        raw = _dot_nt(ik2, iq_all)
        score = jnp.zeros((TK, T), F32)
        for h in range(DSA_IDX_HEADS):
            score = score + iw_h[h] * jnp.maximum(raw[:, h * T:(h + 1) * T], 0.0)
        bits = lax.bitcast_convert_type(score, jnp.int32)
        zero_key = minus_one_minus_row - kt * TK
        negative_key = (bits ^ jnp.int32(0x7FFFFFFF)) - n_keys
        return jnp.where(score == 0.0, zero_key, jnp.where(bits >= 0, bits, negative_key))

    def store_planes(kt, key):
        unsigned_order = key ^ jnp.int32(INT_MIN)
        for g in range(GRPS):
            words = [unsigned_order[g * GRP_ROWS + 8 * j:g * GRP_ROWS + 8 * j + 8, :] for j in range(32)]
            planes = _bit_transpose32(words)
            for b in range(32):
                plane_scr[kt * GRPS + g, b] = planes[b]

    def score_body(kt, c):
        key = score_tile(kt)
        key_scr[kt] = key
        store_planes(kt, key)
        return c

    tile_loop(kt_last, score_body, 0)
    admissible = key_minus_query <= qb * T - kt_last * TK
    key = score_tile(kt_last)
    key_scr[kt_last] = jnp.where(admissible, key, jnp.int32(INT_MIN))
    needs_select = (qb + 1) * T > topk

    @pl.when(needs_select)
    def _():
        store_planes(kt_last, key)

    alive_last = []
    for g in range(GRPS):
        word = jnp.zeros((8, T), jnp.int32)
        for j in range(32):
            adm = admissible[g * GRP_ROWS + 8 * j:g * GRP_ROWS + 8 * j + 8, :]
            word = word | jnp.where(adm, jnp.int32(np.int32(np.uint32(1 << (31 - j)))), 0)
        alive_last.append(word)

    alive = []
    for g in range(n_grp):
        kt = g // GRPS
        full = jnp.broadcast_to(jnp.where(kt < kt_last, jnp.int32(-1), jnp.int32(0)), (8, T))
        alive.append(jnp.where(kt == kt_last, alive_last[g % GRPS], full))

    def bits_body(i, carry):
        above, tau_u, alive = carry
        n_used = len(alive)
        hi_planes = [plane_scr[g, 2 * i] for g in range(n_used)]
        lo_planes = [plane_scr[g, 2 * i + 1] for g in range(n_used)]
        n11 = n10 = n01 = jnp.zeros((8, T), jnp.int32)
        for g in range(n_used):
            with_hi = alive[g] & hi_planes[g]
            both = with_hi & lo_planes[g]
            n11 = n11 + lax.population_count(both)
            n10 = n10 + lax.population_count(with_hi ^ both)
            n01 = n01 + lax.population_count((alive[g] & lo_planes[g]) ^ both)
        c11 = above + rows_to_one(n11, jnp.sum)
        c10 = c11 + rows_to_one(n10, jnp.sum)
        c01 = c10 + rows_to_one(n01, jnp.sum)
        is11, is10, is01 = c11 >= topk, c10 >= topk, c01 >= topk
        hi_bit = is10
        lo_bit = is11 | (is01 & ~is10)
        above = jnp.where(is11, above, jnp.where(is10, c11, jnp.where(is01, c10, c01)))
        flip_hi = jnp.where(hi_bit, jnp.int32(0), jnp.int32(-1))
        flip_lo = jnp.where(lo_bit, jnp.int32(0), jnp.int32(-1))
        alive = tuple(alive[g] & (hi_planes[g] ^ flip_hi) & (lo_planes[g] ^ flip_lo)
                      for g in range(n_used))
        tau_u = (tau_u | jnp.where(hi_bit, lax.shift_left(jnp.int32(1), 31 - 2 * i), 0)
                 | jnp.where(lo_bit, lax.shift_left(jnp.int32(1), 30 - 2 * i), 0))
        return above, tau_u, alive

    def select(n_used):
        def run():
            zero_row = jnp.zeros((1, T), jnp.int32)
            above, tau_u, left = lax.fori_loop(0, 16, bits_body,
                                               (zero_row, zero_row, tuple(alive[:n_used])))
            n_tied = jnp.zeros((8, T), jnp.int32)
            for g in range(n_used):
                n_tied = n_tied + lax.population_count(left[g])
            return above, tau_u, rows_to_one(n_tied, jnp.sum)
        return run

    def select_any():
        half = n_grp // 2
        if half >= GRPS and half % GRPS == 0:
            return lax.cond((kt_last + 1) * GRPS <= half, select(half), select(n_grp))
        return select(n_grp)()

    no_keys = jnp.zeros((1, T), jnp.int32)
    above, tau_u, n_tied = lax.cond(needs_select, select_any, lambda: (no_keys, no_keys, no_keys))
    n_ge = above + n_tied
    tau = jnp.maximum(tau_u ^ jnp.int32(INT_MIN), jnp.int32(INT_MIN + 1))

    @pl.when(jnp.max(n_ge) > topk)
    def _():
        need = (topk - above).astype(F32)

        def tie_body(kt, seen):
            for j in range(SUBS):
                rows = slice(j * T, (j + 1) * T)
                key = key_scr[kt, rows, :]
                eq = key == tau
                eq_f = jnp.where(eq, 1.0, 0.0)
                pref = _dot(ltri_ref[...], eq_f.astype(BF16)) + seen
                key_scr[kt, rows, :] = jnp.where(eq & (pref > need), tau - 1, key)
                seen = seen + rows_to_one(eq_f, jnp.sum)
            return seen

        tile_loop(n_kt, tie_body, jnp.zeros((1, T), F32))

    q_h = [q_ref[:, h * HD:(h + 1) * HD] for h in range(H)]

    def logits_body(kt, m_run):
        sel = key_scr[kt] >= tau
        kinds = [jnp.clip(qb - (kt * SUBS + j), 0, 2) for j in range(SUBS)]
        new_m = []
        for h in range(H):
            bias = jnp.concatenate([bias_ref[kinds[j], h] for j in range(SUBS)], axis=0)
            s = _dot_nt(k_ref[key_rows(kt), h * HD:(h + 1) * HD], q_h[h]) + bias
            s = jnp.where(sel, s, NEG_BIG)
            s_scr[h, kt] = s
            new_m.append(jnp.maximum(m_run[h], fold(s, jnp.max)))
        return tuple(new_m)

    m_run = tile_loop(n_kt, logits_body, tuple(jnp.full((8, T), NEG_BIG, F32) for _ in range(H)))
    m_h = [rows_to_one(m, jnp.max) for m in m_run]

    for h in range(H):
        acc_scr[h] = jnp.zeros((HD, T), F32)

    def pv_body(kt, l_run):
        new_l = []
        for h in range(H):
            p = jnp.exp2(s_scr[h, kt] - m_h[h])
            new_l.append(l_run[h] + fold(p, jnp.sum))
            acc_scr[h] = acc_scr[h] + _dot(vt_ref[kt, h * HD:(h + 1) * HD, :], p.astype(BF16))
        return tuple(new_l)

    l_run = tile_loop(n_kt, pv_body, tuple(jnp.zeros((8, T), F32) for _ in range(H)))

    for h in range(H):
        out_t = acc_scr[h] / rows_to_one(l_run[h], jnp.sum)
        o_ref[:, h * HD:(h + 1) * HD] = out_t.T.astype(o_ref.dtype)


def _dsa(p16, v_t, iw_t, bias_tiles, bsz, seq):
    T = DSA_QBLOCK
    tokens = p16.shape[0]
    nqb = seq // T
    topk = min(DSA_TOPK_MAX, seq // 4)
    TK = min(DSA_KTILE, seq)
    assert TK == DSA_KTILE and seq % TK == 0
    nkt = seq // TK
    ltri = jnp.asarray(np.tril(np.ones((T, T), np.float32))).astype(BF16)
    w = D_BRANCH
    v_t = v_t.reshape(bsz, nkt, w, TK)
    qmap = lambda cb: (lambda b, i: (b * nqb + i, cb))
    return pl.pallas_call(
        functools.partial(_dsa_kernel, topk=topk),
        grid=(bsz, nqb),
        in_specs=[
            pl.BlockSpec((T, w), qmap(C16_Q // w)),
            pl.BlockSpec((T, 2 * LANES), qmap(C16_IQ // (2 * LANES))),
            pl.BlockSpec((8, T), lambda b, i: (0, b * nqb + i)),
            pl.BlockSpec((seq, w), lambda b, i: (b, C16_K // w)),
            pl.BlockSpec((None, nkt, w, TK), lambda b, i: (b, 0, 0, 0)),
            pl.BlockSpec((seq, LANES), lambda b, i: (b, C16_IK // LANES)),
            pl.BlockSpec(bias_tiles.shape, lambda b, i: (0, 0, 0, 0)),
            pl.BlockSpec(ltri.shape, lambda b, i: (0, 0)),
        ],
        out_specs=pl.BlockSpec((T, w), lambda b, i: (b * nqb + i, 0)),
        out_shape=jax.ShapeDtypeStruct((tokens, w), BF16),
        scratch_shapes=[
            pltpu.VMEM((nkt, TK, T), jnp.int32),
            pltpu.VMEM((nkt * TK // 256, 32, 8, T), jnp.int32),
            pltpu.VMEM((DSA_HEADS, nkt, TK, T), F32),
            pltpu.VMEM((DSA_HEADS, DSA_HDIM, T), F32),
        ],
        compiler_params=pltpu.CompilerParams(
            dimension_semantics=("parallel", "arbitrary"), vmem_limit_bytes=VMEM_LIMIT),
        name="dsa_branch",
    )(p16, p16, iw_t, p16, v_t, p16, bias_tiles, ltri)


def _merge_kernel(u_ref, v_ref, ws_ref, bs_ref, yb_ref, yc_ref, ga_ref, gb_ref, gc_ref, x_ref, mod_ref,
                  wbr_ref, wout_ref, gn2_ref, wr_hi_ref, wr_lo_ref, br_ref, x1_ref, h2_ref, comb_ref):
    tm = u_ref.shape[0]
    ya_rows = []
    for c in range(tm // GMLP_CHUNK):
        rows = slice(c * GMLP_CHUNK, (c + 1) * GMLP_CHUNK)
        groups = []
        for g in range(GMLP_GROUPS):
            cols = slice(g * LANES, (g + 1) * LANES)
            mixed = _dot(ws_ref[g], v_ref[rows, cols].astype(BF16)) + bs_ref[:, cols]
            groups.append((u_ref[rows, cols] * mixed).astype(BF16))
        ya_rows.append(jnp.concatenate(groups, axis=1))
    ya = jnp.concatenate(ya_rows, axis=0)

    merged = ga_ref[...] * _dot(ya, wbr_ref[0])
    merged = merged + gb_ref[...] * _dot(yb_ref[...].astype(BF16), wbr_ref[1])
    merged = merged + gc_ref[...] * _dot(yc_ref[...], wbr_ref[2])
    mix = _dot(merged.astype(BF16), wout_ref[...])
    x1 = x_ref[...] + mod_ref[0, 2:3, :] * mix
    x1_ref[...] = x1
    y = x1 * lax.rsqrt(jnp.mean(x1 * x1, axis=-1, keepdims=True) + EPS) * gn2_ref[...]
    h2 = y * (1.0 + mod_ref[0, 4:5, :]) + mod_ref[0, 3:4, :]
    h2_ref[...] = h2.astype(BF16)
    logits = _dot3(h2, wr_hi_ref[...], wr_lo_ref[...]) + br_ref[...]
    comb_ref[...] = _route_packed(logits).astype(BF16)


def _merge(p32, w_tril, gmlp_bias, yb, yc, p16, x2d, mod_l, wbr, wout, layer, gn2, wr_hi, wr_lo, br,
           seq, tm):
    tokens, d = x2d.shape
    tiles_per_seq = seq // tm
    row = lambda i: (i, 0)
    gate = lambda k: (lambda i: (i, C16_GATES // d + k))
    c2 = lambda i: (0, 0)
    of_layer = lambda a: pl.BlockSpec((None,) + a.shape[1:], lambda i: (layer,) + (0,) * (a.ndim - 1))
    return pl.pallas_call(
        _merge_kernel,
        grid=(tokens // tm,),
        in_specs=[
            pl.BlockSpec((tm, D_BRANCH), lambda i: (i, C32_U // D_BRANCH)),
            pl.BlockSpec((tm, D_BRANCH), lambda i: (i, C32_V // D_BRANCH)),
            of_layer(w_tril), of_layer(gmlp_bias),
            pl.BlockSpec((tm, D_BRANCH), row), pl.BlockSpec((tm, D_BRANCH), row),
            pl.BlockSpec((tm, d), gate(0)), pl.BlockSpec((tm, d), gate(1)), pl.BlockSpec((tm, d), gate(2)),
            pl.BlockSpec((tm, d), row),
            pl.BlockSpec((1, 6, d), lambda i: (i // tiles_per_seq, 0, 0)),
            of_layer(wbr), of_layer(wout),
            pl.BlockSpec(gn2.shape, c2),
            of_layer(wr_hi), of_layer(wr_lo), of_layer(br),
        ],
        out_specs=[pl.BlockSpec((tm, d), row), pl.BlockSpec((tm, d), row),
                   pl.BlockSpec((tm, LANES), row)],
        out_shape=[jax.ShapeDtypeStruct((tokens, d), F32), jax.ShapeDtypeStruct((tokens, d), BF16),
                   jax.ShapeDtypeStruct((tokens, LANES), BF16)],
        compiler_params=pltpu.CompilerParams(
            dimension_semantics=("parallel",), vmem_limit_bytes=VMEM_LIMIT),
        name="gmlp_merge_norm_router",
    )(p32, p32, w_tril, gmlp_bias, yb, yc, p16, p16, p16, x2d, mod_l, wbr, wout, gn2, wr_hi, wr_lo, br)


def _route_packed(lg):
    n_rows = lg.shape[0]
    t = lg.T[0:MOE_PART_LANES, :]
    idx = lax.broadcasted_iota(jnp.int32, t.shape, 0)
    big = jnp.int32(10 ** 6)
    top = lambda x, op: op(x, axis=0, keepdims=True)
    is_grp = idx < N_GROUPS
    gl = jnp.where(is_grp, t, -jnp.inf)
    gmax = top(gl, jnp.max)
    p_g = 1.0 / top(jnp.where(is_grp, jnp.exp(t - gmax), 0.0), jnp.sum)
    g_idx = top(jnp.where(gl == gmax, idx, big), jnp.min)
    first = N_GROUPS + g_idx * EXPERTS_PER_GROUP
    in_grp = (idx >= first) & (idx < first + EXPERTS_PER_GROUP)
    e1 = jnp.where(in_grp, t, -jnp.inf)
    v1 = top(e1, jnp.max)
    i1 = top(jnp.where(e1 == v1, idx, big), jnp.min)
    e2 = jnp.where(in_grp & (idx != i1), t, -jnp.inf)
    v2 = top(e2, jnp.max)
    i2 = top(jnp.where(e2 == v2, idx, big), jnp.min)
    r = jnp.exp(v2 - v1)
    w1 = p_g * (1.0 / (1.0 + r))
    w2 = p_g * (r / (1.0 + r))
    comb = jnp.where(idx == i1, w1, jnp.where(idx == i2, w2, 0.0))
    hi = comb.astype(BF16).astype(F32)
    mid = (comb - hi).astype(BF16).astype(F32)
    lo = ((comb - hi) - mid).astype(BF16).astype(F32)
    first_part = jnp.where(idx == g_idx, 1.0, 0.0) + hi
    pad = jnp.zeros((LANES - 3 * MOE_PART_LANES, n_rows), F32)
    return jnp.concatenate([first_part, mid, lo, pad], axis=0).T


def _moe_kernel(h_ref, comb_ref, x1_ref, mod_ref, wg_ref, wu_ref, wd_ref, tril_ref, o_ref,
                hp_scr, cwp_scr, yp_scr, pt_scr, start_ref, nchunk_ref):
    grp = pl.program_id(1)
    tm = h_ref.shape[0]
    n_sorted = hp_scr.shape[0]
    lane = lax.broadcasted_iota(jnp.int32, (1, LANES), 1)

    @pl.when(grp == 0)
    def _():
        onehot = jnp.where(lane < N_GROUPS, comb_ref[...], 0)
        incl = _dot(tril_ref[...], onehot)
        counts = incl[tm - 1:tm, :]
        start = jnp.int32(0)
        start_row = jnp.zeros((1, LANES), F32)
        for g in range(N_GROUPS):
            cnt = jnp.sum(jnp.where(lane == g, counts, 0.0)).astype(jnp.int32)
            start_ref[g] = start
            nchunk_ref[g] = lax.shift_right_logical(cnt + (MOE_CHUNK - 1), MOE_CHUNK.bit_length() - 1)
            start_row = jnp.where(lane == g, start.astype(F32), start_row)
            align = MOE_ALIGN.bit_length() - 1
            start = start + lax.shift_left(lax.shift_right_logical(cnt + (MOE_ALIGN - 1), align), align)
        dest = jnp.sum(onehot.astype(F32) * (incl - 1.0 + start_row), axis=1, keepdims=True)
        dest_col = dest.astype(jnp.int32)
        dest_row = jnp.broadcast_to(dest, (tm, LANES)).T[0:1, :].astype(jnp.int32)
        back = lax.broadcasted_iota(jnp.int32, (tm, n_sorted), 1) == dest_col
        pt_scr[...] = jnp.where(back, 1.0, 0.0).astype(BF16)
        fwd = lax.broadcasted_iota(jnp.int32, (n_sorted, tm), 0) == dest_row
        fwd = jnp.where(fwd, 1.0, 0.0).astype(BF16)
        hp_scr[...] = _dot(fwd, h_ref[...]).astype(BF16)
        cwp_scr[...] = _dot(fwd, comb_ref[...]).astype(BF16)
        yp_scr[...] = jnp.zeros_like(yp_scr)

    row_id = lax.broadcasted_iota(jnp.int32, (LANES, LANES), 0)
    start = start_ref[grp]

    def chunk(c, carry):
        rows = pl.ds(pl.multiple_of(start + c * MOE_CHUNK, MOE_ALIGN), MOE_CHUNK)
        h = hp_scr[rows, :]
        acts = []
        for j in range(EXPERTS_PER_GROUP):
            lane_of_expert = N_GROUPS + grp * EXPERTS_PER_GROUP + j
            pick = ((row_id == lane_of_expert) | (row_id == lane_of_expert + MOE_PART_LANES)
                    | (row_id == lane_of_expert + 2 * MOE_PART_LANES))
            cw = _dot(cwp_scr[rows, :], jnp.where(pick, 1.0, 0.0).astype(BF16))
            act = (jax.nn.silu(_dot(h, wg_ref[j])) * _dot(h, wu_ref[j])
                   * jnp.concatenate([cw, cw], axis=1))
            acts.append(act.astype(BF16))
        yp_scr[rows, :] = _dot(jnp.concatenate(acts, axis=1), wd_ref[...]).astype(BF16)
        return carry

    n_chunks = nchunk_ref[grp]
    lax.fori_loop(0, n_chunks // 2, lambda i, c: chunk(2 * i + 1, chunk(2 * i, c)), 0)

    @pl.when(n_chunks % 2 == 1)
    def _():
        chunk(n_chunks - 1, 0)

    @pl.when(grp == pl.num_programs(1) - 1)
    def _():
        o_ref[...] = x1_ref[...] + mod_ref[0, 5:6, :] * _dot(pt_scr[...], yp_scr[...])


def _moe(h2, comb, x1, mod_l, wg, wu, wd, layer, seq, tm):
    tokens, d = x1.shape
    tiles_per_seq = seq // tm
    n_sorted = tm + N_GROUPS * MOE_ALIGN + MOE_CHUNK
    n_sorted = LANES * pl.cdiv(n_sorted, LANES)
    tril = jnp.asarray(np.tril(np.ones((tm, tm), np.float32))).astype(BF16)
    row = lambda i, g: (i, 0)
    return pl.pallas_call(
        _moe_kernel,
        grid=(tokens // tm, N_GROUPS),
        in_specs=[
            pl.BlockSpec((tm, d), row), pl.BlockSpec((tm, LANES), row),
            pl.BlockSpec((tm, d), row),
            pl.BlockSpec((1, 6, d), lambda i, g: (i // tiles_per_seq, 0, 0)),
            pl.BlockSpec((None, None) + wg.shape[2:], lambda i, g: (layer, g, 0, 0, 0)),
            pl.BlockSpec((None, None) + wu.shape[2:], lambda i, g: (layer, g, 0, 0, 0)),
            pl.BlockSpec((None, None) + wd.shape[2:], lambda i, g: (layer, g, 0, 0)),
            pl.BlockSpec(tril.shape, lambda i, g: (0, 0), pipeline_mode=pl.Buffered(1)),
        ],
        out_specs=pl.BlockSpec((tm, d), row),
        out_shape=jax.ShapeDtypeStruct((tokens, d), F32),
        scratch_shapes=[pltpu.VMEM((n_sorted, d), BF16), pltpu.VMEM((n_sorted, LANES), BF16),
                        pltpu.VMEM((n_sorted, d), BF16), pltpu.VMEM((tm, n_sorted), BF16),
                        pltpu.SMEM((N_GROUPS,), jnp.int32), pltpu.SMEM((N_GROUPS,), jnp.int32)],
        compiler_params=pltpu.CompilerParams(
            dimension_semantics=("parallel", "arbitrary"), vmem_limit_bytes=VMEM_LIMIT),
        name="hier_moe",
    )(h2, comb, x1, mod_l, wg, wu, wd, tril)


def _row_tile(seq, want):
    t = min(want, seq)
    assert seq % t == 0
    return t


def kernel(x, c, w_mod, b_mod, g_norm1, g_norm2, w_in, gmlp_ln_g, gmlp_ln_b, gmlp_w_s, gmlp_b_s,
           gla_w_gate2, gla_b_gate, gla_norm_g, dsa_qnorm_g, dsa_knorm_g, rel_bias, w_branch,
           b_branch_gate, w_out, w_group, b_group, w_router, b_router, w_exp_gate, w_exp_up,
           w_exp_down):
    bsz, seq, d = x.shape
    depth = w_mod.shape[0]
    assert d == D_MODEL and seq % DSA_QBLOCK == 0 and seq % GLA_CHUNK == 0
    tokens = bsz * seq
    tm = _row_tile(seq, 512)

    mods = _modulation(c, w_mod, b_mod)
    bias_tiles = _bias_tiles(rel_bias)
    w32, w16 = _prep_w_in(w_in)

    causal = np.tril(np.ones((GMLP_CHUNK, GMLP_CHUNK), bool))
    w_tril = jnp.where(causal[None, None], gmlp_w_s, 0.0).astype(BF16)
    gmlp_bias = jnp.repeat(jnp.swapaxes(gmlp_b_s, 1, 2), LANES, axis=-1)
    w2p = jnp.zeros((depth, LANES, GLA_HEADS * GLA_DK), F32).at[:, :GLA_GATE_RANK].set(gla_w_gate2)
    w2p = w2p.astype(BF16)
    wr = jnp.concatenate(
        [w_group, w_router, jnp.zeros((depth, d, LANES - N_GROUPS - N_EXPERTS), F32)], axis=-1)
    wr_hi = wr.astype(BF16)
    wr_lo = (wr - wr_hi.astype(F32)).astype(BF16)
    br = jnp.zeros((depth, 1, LANES), F32)
    br = br.at[:, 0, :N_GROUPS].set(b_group).at[:, 0, N_GROUPS:N_GROUPS + N_EXPERTS].set(b_router)
    wbr = w_branch.astype(BF16)
    wout = w_out.astype(BF16)
    grouped = (depth, N_GROUPS, EXPERTS_PER_GROUP, d, D_EXPERT)
    wg, wu = w_exp_gate.astype(BF16).reshape(grouped), w_exp_up.astype(BF16).reshape(grouped)
    wd = w_exp_down.astype(BF16).reshape(depth, N_GROUPS, EXPERTS_PER_GROUP * D_EXPERT, d)
    tm_moe = _row_tile(seq, 1024)

    x2d = x.reshape(tokens, d)
    for l in range(depth):
        mod_l = mods[l]
        aux32 = [gmlp_ln_g[l][None], gmlp_ln_b[l][None]]
        aux16 = [dsa_qnorm_g[l][None], dsa_knorm_g[l][None], b_branch_gate[l].reshape(1, -1)]
        p32, p16, v_t, iw_t = _norm_proj(x2d, mod_l, g_norm1[l][None], w32, w16, l, aux32, aux16,
                                         seq, tm)
        yb = _gla(p32, p16, w2p[l], gla_b_gate[l][None], gla_norm_g[l][None], bsz, seq, tm)
        yc = _dsa(p16, v_t, iw_t, bias_tiles, bsz, seq)
        x1, h2, comb = _merge(p32, w_tril, gmlp_bias, yb, yc, p16, x2d, mod_l, wbr, wout, l,
                              g_norm2[l][None], wr_hi, wr_lo, br, seq, tm)
        x2d = _moe(h2, comb, x1, mod_l, wg, wu, wd, l, seq, tm_moe)
    return x2d.reshape(bsz, seq, d)
```
